```python
import math
import jax
import jax.numpy as jnp
from jax import lax
import numpy as np


D_MODEL = 1024
BATCH = 2
SEQ = 16384
DEPTH = 2
DEC_BATCH = 8
DEC_SEQ = 2048
PAST_LEN = 128

HEAD_DIM = 64
GRID_W = 64
N_MEM = 256
LN_EPS = 1e-5
RMS_EPS = 1e-6
NEG_INF = -1e30

A_HEADS = 12
A_WIDTH = A_HEADS * HEAD_DIM
A_PATTERNS = ((128, 1), (512, 4), (2048, 16))
N_BUCKETS = 32
REL_MAX_DIST = 1024
FNET_GROUPS = 4
FNET_CH = 64
B_WIDTH = FNET_GROUPS * FNET_CH
C_Q_HEADS = 12
C_KV_HEADS = 4
C_WIDTH = C_Q_HEADS * HEAD_DIM
C_KV_WIDTH = C_KV_HEADS * HEAD_DIM
Q_BLOCK = 128
ROPE_THETA = 10000.0
POOL_WINDOWS = (2, 4, 8, 16)
POOL_CH = 64
D_WIDTH = len(POOL_WINDOWS) * POOL_CH
AB_IN = 3 * A_WIDTH + B_WIDTH
CD_IN = C_WIDTH + 2 * C_KV_WIDTH + D_WIDTH
AB_OUT = A_WIDTH + B_WIDTH
CD_OUT = C_WIDTH + D_WIDTH
XA_HEADS = 4
XA_HEAD_DIM = D_MODEL // XA_HEADS
FFN_HIDDEN = -(-8 * D_MODEL // (3 * 256)) * 256
DN_ALPHA = (2 * DEPTH) ** 0.25
DN_BETA = (8 * DEPTH) ** -0.25
N_EVEN = (DEPTH + 1) // 2
N_ODD = DEPTH // 2

kernel_name = 'hybrid_dilated_fourier_gqa_pool_encoder'


def layer_norm(x, g, b):
    xf = x.astype(jnp.float32)
    mu = xf.mean(-1, keepdims=True)
    var = jnp.square(xf - mu).mean(-1, keepdims=True)
    return ((xf - mu) * lax.rsqrt(var + LN_EPS) * g + b).astype(x.dtype)


def rms_norm(x, g):
    xf = x.astype(jnp.float32)
    return (xf * lax.rsqrt(jnp.square(xf).mean(-1, keepdims=True) + RMS_EPS) * g).astype(x.dtype)


def t5_bucket(rel):
    nb = N_BUCKETS // 2
    max_exact = nb // 2
    ret = jnp.where(rel > 0, nb, 0)
    n = jnp.abs(rel)
    nf = jnp.maximum(n, 1).astype(jnp.float32)
    large = max_exact + (jnp.log(nf / max_exact) / math.log(REL_MAX_DIST / max_exact)
                         * (nb - max_exact)).astype(jnp.int32)
    large = jnp.minimum(large, nb - 1)
    return ret + jnp.where(n < max_exact, n, large)


def dilated_window_attention(q, k, v, rel_bias, dilation, half):
    b, n, h, dh = q.shape
    L = n // dilation
    blk = half
    nblk = -(-L // blk)
    Lp = nblk * blk

    def to_strided(t):
        t = t.reshape(b, L, dilation, h, dh).transpose(0, 2, 1, 3, 4)
        return t.reshape(b * dilation, L, h, dh)

    def band(t):
        t = jnp.pad(t, ((0, 0), (blk, Lp - L + blk), (0, 0), (0, 0))).reshape(b * dilation, nblk + 2, blk, h, dh)
        return jnp.concatenate([t[:, :-2], t[:, 1:-1], t[:, 2:]], axis=2)

    qs = jnp.pad(to_strided(q), ((0, 0), (0, Lp - L), (0, 0), (0, 0))).reshape(b * dilation, nblk, blk, h, dh)
    kb = band(to_strided(k))
    vb = band(to_strided(v))

    off = jnp.arange(3 * blk)[None, :] - blk - jnp.arange(blk)[:, None]
    in_band = jnp.abs(off) <= half
    bias = rel_bias[t5_bucket(off * dilation)].transpose(2, 0, 1)
    key_u = jnp.arange(nblk)[:, None] * blk + jnp.arange(3 * blk)[None, :] - blk
    key_ok = (key_u >= 0) & (key_u < L)
    mask = in_band[None] & key_ok[:, None, :]

    logits = jnp.einsum('bnqhd,bnkhd->bnhqk', qs, kb, preferred_element_type=jnp.float32) * (HEAD_DIM ** -0.5)
    logits = logits + bias[None, None].astype(jnp.float32)
    logits = jnp.where(mask[None, :, None], logits, NEG_INF)
    lse = jax.nn.logsumexp(logits, axis=-1)
    p = jnp.exp(logits - lse[..., None]).astype(v.dtype)
    o = jnp.einsum('bnhqk,bnkhd->bnqhd', p, vb)
    o = o.reshape(b, dilation, Lp, h, dh)[:, :, :L].transpose(0, 2, 1, 3, 4).reshape(b, n, h, dh)
    lse = lse.transpose(0, 1, 3, 2).reshape(b, dilation, Lp, h)[:, :, :L].transpose(0, 2, 1, 3).reshape(b, n, h)
    return o, lse


def dilated_mixture(q, k, v, rel_bias):
    outs, lses = [], []
    for window, dilation in A_PATTERNS:
        o, lse = dilated_window_attention(q, k, v, rel_bias, dilation, window // (2 * dilation))
        outs.append(o)
        lses.append(lse)
    wts = jax.nn.softmax(jnp.stack(lses), axis=0).astype(q.dtype)
    return jnp.einsum('gbnh,gbnhd->bnhd', wts, jnp.stack(outs))


def fourier_mixer(u, g, w):
    uf = u.astype(jnp.float32)
    mu = uf.mean(-1, keepdims=True)
    var = jnp.square(uf - mu).mean(-1, keepdims=True)
    un = (uf - mu) * lax.rsqrt(var + LN_EPS) * g
    f = jnp.fft.fft2(un, axes=(1, 3), norm='ortho').real
    return jnp.einsum('bngc,gce->bnge', f.astype(u.dtype), w)


def axial_rope_tables(n):
    rows = n // GRID_W
    row_id = jnp.broadcast_to(jnp.arange(rows)[:, None], (rows, GRID_W)).reshape(n)
    col_id = jnp.broadcast_to(jnp.arange(GRID_W)[None, :], (rows, GRID_W)).reshape(n)
    axis_dim = HEAD_DIM // 2
    freqs = ROPE_THETA ** (-jnp.arange(0, axis_dim, 2, dtype=jnp.float32) / axis_dim)
    ang = jnp.concatenate([row_id[:, None] * freqs, col_id[:, None] * freqs], axis=-1)
    return jnp.cos(ang), jnp.sin(ang)


def apply_rope(x, cos, sin):
    xf = x.astype(jnp.float32).reshape(*x.shape[:-1], HEAD_DIM // 2, 2)
    c = cos[None, :, None]
    s = sin[None, :, None]
    x1, x2 = xf[..., 0], xf[..., 1]
    out = jnp.stack([x1 * c - x2 * s, x1 * s + x2 * c], axis=-1).reshape(x.shape)
    return out.astype(x.dtype)


def gqa_blocked(q, k, v):
    b, n, hq, dh = q.shape
    rep = hq // C_KV_HEADS
    nb = n // Q_BLOCK
    qb = q.reshape(b, nb, Q_BLOCK, C_KV_HEADS, rep, dh).transpose(1, 0, 2, 3, 4, 5)

    def one_block(qblk):
        s = jnp.einsum('bqgrd,bkgd->bgrqk', qblk, k, preferred_element_type=jnp.float32) * (HEAD_DIM ** -0.5)
        p = jax.nn.softmax(s, axis=-1).astype(v.dtype)
        return jnp.einsum('bgrqk,bkgd->bqgrd', p, v)

    o = lax.map(one_block, qb)
    return o.transpose(1, 0, 2, 3, 4, 5).reshape(b, n, hq, dh)


def multiscale_pool(u, w, scale):
    b, n = u.shape[0], u.shape[1]
    uf = u.astype(jnp.float32)
    csum = jnp.pad(jnp.cumsum(uf, axis=1), ((0, 0), (1, 0), (0, 0), (0, 0)))
    t = jnp.arange(n)
    pooled = []
    for gi, wdw in enumerate(POOL_WINDOWS):
        lo = jnp.clip(t - wdw // 2, 0, n)
        hi = jnp.clip(t + wdw // 2, 0, n)
        sg = csum[:, hi, gi] - csum[:, lo, gi]
        pooled.append(sg / (hi - lo).astype(jnp.float32)[None, :, None])
    mixed = (jnp.stack(pooled, axis=2) - uf).astype(u.dtype)
    y = jnp.einsum('bngc,gce->bnge', mixed, w)
    return y.reshape(b, n, D_WIDTH) * scale


def mixer_ab(x, rel_bias, w_in, fnet_g, fnet_w, w_out):
    b, n, _ = x.shape
    z = x @ w_in
    q, k, v, u = jnp.split(z, [A_WIDTH, 2 * A_WIDTH, 3 * A_WIDTH], axis=-1)
    q = q.reshape(b, n, A_HEADS, HEAD_DIM)
    k = k.reshape(b, n, A_HEADS, HEAD_DIM)
    v = v.reshape(b, n, A_HEADS, HEAD_DIM)
    o_a = dilated_mixture(q, k, v, rel_bias).reshape(b, n, A_WIDTH)
    o_b = fourier_mixer(u.reshape(b, n, FNET_GROUPS, FNET_CH), fnet_g, fnet_w).reshape(b, n, B_WIDTH)
    return jnp.concatenate([o_a, o_b], axis=-1) @ w_out


def mixer_cd(x, cos, sin, w_in, q_norm, k_norm, pool_w, pool_scale, w_out):
    b, n, _ = x.shape
    z = x @ w_in
    q, k, v, u = jnp.split(z, [C_WIDTH, C_WIDTH + C_KV_WIDTH, C_WIDTH + 2 * C_KV_WIDTH], axis=-1)
    q = apply_rope(rms_norm(q.reshape(b, n, C_Q_HEADS, HEAD_DIM), q_norm), cos, sin)
    k = apply_rope(rms_norm(k.reshape(b, n, C_KV_HEADS, HEAD_DIM), k_norm), cos, sin)
    v = v.reshape(b, n, C_KV_HEADS, HEAD_DIM)
    o_c = gqa_blocked(q, k, v).reshape(b, n, C_WIDTH)
    o_d = multiscale_pool(u.reshape(b, n, len(POOL_WINDOWS), POOL_CH), pool_w, pool_scale)
    return jnp.concatenate([o_c, o_d], axis=-1) @ w_out


def memory_cross_attention(x, mem, w_q, w_kv, w_o):
    b, n, _ = x.shape
    m = mem.shape[1]
    q = (x @ w_q).reshape(b, n, XA_HEADS, XA_HEAD_DIM)
    kv = (mem @ w_kv).reshape(b, m, 2, XA_HEADS, XA_HEAD_DIM)
    s = jnp.einsum('bnhd,bmhd->bhnm', q, kv[:, :, 0], preferred_element_type=jnp.float32) * (XA_HEAD_DIM ** -0.5)
    p = jax.nn.softmax(s, axis=-1).astype(x.dtype)
    o = jnp.einsum('bhnm,bmhd->bnhd', p, kv[:, :, 1]).reshape(b, n, D_MODEL)
    return o @ w_o


def swiglu(x, w_in, w_out):
    g, u = jnp.split(x @ w_in, 2, axis=-1)
    return (jax.nn.silu(g) * u) @ w_out


def run_trunk(x, mem, rel_bias, ab_w_in, ab_fnet_g, ab_fnet_w, ab_w_out,
              cd_w_in, cd_q_norm, cd_k_norm, cd_pool_w, cd_pool_scale, cd_w_out,
              xa_w_q, xa_w_kv, xa_w_o, ffn_w_in, ffn_w_out, ln_g, ln_b):
    n = x.shape[1]
    cos, sin = axial_rope_tables(n)
    for layer in range(DEPTH):
        i = layer // 2
        if layer % 2 == 0:
            h = mixer_ab(x, rel_bias, ab_w_in[i], ab_fnet_g[i], ab_fnet_w[i], ab_w_out[i])
        else:
            h = mixer_cd(x, cos, sin, cd_w_in[i], cd_q_norm[i], cd_k_norm[i],
                         cd_pool_w[i], cd_pool_scale[i], cd_w_out[i])
        x = layer_norm(DN_ALPHA * x + h, ln_g[layer, 0], ln_b[layer, 0])
        h = memory_cross_attention(x, mem, xa_w_q[layer], xa_w_kv[layer], xa_w_o[layer])
        x = layer_norm(DN_ALPHA * x + h, ln_g[layer, 1], ln_b[layer, 1])
        h = swiglu(x, ffn_w_in[layer], ffn_w_out[layer])
        x = layer_norm(DN_ALPHA * x + h, ln_g[layer, 2], ln_b[layer, 2])
    return x


def setup_inputs(seed: int = 0) -> dict:
    key = jax.random.key(seed)
    ks = jax.random.split(key, 22)
    D = D_MODEL

    def nrm(k, shape, s):
        return jax.random.normal(k, shape, jnp.float32) * s

    return {
        'x_prompt': nrm(ks[0], (BATCH, SEQ, D), 1.0),
        'x_sample': nrm(ks[1], (DEC_BATCH, DEC_SEQ, D), 1.0),
        'mem_prompt': nrm(ks[2], (BATCH, N_MEM, D), 1.0),
        'mem_sample': nrm(ks[3], (DEC_BATCH, N_MEM, D), 1.0),
        'rel_bias': nrm(ks[4], (N_BUCKETS, A_HEADS), 0.2),
        'ab_w_in': nrm(ks[5], (N_EVEN, D, AB_IN), D ** -0.5),
        'ab_fnet_g': 1.0 + nrm(ks[6], (N_EVEN, FNET_GROUPS, FNET_CH), 0.02),
        'ab_fnet_w': nrm(ks[7], (N_EVEN, FNET_GROUPS, FNET_CH, FNET_CH), FNET_CH ** -0.5),
        'ab_w_out': nrm(ks[8], (N_EVEN, AB_OUT, D), AB_OUT ** -0.5 * DN_BETA),
        'cd_w_in': nrm(ks[9], (N_ODD, D, CD_IN), D ** -0.5),
        'cd_q_norm': 1.0 + nrm(ks[10], (N_ODD, HEAD_DIM), 0.02),
        'cd_k_norm': 1.0 + nrm(ks[11], (N_ODD, HEAD_DIM), 0.02),
        'cd_pool_w': nrm(ks[12], (N_ODD, len(POOL_WINDOWS), POOL_CH, POOL_CH), POOL_CH ** -0.5),
        'cd_pool_scale': 1.0 + nrm(ks[13], (N_ODD, D_WIDTH), 0.02),
        'cd_w_out': nrm(ks[14], (N_ODD, CD_OUT, D), CD_OUT ** -0.5 * DN_BETA),
        'xa_w_q': nrm(ks[15], (DEPTH, D, D), D ** -0.5),
        'xa_w_kv': nrm(ks[16], (DEPTH, D, 2 * D), D ** -0.5),
        'xa_w_o': nrm(ks[17], (DEPTH, D, D), D ** -0.5 * DN_BETA),
        'ffn_w_in': nrm(ks[18], (DEPTH, D, 2 * FFN_HIDDEN), D ** -0.5),
        'ffn_w_out': nrm(ks[19], (DEPTH, FFN_HIDDEN, D), FFN_HIDDEN ** -0.5 * DN_BETA),
        'ln_g': 1.0 + nrm(ks[20], (DEPTH, 3, D), 0.02),
        'ln_b': nrm(ks[21], (DEPTH, 3, D), 0.02),
    }


def reference(x_prompt, x_sample, mem_prompt, mem_sample, rel_bias, ab_w_in, ab_fnet_g, ab_fnet_w, ab_w_out,
              cd_w_in, cd_q_norm, cd_k_norm, cd_pool_w, cd_pool_scale, cd_w_out,
              xa_w_q, xa_w_kv, xa_w_o, ffn_w_in, ffn_w_out, ln_g, ln_b):
    y_prompt = run_trunk(x_prompt, mem_prompt, rel_bias, ab_w_in, ab_fnet_g, ab_fnet_w, ab_w_out,
                         cd_w_in, cd_q_norm, cd_k_norm, cd_pool_w, cd_pool_scale, cd_w_out,
                         xa_w_q, xa_w_kv, xa_w_o, ffn_w_in, ffn_w_out, ln_g, ln_b)
    y_sample = run_trunk(x_sample, mem_sample, rel_bias, ab_w_in, ab_fnet_g, ab_fnet_w, ab_w_out,
                         cd_w_in, cd_q_norm, cd_k_norm, cd_pool_w, cd_pool_scale, cd_w_out,
                         xa_w_q, xa_w_kv, xa_w_o, ffn_w_in, ffn_w_out, ln_g, ln_b)
    return (y_prompt, y_sample)
```

```python
import functools
import math

import numpy as np
import jax
import jax.numpy as jnp
from jax import lax
from jax.experimental import pallas as pl
from jax.experimental.pallas import tpu as pltpu

F32 = jnp.float32
BF16 = jnp.bfloat16

D_MODEL = 1024
HEAD_DIM = 64
GRID_W = 64
LN_EPS = 1e-5
RMS_EPS = 1e-6
NEG_INF = -1e30
DEPTH = 2
A_HEADS = 12
A_WIDTH = A_HEADS * HEAD_DIM
A_PATTERNS = ((128, 1), (512, 4), (2048, 16))
A_HALF = 64
N_BUCKETS = 32
REL_MAX_DIST = 1024
B_WIDTH = 256
C_Q_HEADS = 12
C_KV_HEADS = 4
C_REP = C_Q_HEADS // C_KV_HEADS
C_WIDTH = C_Q_HEADS * HEAD_DIM
C_KV_WIDTH = C_KV_HEADS * HEAD_DIM
ROPE_THETA = 10000.0
POOL_WINDOWS = (2, 4, 8, 16)
POOL_HALO = 8
D_WIDTH = 256
XA_HEADS = 4
XA_HEAD_DIM = D_MODEL // XA_HEADS
FFN_HIDDEN = 2816
DN_ALPHA = (2 * DEPTH) ** 0.25
LOG2E = 1.4426950408889634

VMEM_LIMIT = 56 * 1024 * 1024
TM = 512
FFT_N2 = 128
GQA_TQ = 1024
GQA_TKC = 256


def _cparams(*sem):
    return pltpu.CompilerParams(dimension_semantics=sem, vmem_limit_bytes=VMEM_LIMIT)


def _dot(a, b):
    return jnp.dot(a, b, preferred_element_type=F32)


def _dot_nt(a, b):
    return lax.dot_general(a, b, (((1,), (1,)), ((), ())), preferred_element_type=F32)


def _dot_tn(a, b):
    return lax.dot_general(a, b, (((0,), (0,)), ((), ())), preferred_element_type=F32)


def _split(x):
    hi = x.astype(BF16)
    lo = (x - hi.astype(F32)).astype(BF16)
    return hi, lo


def _dot3(ah, al, bh, bl):
    return _dot(ah, bh) + _dot(al, bh) + _dot(ah, bl)


def _np_split(x):
    x = np.asarray(x, np.float64)
    hi = jnp.asarray(x, F32).astype(BF16)
    lo = (jnp.asarray(x, F32) - hi.astype(F32)).astype(BF16)
    return hi, lo


def _layer_norm(h, g, b):
    mu = jnp.mean(h, axis=-1, keepdims=True)
    xc = h - mu
    var = jnp.mean(xc * xc, axis=-1, keepdims=True)
    return xc * lax.rsqrt(var + LN_EPS) * g + b


def _proj_ab_kernel(x_ref, w_ref, qkv_ref, u_ref):
    xb = x_ref[0].astype(BF16)
    for c in range(0, 3 * A_WIDTH, 256):
        qkv_ref[0, :, c:c + 256] = _dot(xb, w_ref[:, c:c + 256]).astype(BF16)
    u_ref[0] = _dot(xb, w_ref[:, 3 * A_WIDTH:])


def _proj_ab(x, w):
    b, n, _ = x.shape
    return pl.pallas_call(
        _proj_ab_kernel,
        grid=(b, n // TM),
        in_specs=[pl.BlockSpec((1, TM, D_MODEL), lambda i, j: (i, j, 0)),
                  pl.BlockSpec(w.shape, lambda i, j: (0, 0))],
        out_specs=[pl.BlockSpec((1, TM, 3 * A_WIDTH), lambda i, j: (i, j, 0)),
                   pl.BlockSpec((1, TM, B_WIDTH), lambda i, j: (i, j, 0))],
        out_shape=[jax.ShapeDtypeStruct((b, n, 3 * A_WIDTH), BF16),
                   jax.ShapeDtypeStruct((b, n, B_WIDTH), F32)],
        compiler_params=_cparams("parallel", "parallel"),
        name="proj_ab",
    )(x, w)


def _t5_bucket_np(rel):
    nb = N_BUCKETS // 2
    max_exact = nb // 2
    ret = np.where(rel > 0, nb, 0)
    n = np.abs(rel)
    nf = np.maximum(n, 1).astype(np.float32)
    large = max_exact + (np.log(nf / max_exact) / math.log(REL_MAX_DIST / max_exact)
                         * (nb - max_exact)).astype(np.int32)
    large = np.minimum(large, nb - 1)
    return ret + np.where(n < max_exact, n, large)


def _band_bias(rel_bias, dilation, tq):
    tk = tq + 2 * A_HALF
    off = np.arange(tk)[None, :] - A_HALF - np.arange(tq)[:, None]
    in_band = np.abs(off) <= A_HALF
    bucket = _t5_bucket_np(np.clip(off, -A_HALF, A_HALF) * dilation)
    bias = rel_bias[jnp.asarray(bucket)].transpose(2, 0, 1).astype(F32)
    return jnp.where(jnp.asarray(in_band)[None], bias, NEG_INF)


def _dilated_kernel(q_ref, kp_ref, kc_ref, kn_ref, vp_ref, vc_ref, vn_ref, bias_ref,
                    o_ref, lse_ref, kbuf, vbuf, *, tq, seq):
    i = pl.program_id(1)
    tk = tq + 2 * A_HALF
    kbuf[0:A_HALF, :] = kp_ref[0]
    kbuf[A_HALF:A_HALF + tq, :] = kc_ref[0]
    kbuf[A_HALF + tq:, :] = kn_ref[0]
    vbuf[0:A_HALF, :] = vp_ref[0]
    vbuf[A_HALF:A_HALF + tq, :] = vc_ref[0]
    vbuf[A_HALF + tq:, :] = vn_ref[0]

    key_pos = i * tq - A_HALF + lax.broadcasted_iota(jnp.int32, (1, tk), 1)
    key_ok = (key_pos >= 0) & (key_pos < seq)
    lane = lax.broadcasted_iota(jnp.int32, (1, 128), 1)
    low = lane < HEAD_DIM
    lse_all = jnp.zeros((tq, 128), F32)
    for pair in range(A_HEADS // 2):
        cols = slice(pair * 128, (pair + 1) * 128)
        q2 = q_ref[0, :, cols]
        k2 = kbuf[:, cols]
        v2 = vbuf[:, cols]
        outs = []
        for half in range(2):
            h = 2 * pair + half
            qm = jnp.where(low if half == 0 else jnp.logical_not(low), q2, jnp.zeros_like(q2))
            s = _dot_nt(qm, k2) + bias_ref[h]
            s = jnp.where(key_ok, s, NEG_INF)
            m = jnp.max(s, axis=-1, keepdims=True)
            e = jnp.exp(s - m)
            l = jnp.sum(e, axis=-1, keepdims=True)
            outs.append(_dot(e.astype(BF16), v2) / l)
            lse_all = jnp.where(lane == h, m + jnp.log(l), lse_all)
        o_ref[0, :, cols] = jnp.where(low, outs[0], outs[1]).astype(BF16)
    lse_ref[0] = lse_all


def _dilated(qkv, rel_bias, dilation):
    b, n, _ = qkv.shape
    seq = n // dilation
    tq = min(256, seq)
    tk = tq + 2 * A_HALF
    w = 3 * A_WIDTH
    view = qkv.reshape(b, seq, dilation * w)
    bias = _band_bias(rel_bias, dilation, tq)
    per = tq // A_HALF
    last = seq // A_HALF - 1

    def cur(c):
        return pl.BlockSpec((1, tq, A_WIDTH), lambda bi, i, r: (bi, i, 3 * r + c))

    def prev(c):
        return pl.BlockSpec((1, A_HALF, A_WIDTH),
                            lambda bi, i, r: (bi, jnp.maximum(i * per - 1, 0), 3 * r + c))

    def nxt(c):
        return pl.BlockSpec((1, A_HALF, A_WIDTH),
                            lambda bi, i, r: (bi, jnp.minimum((i + 1) * per, last), 3 * r + c))

    o, lse = pl.pallas_call(
        functools.partial(_dilated_kernel, tq=tq, seq=seq),
        grid=(b, seq // tq, dilation),
        in_specs=[cur(0), prev(1), cur(1), nxt(1), prev(2), cur(2), nxt(2),
                  pl.BlockSpec((A_HEADS, tq, tk), lambda bi, i, r: (0, 0, 0))],
        out_specs=[pl.BlockSpec((1, tq, A_WIDTH), lambda bi, i, r: (bi, i, r)),
                   pl.BlockSpec((1, tq, 128), lambda bi, i, r: (bi, i, r))],
        out_shape=[jax.ShapeDtypeStruct((b, seq, dilation * A_WIDTH), BF16),
                   jax.ShapeDtypeStruct((b, seq, dilation * 128), F32)],
        scratch_shapes=[pltpu.VMEM((tk, A_WIDTH), BF16), pltpu.VMEM((tk, A_WIDTH), BF16)],
        compiler_params=_cparams("parallel", "parallel", "parallel"),
        name=f"dilated_d{dilation}",
    )(view, view, view, view, view, view, view, bias)
    return o.reshape(b, n, A_WIDTH), lse.reshape(b, n, 128)


def _mixture_kernel(o1_ref, o2_ref, o3_ref, l1_ref, l2_ref, l3_ref, out_ref):
    ls = [l1_ref[0], l2_ref[0], l3_ref[0]]
    mx = jnp.maximum(jnp.maximum(ls[0], ls[1]), ls[2])
    es = [jnp.exp(l - mx) for l in ls]
    inv = 1.0 / (es[0] + es[1] + es[2])
    ws = [e * inv for e in es]
    low = lax.broadcasted_iota(jnp.int32, (1, 128), 1) < HEAD_DIM
    o_refs = (o1_ref, o2_ref, o3_ref)
    for pair in range(A_HEADS // 2):
        cols = slice(pair * 128, (pair + 1) * 128)
        acc = None
        for g in range(3):
            wexp = jnp.where(low, ws[g][:, 2 * pair:2 * pair + 1], ws[g][:, 2 * pair + 1:2 * pair + 2])
            t = wexp * o_refs[g][0, :, cols].astype(F32)
            acc = t if acc is None else acc + t
        out_ref[0, :, cols] = acc.astype(BF16)


def _mixture(os_, ls_):
    b, n, _ = os_[0].shape
    ospec = pl.BlockSpec((1, TM, A_WIDTH), lambda i, j: (i, j, 0))
    lspec = pl.BlockSpec((1, TM, 128), lambda i, j: (i, j, 0))
    return pl.pallas_call(
        _mixture_kernel,
        grid=(b, n // TM),
        in_specs=[ospec] * 3 + [lspec] * 3,
        out_specs=ospec,
        out_shape=jax.ShapeDtypeStruct((b, n, A_WIDTH), BF16),
        compiler_params=_cparams("parallel", "parallel"),
        name="dilated_mixture",
    )(*os_, *ls_)


def _group_mean_matrix():
    g = np.kron(np.eye(4), np.full((64, 64), 1.0 / 64))
    return jnp.asarray(g, BF16)


def _fnet_pre_kernel(u_ref, gm_ref, gain_ref, ch_ref, cl_ref, sh_ref, sl_ref, yr_ref, yi_ref):
    u = u_ref[0]
    gm = gm_ref[...]
    uh, ul = _split(u)
    mean = _dot(uh, gm) + _dot(ul, gm)
    xc = u - mean
    sh, sl = _split(xc * xc)
    var = _dot(sh, gm) + _dot(sl, gm)
    un = xc * lax.rsqrt(var + LN_EPS) * gain_ref[...]
    nh, nl = _split(un)
    yr_ref[0] = _dot3(nh, nl, ch_ref[...], cl_ref[...])
    yi_ref[0] = -_dot3(nh, nl, sh_ref[...], sl_ref[...])


def _fnet_stage1_kernel(yr_ref, yi_ref, a1h_ref, a1l_ref, a2h_ref, a2l_ref, zr_ref, zi_ref):
    y = jnp.concatenate([yr_ref[0], yi_ref[0]], axis=0)
    yh, yl = _split(y)
    zr_ref[0] = _dot3(a1h_ref[...], a1l_ref[...], yh, yl)
    zi_ref[0] = _dot3(a2h_ref[...], a2l_ref[...], yh, yl)


def _fnet_stage2_kernel(zr_ref, zi_ref, tch_ref, tcl_ref, tsh_ref, tsl_ref, w_ref, o_ref, *, kc, scale):
    for kk in range(kc):
        rh, rl = _split(zr_ref[0, kk])
        ih, il = _split(zi_ref[0, kk])
        f = _dot3(tch_ref[kk], tcl_ref[kk], rh, rl) + _dot3(tsh_ref[kk], tsl_ref[kk], ih, il)
        f = f * scale
        o_ref[0, :, kk * B_WIDTH:(kk + 1) * B_WIDTH] = _dot(f.astype(BF16), w_ref[...]).astype(BF16)


def _fnet_tables(n):
    n2 = FFT_N2
    n1 = n // n2
    c = np.arange(64)
    ang = 2 * np.pi * np.outer(c, c) / 64
    cbd = np.kron(np.eye(4), np.cos(ang))
    sbd = np.kron(np.eye(4), np.sin(ang))
    k1 = np.arange(n1)
    ang1 = 2 * np.pi * np.outer(k1, k1) / n1
    c1, s1 = np.cos(ang1), np.sin(ang1)
    a1 = np.concatenate([c1, s1], axis=1)
    a2 = np.concatenate([-s1, c1], axis=1)
    k2 = np.arange(n2)
    npr = k1[:, None, None] + n1 * k2[None, :, None]
    prod = (npr * k2[None, None, :]) % n
    ang2 = 2 * np.pi * prod / n
    return (_np_split(cbd), _np_split(sbd), _np_split(a1), _np_split(a2),
            _np_split(np.cos(ang2)), _np_split(np.sin(ang2)))


def _fnet(u, fnet_g, fnet_w):
    b, n, _ = u.shape
    n2 = FFT_N2
    n1 = n // n2
    (ch, cl), (sh, sl), (a1h, a1l), (a2h, a2l), (tch, tcl), (tsh, tsl) = _fnet_tables(n)
    gain = fnet_g.reshape(1, B_WIDTH)
    wbd = jax.scipy.linalg.block_diag(*[fnet_w[g] for g in range(4)]).astype(BF16)

    tok = pl.BlockSpec((1, TM, B_WIDTH), lambda i, j: (i, j, 0))
    mat = pl.BlockSpec((B_WIDTH, B_WIDTH), lambda i, j: (0, 0))
    yr, yi = pl.pallas_call(
        _fnet_pre_kernel,
        grid=(b, n // TM),
        in_specs=[tok, mat, pl.BlockSpec((1, B_WIDTH), lambda i, j: (0, 0)), mat, mat, mat, mat],
        out_specs=[tok, tok],
        out_shape=[jax.ShapeDtypeStruct((b, n, B_WIDTH), F32)] * 2,
        compiler_params=_cparams("parallel", "parallel"),
        name="fnet_pre",
    )(u, _group_mean_matrix(), gain, ch, cl, sh, sl)

    cols = n2 * B_WIDTH
    tc = 2048
    yspec = pl.BlockSpec((1, n1, tc), lambda i, j: (i, 0, j))
    aspec = pl.BlockSpec((n1, 2 * n1), lambda i, j: (0, 0))
    zr, zi = pl.pallas_call(
        _fnet_stage1_kernel,
        grid=(b, cols // tc),
        in_specs=[yspec, yspec, aspec, aspec, aspec, aspec],
        out_specs=[yspec, yspec],
        out_shape=[jax.ShapeDtypeStruct((b, n1, cols), F32)] * 2,
        compiler_params=_cparams("parallel", "parallel"),
        name="fnet_stage1",
    )(yr.reshape(b, n1, cols), yi.reshape(b, n1, cols), a1h, a1l, a2h, a2l)

    kc = 8
    zspec = pl.BlockSpec((1, kc, n2, B_WIDTH), lambda i, j: (i, j, 0, 0))
    tspec = pl.BlockSpec((kc, n2, n2), lambda i, j: (j, 0, 0))
    out = pl.pallas_call(
        functools.partial(_fnet_stage2_kernel, kc=kc, scale=1.0 / math.sqrt(64.0 * n)),
        grid=(b, n1 // kc),
        in_specs=[zspec, zspec, tspec, tspec, tspec, tspec,
                  pl.BlockSpec((B_WIDTH, B_WIDTH), lambda i, j: (0, 0))],
        out_specs=pl.BlockSpec((1, n2, kc * B_WIDTH), lambda i, j: (i, 0, j)),
        out_shape=jax.ShapeDtypeStruct((b, n2, n1 * B_WIDTH), BF16),
        compiler_params=_cparams("parallel", "parallel"),
        name="fnet_stage2",
    )(zr.reshape(b, n1, n2, B_WIDTH), zi.reshape(b, n1, n2, B_WIDTH), tch, tcl, tsh, tsl, wbd)
    return out.reshape(b, n, B_WIDTH)


def _outproj_kernel(a_ref, c_ref, wa_ref, wc_ref, x_ref, g_ref, b_ref, o_ref, *, a_transposed):
    if a_transposed:
        h = _dot_tn(a_ref[0], wa_ref[...])
    else:
        h = _dot(a_ref[0], wa_ref[...])
    h = h + _dot(c_ref[0], wc_ref[...])
    o_ref[0] = _layer_norm(DN_ALPHA * x_ref[0] + h, g_ref[...], b_ref[...])


def _outproj(a, c, w, x, g, bias, a_transposed):
    b, n, _ = x.shape
    ka = w.shape[0] - c.shape[-1]
    wa = w[:ka].astype(BF16)
    wc = w[ka:].astype(BF16)
    if a_transposed:
        aspec = pl.BlockSpec((1, ka, TM), lambda i, j: (i, 0, j))
    else:
        aspec = pl.BlockSpec((1, TM, ka), lambda i, j: (i, j, 0))
    vec = pl.BlockSpec((1, D_MODEL), lambda i, j: (0, 0))
    xspec = pl.BlockSpec((1, TM, D_MODEL), lambda i, j: (i, j, 0))
    return pl.pallas_call(
        functools.partial(_outproj_kernel, a_transposed=a_transposed),
        grid=(b, n // TM),
        in_specs=[aspec, pl.BlockSpec((1, TM, c.shape[-1]), lambda i, j: (i, j, 0)),
                  pl.BlockSpec(wa.shape, lambda i, j: (0, 0)),
                  pl.BlockSpec(wc.shape, lambda i, j: (0, 0)), xspec, vec, vec],
        out_specs=xspec,
        out_shape=jax.ShapeDtypeStruct((b, n, D_MODEL), F32),
        compiler_params=_cparams("parallel", "parallel"),
        name="outproj_ln",
    )(a, c, wa, wc, x, g.reshape(1, D_MODEL), bias.reshape(1, D_MODEL))


def _mem_kv_kernel(mem_ref, wkt_ref, wv_ref, kt_ref, v_ref):
    m = mem_ref[0].astype(BF16)
    kt_ref[0] = _dot_nt(wkt_ref[...], m).astype(BF16)
    v_ref[0] = _dot(m, wv_ref[...]).astype(BF16)


def _mem_kv(mem, w_kv):
    b, m, _ = mem.shape
    wkt = w_kv[:, :D_MODEL].T.astype(BF16)
    wv = w_kv[:, D_MODEL:].astype(BF16)
    wspec = pl.BlockSpec((D_MODEL, D_MODEL), lambda i: (0, 0))
    return pl.pallas_call(
        _mem_kv_kernel,
        grid=(b,),
        in_specs=[pl.BlockSpec((1, m, D_MODEL), lambda i: (i, 0, 0)), wspec, wspec],
        out_specs=[pl.BlockSpec((1, D_MODEL, m), lambda i: (i, 0, 0)),
                   pl.BlockSpec((1, m, D_MODEL), lambda i: (i, 0, 0))],
        out_shape=[jax.ShapeDtypeStruct((b, D_MODEL, m), BF16),
                   jax.ShapeDtypeStruct((b, m, D_MODEL), BF16)],
        compiler_params=_cparams("parallel"),
        name="mem_kv",
    )(mem, wkt, wv)


def _xattn_kernel(x_ref, wq_ref, kt_ref, v_ref, wo_ref, g_ref, b_ref, o_ref):
    x = x_ref[0]
    xb = x.astype(BF16)
    acc = None
    for h in range(XA_HEADS):
        cols = slice(h * XA_HEAD_DIM, (h + 1) * XA_HEAD_DIM)
        q = _dot(xb, wq_ref[:, cols]).astype(BF16)
        s = _dot(q, kt_ref[0, cols, :])
        m = jnp.max(s, axis=-1, keepdims=True)
        e = jnp.exp(s - m)
        l = jnp.sum(e, axis=-1, keepdims=True)
        o = (_dot(e.astype(BF16), v_ref[0, :, cols]) / l).astype(BF16)
        t = _dot(o, wo_ref[cols, :])
        acc = t if acc is None else acc + t
    o_ref[0] = _layer_norm(DN_ALPHA * x + acc, g_ref[...], b_ref[...])


def _xattn(x, mem, w_q, w_kv, w_o, g, bias):
    b, n, _ = x.shape
    m = mem.shape[1]
    kt, v = _mem_kv(mem, w_kv)
    wq = (w_q * (XA_HEAD_DIM ** -0.5)).astype(BF16)
    wo = w_o.astype(BF16)
    xspec = pl.BlockSpec((1, TM, D_MODEL), lambda i, j: (i, j, 0))
    wspec = pl.BlockSpec((D_MODEL, D_MODEL), lambda i, j: (0, 0))
    vec = pl.BlockSpec((1, D_MODEL), lambda i, j: (0, 0))
    return pl.pallas_call(
        _xattn_kernel,
        grid=(b, n // TM),
        in_specs=[xspec, wspec, pl.BlockSpec((1, D_MODEL, m), lambda i, j: (i, 0, 0)),
                  pl.BlockSpec((1, m, D_MODEL), lambda i, j: (i, 0, 0)), wspec, vec, vec],
        out_specs=xspec,
        out_shape=jax.ShapeDtypeStruct((b, n, D_MODEL), F32),
        compiler_params=_cparams("parallel", "parallel"),
        name="xattn_ln",
    )(x, wq, kt, v, wo, g.reshape(1, D_MODEL), bias.reshape(1, D_MODEL))


FFN_TH = 1408


def _swiglu_kernel(x_ref, wg_ref, wu_ref, wo_ref, g_ref, b_ref, o_ref, acc_ref):
    j = pl.program_id(2)
    xb = x_ref[0].astype(BF16)
    gate = _dot(xb, wg_ref[...])
    up = _dot(xb, wu_ref[...])
    hid = (gate * (1.0 / (1.0 + jnp.exp(-gate))) * up).astype(BF16)
    part = _dot(hid, wo_ref[...])

    @pl.when(j == 0)
    def _():
        acc_ref[...] = part

    @pl.when(j > 0)
    def _():
        acc_ref[...] += part

    @pl.when(j == pl.num_programs(2) - 1)
    def _():
        o_ref[0] = _layer_norm(DN_ALPHA * x_ref[0] + acc_ref[...], g_ref[...], b_ref[...])


def _swiglu(x, w_in, w_out, g, bias):
    b, n, _ = x.shape
    nj = FFN_HIDDEN // FFN_TH
    win = w_in.astype(BF16)
    wout = w_out.astype(BF16)
    xspec = pl.BlockSpec((1, TM, D_MODEL), lambda i, t, j: (i, t, 0))
    vec = pl.BlockSpec((1, D_MODEL), lambda i, t, j: (0, 0))
    return pl.pallas_call(
        _swiglu_kernel,
        grid=(b, n // TM, nj),
        in_specs=[xspec,
                  pl.BlockSpec((D_MODEL, FFN_TH), lambda i, t, j: (0, j)),
                  pl.BlockSpec((D_MODEL, FFN_TH), lambda i, t, j: (0, j + nj)),
                  pl.BlockSpec((FFN_TH, D_MODEL), lambda i, t, j: (j, 0)), vec, vec],
        out_specs=xspec,
        out_shape=jax.ShapeDtypeStruct((b, n, D_MODEL), F32),
        scratch_shapes=[pltpu.VMEM((TM, D_MODEL), F32)],
        compiler_params=_cparams("parallel", "parallel", "arbitrary"),
        name="swiglu_ln",
    )(x, win, win, wout, g.reshape(1, D_MODEL), bias.reshape(1, D_MODEL))


N_QK_HEADS = C_Q_HEADS + C_KV_HEADS
QK_ROWS = N_QK_HEADS * HEAD_DIM


def _proj_cd_kernel(x_ref, wt_ref, wu_ref, gain_ref, cos_ref, sin_ref,
                    qt_ref, k_ref, vt_ref, u_ref, *, tm):
    xb = x_ref[0].astype(BF16)
    u_ref[0] = _dot(xb, wu_ref[...])
    zt = _dot_nt(wt_ref[...], xb)
    z = zt[:QK_ROWS].reshape(N_QK_HEADS, HEAD_DIM, tm)
    ssq = jnp.sum(z * z, axis=1, keepdims=True)
    zn = z * lax.rsqrt(ssq * (1.0 / HEAD_DIM) + RMS_EPS) * gain_ref[...]
    half = HEAD_DIM // 2
    x1 = zn[:, :half]
    x2 = zn[:, half:]
    c = cos_ref[...][None]
    s = sin_ref[...][None]
    rot = jnp.concatenate([x1 * c - x2 * s, x1 * s + x2 * c], axis=1).reshape(QK_ROWS, tm)
    qt_ref[0] = rot[:C_WIDTH].astype(BF16)
    k_ref[0] = rot[C_WIDTH:].T.astype(BF16)
    vt = zt[QK_ROWS:].astype(BF16)
    for c0 in range(tm // GQA_TKC):
        vt_ref[0, c0] = vt[:, c0 * GQA_TKC:(c0 + 1) * GQA_TKC]


def _rope_tables_t(n):
    rows = n // GRID_W
    row_id = jnp.broadcast_to(jnp.arange(rows)[:, None], (rows, GRID_W)).reshape(n)
    col_id = jnp.broadcast_to(jnp.arange(GRID_W)[None, :], (rows, GRID_W)).reshape(n)
    axis_dim = HEAD_DIM // 2
    freqs = ROPE_THETA ** (-jnp.arange(0, axis_dim, 2, dtype=F32) / axis_dim)
    ang = jnp.concatenate([row_id[:, None] * freqs, col_id[:, None] * freqs], axis=-1)
    return jnp.cos(ang).T, jnp.sin(ang).T


def _proj_cd(x, w_in, q_norm, k_norm):
    b, n, _ = x.shape
    tm = TM
    perm = np.concatenate([np.arange(0, HEAD_DIM, 2), np.arange(1, HEAD_DIM, 2)])
    head_perm = (np.arange(N_QK_HEADS)[:, None] * HEAD_DIM + perm[None, :]).reshape(-1)
    rows = np.concatenate([head_perm, np.arange(QK_ROWS, QK_ROWS + C_KV_WIDTH)])
    wt = w_in[:, :QK_ROWS + C_KV_WIDTH].T[rows].astype(BF16)
    wu = w_in[:, QK_ROWS + C_KV_WIDTH:].astype(BF16)
    qg = q_norm[perm] * (HEAD_DIM ** -0.5 * LOG2E)
    kg = k_norm[perm]
    gain = jnp.concatenate([jnp.tile(qg[None], (C_Q_HEADS, 1)), jnp.tile(kg[None], (C_KV_HEADS, 1))])
    gain = gain.reshape(N_QK_HEADS, HEAD_DIM, 1).astype(F32)
    cos_t, sin_t = _rope_tables_t(n)
    nc = n // GQA_TKC
    return pl.pallas_call(
        functools.partial(_proj_cd_kernel, tm=tm),
        grid=(b, n // tm),
        in_specs=[pl.BlockSpec((1, tm, D_MODEL), lambda i, j: (i, j, 0)),
                  pl.BlockSpec(wt.shape, lambda i, j: (0, 0)),
                  pl.BlockSpec(wu.shape, lambda i, j: (0, 0)),
                  pl.BlockSpec(gain.shape, lambda i, j: (0, 0, 0)),
                  pl.BlockSpec((HEAD_DIM // 2, tm), lambda i, j: (0, j)),
                  pl.BlockSpec((HEAD_DIM // 2, tm), lambda i, j: (0, j))],
        out_specs=[pl.BlockSpec((1, C_WIDTH, tm), lambda i, j: (i, 0, j)),
                   pl.BlockSpec((1, tm, C_KV_WIDTH), lambda i, j: (i, j, 0)),
                   pl.BlockSpec((1, tm // GQA_TKC, C_KV_WIDTH, GQA_TKC), lambda i, j: (i, j, 0, 0)),
                   pl.BlockSpec((1, tm, D_WIDTH), lambda i, j: (i, j, 0))],
        out_shape=[jax.ShapeDtypeStruct((b, C_WIDTH, n), BF16),
                   jax.ShapeDtypeStruct((b, n, C_KV_WIDTH), BF16),
                   jax.ShapeDtypeStruct((b, nc, C_KV_WIDTH, GQA_TKC), BF16),
                   jax.ShapeDtypeStruct((b, n, D_WIDTH), F32)],
        compiler_params=_cparams("parallel", "parallel"),
        name="proj_cd",
    )(x, wt, wu, gain, cos_t, sin_t)


def _gqa_kernel(qt_ref, k_ref, vt_ref, ot_ref, m_ref, l_ref, acc_ref, qpad_ref, *, tq, n):
    g = pl.program_id(1)
    row_half = lax.broadcasted_iota(jnp.int32, (128, 1), 0) // HEAD_DIM
    mine = row_half == (g % 2)
    for j in range(C_REP):
        qj = qt_ref[0, j * HEAD_DIM:(j + 1) * HEAD_DIM, :]
        q2 = jnp.concatenate([qj, qj], axis=0)
        qpad_ref[j] = jnp.where(mine, q2, jnp.zeros_like(q2))
    m_ref[...] = jnp.full(m_ref.shape, NEG_INF, F32)
    l_ref[...] = jnp.zeros(l_ref.shape, F32)
    acc_ref[...] = jnp.zeros(acc_ref.shape, F32)

    def body(c, carry):
        start = pl.multiple_of(c * GQA_TKC, GQA_TKC)
        kch = k_ref[0, pl.ds(start, GQA_TKC), :]
        vch = vt_ref[0, c]
        for j in range(C_REP):
            st = _dot(kch, qpad_ref[j])
            m_old = m_ref[j]
            m_new = jnp.maximum(m_old, jnp.max(st, axis=0, keepdims=True))
            alpha = jnp.exp2(m_old - m_new)
            p = jnp.exp2(st - m_new)
            l_ref[j] = alpha * l_ref[j] + jnp.sum(p, axis=0, keepdims=True)
            acc_ref[j] = alpha * acc_ref[j] + _dot(vch, p.astype(BF16))
            m_ref[j] = m_new
        return carry

    lax.fori_loop(0, n // GQA_TKC, body, 0)
    for j in range(C_REP):
        ot_ref[0, j * HEAD_DIM:(j + 1) * HEAD_DIM, :] = (acc_ref[j] / l_ref[j]).astype(BF16)


def _gqa(qt, k, vt):
    b, _, n = qt.shape
    tq = min(GQA_TQ, n)
    nc = n // GQA_TKC
    rows = C_REP * HEAD_DIM
    return pl.pallas_call(
        functools.partial(_gqa_kernel, tq=tq, n=n),
        grid=(b, C_KV_HEADS, n // tq),
        in_specs=[pl.BlockSpec((1, rows, tq), lambda i, g, t: (i, g, t)),
                  pl.BlockSpec((1, n, 128), lambda i, g, t: (i, 0, g // 2)),
                  pl.BlockSpec((1, nc, HEAD_DIM, GQA_TKC), lambda i, g, t: (i, 0, g, 0))],
        out_specs=pl.BlockSpec((1, rows, tq), lambda i, g, t: (i, g, t)),
        out_shape=jax.ShapeDtypeStruct((b, C_WIDTH, n), BF16),
        scratch_shapes=[pltpu.VMEM((C_REP, 1, tq), F32), pltpu.VMEM((C_REP, 1, tq), F32),
                        pltpu.VMEM((C_REP, HEAD_DIM, tq), F32), pltpu.VMEM((C_REP, 128, tq), BF16)],
        compiler_params=_cparams("parallel", "parallel", "parallel"),
        name="gqa_flash",
    )(qt, k, vt)


def _pool_kernel(up_ref, uc_ref, un_ref, w_ref, scale_ref, o_ref, buf, *, tm, n):
    i = pl.program_id(1)
    cur = uc_ref[0]
    buf[0:POOL_HALO, :] = jnp.where(i > 0, up_ref[0], 0.0)
    buf[POOL_HALO:POOL_HALO + tm, :] = cur
    buf[POOL_HALO + tm:, :] = jnp.where(i < pl.num_programs(1) - 1, un_ref[0], 0.0)
    lane_group = lax.broadcasted_iota(jnp.int32, (1, D_WIDTH), 1) // 64
    half_w = jnp.left_shift(1, lane_group)
    acc = jnp.zeros((tm, D_WIDTH), F32)
    for j in range(-POOL_HALO, POOL_HALO):
        inside = (j >= -half_w) & (j < half_w)
        acc = acc + jnp.where(inside, buf[POOL_HALO + j:POOL_HALO + j + tm, :], 0.0)
    t = i * tm + lax.broadcasted_iota(jnp.int32, (tm, 1), 0)
    cnt = jnp.minimum(t + half_w, n) - jnp.maximum(t - half_w, 0)
    mixed = (acc / cnt.astype(F32) - cur).astype(BF16)
    o_ref[0] = (_dot(mixed, w_ref[...]) * scale_ref[...]).astype(BF16)


def _pool(u, pool_w, pool_scale):
    b, n, _ = u.shape
    tm = TM
    per = tm // POOL_HALO
    last = n // POOL_HALO - 1
    wbd = jax.scipy.linalg.block_diag(*[pool_w[g] for g in range(4)]).astype(BF16)
    return pl.pallas_call(
        functools.partial(_pool_kernel, tm=tm, n=n),
        grid=(b, n // tm),
        in_specs=[pl.BlockSpec((1, POOL_HALO, D_WIDTH), lambda i, j: (i, jnp.maximum(j * per - 1, 0), 0)),
                  pl.BlockSpec((1, tm, D_WIDTH), lambda i, j: (i, j, 0)),
                  pl.BlockSpec((1, POOL_HALO, D_WIDTH), lambda i, j: (i, jnp.minimum((j + 1) * per, last), 0)),
                  pl.BlockSpec((D_WIDTH, D_WIDTH), lambda i, j: (0, 0)),
                  pl.BlockSpec((1, D_WIDTH), lambda i, j: (0, 0))],
        out_specs=pl.BlockSpec((1, tm, D_WIDTH), lambda i, j: (i, j, 0)),
        out_shape=jax.ShapeDtypeStruct((b, n, D_WIDTH), BF16),
        scratch_shapes=[pltpu.VMEM((tm + 2 * POOL_HALO, D_WIDTH), F32)],
        compiler_params=_cparams("parallel", "parallel"),
        name="pool",
    )(u, u, u, wbd, pool_scale.reshape(1, D_WIDTH))


def _trunk(x, mem, rel_bias, ab_w_in, ab_fnet_g, ab_fnet_w, ab_w_out,
           cd_w_in, cd_q_norm, cd_k_norm, cd_pool_w, cd_pool_scale, cd_w_out,
           xa_w_q, xa_w_kv, xa_w_o, ffn_w_in, ffn_w_out, ln_g, ln_b):
    for layer in range(DEPTH):
        i = layer // 2
        if layer % 2 == 0:
            w_in = ab_w_in[i]
            w_in = jnp.concatenate([w_in[:, :A_WIDTH] * (HEAD_DIM ** -0.5), w_in[:, A_WIDTH:]], axis=1)
            qkv, u = _proj_ab(x, w_in.astype(BF16))
            outs = [_dilated(qkv, rel_bias, d) for _, d in A_PATTERNS]
            o_a = _mixture([o for o, _ in outs], [l for _, l in outs])
            o_b = _fnet(u, ab_fnet_g[i], ab_fnet_w[i])
            x = _outproj(o_a, o_b, ab_w_out[i], x, ln_g[layer, 0], ln_b[layer, 0], a_transposed=False)
        else:
            qt, k, vt, u = _proj_cd(x, cd_w_in[i], cd_q_norm[i], cd_k_norm[i])
            o_c = _gqa(qt, k, vt)
            o_d = _pool(u, cd_pool_w[i], cd_pool_scale[i])
            x = _outproj(o_c, o_d, cd_w_out[i], x, ln_g[layer, 0], ln_b[layer, 0], a_transposed=True)
        x = _xattn(x, mem, xa_w_q[layer], xa_w_kv[layer], xa_w_o[layer], ln_g[layer, 1], ln_b[layer, 1])
        x = _swiglu(x, ffn_w_in[layer], ffn_w_out[layer], ln_g[layer, 2], ln_b[layer, 2])
    return x


def kernel(x_prompt, x_sample, mem_prompt, mem_sample, rel_bias, ab_w_in, ab_fnet_g, ab_fnet_w, ab_w_out, cd_w_in, cd_q_norm, cd_k_norm, cd_pool_w, cd_pool_scale, cd_w_out, xa_w_q, xa_w_kv, xa_w_o, ffn_w_in, ffn_w_out, ln_g, ln_b):
    params = (rel_bias, ab_w_in, ab_fnet_g, ab_fnet_w, ab_w_out,
              cd_w_in, cd_q_norm, cd_k_norm, cd_pool_w, cd_pool_scale, cd_w_out,
              xa_w_q, xa_w_kv, xa_w_o, ffn_w_in, ffn_w_out, ln_g, ln_b)
    return (_trunk(x_prompt, mem_prompt, *params), _trunk(x_sample, mem_sample, *params))
```

```python
import functools
import math

import numpy as np
import jax
import jax.numpy as jnp
from jax import lax
from jax.experimental import pallas as pl
from jax.experimental.pallas import tpu as pltpu

F32 = jnp.float32
BF16 = jnp.bfloat16

D_MODEL = 1024
HEAD_DIM = 64
GRID_W = 64
LN_EPS = 1e-5
RMS_EPS = 1e-6
NEG_INF = -1e30
DEPTH = 2
A_HEADS = 12
A_WIDTH = A_HEADS * HEAD_DIM
A_PATTERNS = ((128, 1), (512, 4), (2048, 16))
A_HALF = 64
N_BUCKETS = 32
REL_MAX_DIST = 1024
B_WIDTH = 256
C_Q_HEADS = 12
C_KV_HEADS = 4
C_REP = C_Q_HEADS // C_KV_HEADS
C_WIDTH = C_Q_HEADS * HEAD_DIM
C_KV_WIDTH = C_KV_HEADS * HEAD_DIM
ROPE_THETA = 10000.0
POOL_WINDOWS = (2, 4, 8, 16)
POOL_HALO = 8
D_WIDTH = 256
XA_HEADS = 4
XA_HEAD_DIM = D_MODEL // XA_HEADS
FFN_HIDDEN = 2816
DN_ALPHA = (2 * DEPTH) ** 0.25
LOG2E = 1.4426950408889634

VMEM_LIMIT = 56 * 1024 * 1024
TM = 512
FFT_N2 = 128
GQA_TQ = 1024
GQA_TW = 256
GQA_TKC = 256
GQA_VROWS = HEAD_DIM + 16
GQA_UNROLL = 4
GQA_AHEAD = 5


def _cparams(*sem):
    return pltpu.CompilerParams(dimension_semantics=sem, vmem_limit_bytes=VMEM_LIMIT)


def _dot(a, b):
    return jnp.dot(a, b, preferred_element_type=F32)


def _dot_nt(a, b):
    return lax.dot_general(a, b, (((1,), (1,)), ((), ())), preferred_element_type=F32)


def _dot_tn(a, b):
    return lax.dot_general(a, b, (((0,), (0,)), ((), ())), preferred_element_type=F32)


def _split(x):
    hi = x.astype(BF16)
    lo = (x - hi.astype(F32)).astype(BF16)
    return hi, lo


def _dot3(ah, al, bh, bl):
    return _dot(ah, bh) + _dot(al, bh) + _dot(ah, bl)


def _np_split(x):
    x = np.asarray(x, np.float64)
    hi = jnp.asarray(x, F32).astype(BF16)
    lo = (jnp.asarray(x, F32) - hi.astype(F32)).astype(BF16)
    return hi, lo


def _layer_norm(h, g, b):
    mu = jnp.mean(h, axis=-1, keepdims=True)
    xc = h - mu
    var = jnp.mean(xc * xc, axis=-1, keepdims=True)
    return xc * lax.rsqrt(var + LN_EPS) * g + b


def _proj_ab_kernel(x_ref, w_ref, qkv_ref, u_ref):
    xb = x_ref[0].astype(BF16)
    for c in range(0, 3 * A_WIDTH, 256):
        qkv_ref[0, :, c:c + 256] = _dot(xb, w_ref[:, c:c + 256]).astype(BF16)
    u_ref[0] = _dot(xb, w_ref[:, 3 * A_WIDTH:])


def _proj_ab(x, w):
    b, n, _ = x.shape
    return pl.pallas_call(
        _proj_ab_kernel,
        grid=(b, n // TM),
        in_specs=[pl.BlockSpec((1, TM, D_MODEL), lambda i, j: (i, j, 0)),
                  pl.BlockSpec(w.shape, lambda i, j: (0, 0))],
        out_specs=[pl.BlockSpec((1, TM, 3 * A_WIDTH), lambda i, j: (i, j, 0)),
                   pl.BlockSpec((1, TM, B_WIDTH), lambda i, j: (i, j, 0))],
        out_shape=[jax.ShapeDtypeStruct((b, n, 3 * A_WIDTH), BF16),
                   jax.ShapeDtypeStruct((b, n, B_WIDTH), F32)],
        compiler_params=_cparams("parallel", "parallel"),
        name="proj_ab",
    )(x, w)


def _t5_bucket_np(rel):
    nb = N_BUCKETS // 2
    max_exact = nb // 2
    ret = np.where(rel > 0, nb, 0)
    n = np.abs(rel)
    nf = np.maximum(n, 1).astype(np.float32)
    large = max_exact + (np.log(nf / max_exact) / math.log(REL_MAX_DIST / max_exact)
                         * (nb - max_exact)).astype(np.int32)
    large = np.minimum(large, nb - 1)
    return ret + np.where(n < max_exact, n, large)


def _band_bias(rel_bias, dilation, tq):
    tk = tq + 2 * A_HALF
    off = np.arange(tk)[None, :] - A_HALF - np.arange(tq)[:, None]
    in_band = np.abs(off) <= A_HALF
    bucket = _t5_bucket_np(np.clip(off, -A_HALF, A_HALF) * dilation)
    bias = rel_bias[jnp.asarray(bucket)].transpose(2, 0, 1).astype(F32)
    return jnp.where(jnp.asarray(in_band)[None], bias, NEG_INF)


def _dilated_kernel(q_ref, kp_ref, kc_ref, kn_ref, vp_ref, vc_ref, vn_ref, bias_ref,
                    o_ref, lse_ref, kbuf, vbuf, *, tq, seq):
    i = pl.program_id(1)
    tk = tq + 2 * A_HALF
    kbuf[0:A_HALF, :] = kp_ref[0]
    kbuf[A_HALF:A_HALF + tq, :] = kc_ref[0]
    kbuf[A_HALF + tq:, :] = kn_ref[0]
    vbuf[0:A_HALF, :] = vp_ref[0]
    vbuf[A_HALF:A_HALF + tq, :] = vc_ref[0]
    vbuf[A_HALF + tq:, :] = vn_ref[0]

    key_pos = i * tq - A_HALF + lax.broadcasted_iota(jnp.int32, (1, tk), 1)
    key_ok = (key_pos >= 0) & (key_pos < seq)
    lane = lax.broadcasted_iota(jnp.int32, (1, 128), 1)
    low = lane < HEAD_DIM
    lse_all = jnp.zeros((tq, 128), F32)
    for pair in range(A_HEADS // 2):
        cols = slice(pair * 128, (pair + 1) * 128)
        q2 = q_ref[0, :, cols]
        k2 = kbuf[:, cols]
        v2 = vbuf[:, cols]
        outs = []
        for half in range(2):
            h = 2 * pair + half
            qm = jnp.where(low if half == 0 else jnp.logical_not(low), q2, jnp.zeros_like(q2))
            s = _dot_nt(qm, k2) + bias_ref[h]
            s = jnp.where(key_ok, s, NEG_INF)
            m = jnp.max(s, axis=-1, keepdims=True)
            e = jnp.exp(s - m)
            l = jnp.sum(e, axis=-1, keepdims=True)
            outs.append(_dot(e.astype(BF16), v2) / l)
            lse_all = jnp.where(lane == h, m + jnp.log(l), lse_all)
        o_ref[0, :, cols] = jnp.where(low, outs[0], outs[1]).astype(BF16)
    lse_ref[0] = lse_all


def _dilated(qkv, rel_bias, dilation):
    b, n, _ = qkv.shape
    seq = n // dilation
    tq = min(256, seq)
    tk = tq + 2 * A_HALF
    w = 3 * A_WIDTH
    view = qkv.reshape(b, seq, dilation * w)
    bias = _band_bias(rel_bias, dilation, tq)
    per = tq // A_HALF
    last = seq // A_HALF - 1

    def cur(c):
        return pl.BlockSpec((1, tq, A_WIDTH), lambda bi, i, r: (bi, i, 3 * r + c))

    def prev(c):
        return pl.BlockSpec((1, A_HALF, A_WIDTH),
                            lambda bi, i, r: (bi, jnp.maximum(i * per - 1, 0), 3 * r + c))

    def nxt(c):
        return pl.BlockSpec((1, A_HALF, A_WIDTH),
                            lambda bi, i, r: (bi, jnp.minimum((i + 1) * per, last), 3 * r + c))

    o, lse = pl.pallas_call(
        functools.partial(_dilated_kernel, tq=tq, seq=seq),
        grid=(b, seq // tq, dilation),
        in_specs=[cur(0), prev(1), cur(1), nxt(1), prev(2), cur(2), nxt(2),
                  pl.BlockSpec((A_HEADS, tq, tk), lambda bi, i, r: (0, 0, 0))],
        out_specs=[pl.BlockSpec((1, tq, A_WIDTH), lambda bi, i, r: (bi, i, r)),
                   pl.BlockSpec((1, tq, 128), lambda bi, i, r: (bi, i, r))],
        out_shape=[jax.ShapeDtypeStruct((b, seq, dilation * A_WIDTH), BF16),
                   jax.ShapeDtypeStruct((b, seq, dilation * 128), F32)],
        scratch_shapes=[pltpu.VMEM((tk, A_WIDTH), BF16), pltpu.VMEM((tk, A_WIDTH), BF16)],
        compiler_params=_cparams("parallel", "parallel", "parallel"),
        name=f"dilated_d{dilation}",
    )(view, view, view, view, view, view, view, bias)
    return o.reshape(b, n, A_WIDTH), lse.reshape(b, n, 128)


def _mixture_kernel(o1_ref, o2_ref, o3_ref, l1_ref, l2_ref, l3_ref, out_ref):
    ls = [l1_ref[0], l2_ref[0], l3_ref[0]]
    mx = jnp.maximum(jnp.maximum(ls[0], ls[1]), ls[2])
    es = [jnp.exp(l - mx) for l in ls]
    inv = 1.0 / (es[0] + es[1] + es[2])
    ws = [e * inv for e in es]
    low = lax.broadcasted_iota(jnp.int32, (1, 128), 1) < HEAD_DIM
    o_refs = (o1_ref, o2_ref, o3_ref)
    for pair in range(A_HEADS // 2):
        cols = slice(pair * 128, (pair + 1) * 128)
        acc = None
        for g in range(3):
            wexp = jnp.where(low, ws[g][:, 2 * pair:2 * pair + 1], ws[g][:, 2 * pair + 1:2 * pair + 2])
            t = wexp * o_refs[g][0, :, cols].astype(F32)
            acc = t if acc is None else acc + t
        out_ref[0, :, cols] = acc.astype(BF16)


def _mixture(os_, ls_):
    b, n, _ = os_[0].shape
    ospec = pl.BlockSpec((1, TM, A_WIDTH), lambda i, j: (i, j, 0))
    lspec = pl.BlockSpec((1, TM, 128), lambda i, j: (i, j, 0))
    return pl.pallas_call(
        _mixture_kernel,
        grid=(b, n // TM),
        in_specs=[ospec] * 3 + [lspec] * 3,
        out_specs=ospec,
        out_shape=jax.ShapeDtypeStruct((b, n, A_WIDTH), BF16),
        compiler_params=_cparams("parallel", "parallel"),
        name="dilated_mixture",
    )(*os_, *ls_)


def _group_mean_matrix():
    g = np.kron(np.eye(4), np.full((64, 64), 1.0 / 64))
    return jnp.asarray(g, BF16)


def _fnet_pre_kernel(u_ref, gm_ref, gain_ref, ch_ref, cl_ref, sh_ref, sl_ref, yr_ref, yi_ref):
    u = u_ref[0]
    gm = gm_ref[...]
    uh, ul = _split(u)
    mean = _dot(uh, gm) + _dot(ul, gm)
    xc = u - mean
    sh, sl = _split(xc * xc)
    var = _dot(sh, gm) + _dot(sl, gm)
    un = xc * lax.rsqrt(var + LN_EPS) * gain_ref[...]
    nh, nl = _split(un)
    yr_ref[0] = _dot3(nh, nl, ch_ref[...], cl_ref[...])
    yi_ref[0] = -_dot3(nh, nl, sh_ref[...], sl_ref[...])


def _fnet_stage1_kernel(yr_ref, yi_ref, a1h_ref, a1l_ref, a2h_ref, a2l_ref, zr_ref, zi_ref):
    y = jnp.concatenate([yr_ref[0], yi_ref[0]], axis=0)
    yh, yl = _split(y)
    zr_ref[0] = _dot3(a1h_ref[...], a1l_ref[...], yh, yl)
    zi_ref[0] = _dot3(a2h_ref[...], a2l_ref[...], yh, yl)


def _fnet_stage2_kernel(zr_ref, zi_ref, tch_ref, tcl_ref, tsh_ref, tsl_ref, w_ref, o_ref, *, kc, scale):
    for kk in range(kc):
        rh, rl = _split(zr_ref[0, kk])
        ih, il = _split(zi_ref[0, kk])
        f = _dot3(tch_ref[kk], tcl_ref[kk], rh, rl) + _dot3(tsh_ref[kk], tsl_ref[kk], ih, il)
        f = f * scale
        o_ref[0, :, kk * B_WIDTH:(kk + 1) * B_WIDTH] = _dot(f.astype(BF16), w_ref[...]).astype(BF16)


def _fnet_tables(n):
    n2 = FFT_N2
    n1 = n // n2
    c = np.arange(64)
    ang = 2 * np.pi * np.outer(c, c) / 64
    cbd = np.kron(np.eye(4), np.cos(ang))
    sbd = np.kron(np.eye(4), np.sin(ang))
    k1 = np.arange(n1)
    ang1 = 2 * np.pi * np.outer(k1, k1) / n1
    c1, s1 = np.cos(ang1), np.sin(ang1)
    a1 = np.concatenate([c1, s1], axis=1)
    a2 = np.concatenate([-s1, c1], axis=1)
    k2 = np.arange(n2)
    npr = k1[:, None, None] + n1 * k2[None, :, None]
    prod = (npr * k2[None, None, :]) % n
    ang2 = 2 * np.pi * prod / n
    return (_np_split(cbd), _np_split(sbd), _np_split(a1), _np_split(a2),
            _np_split(np.cos(ang2)), _np_split(np.sin(ang2)))


def _fnet(u, fnet_g, fnet_w):
    b, n, _ = u.shape
    n2 = FFT_N2
    n1 = n // n2
    (ch, cl), (sh, sl), (a1h, a1l), (a2h, a2l), (tch, tcl), (tsh, tsl) = _fnet_tables(n)
    gain = fnet_g.reshape(1, B_WIDTH)
    wbd = jax.scipy.linalg.block_diag(*[fnet_w[g] for g in range(4)]).astype(BF16)

    tok = pl.BlockSpec((1, TM, B_WIDTH), lambda i, j: (i, j, 0))
    mat = pl.BlockSpec((B_WIDTH, B_WIDTH), lambda i, j: (0, 0))
    yr, yi = pl.pallas_call(
        _fnet_pre_kernel,
        grid=(b, n // TM),
        in_specs=[tok, mat, pl.BlockSpec((1, B_WIDTH), lambda i, j: (0, 0)), mat, mat, mat, mat],
        out_specs=[tok, tok],
        out_shape=[jax.ShapeDtypeStruct((b, n, B_WIDTH), F32)] * 2,
        compiler_params=_cparams("parallel", "parallel"),
        name="fnet_pre",
    )(u, _group_mean_matrix(), gain, ch, cl, sh, sl)

    cols = n2 * B_WIDTH
    tc = 2048
    yspec = pl.BlockSpec((1, n1, tc), lambda i, j: (i, 0, j))
    aspec = pl.BlockSpec((n1, 2 * n1), lambda i, j: (0, 0))
    zr, zi = pl.pallas_call(
        _fnet_stage1_kernel,
        grid=(b, cols // tc),
        in_specs=[yspec, yspec, aspec, aspec, aspec, aspec],
        out_specs=[yspec, yspec],
        out_shape=[jax.ShapeDtypeStruct((b, n1, cols), F32)] * 2,
        compiler_params=_cparams("parallel", "parallel"),
        name="fnet_stage1",
    )(yr.reshape(b, n1, cols), yi.reshape(b, n1, cols), a1h, a1l, a2h, a2l)

    kc = 8
    zspec = pl.BlockSpec((1, kc, n2, B_WIDTH), lambda i, j: (i, j, 0, 0))
    tspec = pl.BlockSpec((kc, n2, n2), lambda i, j: (j, 0, 0))
    out = pl.pallas_call(
        functools.partial(_fnet_stage2_kernel, kc=kc, scale=1.0 / math.sqrt(64.0 * n)),
        grid=(b, n1 // kc),
        in_specs=[zspec, zspec, tspec, tspec, tspec, tspec,
                  pl.BlockSpec((B_WIDTH, B_WIDTH), lambda i, j: (0, 0))],
        out_specs=pl.BlockSpec((1, n2, kc * B_WIDTH), lambda i, j: (i, 0, j)),
        out_shape=jax.ShapeDtypeStruct((b, n2, n1 * B_WIDTH), BF16),
        compiler_params=_cparams("parallel", "parallel"),
        name="fnet_stage2",
    )(zr.reshape(b, n1, n2, B_WIDTH), zi.reshape(b, n1, n2, B_WIDTH), tch, tcl, tsh, tsl, wbd)
    return out.reshape(b, n, B_WIDTH)


def _outproj_kernel(a_ref, c_ref, wa_ref, wc_ref, x_ref, g_ref, b_ref, o_ref, *, a_transposed):
    if a_transposed:
        h = _dot_tn(a_ref[0], wa_ref[...])
    else:
        h = _dot(a_ref[0], wa_ref[...])
    h = h + _dot(c_ref[0], wc_ref[...])
    o_ref[0] = _layer_norm(DN_ALPHA * x_ref[0] + h, g_ref[...], b_ref[...])


def _outproj(a, c, w, x, g, bias, a_transposed):
    b, n, _ = x.shape
    ka = w.shape[0] - c.shape[-1]
    wa = w[:ka].astype(BF16)
    wc = w[ka:].astype(BF16)
    if a_transposed:
        aspec = pl.BlockSpec((1, ka, TM), lambda i, j: (i, 0, j))
    else:
        aspec = pl.BlockSpec((1, TM, ka), lambda i, j: (i, j, 0))
    vec = pl.BlockSpec((1, D_MODEL), lambda i, j: (0, 0))
    xspec = pl.BlockSpec((1, TM, D_MODEL), lambda i, j: (i, j, 0))
    return pl.pallas_call(
        functools.partial(_outproj_kernel, a_transposed=a_transposed),
        grid=(b, n // TM),
        in_specs=[aspec, pl.BlockSpec((1, TM, c.shape[-1]), lambda i, j: (i, j, 0)),
                  pl.BlockSpec(wa.shape, lambda i, j: (0, 0)),
                  pl.BlockSpec(wc.shape, lambda i, j: (0, 0)), xspec, vec, vec],
        out_specs=xspec,
        out_shape=jax.ShapeDtypeStruct((b, n, D_MODEL), F32),
        compiler_params=_cparams("parallel", "parallel"),
        name="outproj_ln",
    )(a, c, wa, wc, x, g.reshape(1, D_MODEL), bias.reshape(1, D_MODEL))


def _mem_kv_kernel(mem_ref, wkt_ref, wv_ref, kt_ref, v_ref):
    m = mem_ref[0].astype(BF16)
    kt_ref[0] = _dot_nt(wkt_ref[...], m).astype(BF16)
    v_ref[0] = _dot(m, wv_ref[...]).astype(BF16)


def _mem_kv(mem, w_kv):
    b, m, _ = mem.shape
    wkt = w_kv[:, :D_MODEL].T.astype(BF16)
    wv = w_kv[:, D_MODEL:].astype(BF16)
    wspec = pl.BlockSpec((D_MODEL, D_MODEL), lambda i: (0, 0))
    return pl.pallas_call(
        _mem_kv_kernel,
        grid=(b,),
        in_specs=[pl.BlockSpec((1, m, D_MODEL), lambda i: (i, 0, 0)), wspec, wspec],
        out_specs=[pl.BlockSpec((1, D_MODEL, m), lambda i: (i, 0, 0)),
                   pl.BlockSpec((1, m, D_MODEL), lambda i: (i, 0, 0))],
        out_shape=[jax.ShapeDtypeStruct((b, D_MODEL, m), BF16),
                   jax.ShapeDtypeStruct((b, m, D_MODEL), BF16)],
        compiler_params=_cparams("parallel"),
        name="mem_kv",
    )(mem, wkt, wv)


def _xattn_kernel(x_ref, wq_ref, kt_ref, v_ref, wo_ref, g_ref, b_ref, o_ref):
    x = x_ref[0]
    xb = x.astype(BF16)
    acc = None
    for h in range(XA_HEADS):
        cols = slice(h * XA_HEAD_DIM, (h + 1) * XA_HEAD_DIM)
        q = _dot(xb, wq_ref[:, cols]).astype(BF16)
        s = _dot(q, kt_ref[0, cols, :])
        m = jnp.max(s, axis=-1, keepdims=True)
        e = jnp.exp(s - m)
        l = jnp.sum(e, axis=-1, keepdims=True)
        o = (_dot(e.astype(BF16), v_ref[0, :, cols]) / l).astype(BF16)
        t = _dot(o, wo_ref[cols, :])
        acc = t if acc is None else acc + t
    o_ref[0] = _layer_norm(DN_ALPHA * x + acc, g_ref[...], b_ref[...])


def _xattn(x, mem, w_q, w_kv, w_o, g, bias):
    b, n, _ = x.shape
    m = mem.shape[1]
    kt, v = _mem_kv(mem, w_kv)
    wq = (w_q * (XA_HEAD_DIM ** -0.5)).astype(BF16)
    wo = w_o.astype(BF16)
    xspec = pl.BlockSpec((1, TM, D_MODEL), lambda i, j: (i, j, 0))
    wspec = pl.BlockSpec((D_MODEL, D_MODEL), lambda i, j: (0, 0))
    vec = pl.BlockSpec((1, D_MODEL), lambda i, j: (0, 0))
    return pl.pallas_call(
        _xattn_kernel,
        grid=(b, n // TM),
        in_specs=[xspec, wspec, pl.BlockSpec((1, D_MODEL, m), lambda i, j: (i, 0, 0)),
                  pl.BlockSpec((1, m, D_MODEL), lambda i, j: (i, 0, 0)), wspec, vec, vec],
        out_specs=xspec,
        out_shape=jax.ShapeDtypeStruct((b, n, D_MODEL), F32),
        compiler_params=_cparams("parallel", "parallel"),
        name="xattn_ln",
    )(x, wq, kt, v, wo, g.reshape(1, D_MODEL), bias.reshape(1, D_MODEL))


FFN_TH = 1408


def _swiglu_kernel(x_ref, wg_ref, wu_ref, wo_ref, g_ref, b_ref, o_ref, acc_ref):
    j = pl.program_id(2)
    xb = x_ref[0].astype(BF16)
    gate = _dot(xb, wg_ref[...])
    up = _dot(xb, wu_ref[...])
    hid = (gate * (1.0 / (1.0 + jnp.exp(-gate))) * up).astype(BF16)
    part = _dot(hid, wo_ref[...])

    @pl.when(j == 0)
    def _():
        acc_ref[...] = part

    @pl.when(j > 0)
    def _():
        acc_ref[...] += part

    @pl.when(j == pl.num_programs(2) - 1)
    def _():
        o_ref[0] = _layer_norm(DN_ALPHA * x_ref[0] + acc_ref[...], g_ref[...], b_ref[...])


def _swiglu(x, w_in, w_out, g, bias):
    b, n, _ = x.shape
    nj = FFN_HIDDEN // FFN_TH
    win = w_in.astype(BF16)
    wout = w_out.astype(BF16)
    xspec = pl.BlockSpec((1, TM, D_MODEL), lambda i, t, j: (i, t, 0))
    vec = pl.BlockSpec((1, D_MODEL), lambda i, t, j: (0, 0))
    return pl.pallas_call(
        _swiglu_kernel,
        grid=(b, n // TM, nj),
        in_specs=[xspec,
                  pl.BlockSpec((D_MODEL, FFN_TH), lambda i, t, j: (0, j)),
                  pl.BlockSpec((D_MODEL, FFN_TH), lambda i, t, j: (0, j + nj)),
                  pl.BlockSpec((FFN_TH, D_MODEL), lambda i, t, j: (j, 0)), vec, vec],
        out_specs=xspec,
        out_shape=jax.ShapeDtypeStruct((b, n, D_MODEL), F32),
        scratch_shapes=[pltpu.VMEM((TM, D_MODEL), F32)],
        compiler_params=_cparams("parallel", "parallel", "arbitrary"),
        name="swiglu_ln",
    )(x, win, win, wout, g.reshape(1, D_MODEL), bias.reshape(1, D_MODEL))


N_QK_HEADS = C_Q_HEADS + C_KV_HEADS
QK_ROWS = N_QK_HEADS * HEAD_DIM


def _proj_cd_kernel(x_ref, wt_ref, wu_ref, gain_ref, cos_ref, sin_ref,
                    qt_ref, k_ref, vt_ref, u_ref, *, tm):
    xb = x_ref[0].astype(BF16)
    u_ref[0] = _dot(xb, wu_ref[...])
    zt = _dot_nt(wt_ref[...], xb)
    z = zt[:QK_ROWS].reshape(N_QK_HEADS, HEAD_DIM, tm)
    ssq = jnp.sum(z * z, axis=1, keepdims=True)
    zn = z * lax.rsqrt(ssq * (1.0 / HEAD_DIM) + RMS_EPS) * gain_ref[...]
    half = HEAD_DIM // 2
    x1 = zn[:, :half]
    x2 = zn[:, half:]
    c = cos_ref[...][None]
    s = sin_ref[...][None]
    rot = jnp.concatenate([x1 * c - x2 * s, x1 * s + x2 * c], axis=1).reshape(QK_ROWS, tm)
    qt_ref[0] = rot[:C_WIDTH].astype(BF16)
    k_ref[0] = rot[C_WIDTH:].T.astype(BF16)
    vt = zt[QK_ROWS:].astype(BF16)
    for c0 in range(tm // GQA_TKC):
        vt_ref[0, c0] = vt[:, c0 * GQA_TKC:(c0 + 1) * GQA_TKC]


def _rope_tables_t(n):
    rows = n // GRID_W
    row_id = jnp.broadcast_to(jnp.arange(rows)[:, None], (rows, GRID_W)).reshape(n)
    col_id = jnp.broadcast_to(jnp.arange(GRID_W)[None, :], (rows, GRID_W)).reshape(n)
    axis_dim = HEAD_DIM // 2
    freqs = ROPE_THETA ** (-jnp.arange(0, axis_dim, 2, dtype=F32) / axis_dim)
    ang = jnp.concatenate([row_id[:, None] * freqs, col_id[:, None] * freqs], axis=-1)
    return jnp.cos(ang).T, jnp.sin(ang).T


def _proj_cd(x, w_in, q_norm, k_norm):
    b, n, _ = x.shape
    tm = TM
    perm = np.concatenate([np.arange(0, HEAD_DIM, 2), np.arange(1, HEAD_DIM, 2)])
    head_perm = (np.arange(N_QK_HEADS)[:, None] * HEAD_DIM + perm[None, :]).reshape(-1)
    rows = np.concatenate([head_perm, np.arange(QK_ROWS, QK_ROWS + C_KV_WIDTH)])
    wt = w_in[:, :QK_ROWS + C_KV_WIDTH].T[rows].astype(BF16)
    wu = w_in[:, QK_ROWS + C_KV_WIDTH:].astype(BF16)
    qg = q_norm[perm] * (HEAD_DIM ** -0.5 * LOG2E)
    kg = k_norm[perm]
    gain = jnp.concatenate([jnp.tile(qg[None], (C_Q_HEADS, 1)), jnp.tile(kg[None], (C_KV_HEADS, 1))])
    gain = gain.reshape(N_QK_HEADS, HEAD_DIM, 1).astype(F32)
    cos_t, sin_t = _rope_tables_t(n)
    nc = n // GQA_TKC
    return pl.pallas_call(
        functools.partial(_proj_cd_kernel, tm=tm),
        grid=(b, n // tm),
        in_specs=[pl.BlockSpec((1, tm, D_MODEL), lambda i, j: (i, j, 0)),
                  pl.BlockSpec(wt.shape, lambda i, j: (0, 0)),
                  pl.BlockSpec(wu.shape, lambda i, j: (0, 0)),
                  pl.BlockSpec(gain.shape, lambda i, j: (0, 0, 0)),
                  pl.BlockSpec((HEAD_DIM // 2, tm), lambda i, j: (0, j)),
                  pl.BlockSpec((HEAD_DIM // 2, tm), lambda i, j: (0, j))],
        out_specs=[pl.BlockSpec((1, C_WIDTH, tm), lambda i, j: (i, 0, j)),
                   pl.BlockSpec((1, tm, C_KV_WIDTH), lambda i, j: (i, j, 0)),
                   pl.BlockSpec((1, tm // GQA_TKC, C_KV_WIDTH, GQA_TKC), lambda i, j: (i, j, 0, 0)),
                   pl.BlockSpec((1, tm, D_WIDTH), lambda i, j: (i, j, 0))],
        out_shape=[jax.ShapeDtypeStruct((b, C_WIDTH, n), BF16),
                   jax.ShapeDtypeStruct((b, n, C_KV_WIDTH), BF16),
                   jax.ShapeDtypeStruct((b, nc, C_KV_WIDTH, GQA_TKC), BF16),
                   jax.ShapeDtypeStruct((b, n, D_WIDTH), F32)],
        compiler_params=_cparams("parallel", "parallel"),
        name="proj_cd",
    )(x, wt, wu, gain, cos_t, sin_t)


def _gqa_kernel(qt_ref, k_ref, vt_ref, ot_ref, m_ref, acc_ref, qpad_ref, *, tq, n):
    g = pl.program_id(1)
    row_half = lax.broadcasted_iota(jnp.int32, (128, 1), 0) // HEAD_DIM
    mine = row_half == (g % 2)
    for j in range(C_REP):
        qj = qt_ref[0, j * HEAD_DIM:(j + 1) * HEAD_DIM, :]
        q2 = jnp.concatenate([qj, qj], axis=0)
        qpad_ref[j] = jnp.where(mine, q2, jnp.zeros_like(q2))
    m_ref[...] = jnp.full(m_ref.shape, NEG_INF, F32)
    acc_ref[...] = jnp.zeros(acc_ref.shape, F32)
    ones = jnp.ones((GQA_VROWS - HEAD_DIM, GQA_TKC), BF16)

    def body(c, carry):
        kchs, vchs = [], []
        for u in range(GQA_UNROLL):
            cc = c * GQA_UNROLL + u
            start = pl.multiple_of(cc * GQA_TKC, GQA_TKC)
            kchs.append(k_ref[0, pl.ds(start, GQA_TKC), :])
            vchs.append(jnp.concatenate([vt_ref[0, cc], ones], axis=0))
        pieces = [(u, j, slice(s * GQA_TW, (s + 1) * GQA_TW))
                  for u in range(GQA_UNROLL) for j in range(C_REP) for s in range(tq // GQA_TW)]

        def scores(i):
            u, j, cols = pieces[i]
            return _dot(kchs[u], qpad_ref[j, :, cols])

        pending = [scores(i) for i in range(GQA_AHEAD)]
        for i, (u, j, cols) in enumerate(pieces):
            st = pending.pop(0)
            if i + GQA_AHEAD < len(pieces):
                pending.append(scores(i + GQA_AHEAD))
            m_old = m_ref[j, :, cols]
            m_new = jnp.maximum(m_old, jnp.max(st, axis=0, keepdims=True))
            alpha = jnp.exp2(m_old - m_new)
            p = jnp.exp2(st - m_new)
            acc_ref[j, :, cols] = alpha * acc_ref[j, :, cols] + _dot(vchs[u], p.astype(BF16))
            m_ref[j, :, cols] = m_new
        return carry

    lax.fori_loop(0, n // (GQA_TKC * GQA_UNROLL), body, 0)
    for j in range(C_REP):
        l = acc_ref[j, HEAD_DIM:HEAD_DIM + 1, :]
        ot_ref[0, j * HEAD_DIM:(j + 1) * HEAD_DIM, :] = (acc_ref[j, :HEAD_DIM, :] / l).astype(BF16)


def _gqa(qt, k, vt):
    b, _, n = qt.shape
    tq = min(GQA_TQ, n)
    nc = n // GQA_TKC
    rows = C_REP * HEAD_DIM
    return pl.pallas_call(
        functools.partial(_gqa_kernel, tq=tq, n=n),
        grid=(b, C_KV_HEADS, n // tq),
        in_specs=[pl.BlockSpec((1, rows, tq), lambda i, g, t: (i, g, t)),
                  pl.BlockSpec((1, n, 128), lambda i, g, t: (i, 0, g // 2)),
                  pl.BlockSpec((1, nc, HEAD_DIM, GQA_TKC), lambda i, g, t: (i, 0, g, 0))],
        out_specs=pl.BlockSpec((1, rows, tq), lambda i, g, t: (i, g, t)),
        out_shape=jax.ShapeDtypeStruct((b, C_WIDTH, n), BF16),
        scratch_shapes=[pltpu.VMEM((C_REP, 1, tq), F32),
                        pltpu.VMEM((C_REP, GQA_VROWS, tq), F32), pltpu.VMEM((C_REP, 128, tq), BF16)],
        compiler_params=_cparams("parallel", "parallel", "parallel"),
        name="gqa_flash",
    )(qt, k, vt)


def _pool_kernel(up_ref, uc_ref, un_ref, w_ref, scale_ref, o_ref, buf, *, tm, n):
    i = pl.program_id(1)
    cur = uc_ref[0]
    buf[0:POOL_HALO, :] = jnp.where(i > 0, up_ref[0], 0.0)
    buf[POOL_HALO:POOL_HALO + tm, :] = cur
    buf[POOL_HALO + tm:, :] = jnp.where(i < pl.num_programs(1) - 1, un_ref[0], 0.0)
    lane_group = lax.broadcasted_iota(jnp.int32, (1, D_WIDTH), 1) // 64
    half_w = jnp.left_shift(1, lane_group)
    acc = jnp.zeros((tm, D_WIDTH), F32)
    for j in range(-POOL_HALO, POOL_HALO):
        inside = (j >= -half_w) & (j < half_w)
        acc = acc + jnp.where(inside, buf[POOL_HALO + j:POOL_HALO + j + tm, :], 0.0)
    t = i * tm + lax.broadcasted_iota(jnp.int32, (tm, 1), 0)
    cnt = jnp.minimum(t + half_w, n) - jnp.maximum(t - half_w, 0)
    mixed = (acc / cnt.astype(F32) - cur).astype(BF16)
    o_ref[0] = (_dot(mixed, w_ref[...]) * scale_ref[...]).astype(BF16)


def _pool(u, pool_w, pool_scale):
    b, n, _ = u.shape
    tm = TM
    per = tm // POOL_HALO
    last = n // POOL_HALO - 1
    wbd = jax.scipy.linalg.block_diag(*[pool_w[g] for g in range(4)]).astype(BF16)
    return pl.pallas_call(
        functools.partial(_pool_kernel, tm=tm, n=n),
        grid=(b, n // tm),
        in_specs=[pl.BlockSpec((1, POOL_HALO, D_WIDTH), lambda i, j: (i, jnp.maximum(j * per - 1, 0), 0)),
                  pl.BlockSpec((1, tm, D_WIDTH), lambda i, j: (i, j, 0)),
                  pl.BlockSpec((1, POOL_HALO, D_WIDTH), lambda i, j: (i, jnp.minimum((j + 1) * per, last), 0)),
                  pl.BlockSpec((D_WIDTH, D_WIDTH), lambda i, j: (0, 0)),
                  pl.BlockSpec((1, D_WIDTH), lambda i, j: (0, 0))],
        out_specs=pl.BlockSpec((1, tm, D_WIDTH), lambda i, j: (i, j, 0)),
        out_shape=jax.ShapeDtypeStruct((b, n, D_WIDTH), BF16),
        scratch_shapes=[pltpu.VMEM((tm + 2 * POOL_HALO, D_WIDTH), F32)],
        compiler_params=_cparams("parallel", "parallel"),
        name="pool",
    )(u, u, u, wbd, pool_scale.reshape(1, D_WIDTH))


def _trunk(x, mem, rel_bias, ab_w_in, ab_fnet_g, ab_fnet_w, ab_w_out,
           cd_w_in, cd_q_norm, cd_k_norm, cd_pool_w, cd_pool_scale, cd_w_out,
           xa_w_q, xa_w_kv, xa_w_o, ffn_w_in, ffn_w_out, ln_g, ln_b):
    for layer in range(DEPTH):
        i = layer // 2
        if layer % 2 == 0:
            w_in = ab_w_in[i]
            w_in = jnp.concatenate([w_in[:, :A_WIDTH] * (HEAD_DIM ** -0.5), w_in[:, A_WIDTH:]], axis=1)
            qkv, u = _proj_ab(x, w_in.astype(BF16))
            outs = [_dilated(qkv, rel_bias, d) for _, d in A_PATTERNS]
            o_a = _mixture([o for o, _ in outs], [l for _, l in outs])
            o_b = _fnet(u, ab_fnet_g[i], ab_fnet_w[i])
            x = _outproj(o_a, o_b, ab_w_out[i], x, ln_g[layer, 0], ln_b[layer, 0], a_transposed=False)
        else:
            qt, k, vt, u = _proj_cd(x, cd_w_in[i], cd_q_norm[i], cd_k_norm[i])
            o_c = _gqa(qt, k, vt)
            o_d = _pool(u, cd_pool_w[i], cd_pool_scale[i])
            x = _outproj(o_c, o_d, cd_w_out[i], x, ln_g[layer, 0], ln_b[layer, 0], a_transposed=True)
        x = _xattn(x, mem, xa_w_q[layer], xa_w_kv[layer], xa_w_o[layer], ln_g[layer, 1], ln_b[layer, 1])
        x = _swiglu(x, ffn_w_in[layer], ffn_w_out[layer], ln_g[layer, 2], ln_b[layer, 2])
    return x


def kernel(x_prompt, x_sample, mem_prompt, mem_sample, rel_bias, ab_w_in, ab_fnet_g, ab_fnet_w, ab_w_out, cd_w_in, cd_q_norm, cd_k_norm, cd_pool_w, cd_pool_scale, cd_w_out, xa_w_q, xa_w_kv, xa_w_o, ffn_w_in, ffn_w_out, ln_g, ln_b):
    params = (rel_bias, ab_w_in, ab_fnet_g, ab_fnet_w, ab_w_out,
              cd_w_in, cd_q_norm, cd_k_norm, cd_pool_w, cd_pool_scale, cd_w_out,
              xa_w_q, xa_w_kv, xa_w_o, ffn_w_in, ffn_w_out, ln_g, ln_b)
    return (_trunk(x_prompt, mem_prompt, *params), _trunk(x_sample, mem_sample, *params))
```

```python
import functools
import math

import numpy as np
import jax
import jax.numpy as jnp
from jax import lax
from jax.experimental import pallas as pl
from jax.experimental.pallas import tpu as pltpu

F32 = jnp.float32
BF16 = jnp.bfloat16

D_MODEL = 1024
HEAD_DIM = 64
GRID_W = 64
LN_EPS = 1e-5
RMS_EPS = 1e-6
NEG_INF = -1e30
DEPTH = 2
A_HEADS = 12
A_WIDTH = A_HEADS * HEAD_DIM
A_PATTERNS = ((128, 1), (512, 4), (2048, 16))
A_HALF = 64
N_BUCKETS = 32
REL_MAX_DIST = 1024
B_WIDTH = 256
C_Q_HEADS = 12
C_KV_HEADS = 4
C_REP = C_Q_HEADS // C_KV_HEADS
C_WIDTH = C_Q_HEADS * HEAD_DIM
C_KV_WIDTH = C_KV_HEADS * HEAD_DIM
ROPE_THETA = 10000.0
POOL_WINDOWS = (2, 4, 8, 16)
POOL_HALO = 8
D_WIDTH = 256
XA_HEADS = 4
XA_HEAD_DIM = D_MODEL // XA_HEADS
FFN_HIDDEN = 2816
DN_ALPHA = (2 * DEPTH) ** 0.25
LOG2E = 1.4426950408889634

VMEM_LIMIT = 56 * 1024 * 1024
TM = 512
FFT_N2 = 128
DIL_VROWS = HEAD_DIM + 16
DIL_AHEAD = 3
GQA_TQ = 1024
GQA_TW = 256
GQA_TKC = 256
GQA_VROWS = HEAD_DIM + 16
GQA_UNROLL = 8
GQA_AHEAD = 5


def _cparams(*sem):
    return pltpu.CompilerParams(dimension_semantics=sem, vmem_limit_bytes=VMEM_LIMIT)


def _dot(a, b):
    return jnp.dot(a, b, preferred_element_type=F32)


def _dot_nt(a, b):
    return lax.dot_general(a, b, (((1,), (1,)), ((), ())), preferred_element_type=F32)


def _dot_tn(a, b):
    return lax.dot_general(a, b, (((0,), (0,)), ((), ())), preferred_element_type=F32)


def _split(x):
    hi = x.astype(BF16)
    lo = (x - hi.astype(F32)).astype(BF16)
    return hi, lo


def _dot3(ah, al, bh, bl):
    return _dot(ah, bh) + _dot(al, bh) + _dot(ah, bl)


def _np_split(x):
    x = np.asarray(x, np.float64)
    hi = jnp.asarray(x, F32).astype(BF16)
    lo = (jnp.asarray(x, F32) - hi.astype(F32)).astype(BF16)
    return hi, lo


def _layer_norm(h, g, b):
    mu = jnp.mean(h, axis=-1, keepdims=True)
    xc = h - mu
    var = jnp.mean(xc * xc, axis=-1, keepdims=True)
    return xc * lax.rsqrt(var + LN_EPS) * g + b


def _proj_ab_kernel(x_ref, w_ref, qkv_ref, u_ref):
    xb = x_ref[0].astype(BF16)
    for c in range(0, 3 * A_WIDTH, 256):
        qkv_ref[0, :, c:c + 256] = _dot(xb, w_ref[:, c:c + 256]).astype(BF16)
    u_ref[0] = _dot(xb, w_ref[:, 3 * A_WIDTH:])


def _proj_ab(x, w):
    b, n, _ = x.shape
    return pl.pallas_call(
        _proj_ab_kernel,
        grid=(b, n // TM),
        in_specs=[pl.BlockSpec((1, TM, D_MODEL), lambda i, j: (i, j, 0)),
                  pl.BlockSpec(w.shape, lambda i, j: (0, 0))],
        out_specs=[pl.BlockSpec((1, TM, 3 * A_WIDTH), lambda i, j: (i, j, 0)),
                   pl.BlockSpec((1, TM, B_WIDTH), lambda i, j: (i, j, 0))],
        out_shape=[jax.ShapeDtypeStruct((b, n, 3 * A_WIDTH), BF16),
                   jax.ShapeDtypeStruct((b, n, B_WIDTH), F32)],
        compiler_params=_cparams("parallel", "parallel"),
        name="proj_ab",
    )(x, w)


def _t5_bucket_np(rel):
    nb = N_BUCKETS // 2
    max_exact = nb // 2
    ret = np.where(rel > 0, nb, 0)
    n = np.abs(rel)
    nf = np.maximum(n, 1).astype(np.float32)
    large = max_exact + (np.log(nf / max_exact) / math.log(REL_MAX_DIST / max_exact)
                         * (nb - max_exact)).astype(np.int32)
    large = np.minimum(large, nb - 1)
    return ret + np.where(n < max_exact, n, large)


def _band_bias(rel_bias, dilation, tq):
    tk = tq + 2 * A_HALF
    band = 2 * A_HALF + 1
    bucket = _t5_bucket_np((np.arange(band) - A_HALF) * dilation)
    row = (rel_bias[jnp.asarray(bucket)].T * LOG2E).astype(F32)
    row = jnp.concatenate([row, jnp.full((A_HEADS, tk + 1 - band), NEG_INF, F32)], axis=1)
    bias = jnp.tile(row, (1, tq))[:, :tq * tk].reshape(A_HEADS, tq, tk)
    bias = bias.transpose(0, 2, 1)
    key = np.arange(tk)[None, :, None]
    before = jnp.asarray(key < A_HALF)
    after = jnp.asarray(key >= A_HALF + tq)
    first = jnp.where(before, NEG_INF, bias)
    return jnp.stack([bias, first, jnp.where(after, NEG_INF, bias), jnp.where(after, NEG_INF, first)])


def _dilated_kernel(q_ref, kp_ref, kc_ref, kn_ref, vp_ref, vc_ref, vn_ref, bias_ref,
                    o_ref, lse_ref, kbuf, vbuf, *, tq):
    tk = tq + 2 * A_HALF
    kbuf[0:A_HALF, :] = kp_ref[0]
    kbuf[A_HALF:A_HALF + tq, :] = kc_ref[0]
    kbuf[A_HALF + tq:, :] = kn_ref[0]
    vbuf[0:A_HALF, :] = vp_ref[0]
    vbuf[A_HALF:A_HALF + tq, :] = vc_ref[0]
    vbuf[A_HALF + tq:, :] = vn_ref[0]

    row_low = lax.broadcasted_iota(jnp.int32, (128, 1), 0) < HEAD_DIM
    row16 = lax.broadcasted_iota(jnp.int32, (16, 1), 0)
    ones = jnp.ones((DIL_VROWS - HEAD_DIM, tk), BF16)
    qts, vts = {}, {}

    def scores(h):
        pair, half = divmod(h, 2)
        cols = slice(pair * 128, (pair + 1) * 128)
        if pair not in qts:
            qts[pair] = q_ref[0, :, cols].astype(F32).T
        qtm = jnp.where(row_low if half == 0 else jnp.logical_not(row_low), qts[pair], 0.0).astype(BF16)
        return _dot(kbuf[:, cols], qtm) + bias_ref[0, h]

    pending = [scores(h) for h in range(DIL_AHEAD)]
    lse_t = jnp.zeros((16, tq), F32)
    o_low = None
    for h in range(A_HEADS):
        st = pending.pop(0)
        if h + DIL_AHEAD < A_HEADS:
            pending.append(scores(h + DIL_AHEAD))
        pair, half = divmod(h, 2)
        cols = slice(pair * 128, (pair + 1) * 128)
        m = jnp.max(st, axis=0, keepdims=True)
        p = jnp.exp2(st - m).astype(BF16)
        if pair not in vts:
            vts[pair] = vbuf[:, cols].astype(F32).T.astype(BF16)
        vaug = jnp.concatenate([vts[pair][half * HEAD_DIM:(half + 1) * HEAD_DIM], ones], axis=0)
        ot = _dot(vaug, p)
        l = ot[HEAD_DIM:HEAD_DIM + 1]
        o_h = ot[:HEAD_DIM] / l
        lse_t = jnp.where(row16 == h, m + jnp.log2(l), lse_t)
        if half == 0:
            o_low = o_h
        else:
            o_ref[0, :, cols] = jnp.concatenate([o_low, o_h], axis=0).T.astype(BF16)
    lse_ref[0] = jnp.concatenate([lse_t, jnp.zeros((128 - 16, tq), F32)], axis=0).T


def _dilated(qkv, rel_bias, dilation):
    b, n, _ = qkv.shape
    seq = n // dilation
    tq = min(256, seq)
    tk = tq + 2 * A_HALF
    w = 3 * A_WIDTH
    view = qkv.reshape(b, seq, dilation * w)
    bias = _band_bias(rel_bias, dilation, tq)
    per = tq // A_HALF
    last = seq // A_HALF - 1
    n_tiles = seq // tq

    def cur(c):
        return pl.BlockSpec((1, tq, A_WIDTH), lambda bi, i, r: (bi, i, 3 * r + c))

    def prev(c):
        return pl.BlockSpec((1, A_HALF, A_WIDTH),
                            lambda bi, i, r: (bi, jnp.maximum(i * per - 1, 0), 3 * r + c))

    def nxt(c):
        return pl.BlockSpec((1, A_HALF, A_WIDTH),
                            lambda bi, i, r: (bi, jnp.minimum((i + 1) * per, last), 3 * r + c))

    o, lse = pl.pallas_call(
        functools.partial(_dilated_kernel, tq=tq),
        grid=(b, seq // tq, dilation),
        in_specs=[cur(0), prev(1), cur(1), nxt(1), prev(2), cur(2), nxt(2),
                  pl.BlockSpec((1, A_HEADS, tk, tq),
                               lambda bi, i, r: ((i == 0) + 2 * (i == n_tiles - 1), 0, 0, 0))],
        out_specs=[pl.BlockSpec((1, tq, A_WIDTH), lambda bi, i, r: (bi, i, r)),
                   pl.BlockSpec((1, tq, 128), lambda bi, i, r: (bi, i, r))],
        out_shape=[jax.ShapeDtypeStruct((b, seq, dilation * A_WIDTH), BF16),
                   jax.ShapeDtypeStruct((b, seq, dilation * 128), F32)],
        scratch_shapes=[pltpu.VMEM((tk, A_WIDTH), BF16), pltpu.VMEM((tk, A_WIDTH), BF16)],
        compiler_params=_cparams("parallel", "parallel", "parallel"),
        name=f"dilated_d{dilation}",
    )(view, view, view, view, view, view, view, bias)
    return o.reshape(b, n, A_WIDTH), lse.reshape(b, n, 128)


def _mixture_kernel(o1_ref, o2_ref, o3_ref, l1_ref, l2_ref, l3_ref, out_ref):
    ls = [l1_ref[0], l2_ref[0], l3_ref[0]]
    mx = jnp.maximum(jnp.maximum(ls[0], ls[1]), ls[2])
    es = [jnp.exp2(l - mx) for l in ls]
    inv = 1.0 / (es[0] + es[1] + es[2])
    ws = [e * inv for e in es]
    low = lax.broadcasted_iota(jnp.int32, (1, 128), 1) < HEAD_DIM
    o_refs = (o1_ref, o2_ref, o3_ref)
    for pair in range(A_HEADS // 2):
        cols = slice(pair * 128, (pair + 1) * 128)
        acc = None
        for g in range(3):
            wexp = jnp.where(low, ws[g][:, 2 * pair:2 * pair + 1], ws[g][:, 2 * pair + 1:2 * pair + 2])
            t = wexp * o_refs[g][0, :, cols].astype(F32)
            acc = t if acc is None else acc + t
        out_ref[0, :, cols] = acc.astype(BF16)


def _mixture(os_, ls_):
    b, n, _ = os_[0].shape
    ospec = pl.BlockSpec((1, TM, A_WIDTH), lambda i, j: (i, j, 0))
    lspec = pl.BlockSpec((1, TM, 128), lambda i, j: (i, j, 0))
    return pl.pallas_call(
        _mixture_kernel,
        grid=(b, n // TM),
        in_specs=[ospec] * 3 + [lspec] * 3,
        out_specs=ospec,
        out_shape=jax.ShapeDtypeStruct((b, n, A_WIDTH), BF16),
        compiler_params=_cparams("parallel", "parallel"),
        name="dilated_mixture",
    )(*os_, *ls_)


def _group_mean_matrix():
    g = np.kron(np.eye(4), np.full((64, 64), 1.0 / 64))
    return jnp.asarray(g, BF16)


def _fnet_pre_kernel(u_ref, gm_ref, gain_ref, ch_ref, cl_ref, sh_ref, sl_ref, yr_ref, yi_ref):
    u = u_ref[0]
    gm = gm_ref[...]
    uh, ul = _split(u)
    mean = _dot(uh, gm) + _dot(ul, gm)
    xc = u - mean
    sh, sl = _split(xc * xc)
    var = _dot(sh, gm) + _dot(sl, gm)
    un = xc * lax.rsqrt(var + LN_EPS) * gain_ref[...]
    nh, nl = _split(un)
    yr_ref[0] = _dot3(nh, nl, ch_ref[...], cl_ref[...])
    yi_ref[0] = -_dot3(nh, nl, sh_ref[...], sl_ref[...])


def _fnet_stage1_kernel(yr_ref, yi_ref, a1h_ref, a1l_ref, a2h_ref, a2l_ref, zr_ref, zi_ref):
    y = jnp.concatenate([yr_ref[0], yi_ref[0]], axis=0)
    yh, yl = _split(y)
    zr_ref[0] = _dot3(a1h_ref[...], a1l_ref[...], yh, yl)
    zi_ref[0] = _dot3(a2h_ref[...], a2l_ref[...], yh, yl)


def _fnet_stage2_kernel(zr_ref, zi_ref, tch_ref, tcl_ref, tsh_ref, tsl_ref, w_ref, o_ref, *, kc, scale):
    for kk in range(kc):
        rh, rl = _split(zr_ref[0, kk])
        ih, il = _split(zi_ref[0, kk])
        f = _dot3(tch_ref[kk], tcl_ref[kk], rh, rl) + _dot3(tsh_ref[kk], tsl_ref[kk], ih, il)
        f = f * scale
        o_ref[0, :, kk * B_WIDTH:(kk + 1) * B_WIDTH] = _dot(f.astype(BF16), w_ref[...]).astype(BF16)


def _fnet_tables(n):
    n2 = FFT_N2
    n1 = n // n2
    c = np.arange(64)
    ang = 2 * np.pi * np.outer(c, c) / 64
    cbd = np.kron(np.eye(4), np.cos(ang))
    sbd = np.kron(np.eye(4), np.sin(ang))
    k1 = np.arange(n1)
    ang1 = 2 * np.pi * np.outer(k1, k1) / n1
    c1, s1 = np.cos(ang1), np.sin(ang1)
    a1 = np.concatenate([c1, s1], axis=1)
    a2 = np.concatenate([-s1, c1], axis=1)
    k2 = np.arange(n2)
    npr = k1[:, None, None] + n1 * k2[None, :, None]
    prod = (npr * k2[None, None, :]) % n
    ang2 = 2 * np.pi * prod / n
    return (_np_split(cbd), _np_split(sbd), _np_split(a1), _np_split(a2),
            _np_split(np.cos(ang2)), _np_split(np.sin(ang2)))


def _fnet(u, fnet_g, fnet_w):
    b, n, _ = u.shape
    n2 = FFT_N2
    n1 = n // n2
    (ch, cl), (sh, sl), (a1h, a1l), (a2h, a2l), (tch, tcl), (tsh, tsl) = _fnet_tables(n)
    gain = fnet_g.reshape(1, B_WIDTH)
    wbd = jax.scipy.linalg.block_diag(*[fnet_w[g] for g in range(4)]).astype(BF16)

    tok = pl.BlockSpec((1, TM, B_WIDTH), lambda i, j: (i, j, 0))
    mat = pl.BlockSpec((B_WIDTH, B_WIDTH), lambda i, j: (0, 0))
    yr, yi = pl.pallas_call(
        _fnet_pre_kernel,
        grid=(b, n // TM),
        in_specs=[tok, mat, pl.BlockSpec((1, B_WIDTH), lambda i, j: (0, 0)), mat, mat, mat, mat],
        out_specs=[tok, tok],
        out_shape=[jax.ShapeDtypeStruct((b, n, B_WIDTH), F32)] * 2,
        compiler_params=_cparams("parallel", "parallel"),
        name="fnet_pre",
    )(u, _group_mean_matrix(), gain, ch, cl, sh, sl)

    cols = n2 * B_WIDTH
    tc = 2048
    yspec = pl.BlockSpec((1, n1, tc), lambda i, j: (i, 0, j))
    aspec = pl.BlockSpec((n1, 2 * n1), lambda i, j: (0, 0))
    zr, zi = pl.pallas_call(
        _fnet_stage1_kernel,
        grid=(b, cols // tc),
        in_specs=[yspec, yspec, aspec, aspec, aspec, aspec],
        out_specs=[yspec, yspec],
        out_shape=[jax.ShapeDtypeStruct((b, n1, cols), F32)] * 2,
        compiler_params=_cparams("parallel", "parallel"),
        name="fnet_stage1",
    )(yr.reshape(b, n1, cols), yi.reshape(b, n1, cols), a1h, a1l, a2h, a2l)

    kc = 8
    zspec = pl.BlockSpec((1, kc, n2, B_WIDTH), lambda i, j: (i, j, 0, 0))
    tspec = pl.BlockSpec((kc, n2, n2), lambda i, j: (j, 0, 0))
    out = pl.pallas_call(
        functools.partial(_fnet_stage2_kernel, kc=kc, scale=1.0 / math.sqrt(64.0 * n)),
        grid=(b, n1 // kc),
        in_specs=[zspec, zspec, tspec, tspec, tspec, tspec,
                  pl.BlockSpec((B_WIDTH, B_WIDTH), lambda i, j: (0, 0))],
        out_specs=pl.BlockSpec((1, n2, kc * B_WIDTH), lambda i, j: (i, 0, j)),
        out_shape=jax.ShapeDtypeStruct((b, n2, n1 * B_WIDTH), BF16),
        compiler_params=_cparams("parallel", "parallel"),
        name="fnet_stage2",
    )(zr.reshape(b, n1, n2, B_WIDTH), zi.reshape(b, n1, n2, B_WIDTH), tch, tcl, tsh, tsl, wbd)
    return out.reshape(b, n, B_WIDTH)


def _outproj_kernel(a_ref, c_ref, wa_ref, wc_ref, x_ref, g_ref, b_ref, o_ref, *, a_transposed):
    if a_transposed:
        h = _dot_tn(a_ref[0], wa_ref[...])
    else:
        h = _dot(a_ref[0], wa_ref[...])
    h = h + _dot(c_ref[0], wc_ref[...])
    o_ref[0] = _layer_norm(DN_ALPHA * x_ref[0] + h, g_ref[...], b_ref[...])


def _outproj(a, c, w, x, g, bias, a_transposed):
    b, n, _ = x.shape
    ka = w.shape[0] - c.shape[-1]
    wa = w[:ka].astype(BF16)
    wc = w[ka:].astype(BF16)
    if a_transposed:
        aspec = pl.BlockSpec((1, ka, TM), lambda i, j: (i, 0, j))
    else:
        aspec = pl.BlockSpec((1, TM, ka), lambda i, j: (i, j, 0))
    vec = pl.BlockSpec((1, D_MODEL), lambda i, j: (0, 0))
    xspec = pl.BlockSpec((1, TM, D_MODEL), lambda i, j: (i, j, 0))
    return pl.pallas_call(
        functools.partial(_outproj_kernel, a_transposed=a_transposed),
        grid=(b, n // TM),
        in_specs=[aspec, pl.BlockSpec((1, TM, c.shape[-1]), lambda i, j: (i, j, 0)),
                  pl.BlockSpec(wa.shape, lambda i, j: (0, 0)),
                  pl.BlockSpec(wc.shape, lambda i, j: (0, 0)), xspec, vec, vec],
        out_specs=xspec,
        out_shape=jax.ShapeDtypeStruct((b, n, D_MODEL), F32),
        compiler_params=_cparams("parallel", "parallel"),
        name="outproj_ln",
    )(a, c, wa, wc, x, g.reshape(1, D_MODEL), bias.reshape(1, D_MODEL))


def _mem_kv_kernel(mem_ref, wkt_ref, wv_ref, kt_ref, v_ref):
    m = mem_ref[0].astype(BF16)
    kt_ref[0] = _dot_nt(wkt_ref[...], m).astype(BF16)
    v_ref[0] = _dot(m, wv_ref[...]).astype(BF16)


def _mem_kv(mem, w_kv):
    b, m, _ = mem.shape
    wkt = w_kv[:, :D_MODEL].T.astype(BF16)
    wv = w_kv[:, D_MODEL:].astype(BF16)
    wspec = pl.BlockSpec((D_MODEL, D_MODEL), lambda i: (0, 0))
    return pl.pallas_call(
        _mem_kv_kernel,
        grid=(b,),
        in_specs=[pl.BlockSpec((1, m, D_MODEL), lambda i: (i, 0, 0)), wspec, wspec],
        out_specs=[pl.BlockSpec((1, D_MODEL, m), lambda i: (i, 0, 0)),
                   pl.BlockSpec((1, m, D_MODEL), lambda i: (i, 0, 0))],
        out_shape=[jax.ShapeDtypeStruct((b, D_MODEL, m), BF16),
                   jax.ShapeDtypeStruct((b, m, D_MODEL), BF16)],
        compiler_params=_cparams("parallel"),
        name="mem_kv",
    )(mem, wkt, wv)


def _xattn_kernel(x_ref, wq_ref, kt_ref, v_ref, wo_ref, g_ref, b_ref, o_ref):
    x = x_ref[0]
    xb = x.astype(BF16)
    acc = None
    for h in range(XA_HEADS):
        cols = slice(h * XA_HEAD_DIM, (h + 1) * XA_HEAD_DIM)
        q = _dot(xb, wq_ref[:, cols]).astype(BF16)
        s = _dot(q, kt_ref[0, cols, :])
        m = jnp.max(s, axis=-1, keepdims=True)
        e = jnp.exp(s - m)
        l = jnp.sum(e, axis=-1, keepdims=True)
        o = (_dot(e.astype(BF16), v_ref[0, :, cols]) / l).astype(BF16)
        t = _dot(o, wo_ref[cols, :])
        acc = t if acc is None else acc + t
    o_ref[0] = _layer_norm(DN_ALPHA * x + acc, g_ref[...], b_ref[...])


def _xattn(x, mem, w_q, w_kv, w_o, g, bias):
    b, n, _ = x.shape
    m = mem.shape[1]
    kt, v = _mem_kv(mem, w_kv)
    wq = (w_q * (XA_HEAD_DIM ** -0.5)).astype(BF16)
    wo = w_o.astype(BF16)
    xspec = pl.BlockSpec((1, TM, D_MODEL), lambda i, j: (i, j, 0))
    wspec = pl.BlockSpec((D_MODEL, D_MODEL), lambda i, j: (0, 0))
    vec = pl.BlockSpec((1, D_MODEL), lambda i, j: (0, 0))
    return pl.pallas_call(
        _xattn_kernel,
        grid=(b, n // TM),
        in_specs=[xspec, wspec, pl.BlockSpec((1, D_MODEL, m), lambda i, j: (i, 0, 0)),
                  pl.BlockSpec((1, m, D_MODEL), lambda i, j: (i, 0, 0)), wspec, vec, vec],
        out_specs=xspec,
        out_shape=jax.ShapeDtypeStruct((b, n, D_MODEL), F32),
        compiler_params=_cparams("parallel", "parallel"),
        name="xattn_ln",
    )(x, wq, kt, v, wo, g.reshape(1, D_MODEL), bias.reshape(1, D_MODEL))


FFN_TH = 1408


def _swiglu_kernel(x_ref, wg_ref, wu_ref, wo_ref, g_ref, b_ref, o_ref, acc_ref):
    j = pl.program_id(2)
    xb = x_ref[0].astype(BF16)
    gate = _dot(xb, wg_ref[...])
    up = _dot(xb, wu_ref[...])
    hid = (gate * (1.0 / (1.0 + jnp.exp(-gate))) * up).astype(BF16)
    part = _dot(hid, wo_ref[...])

    @pl.when(j == 0)
    def _():
        acc_ref[...] = part

    @pl.when(j > 0)
    def _():
        acc_ref[...] += part

    @pl.when(j == pl.num_programs(2) - 1)
    def _():
        o_ref[0] = _layer_norm(DN_ALPHA * x_ref[0] + acc_ref[...], g_ref[...], b_ref[...])


def _swiglu(x, w_in, w_out, g, bias):
    b, n, _ = x.shape
    nj = FFN_HIDDEN // FFN_TH
    win = w_in.astype(BF16)
    wout = w_out.astype(BF16)
    xspec = pl.BlockSpec((1, TM, D_MODEL), lambda i, t, j: (i, t, 0))
    vec = pl.BlockSpec((1, D_MODEL), lambda i, t, j: (0, 0))
    return pl.pallas_call(
        _swiglu_kernel,
        grid=(b, n // TM, nj),
        in_specs=[xspec,
                  pl.BlockSpec((D_MODEL, FFN_TH), lambda i, t, j: (0, j)),
                  pl.BlockSpec((D_MODEL, FFN_TH), lambda i, t, j: (0, j + nj)),
                  pl.BlockSpec((FFN_TH, D_MODEL), lambda i, t, j: (j, 0)), vec, vec],
        out_specs=xspec,
        out_shape=jax.ShapeDtypeStruct((b, n, D_MODEL), F32),
        scratch_shapes=[pltpu.VMEM((TM, D_MODEL), F32)],
        compiler_params=_cparams("parallel", "parallel", "arbitrary"),
        name="swiglu_ln",
    )(x, win, win, wout, g.reshape(1, D_MODEL), bias.reshape(1, D_MODEL))


N_QK_HEADS = C_Q_HEADS + C_KV_HEADS
QK_ROWS = N_QK_HEADS * HEAD_DIM


def _proj_cd_kernel(x_ref, wt_ref, wu_ref, gain_ref, cos_ref, sin_ref,
                    qt_ref, k_ref, vt_ref, u_ref, *, tm):
    xb = x_ref[0].astype(BF16)
    u_ref[0] = _dot(xb, wu_ref[...])
    zt = _dot_nt(wt_ref[...], xb)
    z = zt[:QK_ROWS].reshape(N_QK_HEADS, HEAD_DIM, tm)
    ssq = jnp.sum(z * z, axis=1, keepdims=True)
    zn = z * lax.rsqrt(ssq * (1.0 / HEAD_DIM) + RMS_EPS) * gain_ref[...]
    half = HEAD_DIM // 2
    x1 = zn[:, :half]
    x2 = zn[:, half:]
    c = cos_ref[...][None]
    s = sin_ref[...][None]
    rot = jnp.concatenate([x1 * c - x2 * s, x1 * s + x2 * c], axis=1).reshape(QK_ROWS, tm)
    qt_ref[0] = rot[:C_WIDTH].astype(BF16)
    k_ref[0] = rot[C_WIDTH:].T.astype(BF16)
    vt = zt[QK_ROWS:].astype(BF16)
    for c0 in range(tm // GQA_TKC):
        vt_ref[0, c0] = vt[:, c0 * GQA_TKC:(c0 + 1) * GQA_TKC]


def _rope_tables_t(n):
    rows = n // GRID_W
    row_id = jnp.broadcast_to(jnp.arange(rows)[:, None], (rows, GRID_W)).reshape(n)
    col_id = jnp.broadcast_to(jnp.arange(GRID_W)[None, :], (rows, GRID_W)).reshape(n)
    axis_dim = HEAD_DIM // 2
    freqs = ROPE_THETA ** (-jnp.arange(0, axis_dim, 2, dtype=F32) / axis_dim)
    ang = jnp.concatenate([row_id[:, None] * freqs, col_id[:, None] * freqs], axis=-1)
    return jnp.cos(ang).T, jnp.sin(ang).T


def _proj_cd(x, w_in, q_norm, k_norm):
    b, n, _ = x.shape
    tm = TM
    half = HEAD_DIM // 2
    wqk = w_in[:, :QK_ROWS].reshape(D_MODEL, N_QK_HEADS, half, 2)
    wqk = wqk.transpose(1, 3, 2, 0).reshape(QK_ROWS, D_MODEL)
    wt = jnp.concatenate([wqk, w_in[:, QK_ROWS:QK_ROWS + C_KV_WIDTH].T], axis=0).astype(BF16)
    wu = w_in[:, QK_ROWS + C_KV_WIDTH:].astype(BF16)
    qg = q_norm.reshape(half, 2).T.reshape(HEAD_DIM) * (HEAD_DIM ** -0.5 * LOG2E)
    kg = k_norm.reshape(half, 2).T.reshape(HEAD_DIM)
    gain = jnp.concatenate([jnp.tile(qg[None], (C_Q_HEADS, 1)), jnp.tile(kg[None], (C_KV_HEADS, 1))])
    gain = gain.reshape(N_QK_HEADS, HEAD_DIM, 1).astype(F32)
    cos_t, sin_t = _rope_tables_t(n)
    nc = n // GQA_TKC
    return pl.pallas_call(
        functools.partial(_proj_cd_kernel, tm=tm),
        grid=(b, n // tm),
        in_specs=[pl.BlockSpec((1, tm, D_MODEL), lambda i, j: (i, j, 0)),
                  pl.BlockSpec(wt.shape, lambda i, j: (0, 0)),
                  pl.BlockSpec(wu.shape, lambda i, j: (0, 0)),
                  pl.BlockSpec(gain.shape, lambda i, j: (0, 0, 0)),
                  pl.BlockSpec((HEAD_DIM // 2, tm), lambda i, j: (0, j)),
                  pl.BlockSpec((HEAD_DIM // 2, tm), lambda i, j: (0, j))],
        out_specs=[pl.BlockSpec((1, C_WIDTH, tm), lambda i, j: (i, 0, j)),
                   pl.BlockSpec((1, tm, C_KV_WIDTH), lambda i, j: (i, j, 0)),
                   pl.BlockSpec((1, tm // GQA_TKC, C_KV_WIDTH, GQA_TKC), lambda i, j: (i, j, 0, 0)),
                   pl.BlockSpec((1, tm, D_WIDTH), lambda i, j: (i, j, 0))],
        out_shape=[jax.ShapeDtypeStruct((b, C_WIDTH, n), BF16),
                   jax.ShapeDtypeStruct((b, n, C_KV_WIDTH), BF16),
                   jax.ShapeDtypeStruct((b, nc, C_KV_WIDTH, GQA_TKC), BF16),
                   jax.ShapeDtypeStruct((b, n, D_WIDTH), F32)],
        compiler_params=_cparams("parallel", "parallel"),
        name="proj_cd",
    )(x, wt, wu, gain, cos_t, sin_t)


def _gqa_kernel(qt_ref, k_ref, vt_ref, ot_ref, m_ref, acc_ref, qpad_ref, *, tq, n):
    g = pl.program_id(1)
    row_half = lax.broadcasted_iota(jnp.int32, (128, 1), 0) // HEAD_DIM
    mine = row_half == (g % 2)
    for j in range(C_REP):
        qj = qt_ref[0, j * HEAD_DIM:(j + 1) * HEAD_DIM, :]
        q2 = jnp.concatenate([qj, qj], axis=0)
        qpad_ref[j] = jnp.where(mine, q2, jnp.zeros_like(q2))
    m_ref[...] = jnp.full(m_ref.shape, NEG_INF, F32)
    acc_ref[...] = jnp.zeros(acc_ref.shape, F32)
    ones = jnp.ones((GQA_VROWS - HEAD_DIM, GQA_TKC), BF16)

    def body(c, carry):
        kchs, vchs = [], []
        for u in range(GQA_UNROLL):
            cc = c * GQA_UNROLL + u
            start = pl.multiple_of(cc * GQA_TKC, GQA_TKC)
            kchs.append(k_ref[0, pl.ds(start, GQA_TKC), :])
            vchs.append(jnp.concatenate([vt_ref[0, cc], ones], axis=0))
        pieces = [(u, j, slice(s * GQA_TW, (s + 1) * GQA_TW))
                  for u in range(GQA_UNROLL) for j in range(C_REP) for s in range(tq // GQA_TW)]

        def scores(i):
            u, j, cols = pieces[i]
            return _dot(kchs[u], qpad_ref[j, :, cols])

        pending = [scores(i) for i in range(GQA_AHEAD)]
        for i, (u, j, cols) in enumerate(pieces):
            st = pending.pop(0)
            if i + GQA_AHEAD < len(pieces):
                pending.append(scores(i + GQA_AHEAD))
            m_old = m_ref[j, :, cols]
            m_new = jnp.maximum(m_old, jnp.max(st, axis=0, keepdims=True))
            alpha = jnp.exp2(m_old - m_new)
            p = jnp.exp2(st - m_new)
            acc_ref[j, :, cols] = alpha * acc_ref[j, :, cols] + _dot(vchs[u], p.astype(BF16))
            m_ref[j, :, cols] = m_new
        return carry

    lax.fori_loop(0, n // (GQA_TKC * GQA_UNROLL), body, 0)
    for j in range(C_REP):
        l = acc_ref[j, HEAD_DIM:HEAD_DIM + 1, :]
        ot_ref[0, j * HEAD_DIM:(j + 1) * HEAD_DIM, :] = (acc_ref[j, :HEAD_DIM, :] / l).astype(BF16)


def _gqa(qt, k, vt):
    b, _, n = qt.shape
    tq = min(GQA_TQ, n)
    nc = n // GQA_TKC
    rows = C_REP * HEAD_DIM
    return pl.pallas_call(
        functools.partial(_gqa_kernel, tq=tq, n=n),
        grid=(b, C_KV_HEADS, n // tq),
        in_specs=[pl.BlockSpec((1, rows, tq), lambda i, g, t: (i, g, t)),
                  pl.BlockSpec((1, n, 128), lambda i, g, t: (i, 0, g // 2)),
                  pl.BlockSpec((1, nc, HEAD_DIM, GQA_TKC), lambda i, g, t: (i, 0, g, 0))],
        out_specs=pl.BlockSpec((1, rows, tq), lambda i, g, t: (i, g, t)),
        out_shape=jax.ShapeDtypeStruct((b, C_WIDTH, n), BF16),
        scratch_shapes=[pltpu.VMEM((C_REP, 1, tq), F32),
                        pltpu.VMEM((C_REP, GQA_VROWS, tq), F32), pltpu.VMEM((C_REP, 128, tq), BF16)],
        compiler_params=_cparams("parallel", "parallel", "parallel"),
        name="gqa_flash",
    )(qt, k, vt)


def _pool_kernel(up_ref, uc_ref, un_ref, w_ref, scale_ref, o_ref, buf, *, tm, n):
    i = pl.program_id(1)
    cur = uc_ref[0]
    buf[0:POOL_HALO, :] = jnp.where(i > 0, up_ref[0], 0.0)
    buf[POOL_HALO:POOL_HALO + tm, :] = cur
    buf[POOL_HALO + tm:, :] = jnp.where(i < pl.num_programs(1) - 1, un_ref[0], 0.0)
    lane_group = lax.broadcasted_iota(jnp.int32, (1, D_WIDTH), 1) // 64
    half_w = jnp.left_shift(1, lane_group)
    acc = jnp.zeros((tm, D_WIDTH), F32)
    for j in range(-POOL_HALO, POOL_HALO):
        inside = (j >= -half_w) & (j < half_w)
        acc = acc + jnp.where(inside, buf[POOL_HALO + j:POOL_HALO + j + tm, :], 0.0)
    t = i * tm + lax.broadcasted_iota(jnp.int32, (tm, 1), 0)
    cnt = jnp.minimum(t + half_w, n) - jnp.maximum(t - half_w, 0)
    mixed = (acc / cnt.astype(F32) - cur).astype(BF16)
    o_ref[0] = (_dot(mixed, w_ref[...]) * scale_ref[...]).astype(BF16)


def _pool(u, pool_w, pool_scale):
    b, n, _ = u.shape
    tm = TM
    per = tm // POOL_HALO
    last = n // POOL_HALO - 1
    wbd = jax.scipy.linalg.block_diag(*[pool_w[g] for g in range(4)]).astype(BF16)
    return pl.pallas_call(
        functools.partial(_pool_kernel, tm=tm, n=n),
        grid=(b, n // tm),
        in_specs=[pl.BlockSpec((1, POOL_HALO, D_WIDTH), lambda i, j: (i, jnp.maximum(j * per - 1, 0), 0)),
                  pl.BlockSpec((1, tm, D_WIDTH), lambda i, j: (i, j, 0)),
                  pl.BlockSpec((1, POOL_HALO, D_WIDTH), lambda i, j: (i, jnp.minimum((j + 1) * per, last), 0)),
                  pl.BlockSpec((D_WIDTH, D_WIDTH), lambda i, j: (0, 0)),
                  pl.BlockSpec((1, D_WIDTH), lambda i, j: (0, 0))],
        out_specs=pl.BlockSpec((1, tm, D_WIDTH), lambda i, j: (i, j, 0)),
        out_shape=jax.ShapeDtypeStruct((b, n, D_WIDTH), BF16),
        scratch_shapes=[pltpu.VMEM((tm + 2 * POOL_HALO, D_WIDTH), F32)],
        compiler_params=_cparams("parallel", "parallel"),
        name="pool",
    )(u, u, u, wbd, pool_scale.reshape(1, D_WIDTH))


def _trunk(x, mem, rel_bias, ab_w_in, ab_fnet_g, ab_fnet_w, ab_w_out,
           cd_w_in, cd_q_norm, cd_k_norm, cd_pool_w, cd_pool_scale, cd_w_out,
           xa_w_q, xa_w_kv, xa_w_o, ffn_w_in, ffn_w_out, ln_g, ln_b):
    for layer in range(DEPTH):
        i = layer // 2
        if layer % 2 == 0:
            w_in = ab_w_in[i]
            w_in = jnp.concatenate([w_in[:, :A_WIDTH] * (HEAD_DIM ** -0.5 * LOG2E), w_in[:, A_WIDTH:]], axis=1)
            qkv, u = _proj_ab(x, w_in.astype(BF16))
            outs = [_dilated(qkv, rel_bias, d) for _, d in A_PATTERNS]
            o_a = _mixture([o for o, _ in outs], [l for _, l in outs])
            o_b = _fnet(u, ab_fnet_g[i], ab_fnet_w[i])
            x = _outproj(o_a, o_b, ab_w_out[i], x, ln_g[layer, 0], ln_b[layer, 0], a_transposed=False)
        else:
            qt, k, vt, u = _proj_cd(x, cd_w_in[i], cd_q_norm[i], cd_k_norm[i])
            o_c = _gqa(qt, k, vt)
            o_d = _pool(u, cd_pool_w[i], cd_pool_scale[i])
            x = _outproj(o_c, o_d, cd_w_out[i], x, ln_g[layer, 0], ln_b[layer, 0], a_transposed=True)
        x = _xattn(x, mem, xa_w_q[layer], xa_w_kv[layer], xa_w_o[layer], ln_g[layer, 1], ln_b[layer, 1])
        x = _swiglu(x, ffn_w_in[layer], ffn_w_out[layer], ln_g[layer, 2], ln_b[layer, 2])
    return x


def kernel(x_prompt, x_sample, mem_prompt, mem_sample, rel_bias, ab_w_in, ab_fnet_g, ab_fnet_w, ab_w_out, cd_w_in, cd_q_norm, cd_k_norm, cd_pool_w, cd_pool_scale, cd_w_out, xa_w_q, xa_w_kv, xa_w_o, ffn_w_in, ffn_w_out, ln_g, ln_b):
    params = (rel_bias, ab_w_in, ab_fnet_g, ab_fnet_w, ab_w_out,
              cd_w_in, cd_q_norm, cd_k_norm, cd_pool_w, cd_pool_scale, cd_w_out,
              xa_w_q, xa_w_kv, xa_w_o, ffn_w_in, ffn_w_out, ln_g, ln_b)
    return (_trunk(x_prompt, mem_prompt, *params), _trunk(x_sample, mem_sample, *params))
```

```python
import functools
import math

import numpy as np
import jax
import jax.numpy as jnp
from jax import lax
from jax.experimental import pallas as pl
from jax.experimental.pallas import tpu as pltpu

F32 = jnp.float32
BF16 = jnp.bfloat16

D_MODEL = 1024
HEAD_DIM = 64
GRID_W = 64
LN_EPS = 1e-5
RMS_EPS = 1e-6
NEG_INF = -1e30
DEPTH = 2
A_HEADS = 12
A_WIDTH = A_HEADS * HEAD_DIM
A_PATTERNS = ((128, 1), (512, 4), (2048, 16))
A_HALF = 64
N_BUCKETS = 32
REL_MAX_DIST = 1024
B_WIDTH = 256
C_Q_HEADS = 12
C_KV_HEADS = 4
C_REP = C_Q_HEADS // C_KV_HEADS
C_WIDTH = C_Q_HEADS * HEAD_DIM
C_KV_WIDTH = C_KV_HEADS * HEAD_DIM
ROPE_THETA = 10000.0
POOL_WINDOWS = (2, 4, 8, 16)
POOL_HALO = 8
D_WIDTH = 256
XA_HEADS = 4
XA_HEAD_DIM = D_MODEL // XA_HEADS
FFN_HIDDEN = 2816
DN_ALPHA = (2 * DEPTH) ** 0.25
LOG2E = 1.4426950408889634

VMEM_LIMIT = 56 * 1024 * 1024
TM = 512
FFT_N2 = 128
DIL_VROWS = HEAD_DIM + 16
DIL_AHEAD = 3
GQA_TQ = 1024
GQA_TW = 256
GQA_TKC = 256
GQA_VROWS = HEAD_DIM + 16
GQA_UNROLL = 8
GQA_AHEAD = 5
GQA_BOUND_SLACK = 1.0 + 2.0 ** -10
GQA_BOUND_LIMIT = 60.0


def _cparams(*sem):
    return pltpu.CompilerParams(dimension_semantics=sem, vmem_limit_bytes=VMEM_LIMIT)


def _dot(a, b):
    return jnp.dot(a, b, preferred_element_type=F32)


def _dot_nt(a, b):
    return lax.dot_general(a, b, (((1,), (1,)), ((), ())), preferred_element_type=F32)


def _dot_tn(a, b):
    return lax.dot_general(a, b, (((0,), (0,)), ((), ())), preferred_element_type=F32)


def _split(x):
    hi = x.astype(BF16)
    lo = (x - hi.astype(F32)).astype(BF16)
    return hi, lo


def _dot3(ah, al, bh, bl):
    return _dot(ah, bh) + _dot(al, bh) + _dot(ah, bl)


def _np_split(x):
    x = np.asarray(x, np.float64)
    hi = jnp.asarray(x, F32).astype(BF16)
    lo = (jnp.asarray(x, F32) - hi.astype(F32)).astype(BF16)
    return hi, lo


def _layer_norm(h, g, b):
    mu = jnp.mean(h, axis=-1, keepdims=True)
    xc = h - mu
    var = jnp.mean(xc * xc, axis=-1, keepdims=True)
    return xc * lax.rsqrt(var + LN_EPS) * g + b


def _proj_ab_kernel(x_ref, w_ref, qkv_ref, u_ref):
    xb = x_ref[0].astype(BF16)
    for c in range(0, 3 * A_WIDTH, 256):
        qkv_ref[0, :, c:c + 256] = _dot(xb, w_ref[:, c:c + 256]).astype(BF16)
    u_ref[0] = _dot(xb, w_ref[:, 3 * A_WIDTH:])


def _proj_ab(x, w):
    b, n, _ = x.shape
    return pl.pallas_call(
        _proj_ab_kernel,
        grid=(b, n // TM),
        in_specs=[pl.BlockSpec((1, TM, D_MODEL), lambda i, j: (i, j, 0)),
                  pl.BlockSpec(w.shape, lambda i, j: (0, 0))],
        out_specs=[pl.BlockSpec((1, TM, 3 * A_WIDTH), lambda i, j: (i, j, 0)),
                   pl.BlockSpec((1, TM, B_WIDTH), lambda i, j: (i, j, 0))],
        out_shape=[jax.ShapeDtypeStruct((b, n, 3 * A_WIDTH), BF16),
                   jax.ShapeDtypeStruct((b, n, B_WIDTH), F32)],
        compiler_params=_cparams("parallel", "parallel"),
        name="proj_ab",
    )(x, w)


def _t5_bucket_np(rel):
    nb = N_BUCKETS // 2
    max_exact = nb // 2
    ret = np.where(rel > 0, nb, 0)
    n = np.abs(rel)
    nf = np.maximum(n, 1).astype(np.float32)
    large = max_exact + (np.log(nf / max_exact) / math.log(REL_MAX_DIST / max_exact)
                         * (nb - max_exact)).astype(np.int32)
    large = np.minimum(large, nb - 1)
    return ret + np.where(n < max_exact, n, large)


def _band_bias(rel_bias, dilation, tq):
    tk = tq + 2 * A_HALF
    band = 2 * A_HALF + 1
    bucket = _t5_bucket_np((np.arange(band) - A_HALF) * dilation)
    row = (rel_bias[jnp.asarray(bucket)].T * LOG2E).astype(F32)
    row = jnp.concatenate([row, jnp.full((A_HEADS, tk + 1 - band), NEG_INF, F32)], axis=1)
    bias = jnp.tile(row, (1, tq))[:, :tq * tk].reshape(A_HEADS, tq, tk)
    bias = bias.transpose(0, 2, 1)
    key = np.arange(tk)[None, :, None]
    before = jnp.asarray(key < A_HALF)
    after = jnp.asarray(key >= A_HALF + tq)
    first = jnp.where(before, NEG_INF, bias)
    return jnp.stack([bias, first, jnp.where(after, NEG_INF, bias), jnp.where(after, NEG_INF, first)])


def _dilated_kernel(q_ref, kp_ref, kc_ref, kn_ref, vp_ref, vc_ref, vn_ref, bias_ref,
                    o_ref, lse_ref, kbuf, vbuf, *, tq):
    tk = tq + 2 * A_HALF
    kbuf[0:A_HALF, :] = kp_ref[0]
    kbuf[A_HALF:A_HALF + tq, :] = kc_ref[0]
    kbuf[A_HALF + tq:, :] = kn_ref[0]
    vbuf[0:A_HALF, :] = vp_ref[0]
    vbuf[A_HALF:A_HALF + tq, :] = vc_ref[0]
    vbuf[A_HALF + tq:, :] = vn_ref[0]

    row_low = lax.broadcasted_iota(jnp.int32, (128, 1), 0) < HEAD_DIM
    row16 = lax.broadcasted_iota(jnp.int32, (16, 1), 0)
    ones = jnp.ones((DIL_VROWS - HEAD_DIM, tk), BF16)
    qts, vts = {}, {}

    def scores(h):
        pair, half = divmod(h, 2)
        cols = slice(pair * 128, (pair + 1) * 128)
        if pair not in qts:
            qts[pair] = q_ref[0, :, cols].astype(F32).T
        qtm = jnp.where(row_low if half == 0 else jnp.logical_not(row_low), qts[pair], 0.0).astype(BF16)
        return _dot(kbuf[:, cols], qtm) + bias_ref[0, h]

    pending = [scores(h) for h in range(DIL_AHEAD)]
    lse_t = jnp.zeros((16, tq), F32)
    o_low = None
    for h in range(A_HEADS):
        st = pending.pop(0)
        if h + DIL_AHEAD < A_HEADS:
            pending.append(scores(h + DIL_AHEAD))
        pair, half = divmod(h, 2)
        cols = slice(pair * 128, (pair + 1) * 128)
        m = jnp.max(st, axis=0, keepdims=True)
        p = jnp.exp2(st - m).astype(BF16)
        if pair not in vts:
            vts[pair] = vbuf[:, cols].astype(F32).T.astype(BF16)
        vaug = jnp.concatenate([vts[pair][half * HEAD_DIM:(half + 1) * HEAD_DIM], ones], axis=0)
        ot = _dot(vaug, p)
        l = ot[HEAD_DIM:HEAD_DIM + 1]
        o_h = ot[:HEAD_DIM] / l
        lse_t = jnp.where(row16 == h, m + jnp.log2(l), lse_t)
        if half == 0:
            o_low = o_h
        else:
            o_ref[0, :, cols] = jnp.concatenate([o_low, o_h], axis=0).T.astype(BF16)
    lse_ref[0] = jnp.concatenate([lse_t, jnp.zeros((128 - 16, tq), F32)], axis=0).T


def _dilated(qkv, rel_bias, dilation):
    b, n, _ = qkv.shape
    seq = n // dilation
    tq = min(256, seq)
    tk = tq + 2 * A_HALF
    w = 3 * A_WIDTH
    view = qkv.reshape(b, seq, dilation * w)
    bias = _band_bias(rel_bias, dilation, tq)
    per = tq // A_HALF
    last = seq // A_HALF - 1
    n_tiles = seq // tq

    def cur(c):
        return pl.BlockSpec((1, tq, A_WIDTH), lambda bi, i, r: (bi, i, 3 * r + c))

    def prev(c):
        return pl.BlockSpec((1, A_HALF, A_WIDTH),
                            lambda bi, i, r: (bi, jnp.maximum(i * per - 1, 0), 3 * r + c))

    def nxt(c):
        return pl.BlockSpec((1, A_HALF, A_WIDTH),
                            lambda bi, i, r: (bi, jnp.minimum((i + 1) * per, last), 3 * r + c))

    o, lse = pl.pallas_call(
        functools.partial(_dilated_kernel, tq=tq),
        grid=(b, seq // tq, dilation),
        in_specs=[cur(0), prev(1), cur(1), nxt(1), prev(2), cur(2), nxt(2),
                  pl.BlockSpec((1, A_HEADS, tk, tq),
                               lambda bi, i, r: ((i == 0) + 2 * (i == n_tiles - 1), 0, 0, 0))],
        out_specs=[pl.BlockSpec((1, tq, A_WIDTH), lambda bi, i, r: (bi, i, r)),
                   pl.BlockSpec((1, tq, 128), lambda bi, i, r: (bi, i, r))],
        out_shape=[jax.ShapeDtypeStruct((b, seq, dilation * A_WIDTH), BF16),
                   jax.ShapeDtypeStruct((b, seq, dilation * 128), F32)],
        scratch_shapes=[pltpu.VMEM((tk, A_WIDTH), BF16), pltpu.VMEM((tk, A_WIDTH), BF16)],
        compiler_params=_cparams("parallel", "parallel", "parallel"),
        name=f"dilated_d{dilation}",
    )(view, view, view, view, view, view, view, bias)
    return o.reshape(b, n, A_WIDTH), lse.reshape(b, n, 128)


def _mixture_kernel(o1_ref, o2_ref, o3_ref, l1_ref, l2_ref, l3_ref, out_ref):
    ls = [l1_ref[0], l2_ref[0], l3_ref[0]]
    mx = jnp.maximum(jnp.maximum(ls[0], ls[1]), ls[2])
    es = [jnp.exp2(l - mx) for l in ls]
    inv = 1.0 / (es[0] + es[1] + es[2])
    ws = [e * inv for e in es]
    low = lax.broadcasted_iota(jnp.int32, (1, 128), 1) < HEAD_DIM
    o_refs = (o1_ref, o2_ref, o3_ref)
    for pair in range(A_HEADS // 2):
        cols = slice(pair * 128, (pair + 1) * 128)
        acc = None
        for g in range(3):
            wexp = jnp.where(low, ws[g][:, 2 * pair:2 * pair + 1], ws[g][:, 2 * pair + 1:2 * pair + 2])
            t = wexp * o_refs[g][0, :, cols].astype(F32)
            acc = t if acc is None else acc + t
        out_ref[0, :, cols] = acc.astype(BF16)


def _mixture(os_, ls_):
    b, n, _ = os_[0].shape
    ospec = pl.BlockSpec((1, TM, A_WIDTH), lambda i, j: (i, j, 0))
    lspec = pl.BlockSpec((1, TM, 128), lambda i, j: (i, j, 0))
    return pl.pallas_call(
        _mixture_kernel,
        grid=(b, n // TM),
        in_specs=[ospec] * 3 + [lspec] * 3,
        out_specs=ospec,
        out_shape=jax.ShapeDtypeStruct((b, n, A_WIDTH), BF16),
        compiler_params=_cparams("parallel", "parallel"),
        name="dilated_mixture",
    )(*os_, *ls_)


def _group_mean_matrix():
    g = np.kron(np.eye(4), np.full((64, 64), 1.0 / 64))
    return jnp.asarray(g, BF16)


def _fnet_pre_kernel(u_ref, gm_ref, gain_ref, ch_ref, cl_ref, sh_ref, sl_ref, yr_ref, yi_ref):
    u = u_ref[0]
    gm = gm_ref[...]
    uh, ul = _split(u)
    mean = _dot(uh, gm) + _dot(ul, gm)
    xc = u - mean
    sh, sl = _split(xc * xc)
    var = _dot(sh, gm) + _dot(sl, gm)
    un = xc * lax.rsqrt(var + LN_EPS) * gain_ref[...]
    nh, nl = _split(un)
    yr_ref[0] = _dot3(nh, nl, ch_ref[...], cl_ref[...])
    yi_ref[0] = -_dot3(nh, nl, sh_ref[...], sl_ref[...])


def _fnet_stage1_kernel(yr_ref, yi_ref, a1h_ref, a1l_ref, a2h_ref, a2l_ref, zr_ref, zi_ref):
    y = jnp.concatenate([yr_ref[0], yi_ref[0]], axis=0)
    yh, yl = _split(y)
    zr_ref[0] = _dot3(a1h_ref[...], a1l_ref[...], yh, yl)
    zi_ref[0] = _dot3(a2h_ref[...], a2l_ref[...], yh, yl)


def _fnet_stage2_kernel(zr_ref, zi_ref, tch_ref, tcl_ref, tsh_ref, tsl_ref, w_ref, o_ref, *, kc, scale):
    for kk in range(kc):
        rh, rl = _split(zr_ref[0, kk])
        ih, il = _split(zi_ref[0, kk])
        f = _dot3(tch_ref[kk], tcl_ref[kk], rh, rl) + _dot3(tsh_ref[kk], tsl_ref[kk], ih, il)
        f = f * scale
        o_ref[0, :, kk * B_WIDTH:(kk + 1) * B_WIDTH] = _dot(f.astype(BF16), w_ref[...]).astype(BF16)


def _fnet_tables(n):
    n2 = FFT_N2
    n1 = n // n2
    c = np.arange(64)
    ang = 2 * np.pi * np.outer(c, c) / 64
    cbd = np.kron(np.eye(4), np.cos(ang))
    sbd = np.kron(np.eye(4), np.sin(ang))
    k1 = np.arange(n1)
    ang1 = 2 * np.pi * np.outer(k1, k1) / n1
    c1, s1 = np.cos(ang1), np.sin(ang1)
    a1 = np.concatenate([c1, s1], axis=1)
    a2 = np.concatenate([-s1, c1], axis=1)
    k2 = np.arange(n2)
    npr = k1[:, None, None] + n1 * k2[None, :, None]
    prod = (npr * k2[None, None, :]) % n
    ang2 = 2 * np.pi * prod / n
    return (_np_split(cbd), _np_split(sbd), _np_split(a1), _np_split(a2),
            _np_split(np.cos(ang2)), _np_split(np.sin(ang2)))


def _fnet(u, fnet_g, fnet_w):
    b, n, _ = u.shape
    n2 = FFT_N2
    n1 = n // n2
    (ch, cl), (sh, sl), (a1h, a1l), (a2h, a2l), (tch, tcl), (tsh, tsl) = _fnet_tables(n)
    gain = fnet_g.reshape(1, B_WIDTH)
    wbd = jax.scipy.linalg.block_diag(*[fnet_w[g] for g in range(4)]).astype(BF16)

    tok = pl.BlockSpec((1, TM, B_WIDTH), lambda i, j: (i, j, 0))
    mat = pl.BlockSpec((B_WIDTH, B_WIDTH), lambda i, j: (0, 0))
    yr, yi = pl.pallas_call(
        _fnet_pre_kernel,
        grid=(b, n // TM),
        in_specs=[tok, mat, pl.BlockSpec((1, B_WIDTH), lambda i, j: (0, 0)), mat, mat, mat, mat],
        out_specs=[tok, tok],
        out_shape=[jax.ShapeDtypeStruct((b, n, B_WIDTH), F32)] * 2,
        compiler_params=_cparams("parallel", "parallel"),
        name="fnet_pre",
    )(u, _group_mean_matrix(), gain, ch, cl, sh, sl)

    cols = n2 * B_WIDTH
    tc = 2048
    yspec = pl.BlockSpec((1, n1, tc), lambda i, j: (i, 0, j))
    aspec = pl.BlockSpec((n1, 2 * n1), lambda i, j: (0, 0))
    zr, zi = pl.pallas_call(
        _fnet_stage1_kernel,
        grid=(b, cols // tc),
        in_specs=[yspec, yspec, aspec, aspec, aspec, aspec],
        out_specs=[yspec, yspec],
        out_shape=[jax.ShapeDtypeStruct((b, n1, cols), F32)] * 2,
        compiler_params=_cparams("parallel", "parallel"),
        name="fnet_stage1",
    )(yr.reshape(b, n1, cols), yi.reshape(b, n1, cols), a1h, a1l, a2h, a2l)

    kc = 8
    zspec = pl.BlockSpec((1, kc, n2, B_WIDTH), lambda i, j: (i, j, 0, 0))
    tspec = pl.BlockSpec((kc, n2, n2), lambda i, j: (j, 0, 0))
    out = pl.pallas_call(
        functools.partial(_fnet_stage2_kernel, kc=kc, scale=1.0 / math.sqrt(64.0 * n)),
        grid=(b, n1 // kc),
        in_specs=[zspec, zspec, tspec, tspec, tspec, tspec,
                  pl.BlockSpec((B_WIDTH, B_WIDTH), lambda i, j: (0, 0))],
        out_specs=pl.BlockSpec((1, n2, kc * B_WIDTH), lambda i, j: (i, 0, j)),
        out_shape=jax.ShapeDtypeStruct((b, n2, n1 * B_WIDTH), BF16),
        compiler_params=_cparams("parallel", "parallel"),
        name="fnet_stage2",
    )(zr.reshape(b, n1, n2, B_WIDTH), zi.reshape(b, n1, n2, B_WIDTH), tch, tcl, tsh, tsl, wbd)
    return out.reshape(b, n, B_WIDTH)


def _outproj_kernel(a_ref, c_ref, wa_ref, wc_ref, x_ref, g_ref, b_ref, o_ref, *, a_transposed):
    if a_transposed:
        h = _dot_tn(a_ref[0], wa_ref[...])
    else:
        h = _dot(a_ref[0], wa_ref[...])
    h = h + _dot(c_ref[0], wc_ref[...])
    o_ref[0] = _layer_norm(DN_ALPHA * x_ref[0] + h, g_ref[...], b_ref[...])


def _outproj(a, c, w, x, g, bias, a_transposed):
    b, n, _ = x.shape
    ka = w.shape[0] - c.shape[-1]
    wa = w[:ka].astype(BF16)
    wc = w[ka:].astype(BF16)
    if a_transposed:
        aspec = pl.BlockSpec((1, ka, TM), lambda i, j: (i, 0, j))
    else:
        aspec = pl.BlockSpec((1, TM, ka), lambda i, j: (i, j, 0))
    vec = pl.BlockSpec((1, D_MODEL), lambda i, j: (0, 0))
    xspec = pl.BlockSpec((1, TM, D_MODEL), lambda i, j: (i, j, 0))
    return pl.pallas_call(
        functools.partial(_outproj_kernel, a_transposed=a_transposed),
        grid=(b, n // TM),
        in_specs=[aspec, pl.BlockSpec((1, TM, c.shape[-1]), lambda i, j: (i, j, 0)),
                  pl.BlockSpec(wa.shape, lambda i, j: (0, 0)),
                  pl.BlockSpec(wc.shape, lambda i, j: (0, 0)), xspec, vec, vec],
        out_specs=xspec,
        out_shape=jax.ShapeDtypeStruct((b, n, D_MODEL), F32),
        compiler_params=_cparams("parallel", "parallel"),
        name="outproj_ln",
    )(a, c, wa, wc, x, g.reshape(1, D_MODEL), bias.reshape(1, D_MODEL))


def _mem_kv_kernel(mem_ref, wkt_ref, wv_ref, kt_ref, v_ref):
    m = mem_ref[0].astype(BF16)
    kt_ref[0] = _dot_nt(wkt_ref[...], m).astype(BF16)
    v_ref[0] = _dot(m, wv_ref[...]).astype(BF16)


def _mem_kv(mem, w_kv):
    b, m, _ = mem.shape
    wkt = w_kv[:, :D_MODEL].T.astype(BF16)
    wv = w_kv[:, D_MODEL:].astype(BF16)
    wspec = pl.BlockSpec((D_MODEL, D_MODEL), lambda i: (0, 0))
    return pl.pallas_call(
        _mem_kv_kernel,
        grid=(b,),
        in_specs=[pl.BlockSpec((1, m, D_MODEL), lambda i: (i, 0, 0)), wspec, wspec],
        out_specs=[pl.BlockSpec((1, D_MODEL, m), lambda i: (i, 0, 0)),
                   pl.BlockSpec((1, m, D_MODEL), lambda i: (i, 0, 0))],
        out_shape=[jax.ShapeDtypeStruct((b, D_MODEL, m), BF16),
                   jax.ShapeDtypeStruct((b, m, D_MODEL), BF16)],
        compiler_params=_cparams("parallel"),
        name="mem_kv",
    )(mem, wkt, wv)


def _xattn_kernel(x_ref, wq_ref, kt_ref, v_ref, wo_ref, g_ref, b_ref, o_ref):
    x = x_ref[0]
    xb = x.astype(BF16)
    acc = None
    for h in range(XA_HEADS):
        cols = slice(h * XA_HEAD_DIM, (h + 1) * XA_HEAD_DIM)
        q = _dot(xb, wq_ref[:, cols]).astype(BF16)
        s = _dot(q, kt_ref[0, cols, :])
        m = jnp.max(s, axis=-1, keepdims=True)
        e = jnp.exp(s - m)
        l = jnp.sum(e, axis=-1, keepdims=True)
        o = (_dot(e.astype(BF16), v_ref[0, :, cols]) / l).astype(BF16)
        t = _dot(o, wo_ref[cols, :])
        acc = t if acc is None else acc + t
    o_ref[0] = _layer_norm(DN_ALPHA * x + acc, g_ref[...], b_ref[...])


def _xattn(x, mem, w_q, w_kv, w_o, g, bias):
    b, n, _ = x.shape
    m = mem.shape[1]
    kt, v = _mem_kv(mem, w_kv)
    wq = (w_q * (XA_HEAD_DIM ** -0.5)).astype(BF16)
    wo = w_o.astype(BF16)
    xspec = pl.BlockSpec((1, TM, D_MODEL), lambda i, j: (i, j, 0))
    wspec = pl.BlockSpec((D_MODEL, D_MODEL), lambda i, j: (0, 0))
    vec = pl.BlockSpec((1, D_MODEL), lambda i, j: (0, 0))
    return pl.pallas_call(
        _xattn_kernel,
        grid=(b, n // TM),
        in_specs=[xspec, wspec, pl.BlockSpec((1, D_MODEL, m), lambda i, j: (i, 0, 0)),
                  pl.BlockSpec((1, m, D_MODEL), lambda i, j: (i, 0, 0)), wspec, vec, vec],
        out_specs=xspec,
        out_shape=jax.ShapeDtypeStruct((b, n, D_MODEL), F32),
        compiler_params=_cparams("parallel", "parallel"),
        name="xattn_ln",
    )(x, wq, kt, v, wo, g.reshape(1, D_MODEL), bias.reshape(1, D_MODEL))


FFN_TH = 1408


def _swiglu_kernel(x_ref, wg_ref, wu_ref, wo_ref, g_ref, b_ref, o_ref, acc_ref):
    j = pl.program_id(2)
    xb = x_ref[0].astype(BF16)
    gate = _dot(xb, wg_ref[...])
    up = _dot(xb, wu_ref[...])
    hid = (gate * (1.0 / (1.0 + jnp.exp(-gate))) * up).astype(BF16)
    part = _dot(hid, wo_ref[...])

    @pl.when(j == 0)
    def _():
        acc_ref[...] = part

    @pl.when(j > 0)
    def _():
        acc_ref[...] += part

    @pl.when(j == pl.num_programs(2) - 1)
    def _():
        o_ref[0] = _layer_norm(DN_ALPHA * x_ref[0] + acc_ref[...], g_ref[...], b_ref[...])


def _swiglu(x, w_in, w_out, g, bias):
    b, n, _ = x.shape
    nj = FFN_HIDDEN // FFN_TH
    win = w_in.astype(BF16)
    wout = w_out.astype(BF16)
    xspec = pl.BlockSpec((1, TM, D_MODEL), lambda i, t, j: (i, t, 0))
    vec = pl.BlockSpec((1, D_MODEL), lambda i, t, j: (0, 0))
    return pl.pallas_call(
        _swiglu_kernel,
        grid=(b, n // TM, nj),
        in_specs=[xspec,
                  pl.BlockSpec((D_MODEL, FFN_TH), lambda i, t, j: (0, j)),
                  pl.BlockSpec((D_MODEL, FFN_TH), lambda i, t, j: (0, j + nj)),
                  pl.BlockSpec((FFN_TH, D_MODEL), lambda i, t, j: (j, 0)), vec, vec],
        out_specs=xspec,
        out_shape=jax.ShapeDtypeStruct((b, n, D_MODEL), F32),
        scratch_shapes=[pltpu.VMEM((TM, D_MODEL), F32)],
        compiler_params=_cparams("parallel", "parallel", "arbitrary"),
        name="swiglu_ln",
    )(x, win, win, wout, g.reshape(1, D_MODEL), bias.reshape(1, D_MODEL))


N_QK_HEADS = C_Q_HEADS + C_KV_HEADS
QK_ROWS = N_QK_HEADS * HEAD_DIM


def _proj_cd_kernel(x_ref, wt_ref, wu_ref, gain_ref, cos_ref, sin_ref,
                    qt_ref, k_ref, vt_ref, u_ref, kn_ref, *, tm):
    xb = x_ref[0].astype(BF16)
    u_ref[0] = _dot(xb, wu_ref[...])
    zt = _dot_nt(wt_ref[...], xb)
    z = zt[:QK_ROWS].reshape(N_QK_HEADS, HEAD_DIM, tm)
    ssq = jnp.sum(z * z, axis=1, keepdims=True)
    zn = z * lax.rsqrt(ssq * (1.0 / HEAD_DIM) + RMS_EPS) * gain_ref[...]
    half = HEAD_DIM // 2
    x1 = zn[:, :half]
    x2 = zn[:, half:]
    c = cos_ref[...][None]
    s = sin_ref[...][None]
    rot = jnp.concatenate([x1 * c - x2 * s, x1 * s + x2 * c], axis=1).reshape(QK_ROWS, tm)
    qt_ref[0] = rot[:C_WIDTH].astype(BF16)
    kb = rot[C_WIDTH:].astype(BF16)
    k_ref[0] = kb.astype(F32).T.astype(BF16)
    kf = kb.astype(F32).reshape(C_KV_HEADS, HEAD_DIM, tm)
    kn_ref[0] = jnp.sum(kf * kf, axis=1, keepdims=True)
    vt = zt[QK_ROWS:].astype(BF16)
    for c0 in range(tm // GQA_TKC):
        vt_ref[0, c0] = vt[:, c0 * GQA_TKC:(c0 + 1) * GQA_TKC]


def _rope_tables_t(n):
    rows = n // GRID_W
    row_id = jnp.broadcast_to(jnp.arange(rows)[:, None], (rows, GRID_W)).reshape(n)
    col_id = jnp.broadcast_to(jnp.arange(GRID_W)[None, :], (rows, GRID_W)).reshape(n)
    axis_dim = HEAD_DIM // 2
    freqs = ROPE_THETA ** (-jnp.arange(0, axis_dim, 2, dtype=F32) / axis_dim)
    ang = jnp.concatenate([row_id[:, None] * freqs, col_id[:, None] * freqs], axis=-1)
    return jnp.cos(ang).T, jnp.sin(ang).T


def _proj_cd(x, w_in, q_norm, k_norm):
    b, n, _ = x.shape
    tm = TM
    half = HEAD_DIM // 2
    wqk = w_in[:, :QK_ROWS].reshape(D_MODEL, N_QK_HEADS, half, 2)
    wqk = wqk.transpose(1, 3, 2, 0).reshape(QK_ROWS, D_MODEL)
    wt = jnp.concatenate([wqk, w_in[:, QK_ROWS:QK_ROWS + C_KV_WIDTH].T], axis=0).astype(BF16)
    wu = w_in[:, QK_ROWS + C_KV_WIDTH:].astype(BF16)
    qg = q_norm.reshape(half, 2).T.reshape(HEAD_DIM) * (HEAD_DIM ** -0.5 * LOG2E)
    kg = k_norm.reshape(half, 2).T.reshape(HEAD_DIM)
    gain = jnp.concatenate([jnp.tile(qg[None], (C_Q_HEADS, 1)), jnp.tile(kg[None], (C_KV_HEADS, 1))])
    gain = gain.reshape(N_QK_HEADS, HEAD_DIM, 1).astype(F32)
    cos_t, sin_t = _rope_tables_t(n)
    nc = n // GQA_TKC
    return pl.pallas_call(
        functools.partial(_proj_cd_kernel, tm=tm),
        grid=(b, n // tm),
        in_specs=[pl.BlockSpec((1, tm, D_MODEL), lambda i, j: (i, j, 0)),
                  pl.BlockSpec(wt.shape, lambda i, j: (0, 0)),
                  pl.BlockSpec(wu.shape, lambda i, j: (0, 0)),
                  pl.BlockSpec(gain.shape, lambda i, j: (0, 0, 0)),
                  pl.BlockSpec((HEAD_DIM // 2, tm), lambda i, j: (0, j)),
                  pl.BlockSpec((HEAD_DIM // 2, tm), lambda i, j: (0, j))],
        out_specs=[pl.BlockSpec((1, C_WIDTH, tm), lambda i, j: (i, 0, j)),
                   pl.BlockSpec((1, tm, C_KV_WIDTH), lambda i, j: (i, j, 0)),
                   pl.BlockSpec((1, tm // GQA_TKC, C_KV_WIDTH, GQA_TKC), lambda i, j: (i, j, 0, 0)),
                   pl.BlockSpec((1, tm, D_WIDTH), lambda i, j: (i, j, 0)),
                   pl.BlockSpec((1, C_KV_HEADS, 1, tm), lambda i, j: (i, 0, 0, j))],
        out_shape=[jax.ShapeDtypeStruct((b, C_WIDTH, n), BF16),
                   jax.ShapeDtypeStruct((b, n, C_KV_WIDTH), BF16),
                   jax.ShapeDtypeStruct((b, nc, C_KV_WIDTH, GQA_TKC), BF16),
                   jax.ShapeDtypeStruct((b, n, D_WIDTH), F32),
                   jax.ShapeDtypeStruct((b, C_KV_HEADS, 1, n), F32)],
        compiler_params=_cparams("parallel", "parallel"),
        name="proj_cd",
    )(x, wt, wu, gain, cos_t, sin_t)


def _gqa_kernel(qt_ref, k_ref, vt_ref, kn_ref, ot_ref, m_ref, acc_ref, qpad_ref, *, tq, n):
    g = pl.program_id(1)
    row_half = lax.broadcasted_iota(jnp.int32, (128, 1), 0) // HEAD_DIM
    mine = row_half == (g % 2)
    k_max2 = jnp.max(kn_ref[0, 0], axis=1, keepdims=True)
    bound_max = jnp.zeros((1, 1), F32)
    for j in range(C_REP):
        qj = qt_ref[0, j * HEAD_DIM:(j + 1) * HEAD_DIM, :]
        q2 = jnp.concatenate([qj, qj], axis=0)
        qpad_ref[j] = jnp.where(mine, q2, jnp.zeros_like(q2))
        qf = qj.astype(F32)
        bound = jnp.sqrt(jnp.sum(qf * qf, axis=0, keepdims=True) * k_max2) * GQA_BOUND_SLACK
        m_ref[j] = bound
        bound_max = jnp.maximum(bound_max, jnp.max(bound, axis=1, keepdims=True))
    fixed_shift = bound_max[0, 0] <= GQA_BOUND_LIMIT
    acc_ref[...] = jnp.zeros(acc_ref.shape, F32)
    ones = jnp.ones((GQA_VROWS - HEAD_DIM, GQA_TKC), BF16)

    def body(c, carry, *, online):
        kchs, vchs = [], []
        for u in range(GQA_UNROLL):
            cc = c * GQA_UNROLL + u
            start = pl.multiple_of(cc * GQA_TKC, GQA_TKC)
            kchs.append(k_ref[0, pl.ds(start, GQA_TKC), :])
            vchs.append(jnp.concatenate([vt_ref[0, cc], ones], axis=0))
        pieces = [(u, j, slice(s * GQA_TW, (s + 1) * GQA_TW))
                  for u in range(GQA_UNROLL) for j in range(C_REP) for s in range(tq // GQA_TW)]

        def scores(i):
            u, j, cols = pieces[i]
            return _dot(kchs[u], qpad_ref[j, :, cols])

        pending = [scores(i) for i in range(GQA_AHEAD)]
        for i, (u, j, cols) in enumerate(pieces):
            st = pending.pop(0)
            if i + GQA_AHEAD < len(pieces):
                pending.append(scores(i + GQA_AHEAD))
            m_old = m_ref[j, :, cols]
            if not online:
                p = jnp.exp2(st - m_old)
                acc_ref[j, :, cols] = acc_ref[j, :, cols] + _dot(vchs[u], p.astype(BF16))
                continue
            m_new = jnp.maximum(m_old, jnp.max(st, axis=0, keepdims=True))
            alpha = jnp.exp2(m_old - m_new)
            p = jnp.exp2(st - m_new)
            acc_ref[j, :, cols] = alpha * acc_ref[j, :, cols] + _dot(vchs[u], p.astype(BF16))
            m_ref[j, :, cols] = m_new
        return carry

    trips = n // (GQA_TKC * GQA_UNROLL)

    @pl.when(fixed_shift)
    def _():
        lax.fori_loop(0, trips, functools.partial(body, online=False), 0)

    @pl.when(jnp.logical_not(fixed_shift))
    def _():
        m_ref[...] = jnp.full(m_ref.shape, NEG_INF, F32)
        lax.fori_loop(0, trips, functools.partial(body, online=True), 0)

    for j in range(C_REP):
        l = acc_ref[j, HEAD_DIM:HEAD_DIM + 1, :]
        ot_ref[0, j * HEAD_DIM:(j + 1) * HEAD_DIM, :] = (acc_ref[j, :HEAD_DIM, :] / l).astype(BF16)


def _gqa(qt, k, vt, kn):
    b, _, n = qt.shape
    tq = min(GQA_TQ, n)
    nc = n // GQA_TKC
    rows = C_REP * HEAD_DIM
    return pl.pallas_call(
        functools.partial(_gqa_kernel, tq=tq, n=n),
        grid=(b, C_KV_HEADS, n // tq),
        in_specs=[pl.BlockSpec((1, rows, tq), lambda i, g, t: (i, g, t)),
                  pl.BlockSpec((1, n, 128), lambda i, g, t: (i, 0, g // 2)),
                  pl.BlockSpec((1, nc, HEAD_DIM, GQA_TKC), lambda i, g, t: (i, 0, g, 0)),
                  pl.BlockSpec((1, 1, 1, n), lambda i, g, t: (i, g, 0, 0))],
        out_specs=pl.BlockSpec((1, rows, tq), lambda i, g, t: (i, g, t)),
        out_shape=jax.ShapeDtypeStruct((b, C_WIDTH, n), BF16),
        scratch_shapes=[pltpu.VMEM((C_REP, 1, tq), F32),
                        pltpu.VMEM((C_REP, GQA_VROWS, tq), F32), pltpu.VMEM((C_REP, 128, tq), BF16)],
        compiler_params=_cparams("parallel", "parallel", "parallel"),
        name="gqa_flash",
    )(qt, k, vt, kn)


def _pool_kernel(up_ref, uc_ref, un_ref, w_ref, scale_ref, o_ref, buf, *, tm, n):
    i = pl.program_id(1)
    cur = uc_ref[0]
    buf[0:POOL_HALO, :] = jnp.where(i > 0, up_ref[0], 0.0)
    buf[POOL_HALO:POOL_HALO + tm, :] = cur
    buf[POOL_HALO + tm:, :] = jnp.where(i < pl.num_programs(1) - 1, un_ref[0], 0.0)
    lane_group = lax.broadcasted_iota(jnp.int32, (1, D_WIDTH), 1) // 64
    half_w = jnp.left_shift(1, lane_group)
    acc = jnp.zeros((tm, D_WIDTH), F32)
    for j in range(-POOL_HALO, POOL_HALO):
        inside = (j >= -half_w) & (j < half_w)
        acc = acc + jnp.where(inside, buf[POOL_HALO + j:POOL_HALO + j + tm, :], 0.0)
    t = i * tm + lax.broadcasted_iota(jnp.int32, (tm, 1), 0)
    cnt = jnp.minimum(t + half_w, n) - jnp.maximum(t - half_w, 0)
    mixed = (acc / cnt.astype(F32) - cur).astype(BF16)
    o_ref[0] = (_dot(mixed, w_ref[...]) * scale_ref[...]).astype(BF16)


def _pool(u, pool_w, pool_scale):
    b, n, _ = u.shape
    tm = TM
    per = tm // POOL_HALO
    last = n // POOL_HALO - 1
    wbd = jax.scipy.linalg.block_diag(*[pool_w[g] for g in range(4)]).astype(BF16)
    return pl.pallas_call(
        functools.partial(_pool_kernel, tm=tm, n=n),
        grid=(b, n // tm),
        in_specs=[pl.BlockSpec((1, POOL_HALO, D_WIDTH), lambda i, j: (i, jnp.maximum(j * per - 1, 0), 0)),
                  pl.BlockSpec((1, tm, D_WIDTH), lambda i, j: (i, j, 0)),
                  pl.BlockSpec((1, POOL_HALO, D_WIDTH), lambda i, j: (i, jnp.minimum((j + 1) * per, last), 0)),
                  pl.BlockSpec((D_WIDTH, D_WIDTH), lambda i, j: (0, 0)),
                  pl.BlockSpec((1, D_WIDTH), lambda i, j: (0, 0))],
        out_specs=pl.BlockSpec((1, tm, D_WIDTH), lambda i, j: (i, j, 0)),
        out_shape=jax.ShapeDtypeStruct((b, n, D_WIDTH), BF16),
        scratch_shapes=[pltpu.VMEM((tm + 2 * POOL_HALO, D_WIDTH), F32)],
        compiler_params=_cparams("parallel", "parallel"),
        name="pool",
    )(u, u, u, wbd, pool_scale.reshape(1, D_WIDTH))


def _trunk(x, mem, rel_bias, ab_w_in, ab_fnet_g, ab_fnet_w, ab_w_out,
           cd_w_in, cd_q_norm, cd_k_norm, cd_pool_w, cd_pool_scale, cd_w_out,
           xa_w_q, xa_w_kv, xa_w_o, ffn_w_in, ffn_w_out, ln_g, ln_b):
    for layer in range(DEPTH):
        i = layer // 2
        if layer % 2 == 0:
            w_in = ab_w_in[i]
            w_in = jnp.concatenate([w_in[:, :A_WIDTH] * (HEAD_DIM ** -0.5 * LOG2E), w_in[:, A_WIDTH:]], axis=1)
            qkv, u = _proj_ab(x, w_in.astype(BF16))
            outs = [_dilated(qkv, rel_bias, d) for _, d in A_PATTERNS]
            o_a = _mixture([o for o, _ in outs], [l for _, l in outs])
            o_b = _fnet(u, ab_fnet_g[i], ab_fnet_w[i])
            x = _outproj(o_a, o_b, ab_w_out[i], x, ln_g[layer, 0], ln_b[layer, 0], a_transposed=False)
        else:
            qt, k, vt, u, kn = _proj_cd(x, cd_w_in[i], cd_q_norm[i], cd_k_norm[i])
            o_c = _gqa(qt, k, vt, kn)
            o_d = _pool(u, cd_pool_w[i], cd_pool_scale[i])
            x = _outproj(o_c, o_d, cd_w_out[i], x, ln_g[layer, 0], ln_b[layer, 0], a_transposed=True)
        x = _xattn(x, mem, xa_w_q[layer], xa_w_kv[layer], xa_w_o[layer], ln_g[layer, 1], ln_b[layer, 1])
        x = _swiglu(x, ffn_w_in[layer], ffn_w_out[layer], ln_g[layer, 2], ln_b[layer, 2])
    return x


def kernel(x_prompt, x_sample, mem_prompt, mem_sample, rel_bias, ab_w_in, ab_fnet_g, ab_fnet_w, ab_w_out, cd_w_in, cd_q_norm, cd_k_norm, cd_pool_w, cd_pool_scale, cd_w_out, xa_w_q, xa_w_kv, xa_w_o, ffn_w_in, ffn_w_out, ln_g, ln_b):
    params = (rel_bias, ab_w_in, ab_fnet_g, ab_fnet_w, ab_w_out,
              cd_w_in, cd_q_norm, cd_k_norm, cd_pool_w, cd_pool_scale, cd_w_out,
              xa_w_q, xa_w_kv, xa_w_o, ffn_w_in, ffn_w_out, ln_g, ln_b)
    return (_trunk(x_prompt, mem_prompt, *params), _trunk(x_sample, mem_sample, *params))
```

```python
import functools
import math

import numpy as np
import jax
import jax.numpy as jnp
from jax import lax
from jax.experimental import pallas as pl
from jax.experimental.pallas import tpu as pltpu

F32 = jnp.float32
BF16 = jnp.bfloat16

D_MODEL = 1024
HEAD_DIM = 64
GRID_W = 64
LN_EPS = 1e-5
RMS_EPS = 1e-6
NEG_INF = -1e30
DEPTH = 2
A_HEADS = 12
A_WIDTH = A_HEADS * HEAD_DIM
A_PATTERNS = ((128, 1), (512, 4), (2048, 16))
A_HALF = 64
QKV_WIDTH = 3 * A_WIDTH
N_BUCKETS = 32
REL_MAX_DIST = 1024
B_WIDTH = 256
C_Q_HEADS = 12
C_KV_HEADS = 4
C_REP = C_Q_HEADS // C_KV_HEADS
C_WIDTH = C_Q_HEADS * HEAD_DIM
C_KV_WIDTH = C_KV_HEADS * HEAD_DIM
ROPE_THETA = 10000.0
POOL_WINDOWS = (2, 4, 8, 16)
POOL_HALO = 8
D_WIDTH = 256
XA_HEADS = 4
XA_HEAD_DIM = D_MODEL // XA_HEADS
FFN_HIDDEN = 2816
DN_ALPHA = (2 * DEPTH) ** 0.25
LOG2E = 1.4426950408889634

VMEM_LIMIT = 56 * 1024 * 1024
TM = 512
FFT_N2 = 128
DIL_VROWS = HEAD_DIM + 16
DIL_AHEAD = 3
GQA_TQ = 1024
GQA_TW = 256
GQA_TKC = 256
GQA_VROWS = HEAD_DIM + 16
GQA_UNROLL = 8
GQA_AHEAD = 5
GQA_BOUND_SLACK = 1.0 + 2.0 ** -10
GQA_BOUND_LIMIT = 60.0


def _cparams(*sem):
    return pltpu.CompilerParams(dimension_semantics=sem, vmem_limit_bytes=VMEM_LIMIT)


def _dot(a, b):
    return jnp.dot(a, b, preferred_element_type=F32)


def _dot_nt(a, b):
    return lax.dot_general(a, b, (((1,), (1,)), ((), ())), preferred_element_type=F32)


def _dot_tn(a, b):
    return lax.dot_general(a, b, (((0,), (0,)), ((), ())), preferred_element_type=F32)


def _split(x):
    hi = x.astype(BF16)
    lo = (x - hi.astype(F32)).astype(BF16)
    return hi, lo


def _dot3(ah, al, bh, bl):
    return _dot(ah, bh) + _dot(al, bh) + _dot(ah, bl)


def _np_split(x):
    x = np.asarray(x, np.float64)
    hi = jnp.asarray(x, F32).astype(BF16)
    lo = (jnp.asarray(x, F32) - hi.astype(F32)).astype(BF16)
    return hi, lo


def _layer_norm(h, g, b):
    mu = jnp.mean(h, axis=-1, keepdims=True)
    xc = h - mu
    var = jnp.mean(xc * xc, axis=-1, keepdims=True)
    return xc * lax.rsqrt(var + LN_EPS) * g + b


def _proj_ab_kernel(x_ref, w_ref, qkv1_ref, qkv4_ref, qkv16_ref, u_ref, zbuf):
    xb = x_ref[0].astype(BF16)
    for c in range(0, QKV_WIDTH, 256):
        z = _dot(xb, w_ref[:, c:c + 256])
        zbuf[c // 128] = z[:, :128]
        zbuf[c // 128 + 1] = z[:, 128:]
        qkv1_ref[0, :, c:c + 256] = z.astype(BF16)
    u_ref[0] = _dot(xb, w_ref[:, QKV_WIDTH:])
    for (_, d), ref in zip(A_PATTERNS[1:], (qkv4_ref, qkv16_ref)):
        rows = TM // d
        for r in range(d):
            for ct in range(QKV_WIDTH // 128):
                col = r * QKV_WIDTH + ct * 128
                ref[0, :, col:col + 128] = zbuf[ct, pl.ds(r, rows, stride=d), :].astype(BF16)


def _proj_ab(x, w):
    b, n, _ = x.shape
    dils = [d for _, d in A_PATTERNS]
    return pl.pallas_call(
        _proj_ab_kernel,
        grid=(b, n // TM),
        in_specs=[pl.BlockSpec((1, TM, D_MODEL), lambda i, j: (i, j, 0)),
                  pl.BlockSpec(w.shape, lambda i, j: (0, 0))],
        out_specs=[pl.BlockSpec((1, TM // d, d * QKV_WIDTH), lambda i, j: (i, j, 0)) for d in dils]
        + [pl.BlockSpec((1, TM, B_WIDTH), lambda i, j: (i, j, 0))],
        out_shape=[jax.ShapeDtypeStruct((b, n // d, d * QKV_WIDTH), BF16) for d in dils]
        + [jax.ShapeDtypeStruct((b, n, B_WIDTH), F32)],
        scratch_shapes=[pltpu.VMEM((QKV_WIDTH // 128, TM, 128), F32)],
        compiler_params=_cparams("parallel", "parallel"),
        name="proj_ab",
    )(x, w)


def _t5_bucket_np(rel):
    nb = N_BUCKETS // 2
    max_exact = nb // 2
    ret = np.where(rel > 0, nb, 0)
    n = np.abs(rel)
    nf = np.maximum(n, 1).astype(np.float32)
    large = max_exact + (np.log(nf / max_exact) / math.log(REL_MAX_DIST / max_exact)
                         * (nb - max_exact)).astype(np.int32)
    large = np.minimum(large, nb - 1)
    return ret + np.where(n < max_exact, n, large)


def _band_bias(rel_bias, dilation, tq):
    tk = tq + 2 * A_HALF
    band = 2 * A_HALF + 1
    bucket = _t5_bucket_np((np.arange(band) - A_HALF) * dilation)
    row = (rel_bias[jnp.asarray(bucket)].T * LOG2E).astype(F32)
    row = jnp.concatenate([row, jnp.full((A_HEADS, tk + 1 - band), NEG_INF, F32)], axis=1)
    bias = jnp.tile(row, (1, tq))[:, :tq * tk].reshape(A_HEADS, tq, tk)
    bias = bias.transpose(0, 2, 1)
    key = np.arange(tk)[None, :, None]
    before = jnp.asarray(key < A_HALF)
    after = jnp.asarray(key >= A_HALF + tq)
    first = jnp.where(before, NEG_INF, bias)
    return jnp.stack([bias, first, jnp.where(after, NEG_INF, bias), jnp.where(after, NEG_INF, first)])


def _dilated_kernel(q_ref, kp_ref, kc_ref, kn_ref, vp_ref, vc_ref, vn_ref, bias_ref,
                    o_ref, lse_ref, kbuf, vbuf, *, tq):
    tk = tq + 2 * A_HALF
    kbuf[0:A_HALF, :] = kp_ref[0]
    kbuf[A_HALF:A_HALF + tq, :] = kc_ref[0]
    kbuf[A_HALF + tq:, :] = kn_ref[0]
    vbuf[0:A_HALF, :] = vp_ref[0]
    vbuf[A_HALF:A_HALF + tq, :] = vc_ref[0]
    vbuf[A_HALF + tq:, :] = vn_ref[0]

    row_low = lax.broadcasted_iota(jnp.int32, (128, 1), 0) < HEAD_DIM
    row16 = lax.broadcasted_iota(jnp.int32, (16, 1), 0)
    ones = jnp.ones((DIL_VROWS - HEAD_DIM, tk), BF16)
    qts, vts = {}, {}

    def scores(h):
        pair, half = divmod(h, 2)
        cols = slice(pair * 128, (pair + 1) * 128)
        if pair not in qts:
            qts[pair] = q_ref[0, :, cols].astype(F32).T
        qtm = jnp.where(row_low if half == 0 else jnp.logical_not(row_low), qts[pair], 0.0).astype(BF16)
        return _dot(kbuf[:, cols], qtm) + bias_ref[0, h]

    pending = [scores(h) for h in range(DIL_AHEAD)]
    lse_t = jnp.zeros((16, tq), F32)
    o_low = None
    for h in range(A_HEADS):
        st = pending.pop(0)
        if h + DIL_AHEAD < A_HEADS:
            pending.append(scores(h + DIL_AHEAD))
        pair, half = divmod(h, 2)
        cols = slice(pair * 128, (pair + 1) * 128)
        m = jnp.max(st, axis=0, keepdims=True)
        p = jnp.exp2(st - m).astype(BF16)
        if pair not in vts:
            vts[pair] = vbuf[:, cols].astype(F32).T.astype(BF16)
        vaug = jnp.concatenate([vts[pair][half * HEAD_DIM:(half + 1) * HEAD_DIM], ones], axis=0)
        ot = _dot(vaug, p)
        l = ot[HEAD_DIM:HEAD_DIM + 1]
        o_h = ot[:HEAD_DIM] / l
        lse_t = jnp.where(row16 == h, m + jnp.log2(l), lse_t)
        if half == 0:
            o_low = o_h
        else:
            o_ref[0, :, cols] = jnp.concatenate([o_low, o_h], axis=0).T.astype(BF16)
    lse_ref[0] = jnp.concatenate([lse_t, jnp.zeros((128 - 16, tq), F32)], axis=0).T


def _dilated(view, rel_bias, dilation):
    b, seq, _ = view.shape
    tq = min(256, seq)
    tk = tq + 2 * A_HALF
    bias = _band_bias(rel_bias, dilation, tq)
    per = tq // A_HALF
    last = seq // A_HALF - 1
    n_tiles = seq // tq

    def cur(c):
        return pl.BlockSpec((1, tq, A_WIDTH), lambda bi, i, r: (bi, i, 3 * r + c))

    def prev(c):
        return pl.BlockSpec((1, A_HALF, A_WIDTH),
                            lambda bi, i, r: (bi, jnp.maximum(i * per - 1, 0), 3 * r + c))

    def nxt(c):
        return pl.BlockSpec((1, A_HALF, A_WIDTH),
                            lambda bi, i, r: (bi, jnp.minimum((i + 1) * per, last), 3 * r + c))

    o, lse = pl.pallas_call(
        functools.partial(_dilated_kernel, tq=tq),
        grid=(b, seq // tq, dilation),
        in_specs=[cur(0), prev(1), cur(1), nxt(1), prev(2), cur(2), nxt(2),
                  pl.BlockSpec((1, A_HEADS, tk, tq),
                               lambda bi, i, r: ((i == 0) + 2 * (i == n_tiles - 1), 0, 0, 0))],
        out_specs=[pl.BlockSpec((1, tq, A_WIDTH), lambda bi, i, r: (bi, i, r)),
                   pl.BlockSpec((1, tq, 128), lambda bi, i, r: (bi, i, r))],
        out_shape=[jax.ShapeDtypeStruct((b, seq, dilation * A_WIDTH), BF16),
                   jax.ShapeDtypeStruct((b, seq, dilation * 128), F32)],
        scratch_shapes=[pltpu.VMEM((tk, A_WIDTH), BF16), pltpu.VMEM((tk, A_WIDTH), BF16)],
        compiler_params=_cparams("parallel", "parallel", "parallel"),
        name=f"dilated_d{dilation}",
    )(view, view, view, view, view, view, view, bias)
    return o, lse


def _mixture_kernel(o1_ref, o4_ref, o16_ref, l1_ref, l4_ref, l16_ref, expand_ref, out_ref, obuf, lbuf):
    for idx, (d, o_ref, l_ref) in enumerate(((A_PATTERNS[1][1], o4_ref, l4_ref), (A_PATTERNS[2][1], o16_ref, l16_ref))):
        rows = TM // d
        for r in range(d):
            for ct in range(A_WIDTH // 128):
                col = r * A_WIDTH + ct * 128
                obuf[idx, ct, pl.ds(r, rows, stride=d), :] = o_ref[0, :, col:col + 128].astype(F32)
            lbuf[idx, pl.ds(r, rows, stride=d), :] = l_ref[0, :, r * 128:(r + 1) * 128]
    ls = [l1_ref[0], lbuf[0], lbuf[1]]
    mx = jnp.maximum(jnp.maximum(ls[0], ls[1]), ls[2])
    es = [jnp.exp2(l - mx) for l in ls]
    inv = 1.0 / (es[0] + es[1] + es[2])
    ws = []
    for e in es:
        wh, wl = _split(e * inv)
        ws.append(_dot(wh, expand_ref[...]) + _dot(wl, expand_ref[...]))
    for pair in range(A_HEADS // 2):
        cols = slice(pair * 128, (pair + 1) * 128)
        os_ = [o1_ref[0, :, cols].astype(F32), obuf[0, pair], obuf[1, pair]]
        acc = None
        for g in range(3):
            t = ws[g][:, cols] * os_[g]
            acc = t if acc is None else acc + t
        out_ref[0, :, cols] = acc.astype(BF16)


def _mixture(os_, ls_):
    b, n, _ = os_[0].shape
    dils = [d for _, d in A_PATTERNS]
    expand = jnp.asarray(np.arange(128)[:, None] == np.arange(A_WIDTH)[None, :] // HEAD_DIM, BF16)
    return pl.pallas_call(
        _mixture_kernel,
        grid=(b, n // TM),
        in_specs=[pl.BlockSpec((1, TM // d, d * A_WIDTH), lambda i, j: (i, j, 0)) for d in dils]
        + [pl.BlockSpec((1, TM // d, d * 128), lambda i, j: (i, j, 0)) for d in dils]
        + [pl.BlockSpec((128, A_WIDTH), lambda i, j: (0, 0))],
        out_specs=pl.BlockSpec((1, TM, A_WIDTH), lambda i, j: (i, j, 0)),
        out_shape=jax.ShapeDtypeStruct((b, n, A_WIDTH), BF16),
        scratch_shapes=[pltpu.VMEM((2, A_WIDTH // 128, TM, 128), F32), pltpu.VMEM((2, TM, 128), F32)],
        compiler_params=_cparams("parallel", "parallel"),
        name="dilated_mixture",
    )(*os_, *ls_, expand)


def _group_mean_matrix():
    g = np.kron(np.eye(4), np.full((64, 64), 1.0 / 64))
    return jnp.asarray(g, BF16)


def _fnet_pre_kernel(u_ref, gm_ref, gain_ref, ch_ref, cl_ref, sh_ref, sl_ref, yr_ref, yi_ref):
    u = u_ref[0]
    gm = gm_ref[...]
    uh, ul = _split(u)
    mean = _dot(uh, gm) + _dot(ul, gm)
    xc = u - mean
    sh, sl = _split(xc * xc)
    var = _dot(sh, gm) + _dot(sl, gm)
    un = xc * lax.rsqrt(var + LN_EPS) * gain_ref[...]
    nh, nl = _split(un)
    yr_ref[0] = _dot3(nh, nl, ch_ref[...], cl_ref[...])
    yi_ref[0] = -_dot3(nh, nl, sh_ref[...], sl_ref[...])


def _fnet_stage1_kernel(yr_ref, yi_ref, a1h_ref, a1l_ref, a2h_ref, a2l_ref, zr_ref, zi_ref):
    y = jnp.concatenate([yr_ref[0], yi_ref[0]], axis=0)
    yh, yl = _split(y)
    zr_ref[0] = _dot3(a1h_ref[...], a1l_ref[...], yh, yl)
    zi_ref[0] = _dot3(a2h_ref[...], a2l_ref[...], yh, yl)


def _fnet_stage2_kernel(zr_ref, zi_ref, tch_ref, tcl_ref, tsh_ref, tsl_ref, w_ref, o_ref, *, kc, scale):
    for kk in range(kc):
        rh, rl = _split(zr_ref[0, kk])
        ih, il = _split(zi_ref[0, kk])
        f = _dot3(tch_ref[kk], tcl_ref[kk], rh, rl) + _dot3(tsh_ref[kk], tsl_ref[kk], ih, il)
        f = f * scale
        o_ref[0, :, kk * B_WIDTH:(kk + 1) * B_WIDTH] = _dot(f.astype(BF16), w_ref[...]).astype(BF16)


def _fnet_tables(n):
    n2 = FFT_N2
    n1 = n // n2
    c = np.arange(64)
    ang = 2 * np.pi * np.outer(c, c) / 64
    cbd = np.kron(np.eye(4), np.cos(ang))
    sbd = np.kron(np.eye(4), np.sin(ang))
    k1 = np.arange(n1)
    ang1 = 2 * np.pi * np.outer(k1, k1) / n1
    c1, s1 = np.cos(ang1), np.sin(ang1)
    a1 = np.concatenate([c1, s1], axis=1)
    a2 = np.concatenate([-s1, c1], axis=1)
    k2 = np.arange(n2)
    npr = k1[:, None, None] + n1 * k2[None, :, None]
    prod = (npr * k2[None, None, :]) % n
    ang2 = 2 * np.pi * prod / n
    return (_np_split(cbd), _np_split(sbd), _np_split(a1), _np_split(a2),
            _np_split(np.cos(ang2)), _np_split(np.sin(ang2)))


def _fnet(u, fnet_g, fnet_w):
    b, n, _ = u.shape
    n2 = FFT_N2
    n1 = n // n2
    (ch, cl), (sh, sl), (a1h, a1l), (a2h, a2l), (tch, tcl), (tsh, tsl) = _fnet_tables(n)
    gain = fnet_g.reshape(1, B_WIDTH)
    wbd = jax.scipy.linalg.block_diag(*[fnet_w[g] for g in range(4)]).astype(BF16)

    tok = pl.BlockSpec((1, TM, B_WIDTH), lambda i, j: (i, j, 0))
    mat = pl.BlockSpec((B_WIDTH, B_WIDTH), lambda i, j: (0, 0))
    yr, yi = pl.pallas_call(
        _fnet_pre_kernel,
        grid=(b, n // TM),
        in_specs=[tok, mat, pl.BlockSpec((1, B_WIDTH), lambda i, j: (0, 0)), mat, mat, mat, mat],
        out_specs=[tok, tok],
        out_shape=[jax.ShapeDtypeStruct((b, n, B_WIDTH), F32)] * 2,
        compiler_params=_cparams("parallel", "parallel"),
        name="fnet_pre",
    )(u, _group_mean_matrix(), gain, ch, cl, sh, sl)

    cols = n2 * B_WIDTH
    tc = 2048
    yspec = pl.BlockSpec((1, n1, tc), lambda i, j: (i, 0, j))
    aspec = pl.BlockSpec((n1, 2 * n1), lambda i, j: (0, 0))
    zr, zi = pl.pallas_call(
        _fnet_stage1_kernel,
        grid=(b, cols // tc),
        in_specs=[yspec, yspec, aspec, aspec, aspec, aspec],
        out_specs=[yspec, yspec],
        out_shape=[jax.ShapeDtypeStruct((b, n1, cols), F32)] * 2,
        compiler_params=_cparams("parallel", "parallel"),
        name="fnet_stage1",
    )(yr.reshape(b, n1, cols), yi.reshape(b, n1, cols), a1h, a1l, a2h, a2l)

    kc = 8
    zspec = pl.BlockSpec((1, kc, n2, B_WIDTH), lambda i, j: (i, j, 0, 0))
    tspec = pl.BlockSpec((kc, n2, n2), lambda i, j: (j, 0, 0))
    out = pl.pallas_call(
        functools.partial(_fnet_stage2_kernel, kc=kc, scale=1.0 / math.sqrt(64.0 * n)),
        grid=(b, n1 // kc),
        in_specs=[zspec, zspec, tspec, tspec, tspec, tspec,
                  pl.BlockSpec((B_WIDTH, B_WIDTH), lambda i, j: (0, 0))],
        out_specs=pl.BlockSpec((1, n2, kc * B_WIDTH), lambda i, j: (i, 0, j)),
        out_shape=jax.ShapeDtypeStruct((b, n2, n1 * B_WIDTH), BF16),
        compiler_params=_cparams("parallel", "parallel"),
        name="fnet_stage2",
    )(zr.reshape(b, n1, n2, B_WIDTH), zi.reshape(b, n1, n2, B_WIDTH), tch, tcl, tsh, tsl, wbd)
    return out.reshape(b, n, B_WIDTH)


def _outproj_kernel(a_ref, c_ref, wa_ref, wc_ref, x_ref, g_ref, b_ref, o_ref, *, a_transposed):
    if a_transposed:
        h = _dot_tn(a_ref[0], wa_ref[...])
    else:
        h = _dot(a_ref[0], wa_ref[...])
    h = h + _dot(c_ref[0], wc_ref[...])
    o_ref[0] = _layer_norm(DN_ALPHA * x_ref[0] + h, g_ref[...], b_ref[...])


def _outproj(a, c, w, x, g, bias, a_transposed):
    b, n, _ = x.shape
    ka = w.shape[0] - c.shape[-1]
    wa = w[:ka].astype(BF16)
    wc = w[ka:].astype(BF16)
    if a_transposed:
        aspec = pl.BlockSpec((1, ka, TM), lambda i, j: (i, 0, j))
    else:
        aspec = pl.BlockSpec((1, TM, ka), lambda i, j: (i, j, 0))
    vec = pl.BlockSpec((1, D_MODEL), lambda i, j: (0, 0))
    xspec = pl.BlockSpec((1, TM, D_MODEL), lambda i, j: (i, j, 0))
    return pl.pallas_call(
        functools.partial(_outproj_kernel, a_transposed=a_transposed),
        grid=(b, n // TM),
        in_specs=[aspec, pl.BlockSpec((1, TM, c.shape[-1]), lambda i, j: (i, j, 0)),
                  pl.BlockSpec(wa.shape, lambda i, j: (0, 0)),
                  pl.BlockSpec(wc.shape, lambda i, j: (0, 0)), xspec, vec, vec],
        out_specs=xspec,
        out_shape=jax.ShapeDtypeStruct((b, n, D_MODEL), F32),
        compiler_params=_cparams("parallel", "parallel"),
        name="outproj_ln",
    )(a, c, wa, wc, x, g.reshape(1, D_MODEL), bias.reshape(1, D_MODEL))


def _mem_kv_kernel(mem_ref, wk_ref, wvt_ref, k_ref, vt_ref):
    m = mem_ref[0].astype(BF16)
    k_ref[0] = _dot(m, wk_ref[...]).astype(BF16)
    vt_ref[0] = _dot_nt(wvt_ref[...], m).astype(BF16)


def _mem_kv(mem, w_kv):
    b, m, _ = mem.shape
    wk = w_kv[:, :D_MODEL].astype(BF16)
    wvt = w_kv[:, D_MODEL:].T.astype(BF16)
    wspec = pl.BlockSpec((D_MODEL, D_MODEL), lambda i: (0, 0))
    return pl.pallas_call(
        _mem_kv_kernel,
        grid=(b,),
        in_specs=[pl.BlockSpec((1, m, D_MODEL), lambda i: (i, 0, 0)), wspec, wspec],
        out_specs=[pl.BlockSpec((1, m, D_MODEL), lambda i: (i, 0, 0)),
                   pl.BlockSpec((1, D_MODEL, m), lambda i: (i, 0, 0))],
        out_shape=[jax.ShapeDtypeStruct((b, m, D_MODEL), BF16),
                   jax.ShapeDtypeStruct((b, D_MODEL, m), BF16)],
        compiler_params=_cparams("parallel"),
        name="mem_kv",
    )(mem, wk, wvt)


def _xattn_kernel(x_ref, wqt_ref, k_ref, vt_ref, wo_ref, g_ref, b_ref, o_ref):
    x = x_ref[0]
    xb = x.astype(BF16)
    qt = _dot_nt(wqt_ref[...], xb).astype(BF16)
    heads = [slice(h * XA_HEAD_DIM, (h + 1) * XA_HEAD_DIM) for h in range(XA_HEADS)]
    scores = [_dot(k_ref[0, :, hd], qt[hd]) for hd in heads]
    ones = jnp.ones((16, k_ref.shape[1]), BF16)
    outs = []
    for hd, st in zip(heads, scores):
        p = jnp.exp2(st - jnp.max(st, axis=0, keepdims=True)).astype(BF16)
        ot = _dot(jnp.concatenate([vt_ref[0, hd, :], ones], axis=0), p)
        outs.append((ot[:XA_HEAD_DIM] / ot[XA_HEAD_DIM:XA_HEAD_DIM + 1]).astype(BF16))
    h = _dot_tn(jnp.concatenate(outs, axis=0), wo_ref[...])
    o_ref[0] = _layer_norm(DN_ALPHA * x + h, g_ref[...], b_ref[...])


def _xattn(x, mem, w_q, w_kv, w_o, g, bias):
    b, n, _ = x.shape
    m = mem.shape[1]
    k, vt = _mem_kv(mem, w_kv)
    wqt = (w_q * (XA_HEAD_DIM ** -0.5 * LOG2E)).T.astype(BF16)
    wo = w_o.astype(BF16)
    xspec = pl.BlockSpec((1, TM, D_MODEL), lambda i, j: (i, j, 0))
    wspec = pl.BlockSpec((D_MODEL, D_MODEL), lambda i, j: (0, 0))
    vec = pl.BlockSpec((1, D_MODEL), lambda i, j: (0, 0))
    return pl.pallas_call(
        _xattn_kernel,
        grid=(b, n // TM),
        in_specs=[xspec, wspec, pl.BlockSpec((1, m, D_MODEL), lambda i, j: (i, 0, 0)),
                  pl.BlockSpec((1, D_MODEL, m), lambda i, j: (i, 0, 0)), wspec, vec, vec],
        out_specs=xspec,
        out_shape=jax.ShapeDtypeStruct((b, n, D_MODEL), F32),
        compiler_params=_cparams("parallel", "parallel"),
        name="xattn_ln",
    )(x, wqt, k, vt, wo, g.reshape(1, D_MODEL), bias.reshape(1, D_MODEL))


FFN_TH = 1408


def _swiglu_kernel(x_ref, wg_ref, wu_ref, wo_ref, g_ref, b_ref, o_ref, acc_ref):
    j = pl.program_id(2)
    xb = x_ref[0].astype(BF16)
    gate = _dot(xb, wg_ref[...])
    up = _dot(xb, wu_ref[...])
    hid = (gate * (1.0 / (1.0 + jnp.exp(-gate))) * up).astype(BF16)
    part = _dot(hid, wo_ref[...])

    @pl.when(j == 0)
    def _():
        acc_ref[...] = part

    @pl.when(j > 0)
    def _():
        acc_ref[...] += part

    @pl.when(j == pl.num_programs(2) - 1)
    def _():
        o_ref[0] = _layer_norm(DN_ALPHA * x_ref[0] + acc_ref[...], g_ref[...], b_ref[...])


def _swiglu(x, w_in, w_out, g, bias):
    b, n, _ = x.shape
    nj = FFN_HIDDEN // FFN_TH
    win = w_in.astype(BF16)
    wout = w_out.astype(BF16)
    xspec = pl.BlockSpec((1, TM, D_MODEL), lambda i, t, j: (i, t, 0))
    vec = pl.BlockSpec((1, D_MODEL), lambda i, t, j: (0, 0))
    return pl.pallas_call(
        _swiglu_kernel,
        grid=(b, n // TM, nj),
        in_specs=[xspec,
                  pl.BlockSpec((D_MODEL, FFN_TH), lambda i, t, j: (0, j)),
                  pl.BlockSpec((D_MODEL, FFN_TH), lambda i, t, j: (0, j + nj)),
                  pl.BlockSpec((FFN_TH, D_MODEL), lambda i, t, j: (j, 0)), vec, vec],
        out_specs=xspec,
        out_shape=jax.ShapeDtypeStruct((b, n, D_MODEL), F32),
        scratch_shapes=[pltpu.VMEM((TM, D_MODEL), F32)],
        compiler_params=_cparams("parallel", "parallel", "arbitrary"),
        name="swiglu_ln",
    )(x, win, win, wout, g.reshape(1, D_MODEL), bias.reshape(1, D_MODEL))


N_QK_HEADS = C_Q_HEADS + C_KV_HEADS
QK_ROWS = N_QK_HEADS * HEAD_DIM


def _proj_cd_kernel(x_ref, wt_ref, wu_ref, gain_ref, cos_ref, sin_ref,
                    qt_ref, k_ref, vt_ref, u_ref, kn_ref, *, tm):
    xb = x_ref[0].astype(BF16)
    u_ref[0] = _dot(xb, wu_ref[...])
    zt = _dot_nt(wt_ref[...], xb)
    z = zt[:QK_ROWS].reshape(N_QK_HEADS, HEAD_DIM, tm)
    ssq = jnp.sum(z * z, axis=1, keepdims=True)
    zn = z * lax.rsqrt(ssq * (1.0 / HEAD_DIM) + RMS_EPS) * gain_ref[...]
    half = HEAD_DIM // 2
    x1 = zn[:, :half]
    x2 = zn[:, half:]
    c = cos_ref[...][None]
    s = sin_ref[...][None]
    rot = jnp.concatenate([x1 * c - x2 * s, x1 * s + x2 * c], axis=1).reshape(QK_ROWS, tm)
    qt_ref[0] = rot[:C_WIDTH].astype(BF16)
    kb = rot[C_WIDTH:].astype(BF16)
    k_ref[0] = kb.astype(F32).T.astype(BF16)
    kf = kb.astype(F32).reshape(C_KV_HEADS, HEAD_DIM, tm)
    kn_ref[0] = jnp.sum(kf * kf, axis=1, keepdims=True)
    vt = zt[QK_ROWS:].astype(BF16)
    for c0 in range(tm // GQA_TKC):
        vt_ref[0, c0] = vt[:, c0 * GQA_TKC:(c0 + 1) * GQA_TKC]


def _rope_tables_t(n):
    rows = n // GRID_W
    row_id = jnp.broadcast_to(jnp.arange(rows)[:, None], (rows, GRID_W)).reshape(n)
    col_id = jnp.broadcast_to(jnp.arange(GRID_W)[None, :], (rows, GRID_W)).reshape(n)
    axis_dim = HEAD_DIM // 2
    freqs = ROPE_THETA ** (-jnp.arange(0, axis_dim, 2, dtype=F32) / axis_dim)
    ang = jnp.concatenate([row_id[:, None] * freqs, col_id[:, None] * freqs], axis=-1)
    return jnp.cos(ang).T, jnp.sin(ang).T


def _proj_cd(x, w_in, q_norm, k_norm):
    b, n, _ = x.shape
    tm = TM
    half = HEAD_DIM // 2
    wqk = w_in[:, :QK_ROWS].reshape(D_MODEL, N_QK_HEADS, half, 2)
    wqk = wqk.transpose(1, 3, 2, 0).reshape(QK_ROWS, D_MODEL)
    wt = jnp.concatenate([wqk, w_in[:, QK_ROWS:QK_ROWS + C_KV_WIDTH].T], axis=0).astype(BF16)
    wu = w_in[:, QK_ROWS + C_KV_WIDTH:].astype(BF16)
    qg = q_norm.reshape(half, 2).T.reshape(HEAD_DIM) * (HEAD_DIM ** -0.5 * LOG2E)
    kg = k_norm.reshape(half, 2).T.reshape(HEAD_DIM)
    gain = jnp.concatenate([jnp.tile(qg[None], (C_Q_HEADS, 1)), jnp.tile(kg[None], (C_KV_HEADS, 1))])
    gain = gain.reshape(N_QK_HEADS, HEAD_DIM, 1).astype(F32)
    cos_t, sin_t = _rope_tables_t(n)
    nc = n // GQA_TKC
    return pl.pallas_call(
        functools.partial(_proj_cd_kernel, tm=tm),
        grid=(b, n // tm),
        in_specs=[pl.BlockSpec((1, tm, D_MODEL), lambda i, j: (i, j, 0)),
                  pl.BlockSpec(wt.shape, lambda i, j: (0, 0)),
                  pl.BlockSpec(wu.shape, lambda i, j: (0, 0)),
                  pl.BlockSpec(gain.shape, lambda i, j: (0, 0, 0)),
                  pl.BlockSpec((HEAD_DIM // 2, tm), lambda i, j: (0, j)),
                  pl.BlockSpec((HEAD_DIM // 2, tm), lambda i, j: (0, j))],
        out_specs=[pl.BlockSpec((1, C_WIDTH, tm), lambda i, j: (i, 0, j)),
                   pl.BlockSpec((1, tm, C_KV_WIDTH), lambda i, j: (i, j, 0)),
                   pl.BlockSpec((1, tm // GQA_TKC, C_KV_WIDTH, GQA_TKC), lambda i, j: (i, j, 0, 0)),
                   pl.BlockSpec((1, tm, D_WIDTH), lambda i, j: (i, j, 0)),
                   pl.BlockSpec((1, C_KV_HEADS, 1, tm), lambda i, j: (i, 0, 0, j))],
        out_shape=[jax.ShapeDtypeStruct((b, C_WIDTH, n), BF16),
                   jax.ShapeDtypeStruct((b, n, C_KV_WIDTH), BF16),
                   jax.ShapeDtypeStruct((b, nc, C_KV_WIDTH, GQA_TKC), BF16),
                   jax.ShapeDtypeStruct((b, n, D_WIDTH), F32),
                   jax.ShapeDtypeStruct((b, C_KV_HEADS, 1, n), F32)],
        compiler_params=_cparams("parallel", "parallel"),
        name="proj_cd",
    )(x, wt, wu, gain, cos_t, sin_t)


def _gqa_kernel(qt_ref, k_ref, vt_ref, kn_ref, ot_ref, m_ref, acc_ref, qpad_ref, *, tq, n):
    g = pl.program_id(1)
    row_half = lax.broadcasted_iota(jnp.int32, (128, 1), 0) // HEAD_DIM
    mine = row_half == (g % 2)
    k_max2 = jnp.max(kn_ref[0, 0], axis=1, keepdims=True)
    bound_max = jnp.zeros((1, 1), F32)
    for j in range(C_REP):
        qj = qt_ref[0, j * HEAD_DIM:(j + 1) * HEAD_DIM, :]
        q2 = jnp.concatenate([qj, qj], axis=0)
        qpad_ref[j] = jnp.where(mine, q2, jnp.zeros_like(q2))
        qf = qj.astype(F32)
        bound = jnp.sqrt(jnp.sum(qf * qf, axis=0, keepdims=True) * k_max2) * GQA_BOUND_SLACK
        m_ref[j] = bound
        bound_max = jnp.maximum(bound_max, jnp.max(bound, axis=1, keepdims=True))
    fixed_shift = bound_max[0, 0] <= GQA_BOUND_LIMIT
    acc_ref[...] = jnp.zeros(acc_ref.shape, F32)
    ones = jnp.ones((GQA_VROWS - HEAD_DIM, GQA_TKC), BF16)

    def body(c, carry, *, online):
        kchs, vchs = [], []
        for u in range(GQA_UNROLL):
            cc = c * GQA_UNROLL + u
            start = pl.multiple_of(cc * GQA_TKC, GQA_TKC)
            kchs.append(k_ref[0, pl.ds(start, GQA_TKC), :])
            vchs.append(jnp.concatenate([vt_ref[0, cc], ones], axis=0))
        pieces = [(u, j, slice(s * GQA_TW, (s + 1) * GQA_TW))
                  for u in range(GQA_UNROLL) for j in range(C_REP) for s in range(tq // GQA_TW)]

        def scores(i):
            u, j, cols = pieces[i]
            return _dot(kchs[u], qpad_ref[j, :, cols])

        pending = [scores(i) for i in range(GQA_AHEAD)]
        for i, (u, j, cols) in enumerate(pieces):
            st = pending.pop(0)
            if i + GQA_AHEAD < len(pieces):
                pending.append(scores(i + GQA_AHEAD))
            m_old = m_ref[j, :, cols]
            if not online:
                p = jnp.exp2(st - m_old)
                acc_ref[j, :, cols] = acc_ref[j, :, cols] + _dot(vchs[u], p.astype(BF16))
                continue
            m_new = jnp.maximum(m_old, jnp.max(st, axis=0, keepdims=True))
            alpha = jnp.exp2(m_old - m_new)
            p = jnp.exp2(st - m_new)
            acc_ref[j, :, cols] = alpha * acc_ref[j, :, cols] + _dot(vchs[u], p.astype(BF16))
            m_ref[j, :, cols] = m_new
        return carry

    trips = n // (GQA_TKC * GQA_UNROLL)

    @pl.when(fixed_shift)
    def _():
        lax.fori_loop(0, trips, functools.partial(body, online=False), 0)

    @pl.when(jnp.logical_not(fixed_shift))
    def _():
        m_ref[...] = jnp.full(m_ref.shape, NEG_INF, F32)
        lax.fori_loop(0, trips, functools.partial(body, online=True), 0)

    for j in range(C_REP):
        l = acc_ref[j, HEAD_DIM:HEAD_DIM + 1, :]
        ot_ref[0, j * HEAD_DIM:(j + 1) * HEAD_DIM, :] = (acc_ref[j, :HEAD_DIM, :] / l).astype(BF16)


def _gqa(qt, k, vt, kn):
    b, _, n = qt.shape
    tq = min(GQA_TQ, n)
    nc = n // GQA_TKC
    rows = C_REP * HEAD_DIM
    return pl.pallas_call(
        functools.partial(_gqa_kernel, tq=tq, n=n),
        grid=(b, C_KV_HEADS, n // tq),
        in_specs=[pl.BlockSpec((1, rows, tq), lambda i, g, t: (i, g, t)),
                  pl.BlockSpec((1, n, 128), lambda i, g, t: (i, 0, g // 2)),
                  pl.BlockSpec((1, nc, HEAD_DIM, GQA_TKC), lambda i, g, t: (i, 0, g, 0)),
                  pl.BlockSpec((1, 1, 1, n), lambda i, g, t: (i, g, 0, 0))],
        out_specs=pl.BlockSpec((1, rows, tq), lambda i, g, t: (i, g, t)),
        out_shape=jax.ShapeDtypeStruct((b, C_WIDTH, n), BF16),
        scratch_shapes=[pltpu.VMEM((C_REP, 1, tq), F32),
                        pltpu.VMEM((C_REP, GQA_VROWS, tq), F32), pltpu.VMEM((C_REP, 128, tq), BF16)],
        compiler_params=_cparams("parallel", "parallel", "parallel"),
        name="gqa_flash",
    )(qt, k, vt, kn)


def _pool_kernel(up_ref, uc_ref, un_ref, w_ref, scale_ref, o_ref, buf, *, tm, n):
    i = pl.program_id(1)
    cur = uc_ref[0]
    buf[0:POOL_HALO, :] = jnp.where(i > 0, up_ref[0], 0.0)
    buf[POOL_HALO:POOL_HALO + tm, :] = cur
    buf[POOL_HALO + tm:, :] = jnp.where(i < pl.num_programs(1) - 1, un_ref[0], 0.0)
    lane_group = lax.broadcasted_iota(jnp.int32, (1, D_WIDTH), 1) // 64
    half_w = jnp.left_shift(1, lane_group)
    acc = jnp.zeros((tm, D_WIDTH), F32)
    for j in range(-POOL_HALO, POOL_HALO):
        inside = (j >= -half_w) & (j < half_w)
        acc = acc + jnp.where(inside, buf[POOL_HALO + j:POOL_HALO + j + tm, :], 0.0)
    t = i * tm + lax.broadcasted_iota(jnp.int32, (tm, 1), 0)
    cnt = jnp.minimum(t + half_w, n) - jnp.maximum(t - half_w, 0)
    mixed = (acc / cnt.astype(F32) - cur).astype(BF16)
    o_ref[0] = (_dot(mixed, w_ref[...]) * scale_ref[...]).astype(BF16)


def _pool(u, pool_w, pool_scale):
    b, n, _ = u.shape
    tm = TM
    per = tm // POOL_HALO
    last = n // POOL_HALO - 1
    wbd = jax.scipy.linalg.block_diag(*[pool_w[g] for g in range(4)]).astype(BF16)
    return pl.pallas_call(
        functools.partial(_pool_kernel, tm=tm, n=n),
        grid=(b, n // tm),
        in_specs=[pl.BlockSpec((1, POOL_HALO, D_WIDTH), lambda i, j: (i, jnp.maximum(j * per - 1, 0), 0)),
                  pl.BlockSpec((1, tm, D_WIDTH), lambda i, j: (i, j, 0)),
                  pl.BlockSpec((1, POOL_HALO, D_WIDTH), lambda i, j: (i, jnp.minimum((j + 1) * per, last), 0)),
                  pl.BlockSpec((D_WIDTH, D_WIDTH), lambda i, j: (0, 0)),
                  pl.BlockSpec((1, D_WIDTH), lambda i, j: (0, 0))],
        out_specs=pl.BlockSpec((1, tm, D_WIDTH), lambda i, j: (i, j, 0)),
        out_shape=jax.ShapeDtypeStruct((b, n, D_WIDTH), BF16),
        scratch_shapes=[pltpu.VMEM((tm + 2 * POOL_HALO, D_WIDTH), F32)],
        compiler_params=_cparams("parallel", "parallel"),
        name="pool",
    )(u, u, u, wbd, pool_scale.reshape(1, D_WIDTH))


def _trunk(x, mem, rel_bias, ab_w_in, ab_fnet_g, ab_fnet_w, ab_w_out,
           cd_w_in, cd_q_norm, cd_k_norm, cd_pool_w, cd_pool_scale, cd_w_out,
           xa_w_q, xa_w_kv, xa_w_o, ffn_w_in, ffn_w_out, ln_g, ln_b):
    for layer in range(DEPTH):
        i = layer // 2
        if layer % 2 == 0:
            w_in = ab_w_in[i]
            w_in = jnp.concatenate([w_in[:, :A_WIDTH] * (HEAD_DIM ** -0.5 * LOG2E), w_in[:, A_WIDTH:]], axis=1)
            *views, u = _proj_ab(x, w_in.astype(BF16))
            outs = [_dilated(view, rel_bias, d) for view, (_, d) in zip(views, A_PATTERNS)]
            o_a = _mixture([o for o, _ in outs], [l for _, l in outs])
            o_b = _fnet(u, ab_fnet_g[i], ab_fnet_w[i])
            x = _outproj(o_a, o_b, ab_w_out[i], x, ln_g[layer, 0], ln_b[layer, 0], a_transposed=False)
        else:
            qt, k, vt, u, kn = _proj_cd(x, cd_w_in[i], cd_q_norm[i], cd_k_norm[i])
            o_c = _gqa(qt, k, vt, kn)
            o_d = _pool(u, cd_pool_w[i], cd_pool_scale[i])
            x = _outproj(o_c, o_d, cd_w_out[i], x, ln_g[layer, 0], ln_b[layer, 0], a_transposed=True)
        x = _xattn(x, mem, xa_w_q[layer], xa_w_kv[layer], xa_w_o[layer], ln_g[layer, 1], ln_b[layer, 1])
        x = _swiglu(x, ffn_w_in[layer], ffn_w_out[layer], ln_g[layer, 2], ln_b[layer, 2])
    return x


def kernel(x_prompt, x_sample, mem_prompt, mem_sample, rel_bias, ab_w_in, ab_fnet_g, ab_fnet_w, ab_w_out, cd_w_in, cd_q_norm, cd_k_norm, cd_pool_w, cd_pool_scale, cd_w_out, xa_w_q, xa_w_kv, xa_w_o, ffn_w_in, ffn_w_out, ln_g, ln_b):
    params = (rel_bias, ab_w_in, ab_fnet_g, ab_fnet_w, ab_w_out,
              cd_w_in, cd_q_norm, cd_k_norm, cd_pool_w, cd_pool_scale, cd_w_out,
              xa_w_q, xa_w_kv, xa_w_o, ffn_w_in, ffn_w_out, ln_g, ln_b)
    return (_trunk(x_prompt, mem_prompt, *params), _trunk(x_sample, mem_sample, *params))
```

```python
import functools
import math

import numpy as np
import jax
import jax.numpy as jnp
from jax import lax
from jax.experimental import pallas as pl
from jax.experimental.pallas import tpu as pltpu

F32 = jnp.float32
BF16 = jnp.bfloat16

D_MODEL = 1024
HEAD_DIM = 64
GRID_W = 64
LN_EPS = 1e-5
RMS_EPS = 1e-6
NEG_INF = -1e30
DEPTH = 2
A_HEADS = 12
A_WIDTH = A_HEADS * HEAD_DIM
A_PATTERNS = ((128, 1), (512, 4), (2048, 16))
A_HALF = 64
QKV_WIDTH = 3 * A_WIDTH
N_BUCKETS = 32
REL_MAX_DIST = 1024
B_WIDTH = 256
C_Q_HEADS = 12
C_KV_HEADS = 4
C_REP = C_Q_HEADS // C_KV_HEADS
C_WIDTH = C_Q_HEADS * HEAD_DIM
C_KV_WIDTH = C_KV_HEADS * HEAD_DIM
ROPE_THETA = 10000.0
POOL_WINDOWS = (2, 4, 8, 16)
POOL_HALO = 8
D_WIDTH = 256
XA_HEADS = 4
XA_HEAD_DIM = D_MODEL // XA_HEADS
FFN_HIDDEN = 2816
DN_ALPHA = (2 * DEPTH) ** 0.25
LOG2E = 1.4426950408889634

VMEM_LIMIT = 56 * 1024 * 1024
TM = 512
FFT_N2 = 128
DIL_VROWS = HEAD_DIM + 16
DIL_AHEAD = 5
GQA_TQ = 1024
GQA_TW = 256
GQA_TKC = 256
GQA_VROWS = HEAD_DIM + 16
GQA_UNROLL = 8
GQA_AHEAD = 5
GQA_BOUND_SLACK = 1.0 + 2.0 ** -10
GQA_BOUND_LIMIT = 60.0


def _cparams(*sem):
    return pltpu.CompilerParams(dimension_semantics=sem, vmem_limit_bytes=VMEM_LIMIT)


def _dot(a, b):
    return jnp.dot(a, b, preferred_element_type=F32)


def _dot_nt(a, b):
    return lax.dot_general(a, b, (((1,), (1,)), ((), ())), preferred_element_type=F32)


def _dot_tn(a, b):
    return lax.dot_general(a, b, (((0,), (0,)), ((), ())), preferred_element_type=F32)


def _split(x):
    hi = x.astype(BF16)
    lo = (x - hi.astype(F32)).astype(BF16)
    return hi, lo


def _dot3(ah, al, bh, bl):
    return _dot(ah, bh) + _dot(al, bh) + _dot(ah, bl)


def _np_split(x):
    x = np.asarray(x, np.float64)
    hi = jnp.asarray(x, F32).astype(BF16)
    lo = (jnp.asarray(x, F32) - hi.astype(F32)).astype(BF16)
    return hi, lo


def _layer_norm(h, g, b):
    mu = jnp.mean(h, axis=-1, keepdims=True)
    xc = h - mu
    var = jnp.mean(xc * xc, axis=-1, keepdims=True)
    return xc * lax.rsqrt(var + LN_EPS) * g + b


def _proj_ab_kernel(x_ref, w_ref, qkv1_ref, qkv4_ref, qkv16_ref, u_ref, zbuf):
    xb = x_ref[0].astype(BF16)
    for c in range(0, QKV_WIDTH, 256):
        z = _dot(xb, w_ref[:, c:c + 256])
        zbuf[c // 128] = z[:, :128]
        zbuf[c // 128 + 1] = z[:, 128:]
        qkv1_ref[0, :, c:c + 256] = z.astype(BF16)
    u_ref[0] = _dot(xb, w_ref[:, QKV_WIDTH:])
    for (_, d), ref in zip(A_PATTERNS[1:], (qkv4_ref, qkv16_ref)):
        rows = TM // d
        for r in range(d):
            for ct in range(QKV_WIDTH // 128):
                col = r * QKV_WIDTH + ct * 128
                ref[0, :, col:col + 128] = zbuf[ct, pl.ds(r, rows, stride=d), :].astype(BF16)


def _proj_ab(x, w):
    b, n, _ = x.shape
    dils = [d for _, d in A_PATTERNS]
    return pl.pallas_call(
        _proj_ab_kernel,
        grid=(b, n // TM),
        in_specs=[pl.BlockSpec((1, TM, D_MODEL), lambda i, j: (i, j, 0)),
                  pl.BlockSpec(w.shape, lambda i, j: (0, 0))],
        out_specs=[pl.BlockSpec((1, TM // d, d * QKV_WIDTH), lambda i, j: (i, j, 0)) for d in dils]
        + [pl.BlockSpec((1, TM, B_WIDTH), lambda i, j: (i, j, 0))],
        out_shape=[jax.ShapeDtypeStruct((b, n // d, d * QKV_WIDTH), BF16) for d in dils]
        + [jax.ShapeDtypeStruct((b, n, B_WIDTH), F32)],
        scratch_shapes=[pltpu.VMEM((QKV_WIDTH // 128, TM, 128), F32)],
        compiler_params=_cparams("parallel", "parallel"),
        name="proj_ab",
    )(x, w)


def _t5_bucket_np(rel):
    nb = N_BUCKETS // 2
    max_exact = nb // 2
    ret = np.where(rel > 0, nb, 0)
    n = np.abs(rel)
    nf = np.maximum(n, 1).astype(np.float32)
    large = max_exact + (np.log(nf / max_exact) / math.log(REL_MAX_DIST / max_exact)
                         * (nb - max_exact)).astype(np.int32)
    large = np.minimum(large, nb - 1)
    return ret + np.where(n < max_exact, n, large)


def _band_bias(rel_bias, dilation, tq):
    tk = tq + 2 * A_HALF
    band = 2 * A_HALF + 1
    bucket = _t5_bucket_np((np.arange(band) - A_HALF) * dilation)
    row = (rel_bias[jnp.asarray(bucket)].T * LOG2E).astype(F32)
    row = jnp.concatenate([row, jnp.full((A_HEADS, tk + 1 - band), NEG_INF, F32)], axis=1)
    bias = jnp.tile(row, (1, tq))[:, :tq * tk].reshape(A_HEADS, tq, tk)
    bias = bias.transpose(0, 2, 1)
    key = np.arange(tk)[None, :, None]
    before = jnp.asarray(key < A_HALF)
    after = jnp.asarray(key >= A_HALF + tq)
    first = jnp.where(before, NEG_INF, bias)
    return jnp.stack([bias, first, jnp.where(after, NEG_INF, bias), jnp.where(after, NEG_INF, first)])


def _dilated_kernel(q_ref, kp_ref, kc_ref, kn_ref, vp_ref, vc_ref, vn_ref, bias_ref,
                    o_ref, lse_ref, kbuf, vbuf, *, tq):
    tk = tq + 2 * A_HALF
    kbuf[0:A_HALF, :] = kp_ref[0]
    kbuf[A_HALF:A_HALF + tq, :] = kc_ref[0]
    kbuf[A_HALF + tq:, :] = kn_ref[0]
    vbuf[0:A_HALF, :] = vp_ref[0]
    vbuf[A_HALF:A_HALF + tq, :] = vc_ref[0]
    vbuf[A_HALF + tq:, :] = vn_ref[0]

    row_low = lax.broadcasted_iota(jnp.int32, (128, 1), 0) < HEAD_DIM
    row16 = lax.broadcasted_iota(jnp.int32, (16, 1), 0)
    ones = jnp.ones((DIL_VROWS - HEAD_DIM, tk), BF16)
    qts, vts = {}, {}

    def scores(h):
        pair, half = divmod(h, 2)
        cols = slice(pair * 128, (pair + 1) * 128)
        if pair not in qts:
            qts[pair] = q_ref[0, :, cols].astype(F32).T
        qtm = jnp.where(row_low if half == 0 else jnp.logical_not(row_low), qts[pair], 0.0).astype(BF16)
        return _dot(kbuf[:, cols], qtm) + bias_ref[0, h]

    pending = [scores(h) for h in range(DIL_AHEAD)]
    lse_t = jnp.zeros((16, tq), F32)
    o_low = None
    for h in range(A_HEADS):
        st = pending.pop(0)
        if h + DIL_AHEAD < A_HEADS:
            pending.append(scores(h + DIL_AHEAD))
        pair, half = divmod(h, 2)
        cols = slice(pair * 128, (pair + 1) * 128)
        m = jnp.max(st, axis=0, keepdims=True)
        p = jnp.exp2(st - m).astype(BF16)
        if pair not in vts:
            vts[pair] = vbuf[:, cols].astype(F32).T.astype(BF16)
        vaug = jnp.concatenate([vts[pair][half * HEAD_DIM:(half + 1) * HEAD_DIM], ones], axis=0)
        ot = _dot(vaug, p)
        l = ot[HEAD_DIM:HEAD_DIM + 1]
        o_h = ot[:HEAD_DIM] / l
        lse_t = jnp.where(row16 == h, m + jnp.log2(l), lse_t)
        if half == 0:
            o_low = o_h
        else:
            o_ref[0, :, cols] = jnp.concatenate([o_low, o_h], axis=0).T.astype(BF16)
    lse_ref[0] = jnp.concatenate([lse_t, jnp.zeros((128 - 16, tq), F32)], axis=0).T


def _dilated(view, rel_bias, dilation):
    b, seq, _ = view.shape
    tq = min(256, seq)
    tk = tq + 2 * A_HALF
    bias = _band_bias(rel_bias, dilation, tq)
    per = tq // A_HALF
    last = seq // A_HALF - 1
    n_tiles = seq // tq

    def cur(c):
        return pl.BlockSpec((1, tq, A_WIDTH), lambda bi, i, r: (bi, i, 3 * r + c))

    def prev(c):
        return pl.BlockSpec((1, A_HALF, A_WIDTH),
                            lambda bi, i, r: (bi, jnp.maximum(i * per - 1, 0), 3 * r + c))

    def nxt(c):
        return pl.BlockSpec((1, A_HALF, A_WIDTH),
                            lambda bi, i, r: (bi, jnp.minimum((i + 1) * per, last), 3 * r + c))

    o, lse = pl.pallas_call(
        functools.partial(_dilated_kernel, tq=tq),
        grid=(b, seq // tq, dilation),
        in_specs=[cur(0), prev(1), cur(1), nxt(1), prev(2), cur(2), nxt(2),
                  pl.BlockSpec((1, A_HEADS, tk, tq),
                               lambda bi, i, r: ((i == 0) + 2 * (i == n_tiles - 1), 0, 0, 0))],
        out_specs=[pl.BlockSpec((1, tq, A_WIDTH), lambda bi, i, r: (bi, i, r)),
                   pl.BlockSpec((1, tq, 128), lambda bi, i, r: (bi, i, r))],
        out_shape=[jax.ShapeDtypeStruct((b, seq, dilation * A_WIDTH), BF16),
                   jax.ShapeDtypeStruct((b, seq, dilation * 128), F32)],
        scratch_shapes=[pltpu.VMEM((tk, A_WIDTH), BF16), pltpu.VMEM((tk, A_WIDTH), BF16)],
        compiler_params=_cparams("parallel", "parallel", "parallel"),
        name=f"dilated_d{dilation}",
    )(view, view, view, view, view, view, view, bias)
    return o, lse


def _mixture_kernel(o1_ref, o4_ref, o16_ref, l1_ref, l4_ref, l16_ref, expand_ref, out_ref, obuf, lbuf):
    for idx, (d, o_ref, l_ref) in enumerate(((A_PATTERNS[1][1], o4_ref, l4_ref), (A_PATTERNS[2][1], o16_ref, l16_ref))):
        rows = TM // d
        for r in range(d):
            for ct in range(A_WIDTH // 128):
                col = r * A_WIDTH + ct * 128
                obuf[idx, ct, pl.ds(r, rows, stride=d), :] = o_ref[0, :, col:col + 128].astype(F32)
            lbuf[idx, pl.ds(r, rows, stride=d), :] = l_ref[0, :, r * 128:(r + 1) * 128]
    ls = [l1_ref[0], lbuf[0], lbuf[1]]
    mx = jnp.maximum(jnp.maximum(ls[0], ls[1]), ls[2])
    es = [jnp.exp2(l - mx) for l in ls]
    inv = 1.0 / (es[0] + es[1] + es[2])
    ws = []
    for e in es:
        wh, wl = _split(e * inv)
        ws.append(_dot(wh, expand_ref[...]) + _dot(wl, expand_ref[...]))
    for pair in range(A_HEADS // 2):
        cols = slice(pair * 128, (pair + 1) * 128)
        os_ = [o1_ref[0, :, cols].astype(F32), obuf[0, pair], obuf[1, pair]]
        acc = None
        for g in range(3):
            t = ws[g][:, cols] * os_[g]
            acc = t if acc is None else acc + t
        out_ref[0, :, cols] = acc.astype(BF16)


def _mixture(os_, ls_):
    b, n, _ = os_[0].shape
    dils = [d for _, d in A_PATTERNS]
    expand = jnp.asarray(np.arange(128)[:, None] == np.arange(A_WIDTH)[None, :] // HEAD_DIM, BF16)
    return pl.pallas_call(
        _mixture_kernel,
        grid=(b, n // TM),
        in_specs=[pl.BlockSpec((1, TM // d, d * A_WIDTH), lambda i, j: (i, j, 0)) for d in dils]
        + [pl.BlockSpec((1, TM // d, d * 128), lambda i, j: (i, j, 0)) for d in dils]
        + [pl.BlockSpec((128, A_WIDTH), lambda i, j: (0, 0))],
        out_specs=pl.BlockSpec((1, TM, A_WIDTH), lambda i, j: (i, j, 0)),
        out_shape=jax.ShapeDtypeStruct((b, n, A_WIDTH), BF16),
        scratch_shapes=[pltpu.VMEM((2, A_WIDTH // 128, TM, 128), F32), pltpu.VMEM((2, TM, 128), F32)],
        compiler_params=_cparams("parallel", "parallel"),
        name="dilated_mixture",
    )(*os_, *ls_, expand)


def _group_mean_matrix():
    g = np.kron(np.eye(4), np.full((64, 64), 1.0 / 64))
    return jnp.asarray(g, BF16)


def _fnet_pre_kernel(u_ref, gm_ref, gain_ref, ch_ref, cl_ref, sh_ref, sl_ref, yr_ref, yi_ref):
    u = u_ref[0]
    gm = gm_ref[...]
    uh, ul = _split(u)
    mean = _dot(uh, gm) + _dot(ul, gm)
    xc = u - mean
    sh, sl = _split(xc * xc)
    var = _dot(sh, gm) + _dot(sl, gm)
    un = xc * lax.rsqrt(var + LN_EPS) * gain_ref[...]
    nh, nl = _split(un)
    yr_ref[0] = _dot3(nh, nl, ch_ref[...], cl_ref[...])
    yi_ref[0] = -_dot3(nh, nl, sh_ref[...], sl_ref[...])


def _fnet_stage1_kernel(yr_ref, yi_ref, a1h_ref, a1l_ref, a2h_ref, a2l_ref, zr_ref, zi_ref):
    y = jnp.concatenate([yr_ref[0], yi_ref[0]], axis=0)
    yh, yl = _split(y)
    zr_ref[0] = _dot3(a1h_ref[...], a1l_ref[...], yh, yl)
    zi_ref[0] = _dot3(a2h_ref[...], a2l_ref[...], yh, yl)


def _fnet_stage2_kernel(zr_ref, zi_ref, tch_ref, tcl_ref, tsh_ref, tsl_ref, w_ref, o_ref, *, kc, scale):
    for kk in range(kc):
        rh, rl = _split(zr_ref[0, kk])
        ih, il = _split(zi_ref[0, kk])
        f = _dot3(tch_ref[kk], tcl_ref[kk], rh, rl) + _dot3(tsh_ref[kk], tsl_ref[kk], ih, il)
        f = f * scale
        o_ref[0, :, kk * B_WIDTH:(kk + 1) * B_WIDTH] = _dot(f.astype(BF16), w_ref[...]).astype(BF16)


def _fnet_tables(n):
    n2 = FFT_N2
    n1 = n // n2
    c = np.arange(64)
    ang = 2 * np.pi * np.outer(c, c) / 64
    cbd = np.kron(np.eye(4), np.cos(ang))
    sbd = np.kron(np.eye(4), np.sin(ang))
    k1 = np.arange(n1)
    ang1 = 2 * np.pi * np.outer(k1, k1) / n1
    c1, s1 = np.cos(ang1), np.sin(ang1)
    a1 = np.concatenate([c1, s1], axis=1)
    a2 = np.concatenate([-s1, c1], axis=1)
    k2 = np.arange(n2)
    npr = k1[:, None, None] + n1 * k2[None, :, None]
    prod = (npr * k2[None, None, :]) % n
    ang2 = 2 * np.pi * prod / n
    return (_np_split(cbd), _np_split(sbd), _np_split(a1), _np_split(a2),
            _np_split(np.cos(ang2)), _np_split(np.sin(ang2)))


def _fnet(u, fnet_g, fnet_w):
    b, n, _ = u.shape
    n2 = FFT_N2
    n1 = n // n2
    (ch, cl), (sh, sl), (a1h, a1l), (a2h, a2l), (tch, tcl), (tsh, tsl) = _fnet_tables(n)
    gain = fnet_g.reshape(1, B_WIDTH)
    wbd = jax.scipy.linalg.block_diag(*[fnet_w[g] for g in range(4)]).astype(BF16)

    tok = pl.BlockSpec((1, TM, B_WIDTH), lambda i, j: (i, j, 0))
    mat = pl.BlockSpec((B_WIDTH, B_WIDTH), lambda i, j: (0, 0))
    yr, yi = pl.pallas_call(
        _fnet_pre_kernel,
        grid=(b, n // TM),
        in_specs=[tok, mat, pl.BlockSpec((1, B_WIDTH), lambda i, j: (0, 0)), mat, mat, mat, mat],
        out_specs=[tok, tok],
        out_shape=[jax.ShapeDtypeStruct((b, n, B_WIDTH), F32)] * 2,
        compiler_params=_cparams("parallel", "parallel"),
        name="fnet_pre",
    )(u, _group_mean_matrix(), gain, ch, cl, sh, sl)

    cols = n2 * B_WIDTH
    tc = 2048
    yspec = pl.BlockSpec((1, n1, tc), lambda i, j: (i, 0, j))
    aspec = pl.BlockSpec((n1, 2 * n1), lambda i, j: (0, 0))
    zr, zi = pl.pallas_call(
        _fnet_stage1_kernel,
        grid=(b, cols // tc),
        in_specs=[yspec, yspec, aspec, aspec, aspec, aspec],
        out_specs=[yspec, yspec],
        out_shape=[jax.ShapeDtypeStruct((b, n1, cols), F32)] * 2,
        compiler_params=_cparams("parallel", "parallel"),
        name="fnet_stage1",
    )(yr.reshape(b, n1, cols), yi.reshape(b, n1, cols), a1h, a1l, a2h, a2l)

    kc = 8
    zspec = pl.BlockSpec((1, kc, n2, B_WIDTH), lambda i, j: (i, j, 0, 0))
    tspec = pl.BlockSpec((kc, n2, n2), lambda i, j: (j, 0, 0))
    out = pl.pallas_call(
        functools.partial(_fnet_stage2_kernel, kc=kc, scale=1.0 / math.sqrt(64.0 * n)),
        grid=(b, n1 // kc),
        in_specs=[zspec, zspec, tspec, tspec, tspec, tspec,
                  pl.BlockSpec((B_WIDTH, B_WIDTH), lambda i, j: (0, 0))],
        out_specs=pl.BlockSpec((1, n2, kc * B_WIDTH), lambda i, j: (i, 0, j)),
        out_shape=jax.ShapeDtypeStruct((b, n2, n1 * B_WIDTH), BF16),
        compiler_params=_cparams("parallel", "parallel"),
        name="fnet_stage2",
    )(zr.reshape(b, n1, n2, B_WIDTH), zi.reshape(b, n1, n2, B_WIDTH), tch, tcl, tsh, tsl, wbd)
    return out.reshape(b, n, B_WIDTH)


def _outproj_kernel(a_ref, c_ref, wa_ref, wc_ref, x_ref, g_ref, b_ref, o_ref, *, a_transposed):
    halves = [slice(0, TM // 2), slice(TM // 2, TM)]
    hs = []
    for rows in halves:
        if a_transposed:
            h = _dot_tn(a_ref[0, :, rows], wa_ref[...])
        else:
            h = _dot(a_ref[0, rows, :], wa_ref[...])
        hs.append(h + _dot(c_ref[0, rows, :], wc_ref[...]))
    for rows, h in zip(halves, hs):
        o_ref[0, rows, :] = _layer_norm(DN_ALPHA * x_ref[0, rows, :] + h, g_ref[...], b_ref[...])


def _outproj(a, c, w, x, g, bias, a_transposed):
    b, n, _ = x.shape
    ka = w.shape[0] - c.shape[-1]
    wa = w[:ka].astype(BF16)
    wc = w[ka:].astype(BF16)
    if a_transposed:
        aspec = pl.BlockSpec((1, ka, TM), lambda i, j: (i, 0, j))
    else:
        aspec = pl.BlockSpec((1, TM, ka), lambda i, j: (i, j, 0))
    vec = pl.BlockSpec((1, D_MODEL), lambda i, j: (0, 0))
    xspec = pl.BlockSpec((1, TM, D_MODEL), lambda i, j: (i, j, 0))
    return pl.pallas_call(
        functools.partial(_outproj_kernel, a_transposed=a_transposed),
        grid=(b, n // TM),
        in_specs=[aspec, pl.BlockSpec((1, TM, c.shape[-1]), lambda i, j: (i, j, 0)),
                  pl.BlockSpec(wa.shape, lambda i, j: (0, 0)),
                  pl.BlockSpec(wc.shape, lambda i, j: (0, 0)), xspec, vec, vec],
        out_specs=xspec,
        out_shape=jax.ShapeDtypeStruct((b, n, D_MODEL), F32),
        compiler_params=_cparams("parallel", "parallel"),
        name="outproj_ln",
    )(a, c, wa, wc, x, g.reshape(1, D_MODEL), bias.reshape(1, D_MODEL))


def _mem_kv_kernel(mem_ref, wk_ref, wvt_ref, k_ref, vt_ref):
    m = mem_ref[0].astype(BF16)
    k_ref[0] = _dot(m, wk_ref[...]).astype(BF16)
    vt_ref[0] = _dot_nt(wvt_ref[...], m).astype(BF16)


def _mem_kv(mem, w_kv):
    b, m, _ = mem.shape
    wk = w_kv[:, :D_MODEL].astype(BF16)
    wvt = w_kv[:, D_MODEL:].T.astype(BF16)
    wspec = pl.BlockSpec((D_MODEL, D_MODEL), lambda i: (0, 0))
    return pl.pallas_call(
        _mem_kv_kernel,
        grid=(b,),
        in_specs=[pl.BlockSpec((1, m, D_MODEL), lambda i: (i, 0, 0)), wspec, wspec],
        out_specs=[pl.BlockSpec((1, m, D_MODEL), lambda i: (i, 0, 0)),
                   pl.BlockSpec((1, D_MODEL, m), lambda i: (i, 0, 0))],
        out_shape=[jax.ShapeDtypeStruct((b, m, D_MODEL), BF16),
                   jax.ShapeDtypeStruct((b, D_MODEL, m), BF16)],
        compiler_params=_cparams("parallel"),
        name="mem_kv",
    )(mem, wk, wvt)


def _xattn_kernel(x_ref, wqt_ref, k_ref, vt_ref, wo_ref, g_ref, b_ref, o_ref):
    x = x_ref[0]
    xb = x.astype(BF16)
    qt = _dot_nt(wqt_ref[...], xb).astype(BF16)
    heads = [slice(h * XA_HEAD_DIM, (h + 1) * XA_HEAD_DIM) for h in range(XA_HEADS)]
    scores = [_dot(k_ref[0, :, hd], qt[hd]) for hd in heads]
    ones = jnp.ones((16, k_ref.shape[1]), BF16)
    outs = []
    for hd, st in zip(heads, scores):
        p = jnp.exp2(st - jnp.max(st, axis=0, keepdims=True)).astype(BF16)
        ot = _dot(jnp.concatenate([vt_ref[0, hd, :], ones], axis=0), p)
        outs.append((ot[:XA_HEAD_DIM] / ot[XA_HEAD_DIM:XA_HEAD_DIM + 1]).astype(BF16))
    ot_all = jnp.concatenate(outs, axis=0)
    halves = [slice(0, TM // 2), slice(TM // 2, TM)]
    hs = [_dot_tn(ot_all[:, rows], wo_ref[...]) for rows in halves]
    for rows, h in zip(halves, hs):
        o_ref[0, rows, :] = _layer_norm(DN_ALPHA * x[rows] + h, g_ref[...], b_ref[...])


def _xattn(x, mem, w_q, w_kv, w_o, g, bias):
    b, n, _ = x.shape
    m = mem.shape[1]
    k, vt = _mem_kv(mem, w_kv)
    wqt = (w_q * (XA_HEAD_DIM ** -0.5 * LOG2E)).T.astype(BF16)
    wo = w_o.astype(BF16)
    xspec = pl.BlockSpec((1, TM, D_MODEL), lambda i, j: (i, j, 0))
    wspec = pl.BlockSpec((D_MODEL, D_MODEL), lambda i, j: (0, 0))
    vec = pl.BlockSpec((1, D_MODEL), lambda i, j: (0, 0))
    return pl.pallas_call(
        _xattn_kernel,
        grid=(b, n // TM),
        in_specs=[xspec, wspec, pl.BlockSpec((1, m, D_MODEL), lambda i, j: (i, 0, 0)),
                  pl.BlockSpec((1, D_MODEL, m), lambda i, j: (i, 0, 0)), wspec, vec, vec],
        out_specs=xspec,
        out_shape=jax.ShapeDtypeStruct((b, n, D_MODEL), F32),
        compiler_params=_cparams("parallel", "parallel"),
        name="xattn_ln",
    )(x, wqt, k, vt, wo, g.reshape(1, D_MODEL), bias.reshape(1, D_MODEL))


FFN_TH = 1408


def _swiglu_kernel(x_ref, wg_ref, wu_ref, wo_ref, g_ref, b_ref, o_ref, acc_ref):
    j = pl.program_id(2)
    xb = x_ref[0].astype(BF16)
    gate = _dot(xb, wg_ref[...])
    up = _dot(xb, wu_ref[...])
    hid = (gate * (1.0 / (1.0 + jnp.exp(-gate))) * up).astype(BF16)
    last = pl.num_programs(2) - 1

    @pl.when(j == 0)
    def _():
        acc_ref[...] = _dot(hid, wo_ref[...])

    @pl.when(jnp.logical_and(j > 0, j < last))
    def _():
        acc_ref[...] += _dot(hid, wo_ref[...])

    @pl.when(j == last)
    def _():
        halves = [slice(0, TM // 2), slice(TM // 2, TM)]
        parts = [_dot(hid[rows], wo_ref[...]) for rows in halves]
        for rows, part in zip(halves, parts):
            h = DN_ALPHA * x_ref[0, rows, :] + acc_ref[rows, :] + part
            o_ref[0, rows, :] = _layer_norm(h, g_ref[...], b_ref[...])


def _swiglu(x, w_in, w_out, g, bias):
    b, n, _ = x.shape
    nj = FFN_HIDDEN // FFN_TH
    assert nj >= 2 and nj * FFN_TH == FFN_HIDDEN
    win = w_in.astype(BF16)
    wout = w_out.astype(BF16)
    xspec = pl.BlockSpec((1, TM, D_MODEL), lambda i, t, j: (i, t, 0))
    vec = pl.BlockSpec((1, D_MODEL), lambda i, t, j: (0, 0))
    return pl.pallas_call(
        _swiglu_kernel,
        grid=(b, n // TM, nj),
        in_specs=[xspec,
                  pl.BlockSpec((D_MODEL, FFN_TH), lambda i, t, j: (0, j)),
                  pl.BlockSpec((D_MODEL, FFN_TH), lambda i, t, j: (0, j + nj)),
                  pl.BlockSpec((FFN_TH, D_MODEL), lambda i, t, j: (j, 0)), vec, vec],
        out_specs=xspec,
        out_shape=jax.ShapeDtypeStruct((b, n, D_MODEL), F32),
        scratch_shapes=[pltpu.VMEM((TM, D_MODEL), F32)],
        compiler_params=_cparams("parallel", "parallel", "arbitrary"),
        name="swiglu_ln",
    )(x, win, win, wout, g.reshape(1, D_MODEL), bias.reshape(1, D_MODEL))


N_QK_HEADS = C_Q_HEADS + C_KV_HEADS
QK_ROWS = N_QK_HEADS * HEAD_DIM


def _proj_cd_kernel(x_ref, wt_ref, wu_ref, gain_ref, cos_ref, sin_ref,
                    qt_ref, k_ref, vt_ref, u_ref, kn_ref, *, tm):
    xb = x_ref[0].astype(BF16)
    u_ref[0] = _dot(xb, wu_ref[...])
    zt = _dot_nt(wt_ref[...], xb)
    z = zt[:QK_ROWS].reshape(N_QK_HEADS, HEAD_DIM, tm)
    ssq = jnp.sum(z * z, axis=1, keepdims=True)
    zn = z * lax.rsqrt(ssq * (1.0 / HEAD_DIM) + RMS_EPS) * gain_ref[...]
    half = HEAD_DIM // 2
    x1 = zn[:, :half]
    x2 = zn[:, half:]
    c = cos_ref[...][None]
    s = sin_ref[...][None]
    rot = jnp.concatenate([x1 * c - x2 * s, x1 * s + x2 * c], axis=1).reshape(QK_ROWS, tm)
    qt_ref[0] = rot[:C_WIDTH].astype(BF16)
    kb = rot[C_WIDTH:].astype(BF16)
    k_ref[0] = kb.astype(F32).T.astype(BF16)
    kf = kb.astype(F32).reshape(C_KV_HEADS, HEAD_DIM, tm)
    kn_ref[0] = jnp.sum(kf * kf, axis=1, keepdims=True)
    vt = zt[QK_ROWS:].astype(BF16)
    for c0 in range(tm // GQA_TKC):
        vt_ref[0, c0] = vt[:, c0 * GQA_TKC:(c0 + 1) * GQA_TKC]


def _rope_tables_t(n):
    rows = n // GRID_W
    row_id = jnp.broadcast_to(jnp.arange(rows)[:, None], (rows, GRID_W)).reshape(n)
    col_id = jnp.broadcast_to(jnp.arange(GRID_W)[None, :], (rows, GRID_W)).reshape(n)
    axis_dim = HEAD_DIM // 2
    freqs = ROPE_THETA ** (-jnp.arange(0, axis_dim, 2, dtype=F32) / axis_dim)
    ang = jnp.concatenate([row_id[:, None] * freqs, col_id[:, None] * freqs], axis=-1)
    return jnp.cos(ang).T, jnp.sin(ang).T


def _proj_cd(x, w_in, q_norm, k_norm):
    b, n, _ = x.shape
    tm = TM
    half = HEAD_DIM // 2
    wqk = w_in[:, :QK_ROWS].reshape(D_MODEL, N_QK_HEADS, half, 2)
    wqk = wqk.transpose(1, 3, 2, 0).reshape(QK_ROWS, D_MODEL)
    wt = jnp.concatenate([wqk, w_in[:, QK_ROWS:QK_ROWS + C_KV_WIDTH].T], axis=0).astype(BF16)
    wu = w_in[:, QK_ROWS + C_KV_WIDTH:].astype(BF16)
    qg = q_norm.reshape(half, 2).T.reshape(HEAD_DIM) * (HEAD_DIM ** -0.5 * LOG2E)
    kg = k_norm.reshape(half, 2).T.reshape(HEAD_DIM)
    gain = jnp.concatenate([jnp.tile(qg[None], (C_Q_HEADS, 1)), jnp.tile(kg[None], (C_KV_HEADS, 1))])
    gain = gain.reshape(N_QK_HEADS, HEAD_DIM, 1).astype(F32)
    cos_t, sin_t = _rope_tables_t(n)
    nc = n // GQA_TKC
    return pl.pallas_call(
        functools.partial(_proj_cd_kernel, tm=tm),
        grid=(b, n // tm),
        in_specs=[pl.BlockSpec((1, tm, D_MODEL), lambda i, j: (i, j, 0)),
                  pl.BlockSpec(wt.shape, lambda i, j: (0, 0)),
                  pl.BlockSpec(wu.shape, lambda i, j: (0, 0)),
                  pl.BlockSpec(gain.shape, lambda i, j: (0, 0, 0)),
                  pl.BlockSpec((HEAD_DIM // 2, tm), lambda i, j: (0, j)),
                  pl.BlockSpec((HEAD_DIM // 2, tm), lambda i, j: (0, j))],
        out_specs=[pl.BlockSpec((1, C_WIDTH, tm), lambda i, j: (i, 0, j)),
                   pl.BlockSpec((1, tm, C_KV_WIDTH), lambda i, j: (i, j, 0)),
                   pl.BlockSpec((1, tm // GQA_TKC, C_KV_WIDTH, GQA_TKC), lambda i, j: (i, j, 0, 0)),
                   pl.BlockSpec((1, tm, D_WIDTH), lambda i, j: (i, j, 0)),
                   pl.BlockSpec((1, C_KV_HEADS, 1, tm), lambda i, j: (i, 0, 0, j))],
        out_shape=[jax.ShapeDtypeStruct((b, C_WIDTH, n), BF16),
                   jax.ShapeDtypeStruct((b, n, C_KV_WIDTH), BF16),
                   jax.ShapeDtypeStruct((b, nc, C_KV_WIDTH, GQA_TKC), BF16),
                   jax.ShapeDtypeStruct((b, n, D_WIDTH), F32),
                   jax.ShapeDtypeStruct((b, C_KV_HEADS, 1, n), F32)],
        compiler_params=_cparams("parallel", "parallel"),
        name="proj_cd",
    )(x, wt, wu, gain, cos_t, sin_t)


def _gqa_kernel(qt_ref, k_ref, vt_ref, kn_ref, ot_ref, m_ref, l_ref, acc_ref, qpad_ref, *, tq, n):
    g = pl.program_id(1)
    row_half = lax.broadcasted_iota(jnp.int32, (128, 1), 0) // HEAD_DIM
    mine = row_half == (g % 2)
    k_max2 = jnp.max(kn_ref[0, 0], axis=1, keepdims=True)
    bound_max = jnp.zeros((1, 1), F32)
    for j in range(C_REP):
        qj = qt_ref[0, j * HEAD_DIM:(j + 1) * HEAD_DIM, :]
        q2 = jnp.concatenate([qj, qj], axis=0)
        qpad_ref[j] = jnp.where(mine, q2, jnp.zeros_like(q2))
        qf = qj.astype(F32)
        bound = jnp.sqrt(jnp.sum(qf * qf, axis=0, keepdims=True) * k_max2) * GQA_BOUND_SLACK
        m_ref[j] = bound
        bound_max = jnp.maximum(bound_max, jnp.max(bound, axis=1, keepdims=True))
    fixed_shift = bound_max[0, 0] <= GQA_BOUND_LIMIT
    acc_ref[...] = jnp.zeros(acc_ref.shape, F32)
    ones = jnp.ones((GQA_VROWS - HEAD_DIM, GQA_TKC), BF16)

    def body(c, carry, *, online):
        kchs, vchs = [], []
        for u in range(GQA_UNROLL):
            cc = c * GQA_UNROLL + u
            start = pl.multiple_of(cc * GQA_TKC, GQA_TKC)
            kchs.append(k_ref[0, pl.ds(start, GQA_TKC), :])
            vchs.append(jnp.concatenate([vt_ref[0, cc], ones], axis=0))
        pieces = [(u, j, slice(s * GQA_TW, (s + 1) * GQA_TW))
                  for u in range(GQA_UNROLL) for j in range(C_REP) for s in range(tq // GQA_TW)]

        def scores(i):
            u, j, cols = pieces[i]
            return _dot(kchs[u], qpad_ref[j, :, cols])

        pending = [scores(i) for i in range(GQA_AHEAD)]
        for i, (u, j, cols) in enumerate(pieces):
            st = pending.pop(0)
            if i + GQA_AHEAD < len(pieces):
                pending.append(scores(i + GQA_AHEAD))
            m_old = m_ref[j, :, cols]
            if not online:
                p = jnp.exp2(st - m_old)
                l_ref[j, :, cols] = l_ref[j, :, cols] + jnp.sum(p.reshape(GQA_TKC // 8, 8, GQA_TW), axis=0)
                acc_ref[j, :HEAD_DIM, cols] = acc_ref[j, :HEAD_DIM, cols] + _dot(
                    vchs[u][:HEAD_DIM], p.astype(BF16))
                continue
            m_new = jnp.maximum(m_old, jnp.max(st, axis=0, keepdims=True))
            alpha = jnp.exp2(m_old - m_new)
            p = jnp.exp2(st - m_new)
            acc_ref[j, :, cols] = alpha * acc_ref[j, :, cols] + _dot(vchs[u], p.astype(BF16))
            m_ref[j, :, cols] = m_new
        return carry

    trips = n // (GQA_TKC * GQA_UNROLL)

    @pl.when(fixed_shift)
    def _():
        l_ref[...] = jnp.zeros(l_ref.shape, F32)
        lax.fori_loop(0, trips, functools.partial(body, online=False), 0)
        for j in range(C_REP):
            acc_ref[j, HEAD_DIM:HEAD_DIM + 1, :] = jnp.sum(l_ref[j], axis=0, keepdims=True)

    @pl.when(jnp.logical_not(fixed_shift))
    def _():
        m_ref[...] = jnp.full(m_ref.shape, NEG_INF, F32)
        lax.fori_loop(0, trips, functools.partial(body, online=True), 0)

    for j in range(C_REP):
        l = acc_ref[j, HEAD_DIM:HEAD_DIM + 1, :]
        ot_ref[0, j * HEAD_DIM:(j + 1) * HEAD_DIM, :] = (acc_ref[j, :HEAD_DIM, :] / l).astype(BF16)


def _gqa(qt, k, vt, kn):
    b, _, n = qt.shape
    tq = min(GQA_TQ, n)
    nc = n // GQA_TKC
    rows = C_REP * HEAD_DIM
    return pl.pallas_call(
        functools.partial(_gqa_kernel, tq=tq, n=n),
        grid=(b, C_KV_HEADS, n // tq),
        in_specs=[pl.BlockSpec((1, rows, tq), lambda i, g, t: (i, g, t)),
                  pl.BlockSpec((1, n, 128), lambda i, g, t: (i, 0, g // 2)),
                  pl.BlockSpec((1, nc, HEAD_DIM, GQA_TKC), lambda i, g, t: (i, 0, g, 0)),
                  pl.BlockSpec((1, 1, 1, n), lambda i, g, t: (i, g, 0, 0))],
        out_specs=pl.BlockSpec((1, rows, tq), lambda i, g, t: (i, g, t)),
        out_shape=jax.ShapeDtypeStruct((b, C_WIDTH, n), BF16),
        scratch_shapes=[pltpu.VMEM((C_REP, 1, tq), F32), pltpu.VMEM((C_REP, 8, tq), F32),
                        pltpu.VMEM((C_REP, GQA_VROWS, tq), F32), pltpu.VMEM((C_REP, 128, tq), BF16)],
        compiler_params=_cparams("parallel", "parallel", "parallel"),
        name="gqa_flash",
    )(qt, k, vt, kn)


def _pool_kernel(up_ref, uc_ref, un_ref, w_ref, scale_ref, o_ref, buf, *, tm, n):
    i = pl.program_id(1)
    cur = uc_ref[0]
    buf[0:POOL_HALO, :] = jnp.where(i > 0, up_ref[0], 0.0)
    buf[POOL_HALO:POOL_HALO + tm, :] = cur
    buf[POOL_HALO + tm:, :] = jnp.where(i < pl.num_programs(1) - 1, un_ref[0], 0.0)
    lane_group = lax.broadcasted_iota(jnp.int32, (1, D_WIDTH), 1) // 64
    half_w = jnp.left_shift(1, lane_group)
    acc = jnp.zeros((tm, D_WIDTH), F32)
    for j in range(-POOL_HALO, POOL_HALO):
        inside = (j >= -half_w) & (j < half_w)
        acc = acc + jnp.where(inside, buf[POOL_HALO + j:POOL_HALO + j + tm, :], 0.0)
    t = i * tm + lax.broadcasted_iota(jnp.int32, (tm, 1), 0)
    cnt = jnp.minimum(t + half_w, n) - jnp.maximum(t - half_w, 0)
    mixed = (acc / cnt.astype(F32) - cur).astype(BF16)
    o_ref[0] = (_dot(mixed, w_ref[...]) * scale_ref[...]).astype(BF16)


def _pool(u, pool_w, pool_scale):
    b, n, _ = u.shape
    tm = TM
    per = tm // POOL_HALO
    last = n // POOL_HALO - 1
    wbd = jax.scipy.linalg.block_diag(*[pool_w[g] for g in range(4)]).astype(BF16)
    return pl.pallas_call(
        functools.partial(_pool_kernel, tm=tm, n=n),
        grid=(b, n // tm),
        in_specs=[pl.BlockSpec((1, POOL_HALO, D_WIDTH), lambda i, j: (i, jnp.maximum(j * per - 1, 0), 0)),
                  pl.BlockSpec((1, tm, D_WIDTH), lambda i, j: (i, j, 0)),
                  pl.BlockSpec((1, POOL_HALO, D_WIDTH), lambda i, j: (i, jnp.minimum((j + 1) * per, last), 0)),
                  pl.BlockSpec((D_WIDTH, D_WIDTH), lambda i, j: (0, 0)),
                  pl.BlockSpec((1, D_WIDTH), lambda i, j: (0, 0))],
        out_specs=pl.BlockSpec((1, tm, D_WIDTH), lambda i, j: (i, j, 0)),
        out_shape=jax.ShapeDtypeStruct((b, n, D_WIDTH), BF16),
        scratch_shapes=[pltpu.VMEM((tm + 2 * POOL_HALO, D_WIDTH), F32)],
        compiler_params=_cparams("parallel", "parallel"),
        name="pool",
    )(u, u, u, wbd, pool_scale.reshape(1, D_WIDTH))


def _trunk(x, mem, rel_bias, ab_w_in, ab_fnet_g, ab_fnet_w, ab_w_out,
           cd_w_in, cd_q_norm, cd_k_norm, cd_pool_w, cd_pool_scale, cd_w_out,
           xa_w_q, xa_w_kv, xa_w_o, ffn_w_in, ffn_w_out, ln_g, ln_b):
    for layer in range(DEPTH):
        i = layer // 2
        if layer % 2 == 0:
            w_in = ab_w_in[i]
            w_in = jnp.concatenate([w_in[:, :A_WIDTH] * (HEAD_DIM ** -0.5 * LOG2E), w_in[:, A_WIDTH:]], axis=1)
            *views, u = _proj_ab(x, w_in.astype(BF16))
            outs = [_dilated(view, rel_bias, d) for view, (_, d) in zip(views, A_PATTERNS)]
            o_a = _mixture([o for o, _ in outs], [l for _, l in outs])
            o_b = _fnet(u, ab_fnet_g[i], ab_fnet_w[i])
            x = _outproj(o_a, o_b, ab_w_out[i], x, ln_g[layer, 0], ln_b[layer, 0], a_transposed=False)
        else:
            qt, k, vt, u, kn = _proj_cd(x, cd_w_in[i], cd_q_norm[i], cd_k_norm[i])
            o_c = _gqa(qt, k, vt, kn)
            o_d = _pool(u, cd_pool_w[i], cd_pool_scale[i])
            x = _outproj(o_c, o_d, cd_w_out[i], x, ln_g[layer, 0], ln_b[layer, 0], a_transposed=True)
        x = _xattn(x, mem, xa_w_q[layer], xa_w_kv[layer], xa_w_o[layer], ln_g[layer, 1], ln_b[layer, 1])
        x = _swiglu(x, ffn_w_in[layer], ffn_w_out[layer], ln_g[layer, 2], ln_b[layer, 2])
    return x


def kernel(x_prompt, x_sample, mem_prompt, mem_sample, rel_bias, ab_w_in, ab_fnet_g, ab_fnet_w, ab_w_out, cd_w_in, cd_q_norm, cd_k_norm, cd_pool_w, cd_pool_scale, cd_w_out, xa_w_q, xa_w_kv, xa_w_o, ffn_w_in, ffn_w_out, ln_g, ln_b):
    params = (rel_bias, ab_w_in, ab_fnet_g, ab_fnet_w, ab_w_out,
              cd_w_in, cd_q_norm, cd_k_norm, cd_pool_w, cd_pool_scale, cd_w_out,
              xa_w_q, xa_w_kv, xa_w_o, ffn_w_in, ffn_w_out, ln_g, ln_b)
    return (_trunk(x_prompt, mem_prompt, *params), _trunk(x_sample, mem_sample, *params))
```

```python
import functools
import math

import numpy as np
import jax
import jax.numpy as jnp
from jax import lax
from jax.experimental import pallas as pl
from jax.experimental.pallas import tpu as pltpu

F32 = jnp.float32
BF16 = jnp.bfloat16

D_MODEL = 1024
HEAD_DIM = 64
GRID_W = 64
LN_EPS = 1e-5
RMS_EPS = 1e-6
NEG_INF = -1e30
DEPTH = 2
A_HEADS = 12
A_WIDTH = A_HEADS * HEAD_DIM
A_PATTERNS = ((128, 1), (512, 4), (2048, 16))
A_HALF = 64
QKV_WIDTH = 3 * A_WIDTH
N_BUCKETS = 32
REL_MAX_DIST = 1024
B_WIDTH = 256
C_Q_HEADS = 12
C_KV_HEADS = 4
C_REP = C_Q_HEADS // C_KV_HEADS
C_WIDTH = C_Q_HEADS * HEAD_DIM
C_KV_WIDTH = C_KV_HEADS * HEAD_DIM
ROPE_THETA = 10000.0
POOL_WINDOWS = (2, 4, 8, 16)
POOL_HALO = 8
D_WIDTH = 256
XA_HEADS = 4
XA_HEAD_DIM = D_MODEL // XA_HEADS
FFN_HIDDEN = 2816
DN_ALPHA = (2 * DEPTH) ** 0.25
LOG2E = 1.4426950408889634

VMEM_LIMIT = 56 * 1024 * 1024
TM = 512
FFT_N2 = 128
DIL_AHEAD = 3
DIL_TQ = 512
DIL_SQ = 128
DIL_SK = DIL_SQ + 2 * A_HALF
GQA_TQ = 1024
GQA_TW = 256
GQA_TKC = 256
GQA_VROWS = HEAD_DIM + 16
GQA_UNROLL = 8
GQA_AHEAD = 5
GQA_BOUND_SLACK = 1.0 + 2.0 ** -10
GQA_BOUND_LIMIT = 60.0


def _cparams(*sem):
    return pltpu.CompilerParams(dimension_semantics=sem, vmem_limit_bytes=VMEM_LIMIT)


def _dot(a, b):
    return jnp.dot(a, b, preferred_element_type=F32)


def _dot_nt(a, b):
    return lax.dot_general(a, b, (((1,), (1,)), ((), ())), preferred_element_type=F32)


def _dot_tn(a, b):
    return lax.dot_general(a, b, (((0,), (0,)), ((), ())), preferred_element_type=F32)


def _split(x):
    hi = x.astype(BF16)
    lo = (x - hi.astype(F32)).astype(BF16)
    return hi, lo


def _dot3(ah, al, bh, bl):
    return _dot(ah, bh) + _dot(al, bh) + _dot(ah, bl)


def _np_split(x):
    x = np.asarray(x, np.float64)
    hi = jnp.asarray(x, F32).astype(BF16)
    lo = (jnp.asarray(x, F32) - hi.astype(F32)).astype(BF16)
    return hi, lo


def _layer_norm(h, g, b):
    mu = jnp.mean(h, axis=-1, keepdims=True)
    xc = h - mu
    var = jnp.mean(xc * xc, axis=-1, keepdims=True)
    return xc * lax.rsqrt(var + LN_EPS) * g + b


def _proj_ab_kernel(x_ref, w_ref, qkv1_ref, qkv4_ref, qkv16_ref, u_ref, zbuf):
    xb = x_ref[0].astype(BF16)
    for c in range(0, QKV_WIDTH, 256):
        z = _dot(xb, w_ref[:, c:c + 256])
        zbuf[c // 128] = z[:, :128]
        zbuf[c // 128 + 1] = z[:, 128:]
        qkv1_ref[0, :, c:c + 256] = z.astype(BF16)
    u_ref[0] = _dot(xb, w_ref[:, QKV_WIDTH:])
    for (_, d), ref in zip(A_PATTERNS[1:], (qkv4_ref, qkv16_ref)):
        rows = TM // d
        for r in range(d):
            for ct in range(QKV_WIDTH // 128):
                col = r * QKV_WIDTH + ct * 128
                ref[0, :, col:col + 128] = zbuf[ct, pl.ds(r, rows, stride=d), :].astype(BF16)


def _proj_ab(x, w):
    b, n, _ = x.shape
    dils = [d for _, d in A_PATTERNS]
    return pl.pallas_call(
        _proj_ab_kernel,
        grid=(b, n // TM),
        in_specs=[pl.BlockSpec((1, TM, D_MODEL), lambda i, j: (i, j, 0)),
                  pl.BlockSpec(w.shape, lambda i, j: (0, 0))],
        out_specs=[pl.BlockSpec((1, TM // d, d * QKV_WIDTH), lambda i, j: (i, j, 0)) for d in dils]
        + [pl.BlockSpec((1, TM, B_WIDTH), lambda i, j: (i, j, 0))],
        out_shape=[jax.ShapeDtypeStruct((b, n // d, d * QKV_WIDTH), BF16) for d in dils]
        + [jax.ShapeDtypeStruct((b, n, B_WIDTH), F32)],
        scratch_shapes=[pltpu.VMEM((QKV_WIDTH // 128, TM, 128), F32)],
        compiler_params=_cparams("parallel", "parallel"),
        name="proj_ab",
    )(x, w)


def _t5_bucket_np(rel):
    nb = N_BUCKETS // 2
    max_exact = nb // 2
    ret = np.where(rel > 0, nb, 0)
    n = np.abs(rel)
    nf = np.maximum(n, 1).astype(np.float32)
    large = max_exact + (np.log(nf / max_exact) / math.log(REL_MAX_DIST / max_exact)
                         * (nb - max_exact)).astype(np.int32)
    large = np.minimum(large, nb - 1)
    return ret + np.where(n < max_exact, n, large)


def _band_bias(rel_bias, dilation):
    tq, tk = DIL_SQ, DIL_SK
    band = 2 * A_HALF + 1
    bucket = _t5_bucket_np((np.arange(band) - A_HALF) * dilation)
    row = (rel_bias[jnp.asarray(bucket)].T * LOG2E).astype(F32)
    row = jnp.concatenate([row, jnp.full((A_HEADS, tk + 1 - band), NEG_INF, F32)], axis=1)
    bias = jnp.tile(row, (1, tq))[:, :tq * tk].reshape(A_HEADS, tq, tk)
    bias = bias.transpose(0, 2, 1)
    bias = bias.reshape(A_HEADS // 2, 2, tk, tq).transpose(0, 2, 1, 3).reshape(A_HEADS // 2, tk, 2 * tq)
    key = np.arange(tk)[None, :, None]
    before = jnp.asarray(key < A_HALF)
    after = jnp.asarray(key >= A_HALF + tq)
    first = jnp.where(before, NEG_INF, bias)
    return jnp.stack([bias, first, jnp.where(after, NEG_INF, bias), jnp.where(after, NEG_INF, first)])


def _dilated_kernel(q_ref, kp_ref, kc_ref, kn_ref, vp_ref, vc_ref, vn_ref, bias_ref,
                    o_ref, lse_ref, kbuf, vbuf, *, tq):
    tk = tq + 2 * A_HALF
    kbuf[0:A_HALF, :] = kp_ref[0]
    kbuf[A_HALF:A_HALF + tq, :] = kc_ref[0]
    kbuf[A_HALF + tq:, :] = kn_ref[0]
    vbuf[0:A_HALF, :] = vp_ref[0]
    vbuf[A_HALF:A_HALF + tq, :] = vc_ref[0]
    vbuf[A_HALF + tq:, :] = vn_ref[0]

    i = pl.program_id(1)
    n_sub = tq // DIL_SQ
    row_low = lax.broadcasted_iota(jnp.int32, (128, 1), 0) < HEAD_DIM
    row16 = lax.broadcasted_iota(jnp.int32, (16, 1), 0)
    ones = jnp.ones((16, DIL_SK), BF16)
    qts, vts = {}, {}
    pieces = [(pair, sub) for pair in range(A_HEADS // 2) for sub in range(n_sub)]

    def variant(sub):
        v = 0
        if sub == 0:
            v = v + jnp.where(i == 0, 1, 0)
        if sub == n_sub - 1:
            v = v + jnp.where(i == pl.num_programs(1) - 1, 2, 0)
        return v

    def scores(idx):
        pair, sub = pieces[idx]
        cols = slice(pair * 128, (pair + 1) * 128)
        if pair not in qts:
            qts[pair] = q_ref[0, :, cols].astype(F32).T
        qsub = qts[pair][:, sub * DIL_SQ:(sub + 1) * DIL_SQ]
        rhs = jnp.concatenate([jnp.where(row_low, qsub, 0.0), jnp.where(row_low, 0.0, qsub)], axis=1)
        keys = kbuf[sub * DIL_SQ:sub * DIL_SQ + DIL_SK, cols]
        return _dot(keys, rhs.astype(BF16)) + bias_ref[variant(sub), pair]

    pending = [scores(idx) for idx in range(DIL_AHEAD)]
    lse_t = [jnp.zeros((16, DIL_SQ), F32) for _ in range(n_sub)]
    for idx, (pair, sub) in enumerate(pieces):
        st = pending.pop(0)
        if idx + DIL_AHEAD < len(pieces):
            pending.append(scores(idx + DIL_AHEAD))
        cols = slice(pair * 128, (pair + 1) * 128)
        m = jnp.max(st, axis=0, keepdims=True)
        p = jnp.exp2(st - m).astype(BF16)
        if pair not in vts:
            vts[pair] = vbuf[:, cols].astype(F32).T.astype(BF16)
        vaug = jnp.concatenate([vts[pair][:, sub * DIL_SQ:sub * DIL_SQ + DIL_SK], ones], axis=0)
        ot = _dot(vaug, p)
        l = ot[128:129]
        lse2 = m + jnp.log2(l)
        o_pair = jnp.concatenate([ot[:HEAD_DIM, :DIL_SQ] / l[:, :DIL_SQ],
                                  ot[HEAD_DIM:128, DIL_SQ:] / l[:, DIL_SQ:]], axis=0)
        o_ref[0, sub * DIL_SQ:(sub + 1) * DIL_SQ, cols] = o_pair.T.astype(BF16)
        lse_t[sub] = jnp.where(row16 == 2 * pair, lse2[:, :DIL_SQ],
                               jnp.where(row16 == 2 * pair + 1, lse2[:, DIL_SQ:], lse_t[sub]))
    for sub in range(n_sub):
        full = jnp.concatenate([lse_t[sub], jnp.zeros((128 - 16, DIL_SQ), F32)], axis=0)
        lse_ref[0, sub * DIL_SQ:(sub + 1) * DIL_SQ, :] = full.T


def _dilated(view, rel_bias, dilation):
    b, seq, _ = view.shape
    tq = min(DIL_TQ, seq)
    tk = tq + 2 * A_HALF
    bias = _band_bias(rel_bias, dilation)
    per = tq // A_HALF
    last = seq // A_HALF - 1
    n_tiles = seq // tq

    def cur(c):
        return pl.BlockSpec((1, tq, A_WIDTH), lambda bi, i, r: (bi, i, 3 * r + c))

    def prev(c):
        return pl.BlockSpec((1, A_HALF, A_WIDTH),
                            lambda bi, i, r: (bi, jnp.maximum(i * per - 1, 0), 3 * r + c))

    def nxt(c):
        return pl.BlockSpec((1, A_HALF, A_WIDTH),
                            lambda bi, i, r: (bi, jnp.minimum((i + 1) * per, last), 3 * r + c))

    o, lse = pl.pallas_call(
        functools.partial(_dilated_kernel, tq=tq),
        grid=(b, seq // tq, dilation),
        in_specs=[cur(0), prev(1), cur(1), nxt(1), prev(2), cur(2), nxt(2),
                  pl.BlockSpec(bias.shape, lambda bi, i, r: (0, 0, 0, 0))],
        out_specs=[pl.BlockSpec((1, tq, A_WIDTH), lambda bi, i, r: (bi, i, r)),
                   pl.BlockSpec((1, tq, 128), lambda bi, i, r: (bi, i, r))],
        out_shape=[jax.ShapeDtypeStruct((b, seq, dilation * A_WIDTH), BF16),
                   jax.ShapeDtypeStruct((b, seq, dilation * 128), F32)],
        scratch_shapes=[pltpu.VMEM((tk, A_WIDTH), BF16), pltpu.VMEM((tk, A_WIDTH), BF16)],
        compiler_params=_cparams("parallel", "parallel", "parallel"),
        name=f"dilated_d{dilation}",
    )(view, view, view, view, view, view, view, bias)
    return o, lse


def _mixture_kernel(o1_ref, o4_ref, o16_ref, l1_ref, l4_ref, l16_ref, expand_ref, out_ref, obuf, lbuf):
    for idx, (d, o_ref, l_ref) in enumerate(((A_PATTERNS[1][1], o4_ref, l4_ref), (A_PATTERNS[2][1], o16_ref, l16_ref))):
        rows = TM // d
        for r in range(d):
            for ct in range(A_WIDTH // 128):
                col = r * A_WIDTH + ct * 128
                obuf[idx, ct, pl.ds(r, rows, stride=d), :] = o_ref[0, :, col:col + 128].astype(F32)
            lbuf[idx, pl.ds(r, rows, stride=d), :] = l_ref[0, :, r * 128:(r + 1) * 128]
    ls = [l1_ref[0], lbuf[0], lbuf[1]]
    mx = jnp.maximum(jnp.maximum(ls[0], ls[1]), ls[2])
    es = [jnp.exp2(l - mx) for l in ls]
    inv = 1.0 / (es[0] + es[1] + es[2])
    ws = []
    for e in es:
        wh, wl = _split(e * inv)
        ws.append(_dot(wh, expand_ref[...]) + _dot(wl, expand_ref[...]))
    for pair in range(A_HEADS // 2):
        cols = slice(pair * 128, (pair + 1) * 128)
        os_ = [o1_ref[0, :, cols].astype(F32), obuf[0, pair], obuf[1, pair]]
        acc = None
        for g in range(3):
            t = ws[g][:, cols] * os_[g]
            acc = t if acc is None else acc + t
        out_ref[0, :, cols] = acc.astype(BF16)


def _mixture(os_, ls_):
    b, n, _ = os_[0].shape
    dils = [d for _, d in A_PATTERNS]
    expand = jnp.asarray(np.arange(128)[:, None] == np.arange(A_WIDTH)[None, :] // HEAD_DIM, BF16)
    return pl.pallas_call(
        _mixture_kernel,
        grid=(b, n // TM),
        in_specs=[pl.BlockSpec((1, TM // d, d * A_WIDTH), lambda i, j: (i, j, 0)) for d in dils]
        + [pl.BlockSpec((1, TM // d, d * 128), lambda i, j: (i, j, 0)) for d in dils]
        + [pl.BlockSpec((128, A_WIDTH), lambda i, j: (0, 0))],
        out_specs=pl.BlockSpec((1, TM, A_WIDTH), lambda i, j: (i, j, 0)),
        out_shape=jax.ShapeDtypeStruct((b, n, A_WIDTH), BF16),
        scratch_shapes=[pltpu.VMEM((2, A_WIDTH // 128, TM, 128), F32), pltpu.VMEM((2, TM, 128), F32)],
        compiler_params=_cparams("parallel", "parallel"),
        name="dilated_mixture",
    )(*os_, *ls_, expand)


def _group_mean_matrix():
    g = np.kron(np.eye(4), np.full((64, 64), 1.0 / 64))
    return jnp.asarray(g, BF16)


def _fnet_pre_kernel(u_ref, gm_ref, gain_ref, ch_ref, cl_ref, sh_ref, sl_ref, yr_ref, yi_ref):
    u = u_ref[0]
    gm = gm_ref[...]
    uh, ul = _split(u)
    mean = _dot(uh, gm) + _dot(ul, gm)
    xc = u - mean
    sh, sl = _split(xc * xc)
    var = _dot(sh, gm) + _dot(sl, gm)
    un = xc * lax.rsqrt(var + LN_EPS) * gain_ref[...]
    nh, nl = _split(un)
    yr_ref[0] = _dot3(nh, nl, ch_ref[...], cl_ref[...])
    yi_ref[0] = -_dot3(nh, nl, sh_ref[...], sl_ref[...])


def _fnet_stage1_kernel(yr_ref, yi_ref, a1h_ref, a1l_ref, a2h_ref, a2l_ref, zr_ref, zi_ref):
    y = jnp.concatenate([yr_ref[0], yi_ref[0]], axis=0)
    yh, yl = _split(y)
    zr_ref[0] = _dot3(a1h_ref[...], a1l_ref[...], yh, yl)
    zi_ref[0] = _dot3(a2h_ref[...], a2l_ref[...], yh, yl)


def _fnet_stage2_kernel(zr_ref, zi_ref, tch_ref, tcl_ref, tsh_ref, tsl_ref, w_ref, o_ref, *, kc, scale):
    for kk in range(kc):
        rh, rl = _split(zr_ref[0, kk])
        ih, il = _split(zi_ref[0, kk])
        f = _dot3(tch_ref[kk], tcl_ref[kk], rh, rl) + _dot3(tsh_ref[kk], tsl_ref[kk], ih, il)
        f = f * scale
        o_ref[0, :, kk * B_WIDTH:(kk + 1) * B_WIDTH] = _dot(f.astype(BF16), w_ref[...]).astype(BF16)


def _fnet_tables(n):
    n2 = FFT_N2
    n1 = n // n2
    c = np.arange(64)
    ang = 2 * np.pi * np.outer(c, c) / 64
    cbd = np.kron(np.eye(4), np.cos(ang))
    sbd = np.kron(np.eye(4), np.sin(ang))
    k1 = np.arange(n1)
    ang1 = 2 * np.pi * np.outer(k1, k1) / n1
    c1, s1 = np.cos(ang1), np.sin(ang1)
    a1 = np.concatenate([c1, s1], axis=1)
    a2 = np.concatenate([-s1, c1], axis=1)
    k2 = np.arange(n2)
    npr = k1[:, None, None] + n1 * k2[None, :, None]
    prod = (npr * k2[None, None, :]) % n
    ang2 = 2 * np.pi * prod / n
    return (_np_split(cbd), _np_split(sbd), _np_split(a1), _np_split(a2),
            _np_split(np.cos(ang2)), _np_split(np.sin(ang2)))


def _fnet(u, fnet_g, fnet_w):
    b, n, _ = u.shape
    n2 = FFT_N2
    n1 = n // n2
    (ch, cl), (sh, sl), (a1h, a1l), (a2h, a2l), (tch, tcl), (tsh, tsl) = _fnet_tables(n)
    gain = fnet_g.reshape(1, B_WIDTH)
    wbd = jax.scipy.linalg.block_diag(*[fnet_w[g] for g in range(4)]).astype(BF16)

    tok = pl.BlockSpec((1, TM, B_WIDTH), lambda i, j: (i, j, 0))
    mat = pl.BlockSpec((B_WIDTH, B_WIDTH), lambda i, j: (0, 0))
    yr, yi = pl.pallas_call(
        _fnet_pre_kernel,
        grid=(b, n // TM),
        in_specs=[tok, mat, pl.BlockSpec((1, B_WIDTH), lambda i, j: (0, 0)), mat, mat, mat, mat],
        out_specs=[tok, tok],
        out_shape=[jax.ShapeDtypeStruct((b, n, B_WIDTH), F32)] * 2,
        compiler_params=_cparams("parallel", "parallel"),
        name="fnet_pre",
    )(u, _group_mean_matrix(), gain, ch, cl, sh, sl)

    cols = n2 * B_WIDTH
    tc = 2048
    yspec = pl.BlockSpec((1, n1, tc), lambda i, j: (i, 0, j))
    aspec = pl.BlockSpec((n1, 2 * n1), lambda i, j: (0, 0))
    zr, zi = pl.pallas_call(
        _fnet_stage1_kernel,
        grid=(b, cols // tc),
        in_specs=[yspec, yspec, aspec, aspec, aspec, aspec],
        out_specs=[yspec, yspec],
        out_shape=[jax.ShapeDtypeStruct((b, n1, cols), F32)] * 2,
        compiler_params=_cparams("parallel", "parallel"),
        name="fnet_stage1",
    )(yr.reshape(b, n1, cols), yi.reshape(b, n1, cols), a1h, a1l, a2h, a2l)

    kc = 8
    zspec = pl.BlockSpec((1, kc, n2, B_WIDTH), lambda i, j: (i, j, 0, 0))
    tspec = pl.BlockSpec((kc, n2, n2), lambda i, j: (j, 0, 0))
    out = pl.pallas_call(
        functools.partial(_fnet_stage2_kernel, kc=kc, scale=1.0 / math.sqrt(64.0 * n)),
        grid=(b, n1 // kc),
        in_specs=[zspec, zspec, tspec, tspec, tspec, tspec,
                  pl.BlockSpec((B_WIDTH, B_WIDTH), lambda i, j: (0, 0))],
        out_specs=pl.BlockSpec((1, n2, kc * B_WIDTH), lambda i, j: (i, 0, j)),
        out_shape=jax.ShapeDtypeStruct((b, n2, n1 * B_WIDTH), BF16),
        compiler_params=_cparams("parallel", "parallel"),
        name="fnet_stage2",
    )(zr.reshape(b, n1, n2, B_WIDTH), zi.reshape(b, n1, n2, B_WIDTH), tch, tcl, tsh, tsl, wbd)
    return out.reshape(b, n, B_WIDTH)


def _outproj_kernel(a_ref, c_ref, wa_ref, wc_ref, x_ref, g_ref, b_ref, o_ref, *, a_transposed):
    halves = [slice(0, TM // 2), slice(TM // 2, TM)]
    hs = []
    for rows in halves:
        if a_transposed:
            h = _dot_tn(a_ref[0, :, rows], wa_ref[...])
        else:
            h = _dot(a_ref[0, rows, :], wa_ref[...])
        hs.append(h + _dot(c_ref[0, rows, :], wc_ref[...]))
    for rows, h in zip(halves, hs):
        o_ref[0, rows, :] = _layer_norm(DN_ALPHA * x_ref[0, rows, :] + h, g_ref[...], b_ref[...])


def _outproj(a, c, w, x, g, bias, a_transposed):
    b, n, _ = x.shape
    ka = w.shape[0] - c.shape[-1]
    wa = w[:ka].astype(BF16)
    wc = w[ka:].astype(BF16)
    if a_transposed:
        aspec = pl.BlockSpec((1, ka, TM), lambda i, j: (i, 0, j))
    else:
        aspec = pl.BlockSpec((1, TM, ka), lambda i, j: (i, j, 0))
    vec = pl.BlockSpec((1, D_MODEL), lambda i, j: (0, 0))
    xspec = pl.BlockSpec((1, TM, D_MODEL), lambda i, j: (i, j, 0))
    return pl.pallas_call(
        functools.partial(_outproj_kernel, a_transposed=a_transposed),
        grid=(b, n // TM),
        in_specs=[aspec, pl.BlockSpec((1, TM, c.shape[-1]), lambda i, j: (i, j, 0)),
                  pl.BlockSpec(wa.shape, lambda i, j: (0, 0)),
                  pl.BlockSpec(wc.shape, lambda i, j: (0, 0)), xspec, vec, vec],
        out_specs=xspec,
        out_shape=jax.ShapeDtypeStruct((b, n, D_MODEL), F32),
        compiler_params=_cparams("parallel", "parallel"),
        name="outproj_ln",
    )(a, c, wa, wc, x, g.reshape(1, D_MODEL), bias.reshape(1, D_MODEL))


def _mem_kv_kernel(mem_ref, wk_ref, wvt_ref, k_ref, vt_ref):
    m = mem_ref[0].astype(BF16)
    k_ref[0] = _dot(m, wk_ref[...]).astype(BF16)
    vt_ref[0] = _dot_nt(wvt_ref[...], m).astype(BF16)


def _mem_kv(mem, w_kv):
    b, m, _ = mem.shape
    wk = w_kv[:, :D_MODEL].astype(BF16)
    wvt = w_kv[:, D_MODEL:].T.astype(BF16)
    wspec = pl.BlockSpec((D_MODEL, D_MODEL), lambda i: (0, 0))
    return pl.pallas_call(
        _mem_kv_kernel,
        grid=(b,),
        in_specs=[pl.BlockSpec((1, m, D_MODEL), lambda i: (i, 0, 0)), wspec, wspec],
        out_specs=[pl.BlockSpec((1, m, D_MODEL), lambda i: (i, 0, 0)),
                   pl.BlockSpec((1, D_MODEL, m), lambda i: (i, 0, 0))],
        out_shape=[jax.ShapeDtypeStruct((b, m, D_MODEL), BF16),
                   jax.ShapeDtypeStruct((b, D_MODEL, m), BF16)],
        compiler_params=_cparams("parallel"),
        name="mem_kv",
    )(mem, wk, wvt)


def _xattn_kernel(x_ref, wqt_ref, k_ref, vt_ref, wo_ref, g_ref, b_ref, o_ref):
    x = x_ref[0]
    xb = x.astype(BF16)
    qt = _dot_nt(wqt_ref[...], xb).astype(BF16)
    heads = [slice(h * XA_HEAD_DIM, (h + 1) * XA_HEAD_DIM) for h in range(XA_HEADS)]
    scores = [_dot(k_ref[0, :, hd], qt[hd]) for hd in heads]
    ones = jnp.ones((16, k_ref.shape[1]), BF16)
    outs = []
    for hd, st in zip(heads, scores):
        p = jnp.exp2(st - jnp.max(st, axis=0, keepdims=True)).astype(BF16)
        ot = _dot(jnp.concatenate([vt_ref[0, hd, :], ones], axis=0), p)
        outs.append((ot[:XA_HEAD_DIM] / ot[XA_HEAD_DIM:XA_HEAD_DIM + 1]).astype(BF16))
    ot_all = jnp.concatenate(outs, axis=0)
    halves = [slice(0, TM // 2), slice(TM // 2, TM)]
    hs = [_dot_tn(ot_all[:, rows], wo_ref[...]) for rows in halves]
    for rows, h in zip(halves, hs):
        o_ref[0, rows, :] = _layer_norm(DN_ALPHA * x[rows] + h, g_ref[...], b_ref[...])


def _xattn(x, mem, w_q, w_kv, w_o, g, bias):
    b, n, _ = x.shape
    m = mem.shape[1]
    k, vt = _mem_kv(mem, w_kv)
    wqt = (w_q * (XA_HEAD_DIM ** -0.5 * LOG2E)).T.astype(BF16)
    wo = w_o.astype(BF16)
    xspec = pl.BlockSpec((1, TM, D_MODEL), lambda i, j: (i, j, 0))
    wspec = pl.BlockSpec((D_MODEL, D_MODEL), lambda i, j: (0, 0))
    vec = pl.BlockSpec((1, D_MODEL), lambda i, j: (0, 0))
    return pl.pallas_call(
        _xattn_kernel,
        grid=(b, n // TM),
        in_specs=[xspec, wspec, pl.BlockSpec((1, m, D_MODEL), lambda i, j: (i, 0, 0)),
                  pl.BlockSpec((1, D_MODEL, m), lambda i, j: (i, 0, 0)), wspec, vec, vec],
        out_specs=xspec,
        out_shape=jax.ShapeDtypeStruct((b, n, D_MODEL), F32),
        compiler_params=_cparams("parallel", "parallel"),
        name="xattn_ln",
    )(x, wqt, k, vt, wo, g.reshape(1, D_MODEL), bias.reshape(1, D_MODEL))


FFN_TH = 1408


def _swiglu_kernel(x_ref, wg_ref, wu_ref, wo_ref, g_ref, b_ref, o_ref, acc_ref):
    j = pl.program_id(2)
    xb = x_ref[0].astype(BF16)
    gate = _dot(xb, wg_ref[...])
    up = _dot(xb, wu_ref[...])
    hid = (gate * (1.0 / (1.0 + jnp.exp(-gate))) * up).astype(BF16)
    last = pl.num_programs(2) - 1

    @pl.when(j == 0)
    def _():
        acc_ref[...] = _dot(hid, wo_ref[...])

    @pl.when(jnp.logical_and(j > 0, j < last))
    def _():
        acc_ref[...] += _dot(hid, wo_ref[...])

    @pl.when(j == last)
    def _():
        halves = [slice(0, TM // 2), slice(TM // 2, TM)]
        parts = [_dot(hid[rows], wo_ref[...]) for rows in halves]
        for rows, part in zip(halves, parts):
            h = DN_ALPHA * x_ref[0, rows, :] + acc_ref[rows, :] + part
            o_ref[0, rows, :] = _layer_norm(h, g_ref[...], b_ref[...])


def _swiglu(x, w_in, w_out, g, bias):
    b, n, _ = x.shape
    nj = FFN_HIDDEN // FFN_TH
    assert nj >= 2 and nj * FFN_TH == FFN_HIDDEN
    win = w_in.astype(BF16)
    wout = w_out.astype(BF16)
    xspec = pl.BlockSpec((1, TM, D_MODEL), lambda i, t, j: (i, t, 0))
    vec = pl.BlockSpec((1, D_MODEL), lambda i, t, j: (0, 0))
    return pl.pallas_call(
        _swiglu_kernel,
        grid=(b, n // TM, nj),
        in_specs=[xspec,
                  pl.BlockSpec((D_MODEL, FFN_TH), lambda i, t, j: (0, j)),
                  pl.BlockSpec((D_MODEL, FFN_TH), lambda i, t, j: (0, j + nj)),
                  pl.BlockSpec((FFN_TH, D_MODEL), lambda i, t, j: (j, 0)), vec, vec],
        out_specs=xspec,
        out_shape=jax.ShapeDtypeStruct((b, n, D_MODEL), F32),
        scratch_shapes=[pltpu.VMEM((TM, D_MODEL), F32)],
        compiler_params=_cparams("parallel", "parallel", "arbitrary"),
        name="swiglu_ln",
    )(x, win, win, wout, g.reshape(1, D_MODEL), bias.reshape(1, D_MODEL))


N_QK_HEADS = C_Q_HEADS + C_KV_HEADS
QK_ROWS = N_QK_HEADS * HEAD_DIM


def _proj_cd_kernel(x_ref, wt_ref, wu_ref, gain_ref, cos_ref, sin_ref,
                    qt_ref, k_ref, vt_ref, u_ref, kn_ref, *, tm):
    xb = x_ref[0].astype(BF16)
    u_ref[0] = _dot(xb, wu_ref[...])
    zt = _dot_nt(wt_ref[...], xb)
    z = zt[:QK_ROWS].reshape(N_QK_HEADS, HEAD_DIM, tm)
    ssq = jnp.sum(z * z, axis=1, keepdims=True)
    zn = z * lax.rsqrt(ssq * (1.0 / HEAD_DIM) + RMS_EPS) * gain_ref[...]
    half = HEAD_DIM // 2
    x1 = zn[:, :half]
    x2 = zn[:, half:]
    c = cos_ref[...][None]
    s = sin_ref[...][None]
    rot = jnp.concatenate([x1 * c - x2 * s, x1 * s + x2 * c], axis=1).reshape(QK_ROWS, tm)
    qt_ref[0] = rot[:C_WIDTH].astype(BF16)
    kb = rot[C_WIDTH:].astype(BF16)
    k_ref[0] = kb.astype(F32).T.astype(BF16)
    kf = kb.astype(F32).reshape(C_KV_HEADS, HEAD_DIM, tm)
    kn_ref[0] = jnp.sum(kf * kf, axis=1, keepdims=True)
    vt = zt[QK_ROWS:].astype(BF16)
    for c0 in range(tm // GQA_TKC):
        vt_ref[0, c0] = vt[:, c0 * GQA_TKC:(c0 + 1) * GQA_TKC]


def _rope_tables_t(n):
    rows = n // GRID_W
    row_id = jnp.broadcast_to(jnp.arange(rows)[:, None], (rows, GRID_W)).reshape(n)
    col_id = jnp.broadcast_to(jnp.arange(GRID_W)[None, :], (rows, GRID_W)).reshape(n)
    axis_dim = HEAD_DIM // 2
    freqs = ROPE_THETA ** (-jnp.arange(0, axis_dim, 2, dtype=F32) / axis_dim)
    ang = jnp.concatenate([row_id[:, None] * freqs, col_id[:, None] * freqs], axis=-1)
    return jnp.cos(ang).T, jnp.sin(ang).T


def _proj_cd(x, w_in, q_norm, k_norm):
    b, n, _ = x.shape
    tm = TM
    half = HEAD_DIM // 2
    wqk = w_in[:, :QK_ROWS].reshape(D_MODEL, N_QK_HEADS, half, 2)
    wqk = wqk.transpose(1, 3, 2, 0).reshape(QK_ROWS, D_MODEL)
    wt = jnp.concatenate([wqk, w_in[:, QK_ROWS:QK_ROWS + C_KV_WIDTH].T], axis=0).astype(BF16)
    wu = w_in[:, QK_ROWS + C_KV_WIDTH:].astype(BF16)
    qg = q_norm.reshape(half, 2).T.reshape(HEAD_DIM) * (HEAD_DIM ** -0.5 * LOG2E)
    kg = k_norm.reshape(half, 2).T.reshape(HEAD_DIM)
    gain = jnp.concatenate([jnp.tile(qg[None], (C_Q_HEADS, 1)), jnp.tile(kg[None], (C_KV_HEADS, 1))])
    gain = gain.reshape(N_QK_HEADS, HEAD_DIM, 1).astype(F32)
    cos_t, sin_t = _rope_tables_t(n)
    nc = n // GQA_TKC
    return pl.pallas_call(
        functools.partial(_proj_cd_kernel, tm=tm),
        grid=(b, n // tm),
        in_specs=[pl.BlockSpec((1, tm, D_MODEL), lambda i, j: (i, j, 0)),
                  pl.BlockSpec(wt.shape, lambda i, j: (0, 0)),
                  pl.BlockSpec(wu.shape, lambda i, j: (0, 0)),
                  pl.BlockSpec(gain.shape, lambda i, j: (0, 0, 0)),
                  pl.BlockSpec((HEAD_DIM // 2, tm), lambda i, j: (0, j)),
                  pl.BlockSpec((HEAD_DIM // 2, tm), lambda i, j: (0, j))],
        out_specs=[pl.BlockSpec((1, C_WIDTH, tm), lambda i, j: (i, 0, j)),
                   pl.BlockSpec((1, tm, C_KV_WIDTH), lambda i, j: (i, j, 0)),
                   pl.BlockSpec((1, tm // GQA_TKC, C_KV_WIDTH, GQA_TKC), lambda i, j: (i, j, 0, 0)),
                   pl.BlockSpec((1, tm, D_WIDTH), lambda i, j: (i, j, 0)),
                   pl.BlockSpec((1, C_KV_HEADS, 1, tm), lambda i, j: (i, 0, 0, j))],
        out_shape=[jax.ShapeDtypeStruct((b, C_WIDTH, n), BF16),
                   jax.ShapeDtypeStruct((b, n, C_KV_WIDTH), BF16),
                   jax.ShapeDtypeStruct((b, nc, C_KV_WIDTH, GQA_TKC), BF16),
                   jax.ShapeDtypeStruct((b, n, D_WIDTH), F32),
                   jax.ShapeDtypeStruct((b, C_KV_HEADS, 1, n), F32)],
        compiler_params=_cparams("parallel", "parallel"),
        name="proj_cd",
    )(x, wt, wu, gain, cos_t, sin_t)


def _gqa_kernel(qt_ref, k_ref, vt_ref, kn_ref, ot_ref, m_ref, l_ref, acc_ref, qpad_ref, *, tq, n):
    g = pl.program_id(1)
    row_half = lax.broadcasted_iota(jnp.int32, (128, 1), 0) // HEAD_DIM
    mine = row_half == (g % 2)
    k_max2 = jnp.max(kn_ref[0, 0], axis=1, keepdims=True)
    bound_max = jnp.zeros((1, 1), F32)
    for j in range(C_REP):
        qj = qt_ref[0, j * HEAD_DIM:(j + 1) * HEAD_DIM, :]
        q2 = jnp.concatenate([qj, qj], axis=0)
        qpad_ref[j] = jnp.where(mine, q2, jnp.zeros_like(q2))
        qf = qj.astype(F32)
        bound = jnp.sqrt(jnp.sum(qf * qf, axis=0, keepdims=True) * k_max2) * GQA_BOUND_SLACK
        m_ref[j] = bound
        bound_max = jnp.maximum(bound_max, jnp.max(bound, axis=1, keepdims=True))
    fixed_shift = bound_max[0, 0] <= GQA_BOUND_LIMIT
    acc_ref[...] = jnp.zeros(acc_ref.shape, F32)
    ones = jnp.ones((GQA_VROWS - HEAD_DIM, GQA_TKC), BF16)

    def body(c, carry, *, online):
        kchs, vchs = [], []
        for u in range(GQA_UNROLL):
            cc = c * GQA_UNROLL + u
            start = pl.multiple_of(cc * GQA_TKC, GQA_TKC)
            kchs.append(k_ref[0, pl.ds(start, GQA_TKC), :])
            vchs.append(jnp.concatenate([vt_ref[0, cc], ones], axis=0))
        pieces = [(u, j, slice(s * GQA_TW, (s + 1) * GQA_TW))
                  for u in range(GQA_UNROLL) for j in range(C_REP) for s in range(tq // GQA_TW)]

        def scores(i):
            u, j, cols = pieces[i]
            return _dot(kchs[u], qpad_ref[j, :, cols])

        pending = [scores(i) for i in range(GQA_AHEAD)]
        for i, (u, j, cols) in enumerate(pieces):
            st = pending.pop(0)
            if i + GQA_AHEAD < len(pieces):
                pending.append(scores(i + GQA_AHEAD))
            m_old = m_ref[j, :, cols]
            if not online:
                p = jnp.exp2(st - m_old)
                l_ref[j, :, cols] = l_ref[j, :, cols] + jnp.sum(p.reshape(GQA_TKC // 8, 8, GQA_TW), axis=0)
                acc_ref[j, :HEAD_DIM, cols] = acc_ref[j, :HEAD_DIM, cols] + _dot(
                    vchs[u][:HEAD_DIM], p.astype(BF16))
                continue
            m_new = jnp.maximum(m_old, jnp.max(st, axis=0, keepdims=True))
            alpha = jnp.exp2(m_old - m_new)
            p = jnp.exp2(st - m_new)
            acc_ref[j, :, cols] = alpha * acc_ref[j, :, cols] + _dot(vchs[u], p.astype(BF16))
            m_ref[j, :, cols] = m_new
        return carry

    trips = n // (GQA_TKC * GQA_UNROLL)

    @pl.when(fixed_shift)
    def _():
        l_ref[...] = jnp.zeros(l_ref.shape, F32)
        lax.fori_loop(0, trips, functools.partial(body, online=False), 0)
        for j in range(C_REP):
            acc_ref[j, HEAD_DIM:HEAD_DIM + 1, :] = jnp.sum(l_ref[j], axis=0, keepdims=True)

    @pl.when(jnp.logical_not(fixed_shift))
    def _():
        m_ref[...] = jnp.full(m_ref.shape, NEG_INF, F32)
        lax.fori_loop(0, trips, functools.partial(body, online=True), 0)

    for j in range(C_REP):
        l = acc_ref[j, HEAD_DIM:HEAD_DIM + 1, :]
        ot_ref[0, j * HEAD_DIM:(j + 1) * HEAD_DIM, :] = (acc_ref[j, :HEAD_DIM, :] / l).astype(BF16)


def _gqa(qt, k, vt, kn):
    b, _, n = qt.shape
    tq = min(GQA_TQ, n)
    nc = n // GQA_TKC
    rows = C_REP * HEAD_DIM
    return pl.pallas_call(
        functools.partial(_gqa_kernel, tq=tq, n=n),
        grid=(b, C_KV_HEADS, n // tq),
        in_specs=[pl.BlockSpec((1, rows, tq), lambda i, g, t: (i, g, t)),
                  pl.BlockSpec((1, n, 128), lambda i, g, t: (i, 0, g // 2)),
                  pl.BlockSpec((1, nc, HEAD_DIM, GQA_TKC), lambda i, g, t: (i, 0, g, 0)),
                  pl.BlockSpec((1, 1, 1, n), lambda i, g, t: (i, g, 0, 0))],
        out_specs=pl.BlockSpec((1, rows, tq), lambda i, g, t: (i, g, t)),
        out_shape=jax.ShapeDtypeStruct((b, C_WIDTH, n), BF16),
        scratch_shapes=[pltpu.VMEM((C_REP, 1, tq), F32), pltpu.VMEM((C_REP, 8, tq), F32),
                        pltpu.VMEM((C_REP, GQA_VROWS, tq), F32), pltpu.VMEM((C_REP, 128, tq), BF16)],
        compiler_params=_cparams("parallel", "parallel", "parallel"),
        name="gqa_flash",
    )(qt, k, vt, kn)


def _pool_kernel(up_ref, uc_ref, un_ref, w_ref, scale_ref, o_ref, buf, *, tm, n):
    i = pl.program_id(1)
    cur = uc_ref[0]
    buf[0:POOL_HALO, :] = jnp.where(i > 0, up_ref[0], 0.0)
    buf[POOL_HALO:POOL_HALO + tm, :] = cur
    buf[POOL_HALO + tm:, :] = jnp.where(i < pl.num_programs(1) - 1, un_ref[0], 0.0)
    lane_group = lax.broadcasted_iota(jnp.int32, (1, D_WIDTH), 1) // 64
    half_w = jnp.left_shift(1, lane_group)
    acc = jnp.zeros((tm, D_WIDTH), F32)
    for j in range(-POOL_HALO, POOL_HALO):
        inside = (j >= -half_w) & (j < half_w)
        acc = acc + jnp.where(inside, buf[POOL_HALO + j:POOL_HALO + j + tm, :], 0.0)
    t = i * tm + lax.broadcasted_iota(jnp.int32, (tm, 1), 0)
    cnt = jnp.minimum(t + half_w, n) - jnp.maximum(t - half_w, 0)
    mixed = (acc / cnt.astype(F32) - cur).astype(BF16)
    o_ref[0] = (_dot(mixed, w_ref[...]) * scale_ref[...]).astype(BF16)


def _pool(u, pool_w, pool_scale):
    b, n, _ = u.shape
    tm = TM
    per = tm // POOL_HALO
    last = n // POOL_HALO - 1
    wbd = jax.scipy.linalg.block_diag(*[pool_w[g] for g in range(4)]).astype(BF16)
    return pl.pallas_call(
        functools.partial(_pool_kernel, tm=tm, n=n),
        grid=(b, n // tm),
        in_specs=[pl.BlockSpec((1, POOL_HALO, D_WIDTH), lambda i, j: (i, jnp.maximum(j * per - 1, 0), 0)),
                  pl.BlockSpec((1, tm, D_WIDTH), lambda i, j: (i, j, 0)),
                  pl.BlockSpec((1, POOL_HALO, D_WIDTH), lambda i, j: (i, jnp.minimum((j + 1) * per, last), 0)),
                  pl.BlockSpec((D_WIDTH, D_WIDTH), lambda i, j: (0, 0)),
                  pl.BlockSpec((1, D_WIDTH), lambda i, j: (0, 0))],
        out_specs=pl.BlockSpec((1, tm, D_WIDTH), lambda i, j: (i, j, 0)),
        out_shape=jax.ShapeDtypeStruct((b, n, D_WIDTH), BF16),
        scratch_shapes=[pltpu.VMEM((tm + 2 * POOL_HALO, D_WIDTH), F32)],
        compiler_params=_cparams("parallel", "parallel"),
        name="pool",
    )(u, u, u, wbd, pool_scale.reshape(1, D_WIDTH))


def _trunk(x, mem, rel_bias, ab_w_in, ab_fnet_g, ab_fnet_w, ab_w_out,
           cd_w_in, cd_q_norm, cd_k_norm, cd_pool_w, cd_pool_scale, cd_w_out,
           xa_w_q, xa_w_kv, xa_w_o, ffn_w_in, ffn_w_out, ln_g, ln_b):
    for layer in range(DEPTH):
        i = layer // 2
        if layer % 2 == 0:
            w_in = ab_w_in[i]
            w_in = jnp.concatenate([w_in[:, :A_WIDTH] * (HEAD_DIM ** -0.5 * LOG2E), w_in[:, A_WIDTH:]], axis=1)
            *views, u = _proj_ab(x, w_in.astype(BF16))
            outs = [_dilated(view, rel_bias, d) for view, (_, d) in zip(views, A_PATTERNS)]
            o_a = _mixture([o for o, _ in outs], [l for _, l in outs])
            o_b = _fnet(u, ab_fnet_g[i], ab_fnet_w[i])
            x = _outproj(o_a, o_b, ab_w_out[i], x, ln_g[layer, 0], ln_b[layer, 0], a_transposed=False)
        else:
            qt, k, vt, u, kn = _proj_cd(x, cd_w_in[i], cd_q_norm[i], cd_k_norm[i])
            o_c = _gqa(qt, k, vt, kn)
            o_d = _pool(u, cd_pool_w[i], cd_pool_scale[i])
            x = _outproj(o_c, o_d, cd_w_out[i], x, ln_g[layer, 0], ln_b[layer, 0], a_transposed=True)
        x = _xattn(x, mem, xa_w_q[layer], xa_w_kv[layer], xa_w_o[layer], ln_g[layer, 1], ln_b[layer, 1])
        x = _swiglu(x, ffn_w_in[layer], ffn_w_out[layer], ln_g[layer, 2], ln_b[layer, 2])
    return x


def kernel(x_prompt, x_sample, mem_prompt, mem_sample, rel_bias, ab_w_in, ab_fnet_g, ab_fnet_w, ab_w_out, cd_w_in, cd_q_norm, cd_k_norm, cd_pool_w, cd_pool_scale, cd_w_out, xa_w_q, xa_w_kv, xa_w_o, ffn_w_in, ffn_w_out, ln_g, ln_b):
    params = (rel_bias, ab_w_in, ab_fnet_g, ab_fnet_w, ab_w_out,
              cd_w_in, cd_q_norm, cd_k_norm, cd_pool_w, cd_pool_scale, cd_w_out,
              xa_w_q, xa_w_kv, xa_w_o, ffn_w_in, ffn_w_out, ln_g, ln_b)
    return (_trunk(x_prompt, mem_prompt, *params), _trunk(x_sample, mem_sample, *params))
```

```python
import functools
import math

import numpy as np
import jax
import jax.numpy as jnp
from jax import lax
from jax.experimental import pallas as pl
from jax.experimental.pallas import tpu as pltpu

F32 = jnp.float32
BF16 = jnp.bfloat16

D_MODEL = 1024
HEAD_DIM = 64
GRID_W = 64
LN_EPS = 1e-5
RMS_EPS = 1e-6
NEG_INF = -1e30
DEPTH = 2
A_HEADS = 12
A_WIDTH = A_HEADS * HEAD_DIM
A_PATTERNS = ((128, 1), (512, 4), (2048, 16))
A_HALF = 64
QKV_WIDTH = 3 * A_WIDTH
N_BUCKETS = 32
REL_MAX_DIST = 1024
B_WIDTH = 256
C_Q_HEADS = 12
C_KV_HEADS = 4
C_REP = C_Q_HEADS // C_KV_HEADS
C_WIDTH = C_Q_HEADS * HEAD_DIM
C_KV_WIDTH = C_KV_HEADS * HEAD_DIM
ROPE_THETA = 10000.0
POOL_WINDOWS = (2, 4, 8, 16)
POOL_HALO = 8
D_WIDTH = 256
XA_HEADS = 4
XA_HEAD_DIM = D_MODEL // XA_HEADS
FFN_HIDDEN = 2816
DN_ALPHA = (2 * DEPTH) ** 0.25
LOG2E = 1.4426950408889634

VMEM_LIMIT = 56 * 1024 * 1024
TM = 512
FFT_N2 = 128
DIL_AHEAD = 3
DIL_TQ = 512
DIL_SQ = 128
DIL_SK = DIL_SQ + 2 * A_HALF
GQA_TQ = 1024
GQA_TW = 256
GQA_TKC = 256
GQA_VROWS = HEAD_DIM + 16
GQA_UNROLL = 8
GQA_AHEAD = 5
GQA_BOUND_SLACK = 1.0 + 2.0 ** -10
GQA_BOUND_LIMIT = 60.0


def _cparams(*sem):
    return pltpu.CompilerParams(dimension_semantics=sem, vmem_limit_bytes=VMEM_LIMIT)


def _dot(a, b):
    return jnp.dot(a, b, preferred_element_type=F32)


def _dot_nt(a, b):
    return lax.dot_general(a, b, (((1,), (1,)), ((), ())), preferred_element_type=F32)


def _dot_tn(a, b):
    return lax.dot_general(a, b, (((0,), (0,)), ((), ())), preferred_element_type=F32)


def _split(x):
    hi = x.astype(BF16)
    lo = (x - hi.astype(F32)).astype(BF16)
    return hi, lo


def _dot3(ah, al, bh, bl):
    return _dot(ah, bh) + _dot(al, bh) + _dot(ah, bl)


def _np_split(x):
    x = np.asarray(x, np.float32)
    hi = x.astype(BF16)
    lo = (x - hi.astype(np.float32)).astype(BF16)
    return jnp.asarray(hi), jnp.asarray(lo)


def _layer_norm(h, g, b):
    mu = jnp.mean(h, axis=-1, keepdims=True)
    xc = h - mu
    var = jnp.mean(xc * xc, axis=-1, keepdims=True)
    return xc * lax.rsqrt(var + LN_EPS) * g + b


def _proj_ab_kernel(x_ref, w_ref, qkv1_ref, qkv4_ref, qkv16_ref, u_ref, zbuf):
    xb = x_ref[0].astype(BF16)
    chunks = list(range(0, QKV_WIDTH, 256))

    halves = [slice(0, TM // 2), slice(TM // 2, TM)]
    xh = [xb[rows] for rows in halves]
    for c in chunks:
        for rows, xr in zip(halves, xh):
            z = _dot(xr, w_ref[:, c:c + 256])
            qkv1_ref[0, rows, c:c + 256] = z.astype(BF16)
            zbuf[c // 128, rows, :] = z[:, :128]
            zbuf[c // 128 + 1, rows, :] = z[:, 128:]
        for ct in (c // 128, c // 128 + 1):
            for (_, d), ref in zip(A_PATTERNS[1:], (qkv4_ref, qkv16_ref)):
                rows = TM // d
                for r in range(d):
                    col = r * QKV_WIDTH + ct * 128
                    ref[0, :, col:col + 128] = zbuf[ct, pl.ds(r, rows, stride=d), :].astype(BF16)
    for rows, xr in zip(halves, xh):
        u_ref[0, rows, :] = _dot(xr, w_ref[:, QKV_WIDTH:])


def _proj_ab(x, w):
    b, n, _ = x.shape
    dils = [d for _, d in A_PATTERNS]
    return pl.pallas_call(
        _proj_ab_kernel,
        grid=(b, n // TM),
        in_specs=[pl.BlockSpec((1, TM, D_MODEL), lambda i, j: (i, j, 0)),
                  pl.BlockSpec(w.shape, lambda i, j: (0, 0))],
        out_specs=[pl.BlockSpec((1, TM // d, d * QKV_WIDTH), lambda i, j: (i, j, 0)) for d in dils]
        + [pl.BlockSpec((1, TM, B_WIDTH), lambda i, j: (i, j, 0))],
        out_shape=[jax.ShapeDtypeStruct((b, n // d, d * QKV_WIDTH), BF16) for d in dils]
        + [jax.ShapeDtypeStruct((b, n, B_WIDTH), F32)],
        scratch_shapes=[pltpu.VMEM((QKV_WIDTH // 128, TM, 128), F32)],
        compiler_params=_cparams("parallel", "parallel"),
        name="proj_ab",
    )(x, w)


def _t5_bucket_np(rel):
    nb = N_BUCKETS // 2
    max_exact = nb // 2
    ret = np.where(rel > 0, nb, 0)
    n = np.abs(rel)
    nf = np.maximum(n, 1).astype(np.float32)
    large = max_exact + (np.log(nf / max_exact) / math.log(REL_MAX_DIST / max_exact)
                         * (nb - max_exact)).astype(np.int32)
    large = np.minimum(large, nb - 1)
    return ret + np.where(n < max_exact, n, large)


def _band_bias(rel_bias, dilation):
    tq, tk = DIL_SQ, DIL_SK
    band = 2 * A_HALF + 1
    bucket = _t5_bucket_np((np.arange(band) - A_HALF) * dilation)
    row = (rel_bias[jnp.asarray(bucket)].T * LOG2E).astype(F32)
    row = jnp.concatenate([row, jnp.full((A_HEADS, tk + 1 - band), NEG_INF, F32)], axis=1)
    bias = jnp.tile(row, (1, tq))[:, :tq * tk].reshape(A_HEADS, tq, tk)
    bias = bias.transpose(0, 2, 1)
    bias = bias.reshape(A_HEADS // 2, 2, tk, tq).transpose(0, 2, 1, 3).reshape(A_HEADS // 2, tk, 2 * tq)
    key = np.arange(tk)[None, :, None]
    before = jnp.asarray(key < A_HALF)
    after = jnp.asarray(key >= A_HALF + tq)
    first = jnp.where(before, NEG_INF, bias)
    return jnp.stack([bias, first, jnp.where(after, NEG_INF, bias), jnp.where(after, NEG_INF, first)])


def _dilated_kernel(q_ref, kp_ref, kc_ref, kn_ref, vp_ref, vc_ref, vn_ref, bias_ref,
                    o_ref, lse_ref, kbuf, vbuf, *, tq):
    tk = tq + 2 * A_HALF
    kbuf[0:A_HALF, :] = kp_ref[0]
    kbuf[A_HALF:A_HALF + tq, :] = kc_ref[0]
    kbuf[A_HALF + tq:, :] = kn_ref[0]
    vbuf[0:A_HALF, :] = vp_ref[0]
    vbuf[A_HALF:A_HALF + tq, :] = vc_ref[0]
    vbuf[A_HALF + tq:, :] = vn_ref[0]

    i = pl.program_id(1)
    n_sub = tq // DIL_SQ
    row_low = lax.broadcasted_iota(jnp.int32, (128, 1), 0) < HEAD_DIM
    row16 = lax.broadcasted_iota(jnp.int32, (16, 1), 0)
    ones = jnp.ones((16, DIL_SK), BF16)
    qts, vts = {}, {}
    pieces = [(pair, sub) for pair in range(A_HEADS // 2) for sub in range(n_sub)]

    def variant(sub):
        v = 0
        if sub == 0:
            v = v + jnp.where(i == 0, 1, 0)
        if sub == n_sub - 1:
            v = v + jnp.where(i == pl.num_programs(1) - 1, 2, 0)
        return v

    def scores(idx):
        pair, sub = pieces[idx]
        cols = slice(pair * 128, (pair + 1) * 128)
        if pair not in qts:
            qts[pair] = q_ref[0, :, cols].astype(F32).T
        qsub = qts[pair][:, sub * DIL_SQ:(sub + 1) * DIL_SQ]
        rhs = jnp.concatenate([jnp.where(row_low, qsub, 0.0), jnp.where(row_low, 0.0, qsub)], axis=1)
        keys = kbuf[sub * DIL_SQ:sub * DIL_SQ + DIL_SK, cols]
        return _dot(keys, rhs.astype(BF16)) + bias_ref[variant(sub), pair]

    pending = [scores(idx) for idx in range(DIL_AHEAD)]
    lse_t = [jnp.zeros((16, DIL_SQ), F32) for _ in range(n_sub)]
    for idx, (pair, sub) in enumerate(pieces):
        st = pending.pop(0)
        if idx + DIL_AHEAD < len(pieces):
            pending.append(scores(idx + DIL_AHEAD))
        cols = slice(pair * 128, (pair + 1) * 128)
        m = jnp.max(st, axis=0, keepdims=True)
        p = jnp.exp2(st - m).astype(BF16)
        if pair not in vts:
            vts[pair] = vbuf[:, cols].astype(F32).T.astype(BF16)
        vaug = jnp.concatenate([vts[pair][:, sub * DIL_SQ:sub * DIL_SQ + DIL_SK], ones], axis=0)
        ot = _dot(vaug, p)
        l = ot[128:129]
        lse2 = m + jnp.log2(l)
        o_pair = jnp.concatenate([ot[:HEAD_DIM, :DIL_SQ] / l[:, :DIL_SQ],
                                  ot[HEAD_DIM:128, DIL_SQ:] / l[:, DIL_SQ:]], axis=0)
        o_ref[0, sub * DIL_SQ:(sub + 1) * DIL_SQ, cols] = o_pair.T.astype(BF16)
        lse_t[sub] = jnp.where(row16 == 2 * pair, lse2[:, :DIL_SQ],
                               jnp.where(row16 == 2 * pair + 1, lse2[:, DIL_SQ:], lse_t[sub]))
    for sub in range(n_sub):
        full = jnp.concatenate([lse_t[sub], jnp.zeros((128 - 16, DIL_SQ), F32)], axis=0)
        lse_ref[0, sub * DIL_SQ:(sub + 1) * DIL_SQ, :] = full.T


def _dilated(view, rel_bias, dilation):
    b, seq, _ = view.shape
    tq = min(DIL_TQ, seq)
    tk = tq + 2 * A_HALF
    bias = _band_bias(rel_bias, dilation)
    per = tq // A_HALF
    last = seq // A_HALF - 1
    n_tiles = seq // tq

    def cur(c):
        return pl.BlockSpec((1, tq, A_WIDTH), lambda bi, i, r: (bi, i, 3 * r + c))

    def prev(c):
        return pl.BlockSpec((1, A_HALF, A_WIDTH),
                            lambda bi, i, r: (bi, jnp.maximum(i * per - 1, 0), 3 * r + c))

    def nxt(c):
        return pl.BlockSpec((1, A_HALF, A_WIDTH),
                            lambda bi, i, r: (bi, jnp.minimum((i + 1) * per, last), 3 * r + c))

    o, lse = pl.pallas_call(
        functools.partial(_dilated_kernel, tq=tq),
        grid=(b, seq // tq, dilation),
        in_specs=[cur(0), prev(1), cur(1), nxt(1), prev(2), cur(2), nxt(2),
                  pl.BlockSpec(bias.shape, lambda bi, i, r: (0, 0, 0, 0))],
        out_specs=[pl.BlockSpec((1, tq, A_WIDTH), lambda bi, i, r: (bi, i, r)),
                   pl.BlockSpec((1, tq, 128), lambda bi, i, r: (bi, i, r))],
        out_shape=[jax.ShapeDtypeStruct((b, seq, dilation * A_WIDTH), BF16),
                   jax.ShapeDtypeStruct((b, seq, dilation * 128), F32)],
        scratch_shapes=[pltpu.VMEM((tk, A_WIDTH), BF16), pltpu.VMEM((tk, A_WIDTH), BF16)],
        compiler_params=_cparams("parallel", "parallel", "parallel"),
        name=f"dilated_d{dilation}",
    )(view, view, view, view, view, view, view, bias)
    return o, lse


def _mixture_kernel(o1_ref, o4_ref, o16_ref, l1_ref, l4_ref, l16_ref, expand_ref, out_ref, obuf, lbuf):
    for idx, (d, o_ref, l_ref) in enumerate(((A_PATTERNS[1][1], o4_ref, l4_ref), (A_PATTERNS[2][1], o16_ref, l16_ref))):
        rows = TM // d
        for r in range(d):
            for ct in range(A_WIDTH // 128):
                col = r * A_WIDTH + ct * 128
                obuf[idx, ct, pl.ds(r, rows, stride=d), :] = o_ref[0, :, col:col + 128].astype(F32)
            lbuf[idx, pl.ds(r, rows, stride=d), :] = l_ref[0, :, r * 128:(r + 1) * 128]
    ls = [l1_ref[0], lbuf[0], lbuf[1]]
    mx = jnp.maximum(jnp.maximum(ls[0], ls[1]), ls[2])
    es = [jnp.exp2(l - mx) for l in ls]
    inv = 1.0 / (es[0] + es[1] + es[2])
    ws = []
    for e in es:
        wh, wl = _split(e * inv)
        ws.append(_dot(wh, expand_ref[...]) + _dot(wl, expand_ref[...]))
    for pair in range(A_HEADS // 2):
        cols = slice(pair * 128, (pair + 1) * 128)
        os_ = [o1_ref[0, :, cols].astype(F32), obuf[0, pair], obuf[1, pair]]
        acc = None
        for g in range(3):
            t = ws[g][:, cols] * os_[g]
            acc = t if acc is None else acc + t
        out_ref[0, :, cols] = acc.astype(BF16)


def _mixture(os_, ls_):
    b, n, _ = os_[0].shape
    dils = [d for _, d in A_PATTERNS]
    expand = jnp.asarray(np.arange(128)[:, None] == np.arange(A_WIDTH)[None, :] // HEAD_DIM, BF16)
    return pl.pallas_call(
        _mixture_kernel,
        grid=(b, n // TM),
        in_specs=[pl.BlockSpec((1, TM // d, d * A_WIDTH), lambda i, j: (i, j, 0)) for d in dils]
        + [pl.BlockSpec((1, TM // d, d * 128), lambda i, j: (i, j, 0)) for d in dils]
        + [pl.BlockSpec((128, A_WIDTH), lambda i, j: (0, 0))],
        out_specs=pl.BlockSpec((1, TM, A_WIDTH), lambda i, j: (i, j, 0)),
        out_shape=jax.ShapeDtypeStruct((b, n, A_WIDTH), BF16),
        scratch_shapes=[pltpu.VMEM((2, A_WIDTH // 128, TM, 128), F32), pltpu.VMEM((2, TM, 128), F32)],
        compiler_params=_cparams("parallel", "parallel"),
        name="dilated_mixture",
    )(*os_, *ls_, expand)


def _group_mean_matrix():
    g = np.kron(np.eye(4), np.full((64, 64), 1.0 / 64))
    return jnp.asarray(g, BF16)


def _fnet_pre_kernel(u_ref, gm_ref, gain_ref, ch_ref, cl_ref, sh_ref, sl_ref, yr_ref, yi_ref):
    u = u_ref[0]
    gm = gm_ref[...]
    uh, ul = _split(u)
    mean = _dot(uh, gm) + _dot(ul, gm)
    xc = u - mean
    sh, sl = _split(xc * xc)
    var = _dot(sh, gm) + _dot(sl, gm)
    un = xc * lax.rsqrt(var + LN_EPS) * gain_ref[...]
    nh, nl = _split(un)
    yr_ref[0] = _dot3(nh, nl, ch_ref[...], cl_ref[...])
    yi_ref[0] = -_dot3(nh, nl, sh_ref[...], sl_ref[...])


def _fnet_stage1_kernel(yr_ref, yi_ref, a1h_ref, a1l_ref, a2h_ref, a2l_ref, zr_ref, zi_ref):
    y = jnp.concatenate([yr_ref[0], yi_ref[0]], axis=0)
    yh, yl = _split(y)
    zr_ref[0] = _dot3(a1h_ref[...], a1l_ref[...], yh, yl)
    zi_ref[0] = _dot3(a2h_ref[...], a2l_ref[...], yh, yl)


def _fnet_stage2_kernel(zr_ref, zi_ref, tch_ref, tcl_ref, tsh_ref, tsl_ref, w_ref, o_ref, *, kc, scale):
    for kk in range(kc):
        rh, rl = _split(zr_ref[0, kk])
        ih, il = _split(zi_ref[0, kk])
        f = _dot3(tch_ref[kk], tcl_ref[kk], rh, rl) + _dot3(tsh_ref[kk], tsl_ref[kk], ih, il)
        f = f * scale
        o_ref[0, :, kk * B_WIDTH:(kk + 1) * B_WIDTH] = _dot(f.astype(BF16), w_ref[...]).astype(BF16)


def _fnet_tables(n):
    n2 = FFT_N2
    n1 = n // n2
    c = np.arange(64)
    ang = 2 * np.pi * np.outer(c, c) / 64
    cbd = np.kron(np.eye(4), np.cos(ang))
    sbd = np.kron(np.eye(4), np.sin(ang))
    k1 = np.arange(n1)
    ang1 = 2 * np.pi * np.outer(k1, k1) / n1
    c1, s1 = np.cos(ang1), np.sin(ang1)
    a1 = np.concatenate([c1, s1], axis=1)
    a2 = np.concatenate([-s1, c1], axis=1)
    k2 = np.arange(n2)
    npr = k1[:, None, None] + n1 * k2[None, :, None]
    prod = (npr * k2[None, None, :]) % n
    ang2 = 2 * np.pi * prod / n
    return (_np_split(cbd), _np_split(sbd), _np_split(a1), _np_split(a2),
            _np_split(np.cos(ang2)), _np_split(np.sin(ang2)))


def _fnet(u, fnet_g, fnet_w):
    b, n, _ = u.shape
    n2 = FFT_N2
    n1 = n // n2
    (ch, cl), (sh, sl), (a1h, a1l), (a2h, a2l), (tch, tcl), (tsh, tsl) = _fnet_tables(n)
    gain = fnet_g.reshape(1, B_WIDTH)
    wbd = jax.scipy.linalg.block_diag(*[fnet_w[g] for g in range(4)]).astype(BF16)

    tok = pl.BlockSpec((1, TM, B_WIDTH), lambda i, j: (i, j, 0))
    mat = pl.BlockSpec((B_WIDTH, B_WIDTH), lambda i, j: (0, 0))
    yr, yi = pl.pallas_call(
        _fnet_pre_kernel,
        grid=(b, n // TM),
        in_specs=[tok, mat, pl.BlockSpec((1, B_WIDTH), lambda i, j: (0, 0)), mat, mat, mat, mat],
        out_specs=[tok, tok],
        out_shape=[jax.ShapeDtypeStruct((b, n, B_WIDTH), F32)] * 2,
        compiler_params=_cparams("parallel", "parallel"),
        name="fnet_pre",
    )(u, _group_mean_matrix(), gain, ch, cl, sh, sl)

    cols = n2 * B_WIDTH
    tc = 2048
    yspec = pl.BlockSpec((1, n1, tc), lambda i, j: (i, 0, j))
    aspec = pl.BlockSpec((n1, 2 * n1), lambda i, j: (0, 0))
    zr, zi = pl.pallas_call(
        _fnet_stage1_kernel,
        grid=(b, cols // tc),
        in_specs=[yspec, yspec, aspec, aspec, aspec, aspec],
        out_specs=[yspec, yspec],
        out_shape=[jax.ShapeDtypeStruct((b, n1, cols), F32)] * 2,
        compiler_params=_cparams("parallel", "parallel"),
        name="fnet_stage1",
    )(yr.reshape(b, n1, cols), yi.reshape(b, n1, cols), a1h, a1l, a2h, a2l)

    kc = 8
    zspec = pl.BlockSpec((1, kc, n2, B_WIDTH), lambda i, j: (i, j, 0, 0))
    tspec = pl.BlockSpec((kc, n2, n2), lambda i, j: (j, 0, 0))
    out = pl.pallas_call(
        functools.partial(_fnet_stage2_kernel, kc=kc, scale=1.0 / math.sqrt(64.0 * n)),
        grid=(b, n1 // kc),
        in_specs=[zspec, zspec, tspec, tspec, tspec, tspec,
                  pl.BlockSpec((B_WIDTH, B_WIDTH), lambda i, j: (0, 0))],
        out_specs=pl.BlockSpec((1, n2, kc * B_WIDTH), lambda i, j: (i, 0, j)),
        out_shape=jax.ShapeDtypeStruct((b, n2, n1 * B_WIDTH), BF16),
        compiler_params=_cparams("parallel", "parallel"),
        name="fnet_stage2",
    )(zr.reshape(b, n1, n2, B_WIDTH), zi.reshape(b, n1, n2, B_WIDTH), tch, tcl, tsh, tsl, wbd)
    return out.reshape(b, n, B_WIDTH)


def _outproj_kernel(a_ref, c_ref, wa_ref, wc_ref, x_ref, g_ref, b_ref, o_ref, *, a_transposed):
    halves = [slice(0, TM // 2), slice(TM // 2, TM)]
    hs = []
    for rows in halves:
        if a_transposed:
            h = _dot_tn(a_ref[0, :, rows], wa_ref[...])
        else:
            h = _dot(a_ref[0, rows, :], wa_ref[...])
        hs.append(h + _dot(c_ref[0, rows, :], wc_ref[...]))
    for rows, h in zip(halves, hs):
        o_ref[0, rows, :] = _layer_norm(DN_ALPHA * x_ref[0, rows, :] + h, g_ref[...], b_ref[...])


def _outproj(a, c, w, x, g, bias, a_transposed):
    b, n, _ = x.shape
    ka = w.shape[0] - c.shape[-1]
    wa = w[:ka].astype(BF16)
    wc = w[ka:].astype(BF16)
    if a_transposed:
        aspec = pl.BlockSpec((1, ka, TM), lambda i, j: (i, 0, j))
    else:
        aspec = pl.BlockSpec((1, TM, ka), lambda i, j: (i, j, 0))
    vec = pl.BlockSpec((1, D_MODEL), lambda i, j: (0, 0))
    xspec = pl.BlockSpec((1, TM, D_MODEL), lambda i, j: (i, j, 0))
    return pl.pallas_call(
        functools.partial(_outproj_kernel, a_transposed=a_transposed),
        grid=(b, n // TM),
        in_specs=[aspec, pl.BlockSpec((1, TM, c.shape[-1]), lambda i, j: (i, j, 0)),
                  pl.BlockSpec(wa.shape, lambda i, j: (0, 0)),
                  pl.BlockSpec(wc.shape, lambda i, j: (0, 0)), xspec, vec, vec],
        out_specs=xspec,
        out_shape=jax.ShapeDtypeStruct((b, n, D_MODEL), F32),
        compiler_params=_cparams("parallel", "parallel"),
        name="outproj_ln",
    )(a, c, wa, wc, x, g.reshape(1, D_MODEL), bias.reshape(1, D_MODEL))


def _mem_kv_kernel(mem_ref, wk_ref, wvt_ref, k_ref, vt_ref):
    m = mem_ref[0].astype(BF16)
    k_ref[0] = _dot(m, wk_ref[...]).astype(BF16)
    vt_ref[0] = _dot_nt(wvt_ref[...], m).astype(BF16)


def _mem_kv(mem, w_kv):
    b, m, _ = mem.shape
    wk = w_kv[:, :D_MODEL].astype(BF16)
    wvt = w_kv[:, D_MODEL:].T.astype(BF16)
    wspec = pl.BlockSpec((D_MODEL, D_MODEL), lambda i: (0, 0))
    return pl.pallas_call(
        _mem_kv_kernel,
        grid=(b,),
        in_specs=[pl.BlockSpec((1, m, D_MODEL), lambda i: (i, 0, 0)), wspec, wspec],
        out_specs=[pl.BlockSpec((1, m, D_MODEL), lambda i: (i, 0, 0)),
                   pl.BlockSpec((1, D_MODEL, m), lambda i: (i, 0, 0))],
        out_shape=[jax.ShapeDtypeStruct((b, m, D_MODEL), BF16),
                   jax.ShapeDtypeStruct((b, D_MODEL, m), BF16)],
        compiler_params=_cparams("parallel"),
        name="mem_kv",
    )(mem, wk, wvt)


def _xattn_kernel(x_ref, wqt_ref, k_ref, vt_ref, wo_ref, g_ref, b_ref, o_ref):
    x = x_ref[0]
    xb = x.astype(BF16)
    qt = _dot_nt(wqt_ref[...], xb).astype(BF16)
    heads = [slice(h * XA_HEAD_DIM, (h + 1) * XA_HEAD_DIM) for h in range(XA_HEADS)]
    scores = [_dot(k_ref[0, :, hd], qt[hd]) for hd in heads]
    ones = jnp.ones((16, k_ref.shape[1]), BF16)
    outs = []
    for hd, st in zip(heads, scores):
        p = jnp.exp2(st - jnp.max(st, axis=0, keepdims=True)).astype(BF16)
        ot = _dot(jnp.concatenate([vt_ref[0, hd, :], ones], axis=0), p)
        outs.append((ot[:XA_HEAD_DIM] / ot[XA_HEAD_DIM:XA_HEAD_DIM + 1]).astype(BF16))
    ot_all = jnp.concatenate(outs, axis=0)
    halves = [slice(0, TM // 2), slice(TM // 2, TM)]
    hs = [_dot_tn(ot_all[:, rows], wo_ref[...]) for rows in halves]
    for rows, h in zip(halves, hs):
        o_ref[0, rows, :] = _layer_norm(DN_ALPHA * x[rows] + h, g_ref[...], b_ref[...])


def _xattn(x, mem, w_q, w_kv, w_o, g, bias):
    b, n, _ = x.shape
    m = mem.shape[1]
    k, vt = _mem_kv(mem, w_kv)
    wqt = (w_q * (XA_HEAD_DIM ** -0.5 * LOG2E)).T.astype(BF16)
    wo = w_o.astype(BF16)
    xspec = pl.BlockSpec((1, TM, D_MODEL), lambda i, j: (i, j, 0))
    wspec = pl.BlockSpec((D_MODEL, D_MODEL), lambda i, j: (0, 0))
    vec = pl.BlockSpec((1, D_MODEL), lambda i, j: (0, 0))
    return pl.pallas_call(
        _xattn_kernel,
        grid=(b, n // TM),
        in_specs=[xspec, wspec, pl.BlockSpec((1, m, D_MODEL), lambda i, j: (i, 0, 0)),
                  pl.BlockSpec((1, D_MODEL, m), lambda i, j: (i, 0, 0)), wspec, vec, vec],
        out_specs=xspec,
        out_shape=jax.ShapeDtypeStruct((b, n, D_MODEL), F32),
        compiler_params=_cparams("parallel", "parallel"),
        name="xattn_ln",
    )(x, wqt, k, vt, wo, g.reshape(1, D_MODEL), bias.reshape(1, D_MODEL))


def _swiglu_kernel(x_ref, win_ref, wo_ref, g_ref, b_ref, o_ref):
    halves = [slice(0, TM // 2), slice(TM // 2, TM)]
    gate_up = []
    for rows in halves:
        xb = x_ref[0, rows, :].astype(BF16)
        gate_up.append((_dot(xb, win_ref[:, :FFN_HIDDEN]), _dot(xb, win_ref[:, FFN_HIDDEN:])))
    parts = []
    for gate, up in gate_up:
        hid = (gate * (1.0 / (1.0 + jnp.exp(-gate))) * up).astype(BF16)
        parts.append(_dot(hid, wo_ref[...]))
    for rows, part in zip(halves, parts):
        o_ref[0, rows, :] = _layer_norm(DN_ALPHA * x_ref[0, rows, :] + part, g_ref[...], b_ref[...])


def _swiglu(x, w_in, w_out, g, bias):
    b, n, _ = x.shape
    win = w_in.astype(BF16)
    wout = w_out.astype(BF16)
    xspec = pl.BlockSpec((1, TM, D_MODEL), lambda i, t: (i, t, 0))
    vec = pl.BlockSpec((1, D_MODEL), lambda i, t: (0, 0))
    resident = pl.Buffered(1)
    return pl.pallas_call(
        _swiglu_kernel,
        grid=(b, n // TM),
        in_specs=[xspec,
                  pl.BlockSpec(win.shape, lambda i, t: (0, 0), pipeline_mode=resident),
                  pl.BlockSpec(wout.shape, lambda i, t: (0, 0), pipeline_mode=resident), vec, vec],
        out_specs=xspec,
        out_shape=jax.ShapeDtypeStruct((b, n, D_MODEL), F32),
        compiler_params=_cparams("parallel", "parallel"),
        name="swiglu_ln",
    )(x, win, wout, g.reshape(1, D_MODEL), bias.reshape(1, D_MODEL))


N_QK_HEADS = C_Q_HEADS + C_KV_HEADS
QK_ROWS = N_QK_HEADS * HEAD_DIM


def _proj_cd_kernel(x_ref, wt_ref, wu_ref, gain_ref, cos_ref, sin_ref,
                    qt_ref, k_ref, vt_ref, u_ref, kn_ref, *, tm):
    xb = x_ref[0].astype(BF16)
    u_ref[0] = _dot(xb, wu_ref[...])
    zt = _dot_nt(wt_ref[...], xb)
    z = zt[:QK_ROWS].reshape(N_QK_HEADS, HEAD_DIM, tm)
    ssq = jnp.sum(z * z, axis=1, keepdims=True)
    zn = z * lax.rsqrt(ssq * (1.0 / HEAD_DIM) + RMS_EPS) * gain_ref[...]
    half = HEAD_DIM // 2
    x1 = zn[:, :half]
    x2 = zn[:, half:]
    c = cos_ref[...][None]
    s = sin_ref[...][None]
    rot = jnp.concatenate([x1 * c - x2 * s, x1 * s + x2 * c], axis=1).reshape(QK_ROWS, tm)
    qt_ref[0] = rot[:C_WIDTH].astype(BF16)
    kb = rot[C_WIDTH:].astype(BF16)
    k_ref[0] = kb.astype(F32).T.astype(BF16)
    kf = kb.astype(F32).reshape(C_KV_HEADS, HEAD_DIM, tm)
    kn_ref[0] = jnp.sum(kf * kf, axis=1, keepdims=True)
    vt = zt[QK_ROWS:].astype(BF16)
    for c0 in range(tm // GQA_TKC):
        vt_ref[0, c0] = vt[:, c0 * GQA_TKC:(c0 + 1) * GQA_TKC]


def _rope_tables_t(n):
    rows = n // GRID_W
    row_id = jnp.broadcast_to(jnp.arange(rows)[:, None], (rows, GRID_W)).reshape(n)
    col_id = jnp.broadcast_to(jnp.arange(GRID_W)[None, :], (rows, GRID_W)).reshape(n)
    axis_dim = HEAD_DIM // 2
    freqs = ROPE_THETA ** (-jnp.arange(0, axis_dim, 2, dtype=F32) / axis_dim)
    ang = jnp.concatenate([row_id[:, None] * freqs, col_id[:, None] * freqs], axis=-1)
    return jnp.cos(ang).T, jnp.sin(ang).T


def _proj_cd(x, w_in, q_norm, k_norm):
    b, n, _ = x.shape
    tm = TM
    half = HEAD_DIM // 2
    wqk = w_in[:, :QK_ROWS].reshape(D_MODEL, N_QK_HEADS, half, 2)
    wqk = wqk.transpose(1, 3, 2, 0).reshape(QK_ROWS, D_MODEL)
    wt = jnp.concatenate([wqk, w_in[:, QK_ROWS:QK_ROWS + C_KV_WIDTH].T], axis=0).astype(BF16)
    wu = w_in[:, QK_ROWS + C_KV_WIDTH:].astype(BF16)
    qg = q_norm.reshape(half, 2).T.reshape(HEAD_DIM) * (HEAD_DIM ** -0.5 * LOG2E)
    kg = k_norm.reshape(half, 2).T.reshape(HEAD_DIM)
    gain = jnp.concatenate([jnp.tile(qg[None], (C_Q_HEADS, 1)), jnp.tile(kg[None], (C_KV_HEADS, 1))])
    gain = gain.reshape(N_QK_HEADS, HEAD_DIM, 1).astype(F32)
    cos_t, sin_t = _rope_tables_t(n)
    nc = n // GQA_TKC
    return pl.pallas_call(
        functools.partial(_proj_cd_kernel, tm=tm),
        grid=(b, n // tm),
        in_specs=[pl.BlockSpec((1, tm, D_MODEL), lambda i, j: (i, j, 0)),
                  pl.BlockSpec(wt.shape, lambda i, j: (0, 0)),
                  pl.BlockSpec(wu.shape, lambda i, j: (0, 0)),
                  pl.BlockSpec(gain.shape, lambda i, j: (0, 0, 0)),
                  pl.BlockSpec((HEAD_DIM // 2, tm), lambda i, j: (0, j)),
                  pl.BlockSpec((HEAD_DIM // 2, tm), lambda i, j: (0, j))],
        out_specs=[pl.BlockSpec((1, C_WIDTH, tm), lambda i, j: (i, 0, j)),
                   pl.BlockSpec((1, tm, C_KV_WIDTH), lambda i, j: (i, j, 0)),
                   pl.BlockSpec((1, tm // GQA_TKC, C_KV_WIDTH, GQA_TKC), lambda i, j: (i, j, 0, 0)),
                   pl.BlockSpec((1, tm, D_WIDTH), lambda i, j: (i, j, 0)),
                   pl.BlockSpec((1, C_KV_HEADS, 1, tm), lambda i, j: (i, 0, 0, j))],
        out_shape=[jax.ShapeDtypeStruct((b, C_WIDTH, n), BF16),
                   jax.ShapeDtypeStruct((b, n, C_KV_WIDTH), BF16),
                   jax.ShapeDtypeStruct((b, nc, C_KV_WIDTH, GQA_TKC), BF16),
                   jax.ShapeDtypeStruct((b, n, D_WIDTH), F32),
                   jax.ShapeDtypeStruct((b, C_KV_HEADS, 1, n), F32)],
        compiler_params=_cparams("parallel", "parallel"),
        name="proj_cd",
    )(x, wt, wu, gain, cos_t, sin_t)


def _gqa_kernel(qt_ref, k_ref, vt_ref, kn_ref, ot_ref, m_ref, l_ref, acc_ref, qpad_ref, *, tq, n):
    g = pl.program_id(1)
    row_half = lax.broadcasted_iota(jnp.int32, (128, 1), 0) // HEAD_DIM
    mine = row_half == (g % 2)
    k_max2 = jnp.max(kn_ref[0, 0], axis=1, keepdims=True)
    bound_max = jnp.zeros((1, 1), F32)
    for j in range(C_REP):
        qj = qt_ref[0, j * HEAD_DIM:(j + 1) * HEAD_DIM, :]
        q2 = jnp.concatenate([qj, qj], axis=0)
        qpad_ref[j] = jnp.where(mine, q2, jnp.zeros_like(q2))
        qf = qj.astype(F32)
        bound = jnp.sqrt(jnp.sum(qf * qf, axis=0, keepdims=True) * k_max2) * GQA_BOUND_SLACK
        m_ref[j] = bound
        bound_max = jnp.maximum(bound_max, jnp.max(bound, axis=1, keepdims=True))
    fixed_shift = bound_max[0, 0] <= GQA_BOUND_LIMIT
    acc_ref[...] = jnp.zeros(acc_ref.shape, F32)
    ones = jnp.ones((GQA_VROWS - HEAD_DIM, GQA_TKC), BF16)

    def body(c, carry, *, online):
        kchs, vchs = [], []
        for u in range(GQA_UNROLL):
            cc = c * GQA_UNROLL + u
            start = pl.multiple_of(cc * GQA_TKC, GQA_TKC)
            kchs.append(k_ref[0, pl.ds(start, GQA_TKC), :])
            vchs.append(jnp.concatenate([vt_ref[0, cc], ones], axis=0))
        pieces = [(u, j, slice(s * GQA_TW, (s + 1) * GQA_TW))
                  for u in range(GQA_UNROLL) for j in range(C_REP) for s in range(tq // GQA_TW)]

        def scores(i):
            u, j, cols = pieces[i]
            return _dot(kchs[u], qpad_ref[j, :, cols])

        pending = [scores(i) for i in range(GQA_AHEAD)]
        for i, (u, j, cols) in enumerate(pieces):
            st = pending.pop(0)
            if i + GQA_AHEAD < len(pieces):
                pending.append(scores(i + GQA_AHEAD))
            m_old = m_ref[j, :, cols]
            if not online:
                p = jnp.exp2(st - m_old)
                l_ref[j, :, cols] = l_ref[j, :, cols] + jnp.sum(p.reshape(GQA_TKC // 8, 8, GQA_TW), axis=0)
                acc_ref[j, :HEAD_DIM, cols] = acc_ref[j, :HEAD_DIM, cols] + _dot(
                    vchs[u][:HEAD_DIM], p.astype(BF16))
                continue
            m_new = jnp.maximum(m_old, jnp.max(st, axis=0, keepdims=True))
            alpha = jnp.exp2(m_old - m_new)
            p = jnp.exp2(st - m_new)
            acc_ref[j, :, cols] = alpha * acc_ref[j, :, cols] + _dot(vchs[u], p.astype(BF16))
            m_ref[j, :, cols] = m_new
        return carry

    trips = n // (GQA_TKC * GQA_UNROLL)

    @pl.when(fixed_shift)
    def _():
        l_ref[...] = jnp.zeros(l_ref.shape, F32)
        lax.fori_loop(0, trips, functools.partial(body, online=False), 0)
        for j in range(C_REP):
            acc_ref[j, HEAD_DIM:HEAD_DIM + 1, :] = jnp.sum(l_ref[j], axis=0, keepdims=True)

    @pl.when(jnp.logical_not(fixed_shift))
    def _():
        m_ref[...] = jnp.full(m_ref.shape, NEG_INF, F32)
        lax.fori_loop(0, trips, functools.partial(body, online=True), 0)

    for j in range(C_REP):
        l = acc_ref[j, HEAD_DIM:HEAD_DIM + 1, :]
        ot_ref[0, j * HEAD_DIM:(j + 1) * HEAD_DIM, :] = (acc_ref[j, :HEAD_DIM, :] / l).astype(BF16)


def _gqa(qt, k, vt, kn):
    b, _, n = qt.shape
    tq = min(GQA_TQ, n)
    nc = n // GQA_TKC
    rows = C_REP * HEAD_DIM
    return pl.pallas_call(
        functools.partial(_gqa_kernel, tq=tq, n=n),
        grid=(b, C_KV_HEADS, n // tq),
        in_specs=[pl.BlockSpec((1, rows, tq), lambda i, g, t: (i, g, t)),
                  pl.BlockSpec((1, n, 128), lambda i, g, t: (i, 0, g // 2)),
                  pl.BlockSpec((1, nc, HEAD_DIM, GQA_TKC), lambda i, g, t: (i, 0, g, 0)),
                  pl.BlockSpec((1, 1, 1, n), lambda i, g, t: (i, g, 0, 0))],
        out_specs=pl.BlockSpec((1, rows, tq), lambda i, g, t: (i, g, t)),
        out_shape=jax.ShapeDtypeStruct((b, C_WIDTH, n), BF16),
        scratch_shapes=[pltpu.VMEM((C_REP, 1, tq), F32), pltpu.VMEM((C_REP, 8, tq), F32),
                        pltpu.VMEM((C_REP, GQA_VROWS, tq), F32), pltpu.VMEM((C_REP, 128, tq), BF16)],
        compiler_params=_cparams("parallel", "parallel", "parallel"),
        name="gqa_flash",
    )(qt, k, vt, kn)


def _pool_kernel(up_ref, uc_ref, un_ref, w_ref, scale_ref, o_ref, buf, *, tm, n):
    i = pl.program_id(1)
    cur = uc_ref[0]
    buf[0:POOL_HALO, :] = jnp.where(i > 0, up_ref[0], 0.0)
    buf[POOL_HALO:POOL_HALO + tm, :] = cur
    buf[POOL_HALO + tm:, :] = jnp.where(i < pl.num_programs(1) - 1, un_ref[0], 0.0)
    lane_group = lax.broadcasted_iota(jnp.int32, (1, D_WIDTH), 1) // 64
    half_w = jnp.left_shift(1, lane_group)
    acc = jnp.zeros((tm, D_WIDTH), F32)
    for j in range(-POOL_HALO, POOL_HALO):
        inside = (j >= -half_w) & (j < half_w)
        acc = acc + jnp.where(inside, buf[POOL_HALO + j:POOL_HALO + j + tm, :], 0.0)
    t = i * tm + lax.broadcasted_iota(jnp.int32, (tm, 1), 0)
    cnt = jnp.minimum(t + half_w, n) - jnp.maximum(t - half_w, 0)
    mixed = (acc / cnt.astype(F32) - cur).astype(BF16)
    o_ref[0] = (_dot(mixed, w_ref[...]) * scale_ref[...]).astype(BF16)


def _pool(u, pool_w, pool_scale):
    b, n, _ = u.shape
    tm = TM
    per = tm // POOL_HALO
    last = n // POOL_HALO - 1
    wbd = jax.scipy.linalg.block_diag(*[pool_w[g] for g in range(4)]).astype(BF16)
    return pl.pallas_call(
        functools.partial(_pool_kernel, tm=tm, n=n),
        grid=(b, n // tm),
        in_specs=[pl.BlockSpec((1, POOL_HALO, D_WIDTH), lambda i, j: (i, jnp.maximum(j * per - 1, 0), 0)),
                  pl.BlockSpec((1, tm, D_WIDTH), lambda i, j: (i, j, 0)),
                  pl.BlockSpec((1, POOL_HALO, D_WIDTH), lambda i, j: (i, jnp.minimum((j + 1) * per, last), 0)),
                  pl.BlockSpec((D_WIDTH, D_WIDTH), lambda i, j: (0, 0)),
                  pl.BlockSpec((1, D_WIDTH), lambda i, j: (0, 0))],
        out_specs=pl.BlockSpec((1, tm, D_WIDTH), lambda i, j: (i, j, 0)),
        out_shape=jax.ShapeDtypeStruct((b, n, D_WIDTH), BF16),
        scratch_shapes=[pltpu.VMEM((tm + 2 * POOL_HALO, D_WIDTH), F32)],
        compiler_params=_cparams("parallel", "parallel"),
        name="pool",
    )(u, u, u, wbd, pool_scale.reshape(1, D_WIDTH))


def _trunk(x, mem, rel_bias, ab_w_in, ab_fnet_g, ab_fnet_w, ab_w_out,
           cd_w_in, cd_q_norm, cd_k_norm, cd_pool_w, cd_pool_scale, cd_w_out,
           xa_w_q, xa_w_kv, xa_w_o, ffn_w_in, ffn_w_out, ln_g, ln_b):
    for layer in range(DEPTH):
        i = layer // 2
        if layer % 2 == 0:
            w_in = ab_w_in[i]
            w_in = jnp.concatenate([w_in[:, :A_WIDTH] * (HEAD_DIM ** -0.5 * LOG2E), w_in[:, A_WIDTH:]], axis=1)
            *views, u = _proj_ab(x, w_in.astype(BF16))
            outs = [_dilated(view, rel_bias, d) for view, (_, d) in zip(views, A_PATTERNS)]
            o_a = _mixture([o for o, _ in outs], [l for _, l in outs])
            o_b = _fnet(u, ab_fnet_g[i], ab_fnet_w[i])
            x = _outproj(o_a, o_b, ab_w_out[i], x, ln_g[layer, 0], ln_b[layer, 0], a_transposed=False)
        else:
            qt, k, vt, u, kn = _proj_cd(x, cd_w_in[i], cd_q_norm[i], cd_k_norm[i])
            o_c = _gqa(qt, k, vt, kn)
            o_d = _pool(u, cd_pool_w[i], cd_pool_scale[i])
            x = _outproj(o_c, o_d, cd_w_out[i], x, ln_g[layer, 0], ln_b[layer, 0], a_transposed=True)
        x = _xattn(x, mem, xa_w_q[layer], xa_w_kv[layer], xa_w_o[layer], ln_g[layer, 1], ln_b[layer, 1])
        x = _swiglu(x, ffn_w_in[layer], ffn_w_out[layer], ln_g[layer, 2], ln_b[layer, 2])
    return x


def kernel(x_prompt, x_sample, mem_prompt, mem_sample, rel_bias, ab_w_in, ab_fnet_g, ab_fnet_w, ab_w_out, cd_w_in, cd_q_norm, cd_k_norm, cd_pool_w, cd_pool_scale, cd_w_out, xa_w_q, xa_w_kv, xa_w_o, ffn_w_in, ffn_w_out, ln_g, ln_b):
    params = (rel_bias, ab_w_in, ab_fnet_g, ab_fnet_w, ab_w_out,
              cd_w_in, cd_q_norm, cd_k_norm, cd_pool_w, cd_pool_scale, cd_w_out,
              xa_w_q, xa_w_kv, xa_w_o, ffn_w_in, ffn_w_out, ln_g, ln_b)
    return (_trunk(x_prompt, mem_prompt, *params), _trunk(x_sample, mem_sample, *params))
```

```python
import functools
import math

import numpy as np
import jax
import jax.numpy as jnp
from jax import lax
from jax.experimental import pallas as pl
from jax.experimental.pallas import tpu as pltpu

F32 = jnp.float32
BF16 = jnp.bfloat16

D_MODEL = 1024
HEAD_DIM = 64
GRID_W = 64
LN_EPS = 1e-5
RMS_EPS = 1e-6
NEG_INF = -1e30
DEPTH = 2
A_HEADS = 12
A_WIDTH = A_HEADS * HEAD_DIM
A_PATTERNS = ((128, 1), (512, 4), (2048, 16))
A_HALF = 64
QKV_WIDTH = 3 * A_WIDTH
N_BUCKETS = 32
REL_MAX_DIST = 1024
B_WIDTH = 256
C_Q_HEADS = 12
C_KV_HEADS = 4
C_REP = C_Q_HEADS // C_KV_HEADS
C_WIDTH = C_Q_HEADS * HEAD_DIM
C_KV_WIDTH = C_KV_HEADS * HEAD_DIM
ROPE_THETA = 10000.0
POOL_WINDOWS = (2, 4, 8, 16)
POOL_HALO = 8
D_WIDTH = 256
XA_HEADS = 4
XA_HEAD_DIM = D_MODEL // XA_HEADS
FFN_HIDDEN = 2816
DN_ALPHA = (2 * DEPTH) ** 0.25
LOG2E = 1.4426950408889634

VMEM_LIMIT = 56 * 1024 * 1024
TM = 512
FFT_N2 = 128
DIL_AHEAD = 3
DIL_TQ = 512
DIL_SQ = 128
DIL_SK = DIL_SQ + 2 * A_HALF
GQA_TQ = 1024
GQA_TW = 256
GQA_TKC = 256
GQA_VROWS = HEAD_DIM + 16
GQA_UNROLL = 8
GQA_AHEAD = 5
GQA_BOUND_SLACK = 1.0 + 2.0 ** -10
GQA_BOUND_LIMIT = 60.0


def _cparams(*sem):
    return pltpu.CompilerParams(dimension_semantics=sem, vmem_limit_bytes=VMEM_LIMIT)


def _dot(a, b):
    return jnp.dot(a, b, preferred_element_type=F32)


def _dot_nt(a, b):
    return lax.dot_general(a, b, (((1,), (1,)), ((), ())), preferred_element_type=F32)


def _dot_tn(a, b):
    return lax.dot_general(a, b, (((0,), (0,)), ((), ())), preferred_element_type=F32)


def _split(x):
    hi = x.astype(BF16)
    lo = (x - hi.astype(F32)).astype(BF16)
    return hi, lo


def _dot3(ah, al, bh, bl):
    return _dot(ah, bh) + _dot(al, bh) + _dot(ah, bl)


def _np_split(x):
    x = np.asarray(x, np.float32)
    hi = x.astype(BF16)
    lo = (x - hi.astype(np.float32)).astype(BF16)
    return jnp.asarray(hi), jnp.asarray(lo)


def _layer_norm(h, g, b):
    mu = jnp.mean(h, axis=-1, keepdims=True)
    xc = h - mu
    var = jnp.mean(xc * xc, axis=-1, keepdims=True)
    return xc * lax.rsqrt(var + LN_EPS) * g + b


def _proj_ab_kernel(x_ref, w_ref, qkv1_ref, qkv4_ref, qkv16_ref, u_ref, zbuf):
    xb = x_ref[0].astype(BF16)
    chunks = list(range(0, QKV_WIDTH, 256))

    for c in chunks:
        z = _dot(xb, w_ref[:, c:c + 256])
        zbuf[c // 128] = z[:, :128]
        zbuf[c // 128 + 1] = z[:, 128:]
        qkv1_ref[0, :, c:c + 256] = z.astype(BF16)
    u_ref[0] = _dot(xb, w_ref[:, QKV_WIDTH:])
    for (_, d), ref in zip(A_PATTERNS[1:], (qkv4_ref, qkv16_ref)):
        rows = TM // d
        for r in range(d):
            for ct in range(QKV_WIDTH // 128):
                col = r * QKV_WIDTH + ct * 128
                ref[0, :, col:col + 128] = zbuf[ct, pl.ds(r, rows, stride=d), :].astype(BF16)


def _proj_ab(x, w):
    b, n, _ = x.shape
    dils = [d for _, d in A_PATTERNS]
    return pl.pallas_call(
        _proj_ab_kernel,
        grid=(b, n // TM),
        in_specs=[pl.BlockSpec((1, TM, D_MODEL), lambda i, j: (i, j, 0)),
                  pl.BlockSpec(w.shape, lambda i, j: (0, 0))],
        out_specs=[pl.BlockSpec((1, TM // d, d * QKV_WIDTH), lambda i, j: (i, j, 0)) for d in dils]
        + [pl.BlockSpec((1, TM, B_WIDTH), lambda i, j: (i, j, 0))],
        out_shape=[jax.ShapeDtypeStruct((b, n // d, d * QKV_WIDTH), BF16) for d in dils]
        + [jax.ShapeDtypeStruct((b, n, B_WIDTH), F32)],
        scratch_shapes=[pltpu.VMEM((QKV_WIDTH // 128, TM, 128), F32)],
        compiler_params=_cparams("parallel", "parallel"),
        name="proj_ab",
    )(x, w)


def _t5_bucket_np(rel):
    nb = N_BUCKETS // 2
    max_exact = nb // 2
    ret = np.where(rel > 0, nb, 0)
    n = np.abs(rel)
    nf = np.maximum(n, 1).astype(np.float32)
    large = max_exact + (np.log(nf / max_exact) / math.log(REL_MAX_DIST / max_exact)
                         * (nb - max_exact)).astype(np.int32)
    large = np.minimum(large, nb - 1)
    return ret + np.where(n < max_exact, n, large)


def _band_bias(rel_bias, dilation):
    tq, tk = DIL_SQ, DIL_SK
    band = 2 * A_HALF + 1
    bucket = _t5_bucket_np((np.arange(band) - A_HALF) * dilation)
    row = (rel_bias[jnp.asarray(bucket)].T * LOG2E).astype(F32)
    row = jnp.concatenate([row, jnp.full((A_HEADS, tk + 1 - band), NEG_INF, F32)], axis=1)
    bias = jnp.tile(row, (1, tq))[:, :tq * tk].reshape(A_HEADS, tq, tk)
    bias = bias.transpose(0, 2, 1)
    bias = bias.reshape(A_HEADS // 2, 2, tk, tq).transpose(0, 2, 1, 3).reshape(A_HEADS // 2, tk, 2 * tq)
    key = np.arange(tk)[None, :, None]
    before = jnp.asarray(key < A_HALF)
    after = jnp.asarray(key >= A_HALF + tq)
    first = jnp.where(before, NEG_INF, bias)
    return jnp.stack([bias, first, jnp.where(after, NEG_INF, bias), jnp.where(after, NEG_INF, first)])


def _dilated_kernel(q_ref, kp_ref, kc_ref, kn_ref, vp_ref, vc_ref, vn_ref, bias_ref,
                    o_ref, lse_ref, kbuf, vbuf, *, tq):
    tk = tq + 2 * A_HALF
    kbuf[0:A_HALF, :] = kp_ref[0]
    kbuf[A_HALF:A_HALF + tq, :] = kc_ref[0]
    kbuf[A_HALF + tq:, :] = kn_ref[0]
    vbuf[0:A_HALF, :] = vp_ref[0]
    vbuf[A_HALF:A_HALF + tq, :] = vc_ref[0]
    vbuf[A_HALF + tq:, :] = vn_ref[0]

    i = pl.program_id(1)
    n_sub = tq // DIL_SQ
    row_low = lax.broadcasted_iota(jnp.int32, (128, 1), 0) < HEAD_DIM
    row16 = lax.broadcasted_iota(jnp.int32, (16, 1), 0)
    ones = jnp.ones((16, DIL_SK), BF16)
    qts, vts = {}, {}
    pieces = [(pair, sub) for pair in range(A_HEADS // 2) for sub in range(n_sub)]

    def variant(sub):
        v = 0
        if sub == 0:
            v = v + jnp.where(i == 0, 1, 0)
        if sub == n_sub - 1:
            v = v + jnp.where(i == pl.num_programs(1) - 1, 2, 0)
        return v

    def scores(idx):
        pair, sub = pieces[idx]
        cols = slice(pair * 128, (pair + 1) * 128)
        if pair not in qts:
            qts[pair] = q_ref[0, :, cols].astype(F32).T
        qsub = qts[pair][:, sub * DIL_SQ:(sub + 1) * DIL_SQ]
        rhs = jnp.concatenate([jnp.where(row_low, qsub, 0.0), jnp.where(row_low, 0.0, qsub)], axis=1)
        keys = kbuf[sub * DIL_SQ:sub * DIL_SQ + DIL_SK, cols]
        return _dot(keys, rhs.astype(BF16)) + bias_ref[variant(sub), pair]

    pending = [scores(idx) for idx in range(DIL_AHEAD)]
    lse_t = [jnp.zeros((16, DIL_SQ), F32) for _ in range(n_sub)]
    for idx, (pair, sub) in enumerate(pieces):
        st = pending.pop(0)
        if idx + DIL_AHEAD < len(pieces):
            pending.append(scores(idx + DIL_AHEAD))
        cols = slice(pair * 128, (pair + 1) * 128)
        m = jnp.max(st, axis=0, keepdims=True)
        p = jnp.exp2(st - m).astype(BF16)
        if pair not in vts:
            vts[pair] = vbuf[:, cols].astype(F32).T.astype(BF16)
        vaug = jnp.concatenate([vts[pair][:, sub * DIL_SQ:sub * DIL_SQ + DIL_SK], ones], axis=0)
        ot = _dot(vaug, p)
        l = ot[128:129]
        lse2 = m + jnp.log2(l)
        o_pair = jnp.concatenate([ot[:HEAD_DIM, :DIL_SQ] / l[:, :DIL_SQ],
                                  ot[HEAD_DIM:128, DIL_SQ:] / l[:, DIL_SQ:]], axis=0)
        o_ref[0, sub * DIL_SQ:(sub + 1) * DIL_SQ, cols] = o_pair.T.astype(BF16)
        lse_t[sub] = jnp.where(row16 == 2 * pair, lse2[:, :DIL_SQ],
                               jnp.where(row16 == 2 * pair + 1, lse2[:, DIL_SQ:], lse_t[sub]))
    for sub in range(n_sub):
        full = jnp.concatenate([lse_t[sub], jnp.zeros((128 - 16, DIL_SQ), F32)], axis=0)
        lse_ref[0, sub * DIL_SQ:(sub + 1) * DIL_SQ, :] = full.T


def _dilated(view, rel_bias, dilation):
    b, seq, _ = view.shape
    tq = min(DIL_TQ, seq)
    tk = tq + 2 * A_HALF
    bias = _band_bias(rel_bias, dilation)
    per = tq // A_HALF
    last = seq // A_HALF - 1
    n_tiles = seq // tq

    def cur(c):
        return pl.BlockSpec((1, tq, A_WIDTH), lambda bi, i, r: (bi, i, 3 * r + c))

    def prev(c):
        return pl.BlockSpec((1, A_HALF, A_WIDTH),
                            lambda bi, i, r: (bi, jnp.maximum(i * per - 1, 0), 3 * r + c))

    def nxt(c):
        return pl.BlockSpec((1, A_HALF, A_WIDTH),
                            lambda bi, i, r: (bi, jnp.minimum((i + 1) * per, last), 3 * r + c))

    o, lse = pl.pallas_call(
        functools.partial(_dilated_kernel, tq=tq),
        grid=(b, seq // tq, dilation),
        in_specs=[cur(0), prev(1), cur(1), nxt(1), prev(2), cur(2), nxt(2),
                  pl.BlockSpec(bias.shape, lambda bi, i, r: (0, 0, 0, 0))],
        out_specs=[pl.BlockSpec((1, tq, A_WIDTH), lambda bi, i, r: (bi, i, r)),
                   pl.BlockSpec((1, tq, 128), lambda bi, i, r: (bi, i, r))],
        out_shape=[jax.ShapeDtypeStruct((b, seq, dilation * A_WIDTH), BF16),
                   jax.ShapeDtypeStruct((b, seq, dilation * 128), F32)],
        scratch_shapes=[pltpu.VMEM((tk, A_WIDTH), BF16), pltpu.VMEM((tk, A_WIDTH), BF16)],
        compiler_params=_cparams("parallel", "parallel", "parallel"),
        name=f"dilated_d{dilation}",
    )(view, view, view, view, view, view, view, bias)
    return o, lse


def _mixture_kernel(o1_ref, o4_ref, o16_ref, l1_ref, l4_ref, l16_ref, expand_ref, out_ref, obuf, lbuf):
    for idx, (d, o_ref, l_ref) in enumerate(((A_PATTERNS[1][1], o4_ref, l4_ref), (A_PATTERNS[2][1], o16_ref, l16_ref))):
        rows = TM // d
        for r in range(d):
            for ct in range(A_WIDTH // 128):
                col = r * A_WIDTH + ct * 128
                obuf[idx, ct, pl.ds(r, rows, stride=d), :] = o_ref[0, :, col:col + 128].astype(F32)
            lbuf[idx, pl.ds(r, rows, stride=d), :] = l_ref[0, :, r * 128:(r + 1) * 128]
    ls = [l1_ref[0], lbuf[0], lbuf[1]]
    mx = jnp.maximum(jnp.maximum(ls[0], ls[1]), ls[2])
    es = [jnp.exp2(l - mx) for l in ls]
    inv = 1.0 / (es[0] + es[1] + es[2])
    ws = []
    for e in es:
        wh, wl = _split(e * inv)
        ws.append(_dot(wh, expand_ref[...]) + _dot(wl, expand_ref[...]))
    for pair in range(A_HEADS // 2):
        cols = slice(pair * 128, (pair + 1) * 128)
        os_ = [o1_ref[0, :, cols].astype(F32), obuf[0, pair], obuf[1, pair]]
        acc = None
        for g in range(3):
            t = ws[g][:, cols] * os_[g]
            acc = t if acc is None else acc + t
        out_ref[0, :, cols] = acc.astype(BF16)


def _mixture(os_, ls_):
    b, n, _ = os_[0].shape
    dils = [d for _, d in A_PATTERNS]
    expand = jnp.asarray(np.arange(128)[:, None] == np.arange(A_WIDTH)[None, :] // HEAD_DIM, BF16)
    return pl.pallas_call(
        _mixture_kernel,
        grid=(b, n // TM),
        in_specs=[pl.BlockSpec((1, TM // d, d * A_WIDTH), lambda i, j: (i, j, 0)) for d in dils]
        + [pl.BlockSpec((1, TM // d, d * 128), lambda i, j: (i, j, 0)) for d in dils]
        + [pl.BlockSpec((128, A_WIDTH), lambda i, j: (0, 0))],
        out_specs=pl.BlockSpec((1, TM, A_WIDTH), lambda i, j: (i, j, 0)),
        out_shape=jax.ShapeDtypeStruct((b, n, A_WIDTH), BF16),
        scratch_shapes=[pltpu.VMEM((2, A_WIDTH // 128, TM, 128), F32), pltpu.VMEM((2, TM, 128), F32)],
        compiler_params=_cparams("parallel", "parallel"),
        name="dilated_mixture",
    )(*os_, *ls_, expand)


def _group_mean_matrix():
    g = np.kron(np.eye(4), np.full((64, 64), 1.0 / 64))
    return jnp.asarray(g, BF16)


def _fnet_pre_kernel(u_ref, gm_ref, gain_ref, ch_ref, cl_ref, sh_ref, sl_ref, yr_ref, yi_ref):
    u = u_ref[0]
    gm = gm_ref[...]
    uh, ul = _split(u)
    mean = _dot(uh, gm) + _dot(ul, gm)
    xc = u - mean
    sh, sl = _split(xc * xc)
    var = _dot(sh, gm) + _dot(sl, gm)
    un = xc * lax.rsqrt(var + LN_EPS) * gain_ref[...]
    nh, nl = _split(un)
    yr_ref[0] = _dot3(nh, nl, ch_ref[...], cl_ref[...])
    yi_ref[0] = -_dot3(nh, nl, sh_ref[...], sl_ref[...])


def _fnet_stage1_kernel(yr_ref, yi_ref, a1h_ref, a1l_ref, a2h_ref, a2l_ref, zr_ref, zi_ref):
    y = jnp.concatenate([yr_ref[0], yi_ref[0]], axis=0)
    yh, yl = _split(y)
    zr_ref[0] = _dot3(a1h_ref[...], a1l_ref[...], yh, yl)
    zi_ref[0] = _dot3(a2h_ref[...], a2l_ref[...], yh, yl)


def _fnet_stage2_kernel(zr_ref, zi_ref, tch_ref, tcl_ref, tsh_ref, tsl_ref, w_ref, o_ref, *, kc, scale):
    for kk in range(kc):
        rh, rl = _split(zr_ref[0, kk])
        ih, il = _split(zi_ref[0, kk])
        f = _dot3(tch_ref[kk], tcl_ref[kk], rh, rl) + _dot3(tsh_ref[kk], tsl_ref[kk], ih, il)
        f = f * scale
        o_ref[0, :, kk * B_WIDTH:(kk + 1) * B_WIDTH] = _dot(f.astype(BF16), w_ref[...]).astype(BF16)


def _fnet_tables(n):
    n2 = FFT_N2
    n1 = n // n2
    c = np.arange(64)
    ang = 2 * np.pi * np.outer(c, c) / 64
    cbd = np.kron(np.eye(4), np.cos(ang))
    sbd = np.kron(np.eye(4), np.sin(ang))
    k1 = np.arange(n1)
    ang1 = 2 * np.pi * np.outer(k1, k1) / n1
    c1, s1 = np.cos(ang1), np.sin(ang1)
    a1 = np.concatenate([c1, s1], axis=1)
    a2 = np.concatenate([-s1, c1], axis=1)
    k2 = np.arange(n2)
    npr = k1[:, None, None] + n1 * k2[None, :, None]
    prod = (npr * k2[None, None, :]) % n
    ang2 = 2 * np.pi * prod / n
    return (_np_split(cbd), _np_split(sbd), _np_split(a1), _np_split(a2),
            _np_split(np.cos(ang2)), _np_split(np.sin(ang2)))


def _fnet(u, fnet_g, fnet_w):
    b, n, _ = u.shape
    n2 = FFT_N2
    n1 = n // n2
    (ch, cl), (sh, sl), (a1h, a1l), (a2h, a2l), (tch, tcl), (tsh, tsl) = _fnet_tables(n)
    gain = fnet_g.reshape(1, B_WIDTH)
    wbd = jax.scipy.linalg.block_diag(*[fnet_w[g] for g in range(4)]).astype(BF16)

    tok = pl.BlockSpec((1, TM, B_WIDTH), lambda i, j: (i, j, 0))
    mat = pl.BlockSpec((B_WIDTH, B_WIDTH), lambda i, j: (0, 0))
    yr, yi = pl.pallas_call(
        _fnet_pre_kernel,
        grid=(b, n // TM),
        in_specs=[tok, mat, pl.BlockSpec((1, B_WIDTH), lambda i, j: (0, 0)), mat, mat, mat, mat],
        out_specs=[tok, tok],
        out_shape=[jax.ShapeDtypeStruct((b, n, B_WIDTH), F32)] * 2,
        compiler_params=_cparams("parallel", "parallel"),
        name="fnet_pre",
    )(u, _group_mean_matrix(), gain, ch, cl, sh, sl)

    cols = n2 * B_WIDTH
    tc = 2048
    yspec = pl.BlockSpec((1, n1, tc), lambda i, j: (i, 0, j))
    aspec = pl.BlockSpec((n1, 2 * n1), lambda i, j: (0, 0))
    zr, zi = pl.pallas_call(
        _fnet_stage1_kernel,
        grid=(b, cols // tc),
        in_specs=[yspec, yspec, aspec, aspec, aspec, aspec],
        out_specs=[yspec, yspec],
        out_shape=[jax.ShapeDtypeStruct((b, n1, cols), F32)] * 2,
        compiler_params=_cparams("parallel", "parallel"),
        name="fnet_stage1",
    )(yr.reshape(b, n1, cols), yi.reshape(b, n1, cols), a1h, a1l, a2h, a2l)

    kc = 8
    zspec = pl.BlockSpec((1, kc, n2, B_WIDTH), lambda i, j: (i, j, 0, 0))
    tspec = pl.BlockSpec((kc, n2, n2), lambda i, j: (j, 0, 0))
    out = pl.pallas_call(
        functools.partial(_fnet_stage2_kernel, kc=kc, scale=1.0 / math.sqrt(64.0 * n)),
        grid=(b, n1 // kc),
        in_specs=[zspec, zspec, tspec, tspec, tspec, tspec,
                  pl.BlockSpec((B_WIDTH, B_WIDTH), lambda i, j: (0, 0))],
        out_specs=pl.BlockSpec((1, n2, kc * B_WIDTH), lambda i, j: (i, 0, j)),
        out_shape=jax.ShapeDtypeStruct((b, n2, n1 * B_WIDTH), BF16),
        compiler_params=_cparams("parallel", "parallel"),
        name="fnet_stage2",
    )(zr.reshape(b, n1, n2, B_WIDTH), zi.reshape(b, n1, n2, B_WIDTH), tch, tcl, tsh, tsl, wbd)
    return out.reshape(b, n, B_WIDTH)


def _outproj_kernel(a_ref, c_ref, wa_ref, wc_ref, x_ref, g_ref, b_ref, o_ref, *, a_transposed):
    halves = [slice(0, TM // 2), slice(TM // 2, TM)]
    hs = []
    for rows in halves:
        if a_transposed:
            h = _dot_tn(a_ref[0, :, rows], wa_ref[...])
        else:
            h = _dot(a_ref[0, rows, :], wa_ref[...])
        hs.append(h + _dot(c_ref[0, rows, :], wc_ref[...]))
    for rows, h in zip(halves, hs):
        o_ref[0, rows, :] = _layer_norm(DN_ALPHA * x_ref[0, rows, :] + h, g_ref[...], b_ref[...])


def _outproj(a, c, w, x, g, bias, a_transposed):
    b, n, _ = x.shape
    ka = w.shape[0] - c.shape[-1]
    wa = w[:ka].astype(BF16)
    wc = w[ka:].astype(BF16)
    if a_transposed:
        aspec = pl.BlockSpec((1, ka, TM), lambda i, j: (i, 0, j))
    else:
        aspec = pl.BlockSpec((1, TM, ka), lambda i, j: (i, j, 0))
    vec = pl.BlockSpec((1, D_MODEL), lambda i, j: (0, 0))
    xspec = pl.BlockSpec((1, TM, D_MODEL), lambda i, j: (i, j, 0))
    return pl.pallas_call(
        functools.partial(_outproj_kernel, a_transposed=a_transposed),
        grid=(b, n // TM),
        in_specs=[aspec, pl.BlockSpec((1, TM, c.shape[-1]), lambda i, j: (i, j, 0)),
                  pl.BlockSpec(wa.shape, lambda i, j: (0, 0)),
                  pl.BlockSpec(wc.shape, lambda i, j: (0, 0)), xspec, vec, vec],
        out_specs=xspec,
        out_shape=jax.ShapeDtypeStruct((b, n, D_MODEL), F32),
        compiler_params=_cparams("parallel", "parallel"),
        name="outproj_ln",
    )(a, c, wa, wc, x, g.reshape(1, D_MODEL), bias.reshape(1, D_MODEL))


def _mem_kv_kernel(mem_ref, wk_ref, wvt_ref, k_ref, vt_ref):
    m = mem_ref[0].astype(BF16)
    k_ref[0] = _dot(m, wk_ref[...]).astype(BF16)
    vt_ref[0] = _dot_nt(wvt_ref[...], m).astype(BF16)


def _mem_kv(mem, w_kv):
    b, m, _ = mem.shape
    wk = w_kv[:, :D_MODEL].astype(BF16)
    wvt = w_kv[:, D_MODEL:].T.astype(BF16)
    wspec = pl.BlockSpec((D_MODEL, D_MODEL), lambda i: (0, 0))
    return pl.pallas_call(
        _mem_kv_kernel,
        grid=(b,),
        in_specs=[pl.BlockSpec((1, m, D_MODEL), lambda i: (i, 0, 0)), wspec, wspec],
        out_specs=[pl.BlockSpec((1, m, D_MODEL), lambda i: (i, 0, 0)),
                   pl.BlockSpec((1, D_MODEL, m), lambda i: (i, 0, 0))],
        out_shape=[jax.ShapeDtypeStruct((b, m, D_MODEL), BF16),
                   jax.ShapeDtypeStruct((b, D_MODEL, m), BF16)],
        compiler_params=_cparams("parallel"),
        name="mem_kv",
    )(mem, wk, wvt)


def _xattn_kernel(x_ref, wqt_ref, k_ref, vt_ref, wo_ref, g_ref, b_ref, o_ref):
    x = x_ref[0]
    xb = x.astype(BF16)
    qt = _dot_nt(wqt_ref[...], xb).astype(BF16)
    heads = [slice(h * XA_HEAD_DIM, (h + 1) * XA_HEAD_DIM) for h in range(XA_HEADS)]
    scores = [_dot(k_ref[0, :, hd], qt[hd]) for hd in heads]
    ones = jnp.ones((16, k_ref.shape[1]), BF16)
    outs = []
    for hd, st in zip(heads, scores):
        p = jnp.exp2(st - jnp.max(st, axis=0, keepdims=True)).astype(BF16)
        ot = _dot(jnp.concatenate([vt_ref[0, hd, :], ones], axis=0), p)
        outs.append((ot[:XA_HEAD_DIM] / ot[XA_HEAD_DIM:XA_HEAD_DIM + 1]).astype(BF16))
    ot_all = jnp.concatenate(outs, axis=0)
    halves = [slice(0, TM // 2), slice(TM // 2, TM)]
    hs = [_dot_tn(ot_all[:, rows], wo_ref[...]) for rows in halves]
    for rows, h in zip(halves, hs):
        o_ref[0, rows, :] = _layer_norm(DN_ALPHA * x[rows] + h, g_ref[...], b_ref[...])


def _xattn(x, mem, w_q, w_kv, w_o, g, bias):
    b, n, _ = x.shape
    m = mem.shape[1]
    k, vt = _mem_kv(mem, w_kv)
    wqt = (w_q * (XA_HEAD_DIM ** -0.5 * LOG2E)).T.astype(BF16)
    wo = w_o.astype(BF16)
    xspec = pl.BlockSpec((1, TM, D_MODEL), lambda i, j: (i, j, 0))
    wspec = pl.BlockSpec((D_MODEL, D_MODEL), lambda i, j: (0, 0))
    vec = pl.BlockSpec((1, D_MODEL), lambda i, j: (0, 0))
    return pl.pallas_call(
        _xattn_kernel,
        grid=(b, n // TM),
        in_specs=[xspec, wspec, pl.BlockSpec((1, m, D_MODEL), lambda i, j: (i, 0, 0)),
                  pl.BlockSpec((1, D_MODEL, m), lambda i, j: (i, 0, 0)), wspec, vec, vec],
        out_specs=xspec,
        out_shape=jax.ShapeDtypeStruct((b, n, D_MODEL), F32),
        compiler_params=_cparams("parallel", "parallel"),
        name="xattn_ln",
    )(x, wqt, k, vt, wo, g.reshape(1, D_MODEL), bias.reshape(1, D_MODEL))


def _swiglu_kernel(x_ref, win_ref, wo_ref, g_ref, b_ref, o_ref):
    halves = [slice(0, TM // 2), slice(TM // 2, TM)]
    gate_up = []
    for rows in halves:
        xb = x_ref[0, rows, :].astype(BF16)
        gate_up.append((_dot(xb, win_ref[:, :FFN_HIDDEN]), _dot(xb, win_ref[:, FFN_HIDDEN:])))
    parts = []
    for gate, up in gate_up:
        hid = (gate * (1.0 / (1.0 + jnp.exp(-gate))) * up).astype(BF16)
        parts.append(_dot(hid, wo_ref[...]))
    for rows, part in zip(halves, parts):
        o_ref[0, rows, :] = _layer_norm(DN_ALPHA * x_ref[0, rows, :] + part, g_ref[...], b_ref[...])


def _swiglu(x, w_in, w_out, g, bias):
    b, n, _ = x.shape
    win = w_in.astype(BF16)
    wout = w_out.astype(BF16)
    xspec = pl.BlockSpec((1, TM, D_MODEL), lambda i, t: (i, t, 0))
    vec = pl.BlockSpec((1, D_MODEL), lambda i, t: (0, 0))
    resident = pl.Buffered(1)
    return pl.pallas_call(
        _swiglu_kernel,
        grid=(b, n // TM),
        in_specs=[xspec,
                  pl.BlockSpec(win.shape, lambda i, t: (0, 0), pipeline_mode=resident),
                  pl.BlockSpec(wout.shape, lambda i, t: (0, 0), pipeline_mode=resident), vec, vec],
        out_specs=xspec,
        out_shape=jax.ShapeDtypeStruct((b, n, D_MODEL), F32),
        compiler_params=_cparams("parallel", "parallel"),
        name="swiglu_ln",
    )(x, win, wout, g.reshape(1, D_MODEL), bias.reshape(1, D_MODEL))


N_QK_HEADS = C_Q_HEADS + C_KV_HEADS
QK_ROWS = N_QK_HEADS * HEAD_DIM


def _proj_cd_kernel(x_ref, wt_ref, wu_ref, gain_ref, cos_ref, sin_ref,
                    qt_ref, k_ref, vt_ref, u_ref, kn_ref, *, tm):
    xb = x_ref[0].astype(BF16)
    u_ref[0] = _dot(xb, wu_ref[...])
    zt = _dot_nt(wt_ref[...], xb)
    z = zt[:QK_ROWS].reshape(N_QK_HEADS, HEAD_DIM, tm)
    ssq = jnp.sum(z * z, axis=1, keepdims=True)
    zn = z * lax.rsqrt(ssq * (1.0 / HEAD_DIM) + RMS_EPS) * gain_ref[...]
    half = HEAD_DIM // 2
    x1 = zn[:, :half]
    x2 = zn[:, half:]
    c = cos_ref[...][None]
    s = sin_ref[...][None]
    rot = jnp.concatenate([x1 * c - x2 * s, x1 * s + x2 * c], axis=1).reshape(QK_ROWS, tm)
    qt_ref[0] = rot[:C_WIDTH].astype(BF16)
    kb = rot[C_WIDTH:].astype(BF16)
    k_ref[0] = kb.astype(F32).T.astype(BF16)
    kf = kb.astype(F32).reshape(C_KV_HEADS, HEAD_DIM, tm)
    kn_ref[0] = jnp.sum(kf * kf, axis=1, keepdims=True)
    vt = zt[QK_ROWS:].astype(BF16)
    for c0 in range(tm // GQA_TKC):
        vt_ref[0, c0] = vt[:, c0 * GQA_TKC:(c0 + 1) * GQA_TKC]


def _rope_tables_t(n):
    rows = n // GRID_W
    row_id = jnp.broadcast_to(jnp.arange(rows)[:, None], (rows, GRID_W)).reshape(n)
    col_id = jnp.broadcast_to(jnp.arange(GRID_W)[None, :], (rows, GRID_W)).reshape(n)
    axis_dim = HEAD_DIM // 2
    freqs = ROPE_THETA ** (-jnp.arange(0, axis_dim, 2, dtype=F32) / axis_dim)
    ang = jnp.concatenate([row_id[:, None] * freqs, col_id[:, None] * freqs], axis=-1)
    return jnp.cos(ang).T, jnp.sin(ang).T


def _proj_cd(x, w_in, q_norm, k_norm):
    b, n, _ = x.shape
    tm = TM
    half = HEAD_DIM // 2
    wqk = w_in[:, :QK_ROWS].reshape(D_MODEL, N_QK_HEADS, half, 2)
    wqk = wqk.transpose(1, 3, 2, 0).reshape(QK_ROWS, D_MODEL)
    wt = jnp.concatenate([wqk, w_in[:, QK_ROWS:QK_ROWS + C_KV_WIDTH].T], axis=0).astype(BF16)
    wu = w_in[:, QK_ROWS + C_KV_WIDTH:].astype(BF16)
    qg = q_norm.reshape(half, 2).T.reshape(HEAD_DIM) * (HEAD_DIM ** -0.5 * LOG2E)
    kg = k_norm.reshape(half, 2).T.reshape(HEAD_DIM)
    gain = jnp.concatenate([jnp.tile(qg[None], (C_Q_HEADS, 1)), jnp.tile(kg[None], (C_KV_HEADS, 1))])
    gain = gain.reshape(N_QK_HEADS, HEAD_DIM, 1).astype(F32)
    cos_t, sin_t = _rope_tables_t(n)
    nc = n // GQA_TKC
    return pl.pallas_call(
        functools.partial(_proj_cd_kernel, tm=tm),
        grid=(b, n // tm),
        in_specs=[pl.BlockSpec((1, tm, D_MODEL), lambda i, j: (i, j, 0)),
                  pl.BlockSpec(wt.shape, lambda i, j: (0, 0)),
                  pl.BlockSpec(wu.shape, lambda i, j: (0, 0)),
                  pl.BlockSpec(gain.shape, lambda i, j: (0, 0, 0)),
                  pl.BlockSpec((HEAD_DIM // 2, tm), lambda i, j: (0, j)),
                  pl.BlockSpec((HEAD_DIM // 2, tm), lambda i, j: (0, j))],
        out_specs=[pl.BlockSpec((1, C_WIDTH, tm), lambda i, j: (i, 0, j)),
                   pl.BlockSpec((1, tm, C_KV_WIDTH), lambda i, j: (i, j, 0)),
                   pl.BlockSpec((1, tm // GQA_TKC, C_KV_WIDTH, GQA_TKC), lambda i, j: (i, j, 0, 0)),
                   pl.BlockSpec((1, tm, D_WIDTH), lambda i, j: (i, j, 0)),
                   pl.BlockSpec((1, C_KV_HEADS, 1, tm), lambda i, j: (i, 0, 0, j))],
        out_shape=[jax.ShapeDtypeStruct((b, C_WIDTH, n), BF16),
                   jax.ShapeDtypeStruct((b, n, C_KV_WIDTH), BF16),
                   jax.ShapeDtypeStruct((b, nc, C_KV_WIDTH, GQA_TKC), BF16),
                   jax.ShapeDtypeStruct((b, n, D_WIDTH), F32),
                   jax.ShapeDtypeStruct((b, C_KV_HEADS, 1, n), F32)],
        compiler_params=_cparams("parallel", "parallel"),
        name="proj_cd",
    )(x, wt, wu, gain, cos_t, sin_t)


def _gqa_kernel(qt_ref, k_ref, vt_ref, kn_ref, ot_ref, m_ref, l_ref, acc_ref, qpad_ref, *, tq, n):
    g = pl.program_id(1)
    row_half = lax.broadcasted_iota(jnp.int32, (128, 1), 0) // HEAD_DIM
    mine = row_half == (g % 2)
    k_max2 = jnp.max(kn_ref[0, 0], axis=1, keepdims=True)
    bound_max = jnp.zeros((1, 1), F32)
    for j in range(C_REP):
        qj = qt_ref[0, j * HEAD_DIM:(j + 1) * HEAD_DIM, :]
        q2 = jnp.concatenate([qj, qj], axis=0)
        qpad_ref[j] = jnp.where(mine, q2, jnp.zeros_like(q2))
        qf = qj.astype(F32)
        bound = jnp.sqrt(jnp.sum(qf * qf, axis=0, keepdims=True) * k_max2) * GQA_BOUND_SLACK
        m_ref[j] = bound
        bound_max = jnp.maximum(bound_max, jnp.max(bound, axis=1, keepdims=True))
    fixed_shift = bound_max[0, 0] <= GQA_BOUND_LIMIT
    acc_ref[...] = jnp.zeros(acc_ref.shape, F32)
    ones = jnp.ones((GQA_VROWS - HEAD_DIM, GQA_TKC), BF16)

    def body(c, carry, *, online):
        kchs, vchs = [], []
        for u in range(GQA_UNROLL):
            cc = c * GQA_UNROLL + u
            start = pl.multiple_of(cc * GQA_TKC, GQA_TKC)
            kchs.append(k_ref[0, pl.ds(start, GQA_TKC), :])
            vchs.append(jnp.concatenate([vt_ref[0, cc], ones], axis=0))
        pieces = [(u, j, slice(s * GQA_TW, (s + 1) * GQA_TW))
                  for u in range(GQA_UNROLL) for j in range(C_REP) for s in range(tq // GQA_TW)]

        def scores(i):
            u, j, cols = pieces[i]
            return _dot(kchs[u], qpad_ref[j, :, cols])

        pending = [scores(i) for i in range(GQA_AHEAD)]
        for i, (u, j, cols) in enumerate(pieces):
            st = pending.pop(0)
            if i + GQA_AHEAD < len(pieces):
                pending.append(scores(i + GQA_AHEAD))
            m_old = m_ref[j, :, cols]
            if not online:
                p = jnp.exp2((st - m_old).astype(BF16))
                l_ref[j, :, cols] = l_ref[j, :, cols] + jnp.sum(
                    p.astype(F32).reshape(GQA_TKC // 8, 8, GQA_TW), axis=0)
                acc_ref[j, :HEAD_DIM, cols] = acc_ref[j, :HEAD_DIM, cols] + _dot(vchs[u][:HEAD_DIM], p)
                continue
            m_new = jnp.maximum(m_old, jnp.max(st, axis=0, keepdims=True))
            alpha = jnp.exp2(m_old - m_new)
            p = jnp.exp2(st - m_new)
            acc_ref[j, :, cols] = alpha * acc_ref[j, :, cols] + _dot(vchs[u], p.astype(BF16))
            m_ref[j, :, cols] = m_new
        return carry

    trips = n // (GQA_TKC * GQA_UNROLL)

    @pl.when(fixed_shift)
    def _():
        l_ref[...] = jnp.zeros(l_ref.shape, F32)
        lax.fori_loop(0, trips, functools.partial(body, online=False), 0)
        for j in range(C_REP):
            acc_ref[j, HEAD_DIM:HEAD_DIM + 1, :] = jnp.sum(l_ref[j], axis=0, keepdims=True)

    @pl.when(jnp.logical_not(fixed_shift))
    def _():
        m_ref[...] = jnp.full(m_ref.shape, NEG_INF, F32)
        lax.fori_loop(0, trips, functools.partial(body, online=True), 0)

    for j in range(C_REP):
        l = acc_ref[j, HEAD_DIM:HEAD_DIM + 1, :]
        ot_ref[0, j * HEAD_DIM:(j + 1) * HEAD_DIM, :] = (acc_ref[j, :HEAD_DIM, :] / l).astype(BF16)


def _gqa(qt, k, vt, kn):
    b, _, n = qt.shape
    tq = min(GQA_TQ, n)
    nc = n // GQA_TKC
    rows = C_REP * HEAD_DIM
    return pl.pallas_call(
        functools.partial(_gqa_kernel, tq=tq, n=n),
        grid=(b, C_KV_HEADS, n // tq),
        in_specs=[pl.BlockSpec((1, rows, tq), lambda i, g, t: (i, g, t)),
                  pl.BlockSpec((1, n, 128), lambda i, g, t: (i, 0, g // 2)),
                  pl.BlockSpec((1, nc, HEAD_DIM, GQA_TKC), lambda i, g, t: (i, 0, g, 0)),
                  pl.BlockSpec((1, 1, 1, n), lambda i, g, t: (i, g, 0, 0))],
        out_specs=pl.BlockSpec((1, rows, tq), lambda i, g, t: (i, g, t)),
        out_shape=jax.ShapeDtypeStruct((b, C_WIDTH, n), BF16),
        scratch_shapes=[pltpu.VMEM((C_REP, 1, tq), F32), pltpu.VMEM((C_REP, 8, tq), F32),
                        pltpu.VMEM((C_REP, GQA_VROWS, tq), F32), pltpu.VMEM((C_REP, 128, tq), BF16)],
        compiler_params=_cparams("parallel", "parallel", "parallel"),
        name="gqa_flash",
    )(qt, k, vt, kn)


def _pool_kernel(up_ref, uc_ref, un_ref, w_ref, scale_ref, o_ref, buf, *, tm, n):
    i = pl.program_id(1)
    cur = uc_ref[0]
    buf[0:POOL_HALO, :] = jnp.where(i > 0, up_ref[0], 0.0)
    buf[POOL_HALO:POOL_HALO + tm, :] = cur
    buf[POOL_HALO + tm:, :] = jnp.where(i < pl.num_programs(1) - 1, un_ref[0], 0.0)
    lane_group = lax.broadcasted_iota(jnp.int32, (1, D_WIDTH), 1) // 64
    half_w = jnp.left_shift(1, lane_group)
    acc = jnp.zeros((tm, D_WIDTH), F32)
    for j in range(-POOL_HALO, POOL_HALO):
        inside = (j >= -half_w) & (j < half_w)
        acc = acc + jnp.where(inside, buf[POOL_HALO + j:POOL_HALO + j + tm, :], 0.0)
    t = i * tm + lax.broadcasted_iota(jnp.int32, (tm, 1), 0)
    cnt = jnp.minimum(t + half_w, n) - jnp.maximum(t - half_w, 0)
    mixed = (acc / cnt.astype(F32) - cur).astype(BF16)
    o_ref[0] = (_dot(mixed, w_ref[...]) * scale_ref[...]).astype(BF16)


def _pool(u, pool_w, pool_scale):
    b, n, _ = u.shape
    tm = TM
    per = tm // POOL_HALO
    last = n // POOL_HALO - 1
    wbd = jax.scipy.linalg.block_diag(*[pool_w[g] for g in range(4)]).astype(BF16)
    return pl.pallas_call(
        functools.partial(_pool_kernel, tm=tm, n=n),
        grid=(b, n // tm),
        in_specs=[pl.BlockSpec((1, POOL_HALO, D_WIDTH), lambda i, j: (i, jnp.maximum(j * per - 1, 0), 0)),
                  pl.BlockSpec((1, tm, D_WIDTH), lambda i, j: (i, j, 0)),
                  pl.BlockSpec((1, POOL_HALO, D_WIDTH), lambda i, j: (i, jnp.minimum((j + 1) * per, last), 0)),
                  pl.BlockSpec((D_WIDTH, D_WIDTH), lambda i, j: (0, 0)),
                  pl.BlockSpec((1, D_WIDTH), lambda i, j: (0, 0))],
        out_specs=pl.BlockSpec((1, tm, D_WIDTH), lambda i, j: (i, j, 0)),
        out_shape=jax.ShapeDtypeStruct((b, n, D_WIDTH), BF16),
        scratch_shapes=[pltpu.VMEM((tm + 2 * POOL_HALO, D_WIDTH), F32)],
        compiler_params=_cparams("parallel", "parallel"),
        name="pool",
    )(u, u, u, wbd, pool_scale.reshape(1, D_WIDTH))


def _trunk(x, mem, rel_bias, ab_w_in, ab_fnet_g, ab_fnet_w, ab_w_out,
           cd_w_in, cd_q_norm, cd_k_norm, cd_pool_w, cd_pool_scale, cd_w_out,
           xa_w_q, xa_w_kv, xa_w_o, ffn_w_in, ffn_w_out, ln_g, ln_b):
    for layer in range(DEPTH):
        i = layer // 2
        if layer % 2 == 0:
            w_in = ab_w_in[i]
            w_in = jnp.concatenate([w_in[:, :A_WIDTH] * (HEAD_DIM ** -0.5 * LOG2E), w_in[:, A_WIDTH:]], axis=1)
            *views, u = _proj_ab(x, w_in.astype(BF16))
            outs = [_dilated(view, rel_bias, d) for view, (_, d) in zip(views, A_PATTERNS)]
            o_a = _mixture([o for o, _ in outs], [l for _, l in outs])
            o_b = _fnet(u, ab_fnet_g[i], ab_fnet_w[i])
            x = _outproj(o_a, o_b, ab_w_out[i], x, ln_g[layer, 0], ln_b[layer, 0], a_transposed=False)
        else:
            qt, k, vt, u, kn = _proj_cd(x, cd_w_in[i], cd_q_norm[i], cd_k_norm[i])
            o_c = _gqa(qt, k, vt, kn)
            o_d = _pool(u, cd_pool_w[i], cd_pool_scale[i])
            x = _outproj(o_c, o_d, cd_w_out[i], x, ln_g[layer, 0], ln_b[layer, 0], a_transposed=True)
        x = _xattn(x, mem, xa_w_q[layer], xa_w_kv[layer], xa_w_o[layer], ln_g[layer, 1], ln_b[layer, 1])
        x = _swiglu(x, ffn_w_in[layer], ffn_w_out[layer], ln_g[layer, 2], ln_b[layer, 2])
    return x


def kernel(x_prompt, x_sample, mem_prompt, mem_sample, rel_bias, ab_w_in, ab_fnet_g, ab_fnet_w, ab_w_out, cd_w_in, cd_q_norm, cd_k_norm, cd_pool_w, cd_pool_scale, cd_w_out, xa_w_q, xa_w_kv, xa_w_o, ffn_w_in, ffn_w_out, ln_g, ln_b):
    params = (rel_bias, ab_w_in, ab_fnet_g, ab_fnet_w, ab_w_out,
              cd_w_in, cd_q_norm, cd_k_norm, cd_pool_w, cd_pool_scale, cd_w_out,
              xa_w_q, xa_w_kv, xa_w_o, ffn_w_in, ffn_w_out, ln_g, ln_b)
    return (_trunk(x_prompt, mem_prompt, *params), _trunk(x_sample, mem_sample, *params))
```

```python
import functools
import math

import numpy as np
import jax
import jax.numpy as jnp
from jax import lax
from jax.experimental import pallas as pl
from jax.experimental.pallas import tpu as pltpu

F32 = jnp.float32
BF16 = jnp.bfloat16

D_MODEL = 1024
HEAD_DIM = 64
GRID_W = 64
LN_EPS = 1e-5
RMS_EPS = 1e-6
NEG_INF = -1e30
DEPTH = 2
A_HEADS = 12
A_WIDTH = A_HEADS * HEAD_DIM
A_PATTERNS = ((128, 1), (512, 4), (2048, 16))
A_HALF = 64
QKV_WIDTH = 3 * A_WIDTH
N_BUCKETS = 32
REL_MAX_DIST = 1024
B_WIDTH = 256
C_Q_HEADS = 12
C_KV_HEADS = 4
C_REP = C_Q_HEADS // C_KV_HEADS
C_WIDTH = C_Q_HEADS * HEAD_DIM
C_KV_WIDTH = C_KV_HEADS * HEAD_DIM
ROPE_THETA = 10000.0
POOL_WINDOWS = (2, 4, 8, 16)
POOL_HALO = 8
D_WIDTH = 256
XA_HEADS = 4
XA_HEAD_DIM = D_MODEL // XA_HEADS
FFN_HIDDEN = 2816
DN_ALPHA = (2 * DEPTH) ** 0.25
LOG2E = 1.4426950408889634

VMEM_LIMIT = 56 * 1024 * 1024
TM = 512
FFT_N2 = 128
FFT_ROWS = 1024
FFT_KC = 16
DIL_AHEAD = 3
DIL_TQ = 512
DIL_SQ = 128
DIL_SK = DIL_SQ + 2 * A_HALF
GQA_TQ = 1024
GQA_TW = 256
GQA_TKC = 256
GQA_VROWS = HEAD_DIM + 16
GQA_UNROLL = 8
GQA_AHEAD = 5
GQA_BOUND_SLACK = 1.0 + 2.0 ** -10
GQA_BOUND_LIMIT = 60.0


def _cparams(*sem):
    return pltpu.CompilerParams(dimension_semantics=sem, vmem_limit_bytes=VMEM_LIMIT)


def _dot(a, b):
    return jnp.dot(a, b, preferred_element_type=F32)


def _dot_nt(a, b):
    return lax.dot_general(a, b, (((1,), (1,)), ((), ())), preferred_element_type=F32)


def _dot_tn(a, b):
    return lax.dot_general(a, b, (((0,), (0,)), ((), ())), preferred_element_type=F32)


def _split(x):
    hi = x.astype(BF16)
    lo = (x - hi.astype(F32)).astype(BF16)
    return hi, lo


def _dot3(ah, al, bh, bl):
    return _dot(ah, bh) + _dot(al, bh) + _dot(ah, bl)


def _np_split(x):
    x = np.asarray(x, np.float32)
    hi = x.astype(BF16)
    lo = (x - hi.astype(np.float32)).astype(BF16)
    return jnp.asarray(hi), jnp.asarray(lo)


def _layer_norm(h, g, b):
    mu = jnp.mean(h, axis=-1, keepdims=True)
    xc = h - mu
    var = jnp.mean(xc * xc, axis=-1, keepdims=True)
    return xc * lax.rsqrt(var + LN_EPS) * g + b


def _proj_ab_kernel(x_ref, w_ref, qkv1_ref, qkv4_ref, qkv16_ref, u_ref, zbuf):
    xb = x_ref[0].astype(BF16)
    chunks = list(range(0, QKV_WIDTH, 256))

    for c in chunks:
        z = _dot(xb, w_ref[:, c:c + 256])
        zbuf[c // 128] = z[:, :128]
        zbuf[c // 128 + 1] = z[:, 128:]
        qkv1_ref[0, :, c:c + 256] = z.astype(BF16)
    u_ref[0] = _dot(xb, w_ref[:, QKV_WIDTH:])
    for (_, d), ref in zip(A_PATTERNS[1:], (qkv4_ref, qkv16_ref)):
        rows = TM // d
        for r in range(d):
            for ct in range(QKV_WIDTH // 128):
                col = r * QKV_WIDTH + ct * 128
                ref[0, :, col:col + 128] = zbuf[ct, pl.ds(r, rows, stride=d), :].astype(BF16)


def _proj_ab(x, w):
    b, n, _ = x.shape
    dils = [d for _, d in A_PATTERNS]
    return pl.pallas_call(
        _proj_ab_kernel,
        grid=(b, n // TM),
        in_specs=[pl.BlockSpec((1, TM, D_MODEL), lambda i, j: (i, j, 0)),
                  pl.BlockSpec(w.shape, lambda i, j: (0, 0))],
        out_specs=[pl.BlockSpec((1, TM // d, d * QKV_WIDTH), lambda i, j: (i, j, 0)) for d in dils]
        + [pl.BlockSpec((1, TM, B_WIDTH), lambda i, j: (i, j, 0))],
        out_shape=[jax.ShapeDtypeStruct((b, n // d, d * QKV_WIDTH), BF16) for d in dils]
        + [jax.ShapeDtypeStruct((b, n, B_WIDTH), F32)],
        scratch_shapes=[pltpu.VMEM((QKV_WIDTH // 128, TM, 128), F32)],
        compiler_params=_cparams("parallel", "parallel"),
        name="proj_ab",
    )(x, w)


def _t5_bucket_np(rel):
    nb = N_BUCKETS // 2
    max_exact = nb // 2
    ret = np.where(rel > 0, nb, 0)
    n = np.abs(rel)
    nf = np.maximum(n, 1).astype(np.float32)
    large = max_exact + (np.log(nf / max_exact) / math.log(REL_MAX_DIST / max_exact)
                         * (nb - max_exact)).astype(np.int32)
    large = np.minimum(large, nb - 1)
    return ret + np.where(n < max_exact, n, large)


def _band_bias(rel_bias, dilation):
    tq, tk = DIL_SQ, DIL_SK
    band = 2 * A_HALF + 1
    bucket = _t5_bucket_np((np.arange(band) - A_HALF) * dilation)
    row = (rel_bias[jnp.asarray(bucket)].T * LOG2E).astype(F32)
    row = jnp.concatenate([row, jnp.full((A_HEADS, tk + 1 - band), NEG_INF, F32)], axis=1)
    bias = jnp.tile(row, (1, tq))[:, :tq * tk].reshape(A_HEADS, tq, tk)
    bias = bias.transpose(0, 2, 1)
    bias = bias.reshape(A_HEADS // 2, 2, tk, tq).transpose(0, 2, 1, 3).reshape(A_HEADS // 2, tk, 2 * tq)
    key = np.arange(tk)[None, :, None]
    before = jnp.asarray(key < A_HALF)
    after = jnp.asarray(key >= A_HALF + tq)
    first = jnp.where(before, NEG_INF, bias)
    return jnp.stack([bias, first, jnp.where(after, NEG_INF, bias), jnp.where(after, NEG_INF, first)])


def _dilated_kernel(q_ref, kp_ref, kc_ref, kn_ref, vp_ref, vc_ref, vn_ref, bias_ref,
                    o_ref, lse_ref, kbuf, vbuf, *, tq):
    tk = tq + 2 * A_HALF
    kbuf[0:A_HALF, :] = kp_ref[0]
    kbuf[A_HALF:A_HALF + tq, :] = kc_ref[0]
    kbuf[A_HALF + tq:, :] = kn_ref[0]
    vbuf[0:A_HALF, :] = vp_ref[0]
    vbuf[A_HALF:A_HALF + tq, :] = vc_ref[0]
    vbuf[A_HALF + tq:, :] = vn_ref[0]

    i = pl.program_id(1)
    n_sub = tq // DIL_SQ
    row_low = lax.broadcasted_iota(jnp.int32, (128, 1), 0) < HEAD_DIM
    row16 = lax.broadcasted_iota(jnp.int32, (16, 1), 0)
    ones = jnp.ones((16, DIL_SK), BF16)
    qts, vts = {}, {}
    pieces = [(pair, sub) for pair in range(A_HEADS // 2) for sub in range(n_sub)]

    def variant(sub):
        v = 0
        if sub == 0:
            v = v + jnp.where(i == 0, 1, 0)
        if sub == n_sub - 1:
            v = v + jnp.where(i == pl.num_programs(1) - 1, 2, 0)
        return v

    def scores(idx):
        pair, sub = pieces[idx]
        cols = slice(pair * 128, (pair + 1) * 128)
        if pair not in qts:
            qts[pair] = q_ref[0, :, cols].astype(F32).T
        qsub = qts[pair][:, sub * DIL_SQ:(sub + 1) * DIL_SQ]
        rhs = jnp.concatenate([jnp.where(row_low, qsub, 0.0), jnp.where(row_low, 0.0, qsub)], axis=1)
        keys = kbuf[sub * DIL_SQ:sub * DIL_SQ + DIL_SK, cols]
        return _dot(keys, rhs.astype(BF16)) + bias_ref[variant(sub), pair]

    pending = [scores(idx) for idx in range(DIL_AHEAD)]
    lse_t = [jnp.zeros((16, DIL_SQ), F32) for _ in range(n_sub)]
    for idx, (pair, sub) in enumerate(pieces):
        st = pending.pop(0)
        if idx + DIL_AHEAD < len(pieces):
            pending.append(scores(idx + DIL_AHEAD))
        cols = slice(pair * 128, (pair + 1) * 128)
        m = jnp.max(st, axis=0, keepdims=True)
        p = jnp.exp2(st - m).astype(BF16)
        if pair not in vts:
            vts[pair] = vbuf[:, cols].astype(F32).T.astype(BF16)
        vaug = jnp.concatenate([vts[pair][:, sub * DIL_SQ:sub * DIL_SQ + DIL_SK], ones], axis=0)
        ot = _dot(vaug, p)
        l = ot[128:129]
        lse2 = m + jnp.log2(l)
        o_pair = jnp.concatenate([ot[:HEAD_DIM, :DIL_SQ] / l[:, :DIL_SQ],
                                  ot[HEAD_DIM:128, DIL_SQ:] / l[:, DIL_SQ:]], axis=0)
        o_ref[0, sub * DIL_SQ:(sub + 1) * DIL_SQ, cols] = o_pair.T.astype(BF16)
        lse_t[sub] = jnp.where(row16 == 2 * pair, lse2[:, :DIL_SQ],
                               jnp.where(row16 == 2 * pair + 1, lse2[:, DIL_SQ:], lse_t[sub]))
    for sub in range(n_sub):
        full = jnp.concatenate([lse_t[sub], jnp.zeros((128 - 16, DIL_SQ), F32)], axis=0)
        lse_ref[0, sub * DIL_SQ:(sub + 1) * DIL_SQ, :] = full.T


def _dilated(view, rel_bias, dilation):
    b, seq, _ = view.shape
    tq = min(DIL_TQ, seq)
    tk = tq + 2 * A_HALF
    bias = _band_bias(rel_bias, dilation)
    per = tq // A_HALF
    last = seq // A_HALF - 1
    n_tiles = seq // tq

    def cur(c):
        return pl.BlockSpec((1, tq, A_WIDTH), lambda bi, i, r: (bi, i, 3 * r + c))

    def prev(c):
        return pl.BlockSpec((1, A_HALF, A_WIDTH),
                            lambda bi, i, r: (bi, jnp.maximum(i * per - 1, 0), 3 * r + c))

    def nxt(c):
        return pl.BlockSpec((1, A_HALF, A_WIDTH),
                            lambda bi, i, r: (bi, jnp.minimum((i + 1) * per, last), 3 * r + c))

    o, lse = pl.pallas_call(
        functools.partial(_dilated_kernel, tq=tq),
        grid=(b, seq // tq, dilation),
        in_specs=[cur(0), prev(1), cur(1), nxt(1), prev(2), cur(2), nxt(2),
                  pl.BlockSpec(bias.shape, lambda bi, i, r: (0, 0, 0, 0))],
        out_specs=[pl.BlockSpec((1, tq, A_WIDTH), lambda bi, i, r: (bi, i, r)),
                   pl.BlockSpec((1, tq, 128), lambda bi, i, r: (bi, i, r))],
        out_shape=[jax.ShapeDtypeStruct((b, seq, dilation * A_WIDTH), BF16),
                   jax.ShapeDtypeStruct((b, seq, dilation * 128), F32)],
        scratch_shapes=[pltpu.VMEM((tk, A_WIDTH), BF16), pltpu.VMEM((tk, A_WIDTH), BF16)],
        compiler_params=_cparams("parallel", "parallel", "parallel"),
        name=f"dilated_d{dilation}",
    )(view, view, view, view, view, view, view, bias)
    return o, lse


def _mixture_kernel(o1_ref, o4_ref, o16_ref, l1_ref, l4_ref, l16_ref, expand_ref, out_ref, obuf, lbuf):
    for idx, (d, o_ref, l_ref) in enumerate(((A_PATTERNS[1][1], o4_ref, l4_ref), (A_PATTERNS[2][1], o16_ref, l16_ref))):
        rows = TM // d
        for r in range(d):
            for ct in range(A_WIDTH // 128):
                col = r * A_WIDTH + ct * 128
                obuf[idx, ct, pl.ds(r, rows, stride=d), :] = o_ref[0, :, col:col + 128].astype(F32)
            lbuf[idx, pl.ds(r, rows, stride=d), :] = l_ref[0, :, r * 128:(r + 1) * 128]
    ls = [l1_ref[0], lbuf[0], lbuf[1]]
    mx = jnp.maximum(jnp.maximum(ls[0], ls[1]), ls[2])
    es = [jnp.exp2(l - mx) for l in ls]
    inv = 1.0 / (es[0] + es[1] + es[2])
    ws = []
    for e in es:
        wh, wl = _split(e * inv)
        ws.append(_dot(wh, expand_ref[...]) + _dot(wl, expand_ref[...]))
    for pair in range(A_HEADS // 2):
        cols = slice(pair * 128, (pair + 1) * 128)
        os_ = [o1_ref[0, :, cols].astype(F32), obuf[0, pair], obuf[1, pair]]
        acc = None
        for g in range(3):
            t = ws[g][:, cols] * os_[g]
            acc = t if acc is None else acc + t
        out_ref[0, :, cols] = acc.astype(BF16)


def _mixture(os_, ls_):
    b, n, _ = os_[0].shape
    dils = [d for _, d in A_PATTERNS]
    expand = jnp.asarray(np.arange(128)[:, None] == np.arange(A_WIDTH)[None, :] // HEAD_DIM, BF16)
    return pl.pallas_call(
        _mixture_kernel,
        grid=(b, n // TM),
        in_specs=[pl.BlockSpec((1, TM // d, d * A_WIDTH), lambda i, j: (i, j, 0)) for d in dils]
        + [pl.BlockSpec((1, TM // d, d * 128), lambda i, j: (i, j, 0)) for d in dils]
        + [pl.BlockSpec((128, A_WIDTH), lambda i, j: (0, 0))],
        out_specs=pl.BlockSpec((1, TM, A_WIDTH), lambda i, j: (i, j, 0)),
        out_shape=jax.ShapeDtypeStruct((b, n, A_WIDTH), BF16),
        scratch_shapes=[pltpu.VMEM((2, A_WIDTH // 128, TM, 128), F32), pltpu.VMEM((2, TM, 128), F32)],
        compiler_params=_cparams("parallel", "parallel"),
        name="dilated_mixture",
    )(*os_, *ls_, expand)


def _group_mean_matrix():
    g = np.kron(np.eye(4), np.full((64, 64), 1.0 / 64))
    return jnp.asarray(g, BF16)


def _fnet_front_kernel(u_ref, gm_ref, gain_ref, ch_ref, cl_ref, sh_ref, sl_ref, m1h_ref, m1l_ref,
                       zr_ref, zi_ref, ybuf, zbuf, *, n1, tcn):
    rows = n1 * tcn
    u = u_ref[0].reshape(rows, B_WIDTH)
    gm = gm_ref[...]
    uh, ul = _split(u)
    mean = _dot(uh, gm) + _dot(ul, gm)
    xc = u - mean
    qh, ql = _split(xc * xc)
    var = _dot(qh, gm) + _dot(ql, gm)
    un = xc * lax.rsqrt(var + LN_EPS) * gain_ref[...]
    nh, nl = _split(un)
    yr = _dot3(nh, nl, ch_ref[...], cl_ref[...])
    yi = -_dot3(nh, nl, sh_ref[...], sl_ref[...])
    for q, val in enumerate((yr[:, :128], yr[:, 128:], yi[:, :128], yi[:, 128:])):
        ybuf[q] = val
    for n2 in range(tcn):
        col = [ybuf[q, pl.ds(n2, n1, stride=tcn), :] for q in range(4)]
        y = jnp.concatenate([jnp.concatenate(col[:2], axis=1), jnp.concatenate(col[2:], axis=1)], axis=0)
        yh, yl = _split(y)
        z = _dot3(m1h_ref[...], m1l_ref[...], yh, yl)
        for q, val in enumerate((z[:n1, :128], z[:n1, 128:], z[n1:, :128], z[n1:, 128:])):
            zbuf[q, pl.ds(n2, n1, stride=tcn), :] = val
    zr_ref[0] = jnp.concatenate([zbuf[0], zbuf[1]], axis=1).reshape(n1, tcn, B_WIDTH)
    zi_ref[0] = jnp.concatenate([zbuf[2], zbuf[3]], axis=1).reshape(n1, tcn, B_WIDTH)


def _fnet_back_kernel(zr_ref, zi_ref, tch_ref, tcl_ref, tsh_ref, tsl_ref, w_ref, o_ref, obuf, *, kc, scale):
    n2 = FFT_N2
    for kk in range(kc):
        rh, rl = _split(zr_ref[0, kk])
        ih, il = _split(zi_ref[0, kk])
        f = _dot3(tch_ref[kk], tcl_ref[kk], rh, rl) + _dot3(tsh_ref[kk], tsl_ref[kk], ih, il)
        o = _dot((f * scale).astype(BF16), w_ref[...])
        obuf[0, pl.ds(kk, n2, stride=kc), :] = o[:, :128]
        obuf[1, pl.ds(kk, n2, stride=kc), :] = o[:, 128:]
    out = jnp.concatenate([obuf[0], obuf[1]], axis=1).reshape(n2, kc, B_WIDTH)
    o_ref[0] = out.astype(BF16)


def _fnet_tables(n):
    n2 = FFT_N2
    n1 = n // n2
    c = np.arange(64)
    ang = 2 * np.pi * np.outer(c, c) / 64
    cbd = np.kron(np.eye(4), np.cos(ang))
    sbd = np.kron(np.eye(4), np.sin(ang))
    k1 = np.arange(n1)
    ang1 = 2 * np.pi * np.outer(k1, k1) / n1
    c1, s1 = np.cos(ang1), np.sin(ang1)
    m1 = np.block([[c1, s1], [-s1, c1]])
    k2 = np.arange(n2)
    npr = k1[:, None, None] + n1 * k2[None, :, None]
    prod = (npr * k2[None, None, :]) % n
    ang2 = 2 * np.pi * prod / n
    return (_np_split(cbd), _np_split(sbd), _np_split(m1), _np_split(np.cos(ang2)), _np_split(np.sin(ang2)))


def _fnet(u, fnet_g, fnet_w):
    b, n, _ = u.shape
    n2 = FFT_N2
    n1 = n // n2
    tcn = FFT_ROWS // n1
    (ch, cl), (sh, sl), (m1h, m1l), (tch, tcl), (tsh, tsl) = _fnet_tables(n)
    gain = fnet_g.reshape(1, B_WIDTH)
    wbd = jax.scipy.linalg.block_diag(*[fnet_w[g] for g in range(4)]).astype(BF16)

    mat = pl.BlockSpec((B_WIDTH, B_WIDTH), lambda i, j: (0, 0))
    m1spec = pl.BlockSpec((2 * n1, 2 * n1), lambda i, j: (0, 0))
    tile = pl.BlockSpec((1, n1, tcn, B_WIDTH), lambda i, j: (i, 0, j, 0))
    zr, zi = pl.pallas_call(
        functools.partial(_fnet_front_kernel, n1=n1, tcn=tcn),
        grid=(b, n2 // tcn),
        in_specs=[tile, mat, pl.BlockSpec((1, B_WIDTH), lambda i, j: (0, 0)), mat, mat, mat, mat, m1spec, m1spec],
        out_specs=[tile, tile],
        out_shape=[jax.ShapeDtypeStruct((b, n1, n2, B_WIDTH), F32)] * 2,
        scratch_shapes=[pltpu.VMEM((4, FFT_ROWS, 128), F32), pltpu.VMEM((4, FFT_ROWS, 128), F32)],
        compiler_params=_cparams("parallel", "parallel"),
        name="fnet_front",
    )(u.reshape(b, n1, n2, B_WIDTH), _group_mean_matrix(), gain, ch, cl, sh, sl, m1h, m1l)

    kc = FFT_KC
    zspec = pl.BlockSpec((1, kc, n2, B_WIDTH), lambda i, j: (i, j, 0, 0))
    tspec = pl.BlockSpec((kc, n2, n2), lambda i, j: (j, 0, 0))
    out = pl.pallas_call(
        functools.partial(_fnet_back_kernel, kc=kc, scale=1.0 / math.sqrt(64.0 * n)),
        grid=(b, n1 // kc),
        in_specs=[zspec, zspec, tspec, tspec, tspec, tspec,
                  pl.BlockSpec((B_WIDTH, B_WIDTH), lambda i, j: (0, 0))],
        out_specs=pl.BlockSpec((1, n2, kc, B_WIDTH), lambda i, j: (i, 0, j, 0)),
        out_shape=jax.ShapeDtypeStruct((b, n2, n1, B_WIDTH), BF16),
        scratch_shapes=[pltpu.VMEM((2, n2 * kc, 128), F32)],
        compiler_params=_cparams("parallel", "parallel"),
        name="fnet_back",
    )(zr, zi, tch, tcl, tsh, tsl, wbd)
    return out.reshape(b, n, B_WIDTH)


def _mem_kv_kernel(mem_ref, wk_ref, wvt_ref, k_ref, vt_ref):
    m = mem_ref[0].astype(BF16)
    k_ref[0] = _dot(m, wk_ref[...]).astype(BF16)
    vt_ref[0] = _dot_nt(wvt_ref[...], m).astype(BF16)


def _mem_kv(mem, w_kv):
    b, m, _ = mem.shape
    wk = w_kv[:, :D_MODEL].astype(BF16)
    wvt = w_kv[:, D_MODEL:].T.astype(BF16)
    wspec = pl.BlockSpec((D_MODEL, D_MODEL), lambda i: (0, 0))
    return pl.pallas_call(
        _mem_kv_kernel,
        grid=(b,),
        in_specs=[pl.BlockSpec((1, m, D_MODEL), lambda i: (i, 0, 0)), wspec, wspec],
        out_specs=[pl.BlockSpec((1, m, D_MODEL), lambda i: (i, 0, 0)),
                   pl.BlockSpec((1, D_MODEL, m), lambda i: (i, 0, 0))],
        out_shape=[jax.ShapeDtypeStruct((b, m, D_MODEL), BF16),
                   jax.ShapeDtypeStruct((b, D_MODEL, m), BF16)],
        compiler_params=_cparams("parallel"),
        name="mem_kv",
    )(mem, wk, wvt)


def _mix_xattn_kernel(a_ref, c_ref, wa_ref, wc_ref, x_ref, g0_ref, b0_ref,
                      wqt_ref, k_ref, vt_ref, wo_ref, g1_ref, b1_ref, o_ref, *, a_transposed):
    halves = [slice(0, TM // 2), slice(TM // 2, TM)]
    hs = []
    for rows in halves:
        if a_transposed:
            h = _dot_tn(a_ref[0, :, rows], wa_ref[...])
        else:
            h = _dot(a_ref[0, rows, :], wa_ref[...])
        hs.append(h + _dot(c_ref[0, rows, :], wc_ref[...]))
    x1s, qts = [], []
    for rows, h in zip(halves, hs):
        x1 = _layer_norm(DN_ALPHA * x_ref[0, rows, :] + h, g0_ref[...], b0_ref[...])
        x1s.append(x1)
        qts.append(_dot_nt(wqt_ref[...], x1.astype(BF16)).astype(BF16))
    qt = jnp.concatenate(qts, axis=1)
    heads = [slice(h * XA_HEAD_DIM, (h + 1) * XA_HEAD_DIM) for h in range(XA_HEADS)]
    scores = [_dot(k_ref[0, :, hd], qt[hd]) for hd in heads]
    ones = jnp.ones((16, k_ref.shape[1]), BF16)
    outs = []
    for hd, st in zip(heads, scores):
        p = jnp.exp2(st - jnp.max(st, axis=0, keepdims=True)).astype(BF16)
        ot = _dot(jnp.concatenate([vt_ref[0, hd, :], ones], axis=0), p)
        outs.append((ot[:XA_HEAD_DIM] / ot[XA_HEAD_DIM:XA_HEAD_DIM + 1]).astype(BF16))
    ot_all = jnp.concatenate(outs, axis=0)
    hs = [_dot_tn(ot_all[:, rows], wo_ref[...]) for rows in halves]
    for rows, x1, h in zip(halves, x1s, hs):
        o_ref[0, rows, :] = _layer_norm(DN_ALPHA * x1 + h, g1_ref[...], b1_ref[...])


def _mix_xattn(a, c, w_out, x, g0, b0, mem, w_q, w_kv, w_o, g1, b1, a_transposed):
    b, n, _ = x.shape
    m = mem.shape[1]
    ka = w_out.shape[0] - c.shape[-1]
    wa = w_out[:ka].astype(BF16)
    wc = w_out[ka:].astype(BF16)
    k, vt = _mem_kv(mem, w_kv)
    wqt = (w_q * (XA_HEAD_DIM ** -0.5 * LOG2E)).T.astype(BF16)
    wo = w_o.astype(BF16)
    if a_transposed:
        aspec = pl.BlockSpec((1, ka, TM), lambda i, j: (i, 0, j))
    else:
        aspec = pl.BlockSpec((1, TM, ka), lambda i, j: (i, j, 0))
    xspec = pl.BlockSpec((1, TM, D_MODEL), lambda i, j: (i, j, 0))
    wspec = pl.BlockSpec((D_MODEL, D_MODEL), lambda i, j: (0, 0))
    vec = pl.BlockSpec((1, D_MODEL), lambda i, j: (0, 0))
    row = lambda v: v.reshape(1, D_MODEL)
    return pl.pallas_call(
        functools.partial(_mix_xattn_kernel, a_transposed=a_transposed),
        grid=(b, n // TM),
        in_specs=[aspec, pl.BlockSpec((1, TM, c.shape[-1]), lambda i, j: (i, j, 0)),
                  pl.BlockSpec(wa.shape, lambda i, j: (0, 0)), pl.BlockSpec(wc.shape, lambda i, j: (0, 0)),
                  xspec, vec, vec, wspec, pl.BlockSpec((1, m, D_MODEL), lambda i, j: (i, 0, 0)),
                  pl.BlockSpec((1, D_MODEL, m), lambda i, j: (i, 0, 0)), wspec, vec, vec],
        out_specs=xspec,
        out_shape=jax.ShapeDtypeStruct((b, n, D_MODEL), F32),
        compiler_params=_cparams("parallel", "parallel"),
        name="mix_xattn_ln",
    )(a, c, wa, wc, x, row(g0), row(b0), wqt, k, vt, wo, row(g1), row(b1))


def _swiglu_kernel(x_ref, win_ref, wo_ref, g_ref, b_ref, o_ref):
    halves = [slice(0, TM // 2), slice(TM // 2, TM)]
    gate_up = []
    for rows in halves:
        xb = x_ref[0, rows, :].astype(BF16)
        gate_up.append((_dot(xb, win_ref[:, :FFN_HIDDEN]), _dot(xb, win_ref[:, FFN_HIDDEN:])))
    parts = []
    for gate, up in gate_up:
        hid = (gate * (1.0 / (1.0 + jnp.exp(-gate))) * up).astype(BF16)
        parts.append(_dot(hid, wo_ref[...]))
    for rows, part in zip(halves, parts):
        o_ref[0, rows, :] = _layer_norm(DN_ALPHA * x_ref[0, rows, :] + part, g_ref[...], b_ref[...])


def _swiglu(x, w_in, w_out, g, bias):
    b, n, _ = x.shape
    win = w_in.astype(BF16)
    wout = w_out.astype(BF16)
    xspec = pl.BlockSpec((1, TM, D_MODEL), lambda i, t: (i, t, 0))
    vec = pl.BlockSpec((1, D_MODEL), lambda i, t: (0, 0))
    resident = pl.Buffered(1)
    return pl.pallas_call(
        _swiglu_kernel,
        grid=(b, n // TM),
        in_specs=[xspec,
                  pl.BlockSpec(win.shape, lambda i, t: (0, 0), pipeline_mode=resident),
                  pl.BlockSpec(wout.shape, lambda i, t: (0, 0), pipeline_mode=resident), vec, vec],
        out_specs=xspec,
        out_shape=jax.ShapeDtypeStruct((b, n, D_MODEL), F32),
        compiler_params=_cparams("parallel", "parallel"),
        name="swiglu_ln",
    )(x, win, wout, g.reshape(1, D_MODEL), bias.reshape(1, D_MODEL))


N_QK_HEADS = C_Q_HEADS + C_KV_HEADS
QK_ROWS = N_QK_HEADS * HEAD_DIM


def _proj_cd_kernel(x_ref, wt_ref, wu_ref, gain_ref, cos_ref, sin_ref,
                    qt_ref, k_ref, vt_ref, u_ref, kn_ref, *, tm):
    xb = x_ref[0].astype(BF16)
    u_ref[0] = _dot(xb, wu_ref[...])
    zt = _dot_nt(wt_ref[...], xb)
    z = zt[:QK_ROWS].reshape(N_QK_HEADS, HEAD_DIM, tm)
    ssq = jnp.sum(z * z, axis=1, keepdims=True)
    zn = z * lax.rsqrt(ssq * (1.0 / HEAD_DIM) + RMS_EPS) * gain_ref[...]
    half = HEAD_DIM // 2
    x1 = zn[:, :half]
    x2 = zn[:, half:]
    c = cos_ref[...][None]
    s = sin_ref[...][None]
    rot = jnp.concatenate([x1 * c - x2 * s, x1 * s + x2 * c], axis=1).reshape(QK_ROWS, tm)
    qt_ref[0] = rot[:C_WIDTH].astype(BF16)
    kb = rot[C_WIDTH:].astype(BF16)
    k_ref[0] = kb.astype(F32).T.astype(BF16)
    kf = kb.astype(F32).reshape(C_KV_HEADS, HEAD_DIM, tm)
    kn_ref[0] = jnp.sum(kf * kf, axis=1, keepdims=True)
    vt = zt[QK_ROWS:].astype(BF16)
    for c0 in range(tm // GQA_TKC):
        vt_ref[0, c0] = vt[:, c0 * GQA_TKC:(c0 + 1) * GQA_TKC]


def _rope_tables_t(n):
    rows = n // GRID_W
    row_id = jnp.broadcast_to(jnp.arange(rows)[:, None], (rows, GRID_W)).reshape(n)
    col_id = jnp.broadcast_to(jnp.arange(GRID_W)[None, :], (rows, GRID_W)).reshape(n)
    axis_dim = HEAD_DIM // 2
    freqs = ROPE_THETA ** (-jnp.arange(0, axis_dim, 2, dtype=F32) / axis_dim)
    ang = jnp.concatenate([row_id[:, None] * freqs, col_id[:, None] * freqs], axis=-1)
    return jnp.cos(ang).T, jnp.sin(ang).T


def _proj_cd(x, w_in, q_norm, k_norm):
    b, n, _ = x.shape
    tm = TM
    half = HEAD_DIM // 2
    wqk = w_in[:, :QK_ROWS].reshape(D_MODEL, N_QK_HEADS, half, 2)
    wqk = wqk.transpose(1, 3, 2, 0).reshape(QK_ROWS, D_MODEL)
    wt = jnp.concatenate([wqk, w_in[:, QK_ROWS:QK_ROWS + C_KV_WIDTH].T], axis=0).astype(BF16)
    wu = w_in[:, QK_ROWS + C_KV_WIDTH:].astype(BF16)
    qg = q_norm.reshape(half, 2).T.reshape(HEAD_DIM) * (HEAD_DIM ** -0.5 * LOG2E)
    kg = k_norm.reshape(half, 2).T.reshape(HEAD_DIM)
    gain = jnp.concatenate([jnp.tile(qg[None], (C_Q_HEADS, 1)), jnp.tile(kg[None], (C_KV_HEADS, 1))])
    gain = gain.reshape(N_QK_HEADS, HEAD_DIM, 1).astype(F32)
    cos_t, sin_t = _rope_tables_t(n)
    nc = n // GQA_TKC
    return pl.pallas_call(
        functools.partial(_proj_cd_kernel, tm=tm),
        grid=(b, n // tm),
        in_specs=[pl.BlockSpec((1, tm, D_MODEL), lambda i, j: (i, j, 0)),
                  pl.BlockSpec(wt.shape, lambda i, j: (0, 0)),
                  pl.BlockSpec(wu.shape, lambda i, j: (0, 0)),
                  pl.BlockSpec(gain.shape, lambda i, j: (0, 0, 0)),
                  pl.BlockSpec((HEAD_DIM // 2, tm), lambda i, j: (0, j)),
                  pl.BlockSpec((HEAD_DIM // 2, tm), lambda i, j: (0, j))],
        out_specs=[pl.BlockSpec((1, C_WIDTH, tm), lambda i, j: (i, 0, j)),
                   pl.BlockSpec((1, tm, C_KV_WIDTH), lambda i, j: (i, j, 0)),
                   pl.BlockSpec((1, tm // GQA_TKC, C_KV_WIDTH, GQA_TKC), lambda i, j: (i, j, 0, 0)),
                   pl.BlockSpec((1, tm, D_WIDTH), lambda i, j: (i, j, 0)),
                   pl.BlockSpec((1, C_KV_HEADS, 1, tm), lambda i, j: (i, 0, 0, j))],
        out_shape=[jax.ShapeDtypeStruct((b, C_WIDTH, n), BF16),
                   jax.ShapeDtypeStruct((b, n, C_KV_WIDTH), BF16),
                   jax.ShapeDtypeStruct((b, nc, C_KV_WIDTH, GQA_TKC), BF16),
                   jax.ShapeDtypeStruct((b, n, D_WIDTH), F32),
                   jax.ShapeDtypeStruct((b, C_KV_HEADS, 1, n), F32)],
        compiler_params=_cparams("parallel", "parallel"),
        name="proj_cd",
    )(x, wt, wu, gain, cos_t, sin_t)


def _gqa_kernel(qt_ref, k_ref, vt_ref, kn_ref, ot_ref, m_ref, l_ref, acc_ref, qpad_ref, *, tq, n):
    g = pl.program_id(1)
    row_half = lax.broadcasted_iota(jnp.int32, (128, 1), 0) // HEAD_DIM
    mine = row_half == (g % 2)
    k_max2 = jnp.max(kn_ref[0, 0], axis=1, keepdims=True)
    bound_max = jnp.zeros((1, 1), F32)
    for j in range(C_REP):
        qj = qt_ref[0, j * HEAD_DIM:(j + 1) * HEAD_DIM, :]
        q2 = jnp.concatenate([qj, qj], axis=0)
        qpad_ref[j] = jnp.where(mine, q2, jnp.zeros_like(q2))
        qf = qj.astype(F32)
        bound = jnp.sqrt(jnp.sum(qf * qf, axis=0, keepdims=True) * k_max2) * GQA_BOUND_SLACK
        m_ref[j] = bound
        bound_max = jnp.maximum(bound_max, jnp.max(bound, axis=1, keepdims=True))
    fixed_shift = bound_max[0, 0] <= GQA_BOUND_LIMIT
    acc_ref[...] = jnp.zeros(acc_ref.shape, F32)
    ones = jnp.ones((GQA_VROWS - HEAD_DIM, GQA_TKC), BF16)

    def body(c, carry, *, online):
        kchs, vchs = [], []
        for u in range(GQA_UNROLL):
            cc = c * GQA_UNROLL + u
            start = pl.multiple_of(cc * GQA_TKC, GQA_TKC)
            kchs.append(k_ref[0, pl.ds(start, GQA_TKC), :])
            vchs.append(jnp.concatenate([vt_ref[0, cc], ones], axis=0))
        pieces = [(u, j, slice(s * GQA_TW, (s + 1) * GQA_TW))
                  for u in range(GQA_UNROLL) for j in range(C_REP) for s in range(tq // GQA_TW)]

        def scores(i):
            u, j, cols = pieces[i]
            return _dot(kchs[u], qpad_ref[j, :, cols])

        pending = [scores(i) for i in range(GQA_AHEAD)]
        for i, (u, j, cols) in enumerate(pieces):
            st = pending.pop(0)
            if i + GQA_AHEAD < len(pieces):
                pending.append(scores(i + GQA_AHEAD))
            m_old = m_ref[j, :, cols]
            if not online:
                p = jnp.exp2(st - m_old)
                l_ref[j, :, cols] = l_ref[j, :, cols] + jnp.sum(p.reshape(GQA_TKC // 8, 8, GQA_TW), axis=0)
                acc_ref[j, :HEAD_DIM, cols] = acc_ref[j, :HEAD_DIM, cols] + _dot(
                    vchs[u][:HEAD_DIM], p.astype(BF16))
                continue
            m_new = jnp.maximum(m_old, jnp.max(st, axis=0, keepdims=True))
            alpha = jnp.exp2(m_old - m_new)
            p = jnp.exp2(st - m_new)
            acc_ref[j, :, cols] = alpha * acc_ref[j, :, cols] + _dot(vchs[u], p.astype(BF16))
            m_ref[j, :, cols] = m_new
        return carry

    trips = n // (GQA_TKC * GQA_UNROLL)

    @pl.when(fixed_shift)
    def _():
        l_ref[...] = jnp.zeros(l_ref.shape, F32)
        lax.fori_loop(0, trips, functools.partial(body, online=False), 0)
        for j in range(C_REP):
            acc_ref[j, HEAD_DIM:HEAD_DIM + 1, :] = jnp.sum(l_ref[j], axis=0, keepdims=True)

    @pl.when(jnp.logical_not(fixed_shift))
    def _():
        m_ref[...] = jnp.full(m_ref.shape, NEG_INF, F32)
        lax.fori_loop(0, trips, functools.partial(body, online=True), 0)

    for j in range(C_REP):
        l = acc_ref[j, HEAD_DIM:HEAD_DIM + 1, :]
        ot_ref[0, j * HEAD_DIM:(j + 1) * HEAD_DIM, :] = (acc_ref[j, :HEAD_DIM, :] / l).astype(BF16)


def _gqa(qt, k, vt, kn):
    b, _, n = qt.shape
    tq = min(GQA_TQ, n)
    nc = n // GQA_TKC
    rows = C_REP * HEAD_DIM
    return pl.pallas_call(
        functools.partial(_gqa_kernel, tq=tq, n=n),
        grid=(b, C_KV_HEADS, n // tq),
        in_specs=[pl.BlockSpec((1, rows, tq), lambda i, g, t: (i, g, t)),
                  pl.BlockSpec((1, n, 128), lambda i, g, t: (i, 0, g // 2)),
                  pl.BlockSpec((1, nc, HEAD_DIM, GQA_TKC), lambda i, g, t: (i, 0, g, 0)),
                  pl.BlockSpec((1, 1, 1, n), lambda i, g, t: (i, g, 0, 0))],
        out_specs=pl.BlockSpec((1, rows, tq), lambda i, g, t: (i, g, t)),
        out_shape=jax.ShapeDtypeStruct((b, C_WIDTH, n), BF16),
        scratch_shapes=[pltpu.VMEM((C_REP, 1, tq), F32), pltpu.VMEM((C_REP, 8, tq), F32),
                        pltpu.VMEM((C_REP, GQA_VROWS, tq), F32), pltpu.VMEM((C_REP, 128, tq), BF16)],
        compiler_params=_cparams("parallel", "parallel", "parallel"),
        name="gqa_flash",
    )(qt, k, vt, kn)


def _pool_kernel(up_ref, uc_ref, un_ref, w_ref, scale_ref, o_ref, buf, *, tm, n):
    i = pl.program_id(1)
    cur = uc_ref[0]
    buf[0:POOL_HALO, :] = jnp.where(i > 0, up_ref[0], 0.0)
    buf[POOL_HALO:POOL_HALO + tm, :] = cur
    buf[POOL_HALO + tm:, :] = jnp.where(i < pl.num_programs(1) - 1, un_ref[0], 0.0)
    lane_group = lax.broadcasted_iota(jnp.int32, (1, D_WIDTH), 1) // 64
    half_w = jnp.left_shift(1, lane_group)
    acc = jnp.zeros((tm, D_WIDTH), F32)
    for j in range(-POOL_HALO, POOL_HALO):
        inside = (j >= -half_w) & (j < half_w)
        acc = acc + jnp.where(inside, buf[POOL_HALO + j:POOL_HALO + j + tm, :], 0.0)
    t = i * tm + lax.broadcasted_iota(jnp.int32, (tm, 1), 0)
    cnt = jnp.minimum(t + half_w, n) - jnp.maximum(t - half_w, 0)
    mixed = (acc / cnt.astype(F32) - cur).astype(BF16)
    o_ref[0] = (_dot(mixed, w_ref[...]) * scale_ref[...]).astype(BF16)


def _pool(u, pool_w, pool_scale):
    b, n, _ = u.shape
    tm = TM
    per = tm // POOL_HALO
    last = n // POOL_HALO - 1
    wbd = jax.scipy.linalg.block_diag(*[pool_w[g] for g in range(4)]).astype(BF16)
    return pl.pallas_call(
        functools.partial(_pool_kernel, tm=tm, n=n),
        grid=(b, n // tm),
        in_specs=[pl.BlockSpec((1, POOL_HALO, D_WIDTH), lambda i, j: (i, jnp.maximum(j * per - 1, 0), 0)),
                  pl.BlockSpec((1, tm, D_WIDTH), lambda i, j: (i, j, 0)),
                  pl.BlockSpec((1, POOL_HALO, D_WIDTH), lambda i, j: (i, jnp.minimum((j + 1) * per, last), 0)),
                  pl.BlockSpec((D_WIDTH, D_WIDTH), lambda i, j: (0, 0)),
                  pl.BlockSpec((1, D_WIDTH), lambda i, j: (0, 0))],
        out_specs=pl.BlockSpec((1, tm, D_WIDTH), lambda i, j: (i, j, 0)),
        out_shape=jax.ShapeDtypeStruct((b, n, D_WIDTH), BF16),
        scratch_shapes=[pltpu.VMEM((tm + 2 * POOL_HALO, D_WIDTH), F32)],
        compiler_params=_cparams("parallel", "parallel"),
        name="pool",
    )(u, u, u, wbd, pool_scale.reshape(1, D_WIDTH))


def _trunk(x, mem, rel_bias, ab_w_in, ab_fnet_g, ab_fnet_w, ab_w_out,
           cd_w_in, cd_q_norm, cd_k_norm, cd_pool_w, cd_pool_scale, cd_w_out,
           xa_w_q, xa_w_kv, xa_w_o, ffn_w_in, ffn_w_out, ln_g, ln_b):
    for layer in range(DEPTH):
        i = layer // 2
        if layer % 2 == 0:
            w_in = ab_w_in[i]
            w_in = jnp.concatenate([w_in[:, :A_WIDTH] * (HEAD_DIM ** -0.5 * LOG2E), w_in[:, A_WIDTH:]], axis=1)
            *views, u = _proj_ab(x, w_in.astype(BF16))
            outs = [_dilated(view, rel_bias, d) for view, (_, d) in zip(views, A_PATTERNS)]
            o_a = _mixture([o for o, _ in outs], [l for _, l in outs])
            o_b = _fnet(u, ab_fnet_g[i], ab_fnet_w[i])
            mixed, w_out, a_transposed = (o_a, o_b), ab_w_out[i], False
        else:
            qt, k, vt, u, kn = _proj_cd(x, cd_w_in[i], cd_q_norm[i], cd_k_norm[i])
            o_c = _gqa(qt, k, vt, kn)
            o_d = _pool(u, cd_pool_w[i], cd_pool_scale[i])
            mixed, w_out, a_transposed = (o_c, o_d), cd_w_out[i], True
        x = _mix_xattn(*mixed, w_out, x, ln_g[layer, 0], ln_b[layer, 0], mem, xa_w_q[layer], xa_w_kv[layer],
                       xa_w_o[layer], ln_g[layer, 1], ln_b[layer, 1], a_transposed=a_transposed)
        x = _swiglu(x, ffn_w_in[layer], ffn_w_out[layer], ln_g[layer, 2], ln_b[layer, 2])
    return x


def kernel(x_prompt, x_sample, mem_prompt, mem_sample, rel_bias, ab_w_in, ab_fnet_g, ab_fnet_w, ab_w_out, cd_w_in, cd_q_norm, cd_k_norm, cd_pool_w, cd_pool_scale, cd_w_out, xa_w_q, xa_w_kv, xa_w_o, ffn_w_in, ffn_w_out, ln_g, ln_b):
    params = (rel_bias, ab_w_in, ab_fnet_g, ab_fnet_w, ab_w_out,
              cd_w_in, cd_q_norm, cd_k_norm, cd_pool_w, cd_pool_scale, cd_w_out,
              xa_w_q, xa_w_kv, xa_w_o, ffn_w_in, ffn_w_out, ln_g, ln_b)
    return (_trunk(x_prompt, mem_prompt, *params), _trunk(x_sample, mem_sample, *params))
```

```python
import functools
import math

import numpy as np
import jax
import jax.numpy as jnp
from jax import lax
from jax.experimental import pallas as pl
from jax.experimental.pallas import tpu as pltpu

F32 = jnp.float32
BF16 = jnp.bfloat16

D_MODEL = 1024
HEAD_DIM = 64
GRID_W = 64
LN_EPS = 1e-5
RMS_EPS = 1e-6
NEG_INF = -1e30
DEPTH = 2
A_HEADS = 12
A_WIDTH = A_HEADS * HEAD_DIM
A_PATTERNS = ((128, 1), (512, 4), (2048, 16))
A_HALF = 64
QKV_WIDTH = 3 * A_WIDTH
N_BUCKETS = 32
REL_MAX_DIST = 1024
B_WIDTH = 256
C_Q_HEADS = 12
C_KV_HEADS = 4
C_REP = C_Q_HEADS // C_KV_HEADS
C_WIDTH = C_Q_HEADS * HEAD_DIM
C_KV_WIDTH = C_KV_HEADS * HEAD_DIM
ROPE_THETA = 10000.0
POOL_WINDOWS = (2, 4, 8, 16)
POOL_HALO = 8
D_WIDTH = 256
XA_HEADS = 4
XA_HEAD_DIM = D_MODEL // XA_HEADS
FFN_HIDDEN = 2816
DN_ALPHA = (2 * DEPTH) ** 0.25
LOG2E = 1.4426950408889634

VMEM_LIMIT = 56 * 1024 * 1024
TM = 512
FFT_N2 = 128
FFT_ROWS = 1024
FFT_KC = 16
FFT_AHEAD = 3
DIL_AHEAD = 3
DIL_TQ = 512
DIL_SQ = 128
DIL_SK = DIL_SQ + 2 * A_HALF
DIL_PIECES = 48
GQA_TQ = 1024
GQA_TW = 256
GQA_TKC = 256
GQA_VROWS = HEAD_DIM + 16
GQA_UNROLL = 8
GQA_AHEAD = 5
GQA_BOUND_SLACK = 1.0 + 2.0 ** -10
GQA_BOUND_LIMIT = 60.0


def _cparams(*sem):
    return pltpu.CompilerParams(dimension_semantics=sem, vmem_limit_bytes=VMEM_LIMIT)


def _dot(a, b):
    return jnp.dot(a, b, preferred_element_type=F32)


def _dot_nt(a, b):
    return lax.dot_general(a, b, (((1,), (1,)), ((), ())), preferred_element_type=F32)


def _dot_tn(a, b):
    return lax.dot_general(a, b, (((0,), (0,)), ((), ())), preferred_element_type=F32)


def _split(x):
    hi = x.astype(BF16)
    lo = (x - hi.astype(F32)).astype(BF16)
    return hi, lo


def _dot3(ah, al, bh, bl):
    return _dot(ah, bh) + _dot(al, bh) + _dot(ah, bl)


def _np_split(x):
    x = np.asarray(x, np.float32)
    hi = x.astype(BF16)
    lo = (x - hi.astype(np.float32)).astype(BF16)
    return jnp.asarray(hi), jnp.asarray(lo)


def _layer_norm(h, g, b):
    mu = jnp.mean(h, axis=-1, keepdims=True)
    xc = h - mu
    var = jnp.mean(xc * xc, axis=-1, keepdims=True)
    return xc * lax.rsqrt(var + LN_EPS) * g + b


def _proj_ab_kernel(x_ref, w_ref, qkv1_ref, qkv4_ref, qkv16_ref, u_ref, zbuf):
    xb = x_ref[0].astype(BF16)
    chunks = list(range(0, QKV_WIDTH, 256))

    for c in chunks:
        z = _dot(xb, w_ref[:, c:c + 256])
        zbuf[c // 128] = z[:, :128]
        zbuf[c // 128 + 1] = z[:, 128:]
        qkv1_ref[0, :, c:c + 256] = z.astype(BF16)
    u_ref[0] = _dot(xb, w_ref[:, QKV_WIDTH:])
    for (_, d), ref in zip(A_PATTERNS[1:], (qkv4_ref, qkv16_ref)):
        rows = TM // d
        for r in range(d):
            for ct in range(QKV_WIDTH // 128):
                col = r * QKV_WIDTH + ct * 128
                ref[0, :, col:col + 128] = zbuf[ct, pl.ds(r, rows, stride=d), :].astype(BF16)


def _proj_ab(x, w):
    b, n, _ = x.shape
    dils = [d for _, d in A_PATTERNS]
    return pl.pallas_call(
        _proj_ab_kernel,
        grid=(b, n // TM),
        in_specs=[pl.BlockSpec((1, TM, D_MODEL), lambda i, j: (i, j, 0)),
                  pl.BlockSpec(w.shape, lambda i, j: (0, 0))],
        out_specs=[pl.BlockSpec((1, TM // d, d * QKV_WIDTH), lambda i, j: (i, j, 0)) for d in dils]
        + [pl.BlockSpec((1, TM, B_WIDTH), lambda i, j: (i, j, 0))],
        out_shape=[jax.ShapeDtypeStruct((b, n // d, d * QKV_WIDTH), BF16) for d in dils]
        + [jax.ShapeDtypeStruct((b, n, B_WIDTH), F32)],
        scratch_shapes=[pltpu.VMEM((QKV_WIDTH // 128, TM, 128), F32)],
        compiler_params=_cparams("parallel", "parallel"),
        name="proj_ab",
    )(x, w)


def _t5_bucket_np(rel):
    nb = N_BUCKETS // 2
    max_exact = nb // 2
    ret = np.where(rel > 0, nb, 0)
    n = np.abs(rel)
    nf = np.maximum(n, 1).astype(np.float32)
    large = max_exact + (np.log(nf / max_exact) / math.log(REL_MAX_DIST / max_exact)
                         * (nb - max_exact)).astype(np.int32)
    large = np.minimum(large, nb - 1)
    return ret + np.where(n < max_exact, n, large)


def _band_bias(rel_bias, dilation):
    tq, tk = DIL_SQ, DIL_SK
    band = 2 * A_HALF + 1
    bucket = _t5_bucket_np((np.arange(band) - A_HALF) * dilation)
    row = (rel_bias[jnp.asarray(bucket)].T * LOG2E).astype(F32)
    row = jnp.concatenate([row, jnp.full((A_HEADS, tk + 1 - band), NEG_INF, F32)], axis=1)
    bias = jnp.tile(row, (1, tq))[:, :tq * tk].reshape(A_HEADS, tq, tk)
    bias = bias.transpose(0, 2, 1)
    bias = bias.reshape(A_HEADS // 2, 2, tk, tq).transpose(0, 2, 1, 3).reshape(A_HEADS // 2, tk, 2 * tq)
    key = np.arange(tk)[None, :, None]
    before = jnp.asarray(key < A_HALF)
    after = jnp.asarray(key >= A_HALF + tq)
    first = jnp.where(before, NEG_INF, bias)
    return jnp.stack([bias, first, jnp.where(after, NEG_INF, bias), jnp.where(after, NEG_INF, first)])


def _dilated_kernel(prev_ref, cur_ref, next_ref, bias_ref, o_ref, lse_ref, kbuf, vbuf, *, tq, n_res):
    for rr in range(n_res):
        for buf, c0 in ((kbuf, rr * QKV_WIDTH + A_WIDTH), (vbuf, rr * QKV_WIDTH + 2 * A_WIDTH)):
            buf[rr, 0:A_HALF, :] = prev_ref[0, :, c0:c0 + A_WIDTH]
            buf[rr, A_HALF:A_HALF + tq, :] = cur_ref[0, :, c0:c0 + A_WIDTH]
            buf[rr, A_HALF + tq:, :] = next_ref[0, :, c0:c0 + A_WIDTH]

    i = pl.program_id(1)
    n_sub = tq // DIL_SQ
    row_low = lax.broadcasted_iota(jnp.int32, (128, 1), 0) < HEAD_DIM
    row16 = lax.broadcasted_iota(jnp.int32, (16, 1), 0)
    ones = jnp.ones((16, DIL_SK), BF16)
    qts, vts = {}, {}
    pieces = [(rr, pair, sub) for rr in range(n_res) for pair in range(A_HEADS // 2) for sub in range(n_sub)]

    def variant(sub):
        v = 0
        if sub == 0:
            v = v + jnp.where(i == 0, 1, 0)
        if sub == n_sub - 1:
            v = v + jnp.where(i == pl.num_programs(1) - 1, 2, 0)
        return v

    def scores(idx):
        rr, pair, sub = pieces[idx]
        cols = slice(pair * 128, (pair + 1) * 128)
        if (rr, pair) not in qts:
            qcols = slice(rr * QKV_WIDTH + pair * 128, rr * QKV_WIDTH + (pair + 1) * 128)
            qts[rr, pair] = cur_ref[0, :, qcols].astype(F32).T
        qsub = qts[rr, pair][:, sub * DIL_SQ:(sub + 1) * DIL_SQ]
        rhs = jnp.concatenate([jnp.where(row_low, qsub, 0.0), jnp.where(row_low, 0.0, qsub)], axis=1)
        keys = kbuf[rr, sub * DIL_SQ:sub * DIL_SQ + DIL_SK, cols]
        return _dot(keys, rhs.astype(BF16)) + bias_ref[variant(sub), pair]

    pending = [scores(idx) for idx in range(DIL_AHEAD)]
    lse_t = {(rr, sub): jnp.zeros((16, DIL_SQ), F32) for rr in range(n_res) for sub in range(n_sub)}
    for idx, (rr, pair, sub) in enumerate(pieces):
        st = pending.pop(0)
        if idx + DIL_AHEAD < len(pieces):
            pending.append(scores(idx + DIL_AHEAD))
        cols = slice(pair * 128, (pair + 1) * 128)
        m = jnp.max(st, axis=0, keepdims=True)
        p = jnp.exp2(st - m).astype(BF16)
        if (rr, pair) not in vts:
            vts[rr, pair] = vbuf[rr, :, cols].astype(F32).T.astype(BF16)
        vaug = jnp.concatenate([vts[rr, pair][:, sub * DIL_SQ:sub * DIL_SQ + DIL_SK], ones], axis=0)
        ot = _dot(vaug, p)
        l = ot[128:129]
        lse2 = m + jnp.log2(l)
        o_pair = jnp.concatenate([ot[:HEAD_DIM, :DIL_SQ] / l[:, :DIL_SQ],
                                  ot[HEAD_DIM:128, DIL_SQ:] / l[:, DIL_SQ:]], axis=0)
        ocols = slice(rr * A_WIDTH + pair * 128, rr * A_WIDTH + (pair + 1) * 128)
        o_ref[0, sub * DIL_SQ:(sub + 1) * DIL_SQ, ocols] = o_pair.T.astype(BF16)
        lse_t[rr, sub] = jnp.where(row16 == 2 * pair, lse2[:, :DIL_SQ],
                                   jnp.where(row16 == 2 * pair + 1, lse2[:, DIL_SQ:], lse_t[rr, sub]))
    for (rr, sub), rows16 in lse_t.items():
        full = jnp.concatenate([rows16, jnp.zeros((128 - 16, DIL_SQ), F32)], axis=0)
        lse_ref[0, sub * DIL_SQ:(sub + 1) * DIL_SQ, rr * 128:(rr + 1) * 128] = full.T


def _dilated(view, rel_bias, dilation):
    b, seq, _ = view.shape
    tq = min(DIL_TQ, seq)
    tk = tq + 2 * A_HALF
    n_res = min(dilation, max(1, DIL_PIECES // (A_HEADS // 2 * (tq // DIL_SQ))))
    bias = _band_bias(rel_bias, dilation)
    per = tq // A_HALF
    last = seq // A_HALF - 1
    width = n_res * QKV_WIDTH

    o, lse = pl.pallas_call(
        functools.partial(_dilated_kernel, tq=tq, n_res=n_res),
        grid=(b, seq // tq, dilation // n_res),
        in_specs=[pl.BlockSpec((1, A_HALF, width), lambda bi, i, r: (bi, jnp.maximum(i * per - 1, 0), r)),
                  pl.BlockSpec((1, tq, width), lambda bi, i, r: (bi, i, r)),
                  pl.BlockSpec((1, A_HALF, width), lambda bi, i, r: (bi, jnp.minimum((i + 1) * per, last), r)),
                  pl.BlockSpec(bias.shape, lambda bi, i, r: (0, 0, 0, 0))],
        out_specs=[pl.BlockSpec((1, tq, n_res * A_WIDTH), lambda bi, i, r: (bi, i, r)),
                   pl.BlockSpec((1, tq, n_res * 128), lambda bi, i, r: (bi, i, r))],
        out_shape=[jax.ShapeDtypeStruct((b, seq, dilation * A_WIDTH), BF16),
                   jax.ShapeDtypeStruct((b, seq, dilation * 128), F32)],
        scratch_shapes=[pltpu.VMEM((n_res, tk, A_WIDTH), BF16), pltpu.VMEM((n_res, tk, A_WIDTH), BF16)],
        compiler_params=_cparams("parallel", "parallel", "parallel"),
        name=f"dilated_d{dilation}",
    )(view, view, view, bias)
    return o, lse


def _mixture_kernel(o1_ref, o4_ref, o16_ref, l1_ref, l4_ref, l16_ref, expand_ref, out_ref, obuf, lbuf):
    for idx, (d, o_ref, l_ref) in enumerate(((A_PATTERNS[1][1], o4_ref, l4_ref), (A_PATTERNS[2][1], o16_ref, l16_ref))):
        rows = TM // d
        for r in range(d):
            for ct in range(A_WIDTH // 128):
                col = r * A_WIDTH + ct * 128
                obuf[idx, ct, pl.ds(r, rows, stride=d), :] = o_ref[0, :, col:col + 128].astype(F32)
            lbuf[idx, pl.ds(r, rows, stride=d), :] = l_ref[0, :, r * 128:(r + 1) * 128]
    ls = [l1_ref[0], lbuf[0], lbuf[1]]
    mx = jnp.maximum(jnp.maximum(ls[0], ls[1]), ls[2])
    es = [jnp.exp2(l - mx) for l in ls]
    inv = 1.0 / (es[0] + es[1] + es[2])
    ws = []
    for e in es:
        wh, wl = _split(e * inv)
        ws.append(_dot(wh, expand_ref[...]) + _dot(wl, expand_ref[...]))
    for pair in range(A_HEADS // 2):
        cols = slice(pair * 128, (pair + 1) * 128)
        os_ = [o1_ref[0, :, cols].astype(F32), obuf[0, pair], obuf[1, pair]]
        acc = None
        for g in range(3):
            t = ws[g][:, cols] * os_[g]
            acc = t if acc is None else acc + t
        out_ref[0, :, cols] = acc.astype(BF16)


def _mixture(os_, ls_):
    b, n, _ = os_[0].shape
    dils = [d for _, d in A_PATTERNS]
    expand = jnp.asarray(np.arange(128)[:, None] == np.arange(A_WIDTH)[None, :] // HEAD_DIM, BF16)
    return pl.pallas_call(
        _mixture_kernel,
        grid=(b, n // TM),
        in_specs=[pl.BlockSpec((1, TM // d, d * A_WIDTH), lambda i, j: (i, j, 0)) for d in dils]
        + [pl.BlockSpec((1, TM // d, d * 128), lambda i, j: (i, j, 0)) for d in dils]
        + [pl.BlockSpec((128, A_WIDTH), lambda i, j: (0, 0))],
        out_specs=pl.BlockSpec((1, TM, A_WIDTH), lambda i, j: (i, j, 0)),
        out_shape=jax.ShapeDtypeStruct((b, n, A_WIDTH), BF16),
        scratch_shapes=[pltpu.VMEM((2, A_WIDTH // 128, TM, 128), F32), pltpu.VMEM((2, TM, 128), F32)],
        compiler_params=_cparams("parallel", "parallel"),
        name="dilated_mixture",
    )(*os_, *ls_, expand)


def _group_mean_matrix():
    g = np.kron(np.eye(4), np.full((64, 64), 1.0 / 64))
    return jnp.asarray(g, BF16)


def _fnet_front_kernel(u_ref, gm_ref, gain_ref, ch_ref, cl_ref, sh_ref, sl_ref, m1h_ref, m1l_ref,
                       zr_ref, zi_ref, ybuf, zbuf, *, n1, tcn):
    rows = n1 * tcn
    u = u_ref[0].reshape(rows, B_WIDTH)
    gm = gm_ref[...]
    uh, ul = _split(u)
    mean = _dot(uh, gm) + _dot(ul, gm)
    xc = u - mean
    qh, ql = _split(xc * xc)
    var = _dot(qh, gm) + _dot(ql, gm)
    un = xc * lax.rsqrt(var + LN_EPS) * gain_ref[...]
    nh, nl = _split(un)
    yr = _dot3(nh, nl, ch_ref[...], cl_ref[...])
    yi = -_dot3(nh, nl, sh_ref[...], sl_ref[...])
    for q, val in enumerate((yr[:, :128], yr[:, 128:], yi[:, :128], yi[:, 128:])):
        ybuf[q] = val
    def first_stage(n2):
        col = [ybuf[q, pl.ds(n2, n1, stride=tcn), :] for q in range(4)]
        y = jnp.concatenate([jnp.concatenate(col[:2], axis=1), jnp.concatenate(col[2:], axis=1)], axis=0)
        yh, yl = _split(y)
        return _dot3(m1h_ref[...], m1l_ref[...], yh, yl)

    pending = [first_stage(n2) for n2 in range(min(FFT_AHEAD, tcn))]
    for n2 in range(tcn):
        z = pending.pop(0)
        if n2 + FFT_AHEAD < tcn:
            pending.append(first_stage(n2 + FFT_AHEAD))
        for q, val in enumerate((z[:n1, :128], z[:n1, 128:], z[n1:, :128], z[n1:, 128:])):
            zbuf[q, pl.ds(n2, n1, stride=tcn), :] = val
    zr_ref[0] = jnp.concatenate([zbuf[0], zbuf[1]], axis=1).reshape(n1, tcn, B_WIDTH)
    zi_ref[0] = jnp.concatenate([zbuf[2], zbuf[3]], axis=1).reshape(n1, tcn, B_WIDTH)


def _fnet_back_kernel(zr_ref, zi_ref, tch_ref, tcl_ref, tsh_ref, tsl_ref, w_ref, o_ref, obuf, *, kc, scale):
    n2 = FFT_N2

    def second_stage(kk):
        rh, rl = _split(zr_ref[0, kk])
        ih, il = _split(zi_ref[0, kk])
        return _dot3(tch_ref[kk], tcl_ref[kk], rh, rl) + _dot3(tsh_ref[kk], tsl_ref[kk], ih, il)

    pending = [second_stage(kk) for kk in range(FFT_AHEAD)]
    for kk in range(kc):
        f = pending.pop(0)
        if kk + FFT_AHEAD < kc:
            pending.append(second_stage(kk + FFT_AHEAD))
        o = _dot((f * scale).astype(BF16), w_ref[...])
        obuf[0, pl.ds(kk, n2, stride=kc), :] = o[:, :128]
        obuf[1, pl.ds(kk, n2, stride=kc), :] = o[:, 128:]
    out = jnp.concatenate([obuf[0], obuf[1]], axis=1).reshape(n2, kc, B_WIDTH)
    o_ref[0] = out.astype(BF16)


def _fnet_tables(n):
    n2 = FFT_N2
    n1 = n // n2
    c = np.arange(64)
    ang = 2 * np.pi * np.outer(c, c) / 64
    cbd = np.kron(np.eye(4), np.cos(ang))
    sbd = np.kron(np.eye(4), np.sin(ang))
    k1 = np.arange(n1)
    ang1 = 2 * np.pi * np.outer(k1, k1) / n1
    c1, s1 = np.cos(ang1), np.sin(ang1)
    m1 = np.block([[c1, s1], [-s1, c1]])
    k2 = np.arange(n2)
    npr = k1[:, None, None] + n1 * k2[None, :, None]
    prod = (npr * k2[None, None, :]) % n
    ang2 = 2 * np.pi * prod / n
    return (_np_split(cbd), _np_split(sbd), _np_split(m1), _np_split(np.cos(ang2)), _np_split(np.sin(ang2)))


def _fnet(u, fnet_g, fnet_w):
    b, n, _ = u.shape
    n2 = FFT_N2
    n1 = n // n2
    tcn = FFT_ROWS // n1
    (ch, cl), (sh, sl), (m1h, m1l), (tch, tcl), (tsh, tsl) = _fnet_tables(n)
    gain = fnet_g.reshape(1, B_WIDTH)
    wbd = jax.scipy.linalg.block_diag(*[fnet_w[g] for g in range(4)]).astype(BF16)

    mat = pl.BlockSpec((B_WIDTH, B_WIDTH), lambda i, j: (0, 0))
    m1spec = pl.BlockSpec((2 * n1, 2 * n1), lambda i, j: (0, 0))
    tile = pl.BlockSpec((1, n1, tcn, B_WIDTH), lambda i, j: (i, 0, j, 0))
    zr, zi = pl.pallas_call(
        functools.partial(_fnet_front_kernel, n1=n1, tcn=tcn),
        grid=(b, n2 // tcn),
        in_specs=[tile, mat, pl.BlockSpec((1, B_WIDTH), lambda i, j: (0, 0)), mat, mat, mat, mat, m1spec, m1spec],
        out_specs=[tile, tile],
        out_shape=[jax.ShapeDtypeStruct((b, n1, n2, B_WIDTH), F32)] * 2,
        scratch_shapes=[pltpu.VMEM((4, FFT_ROWS, 128), F32), pltpu.VMEM((4, FFT_ROWS, 128), F32)],
        compiler_params=_cparams("parallel", "parallel"),
        name="fnet_front",
    )(u.reshape(b, n1, n2, B_WIDTH), _group_mean_matrix(), gain, ch, cl, sh, sl, m1h, m1l)

    kc = FFT_KC
    zspec = pl.BlockSpec((1, kc, n2, B_WIDTH), lambda i, j: (i, j, 0, 0))
    tspec = pl.BlockSpec((kc, n2, n2), lambda i, j: (j, 0, 0))
    out = pl.pallas_call(
        functools.partial(_fnet_back_kernel, kc=kc, scale=1.0 / math.sqrt(64.0 * n)),
        grid=(b, n1 // kc),
        in_specs=[zspec, zspec, tspec, tspec, tspec, tspec,
                  pl.BlockSpec((B_WIDTH, B_WIDTH), lambda i, j: (0, 0))],
        out_specs=pl.BlockSpec((1, n2, kc, B_WIDTH), lambda i, j: (i, 0, j, 0)),
        out_shape=jax.ShapeDtypeStruct((b, n2, n1, B_WIDTH), BF16),
        scratch_shapes=[pltpu.VMEM((2, n2 * kc, 128), F32)],
        compiler_params=_cparams("parallel", "parallel"),
        name="fnet_back",
    )(zr, zi, tch, tcl, tsh, tsl, wbd)
    return out.reshape(b, n, B_WIDTH)


def _mem_kv_kernel(mem_ref, wk_ref, wvt_ref, k_ref, vt_ref):
    m = mem_ref[0].astype(BF16)
    k_ref[0] = _dot(m, wk_ref[...]).astype(BF16)
    vt_ref[0] = _dot_nt(wvt_ref[...], m).astype(BF16)


def _mem_kv(mem, w_kv):
    b, m, _ = mem.shape
    wk = w_kv[:, :D_MODEL].astype(BF16)
    wvt = w_kv[:, D_MODEL:].T.astype(BF16)
    wspec = pl.BlockSpec((D_MODEL, D_MODEL), lambda i: (0, 0))
    return pl.pallas_call(
        _mem_kv_kernel,
        grid=(b,),
        in_specs=[pl.BlockSpec((1, m, D_MODEL), lambda i: (i, 0, 0)), wspec, wspec],
        out_specs=[pl.BlockSpec((1, m, D_MODEL), lambda i: (i, 0, 0)),
                   pl.BlockSpec((1, D_MODEL, m), lambda i: (i, 0, 0))],
        out_shape=[jax.ShapeDtypeStruct((b, m, D_MODEL), BF16),
                   jax.ShapeDtypeStruct((b, D_MODEL, m), BF16)],
        compiler_params=_cparams("parallel"),
        name="mem_kv",
    )(mem, wk, wvt)


def _mix_xattn_kernel(a_ref, c_ref, wa_ref, wc_ref, x_ref, g0_ref, b0_ref,
                      wqt_ref, k_ref, vt_ref, wo_ref, g1_ref, b1_ref, o_ref, *, a_transposed):
    halves = [slice(0, TM // 2), slice(TM // 2, TM)]
    hs = []
    for rows in halves:
        if a_transposed:
            h = _dot_tn(a_ref[0, :, rows], wa_ref[...])
        else:
            h = _dot(a_ref[0, rows, :], wa_ref[...])
        hs.append(h + _dot(c_ref[0, rows, :], wc_ref[...]))
    x1s, qts = [], []
    for rows, h in zip(halves, hs):
        x1 = _layer_norm(DN_ALPHA * x_ref[0, rows, :] + h, g0_ref[...], b0_ref[...])
        x1s.append(x1)
        qts.append(_dot_nt(wqt_ref[...], x1.astype(BF16)).astype(BF16))
    qt = jnp.concatenate(qts, axis=1)
    heads = [slice(h * XA_HEAD_DIM, (h + 1) * XA_HEAD_DIM) for h in range(XA_HEADS)]
    scores = [_dot(k_ref[0, :, hd], qt[hd]) for hd in heads]
    ones = jnp.ones((16, k_ref.shape[1]), BF16)
    outs = []
    for hd, st in zip(heads, scores):
        p = jnp.exp2(st - jnp.max(st, axis=0, keepdims=True)).astype(BF16)
        ot = _dot(jnp.concatenate([vt_ref[0, hd, :], ones], axis=0), p)
        outs.append((ot[:XA_HEAD_DIM] / ot[XA_HEAD_DIM:XA_HEAD_DIM + 1]).astype(BF16))
    ot_all = jnp.concatenate(outs, axis=0)
    hs = [_dot_tn(ot_all[:, rows], wo_ref[...]) for rows in halves]
    for rows, x1, h in zip(halves, x1s, hs):
        o_ref[0, rows, :] = _layer_norm(DN_ALPHA * x1 + h, g1_ref[...], b1_ref[...])


def _mix_xattn(a, c, w_out, x, g0, b0, mem, w_q, w_kv, w_o, g1, b1, a_transposed):
    b, n, _ = x.shape
    m = mem.shape[1]
    ka = w_out.shape[0] - c.shape[-1]
    wa = w_out[:ka].astype(BF16)
    wc = w_out[ka:].astype(BF16)
    k, vt = _mem_kv(mem, w_kv)
    wqt = (w_q * (XA_HEAD_DIM ** -0.5 * LOG2E)).T.astype(BF16)
    wo = w_o.astype(BF16)
    if a_transposed:
        aspec = pl.BlockSpec((1, ka, TM), lambda i, j: (i, 0, j))
    else:
        aspec = pl.BlockSpec((1, TM, ka), lambda i, j: (i, j, 0))
    xspec = pl.BlockSpec((1, TM, D_MODEL), lambda i, j: (i, j, 0))
    wspec = pl.BlockSpec((D_MODEL, D_MODEL), lambda i, j: (0, 0))
    vec = pl.BlockSpec((1, D_MODEL), lambda i, j: (0, 0))
    row = lambda v: v.reshape(1, D_MODEL)
    return pl.pallas_call(
        functools.partial(_mix_xattn_kernel, a_transposed=a_transposed),
        grid=(b, n // TM),
        in_specs=[aspec, pl.BlockSpec((1, TM, c.shape[-1]), lambda i, j: (i, j, 0)),
                  pl.BlockSpec(wa.shape, lambda i, j: (0, 0)), pl.BlockSpec(wc.shape, lambda i, j: (0, 0)),
                  xspec, vec, vec, wspec, pl.BlockSpec((1, m, D_MODEL), lambda i, j: (i, 0, 0)),
                  pl.BlockSpec((1, D_MODEL, m), lambda i, j: (i, 0, 0)), wspec, vec, vec],
        out_specs=xspec,
        out_shape=jax.ShapeDtypeStruct((b, n, D_MODEL), F32),
        compiler_params=_cparams("parallel", "parallel"),
        name="mix_xattn_ln",
    )(a, c, wa, wc, x, row(g0), row(b0), wqt, k, vt, wo, row(g1), row(b1))


def _swiglu_kernel(x_ref, win_ref, wo_ref, g_ref, b_ref, o_ref):
    halves = [slice(0, TM // 2), slice(TM // 2, TM)]
    gate_up = []
    for rows in halves:
        xb = x_ref[0, rows, :].astype(BF16)
        gate_up.append((_dot(xb, win_ref[:, :FFN_HIDDEN]), _dot(xb, win_ref[:, FFN_HIDDEN:])))
    parts = []
    for gate, up in gate_up:
        hid = (gate * (1.0 / (1.0 + jnp.exp(-gate))) * up).astype(BF16)
        parts.append(_dot(hid, wo_ref[...]))
    for rows, part in zip(halves, parts):
        o_ref[0, rows, :] = _layer_norm(DN_ALPHA * x_ref[0, rows, :] + part, g_ref[...], b_ref[...])


def _swiglu(x, w_in, w_out, g, bias):
    b, n, _ = x.shape
    win = w_in.astype(BF16)
    wout = w_out.astype(BF16)
    xspec = pl.BlockSpec((1, TM, D_MODEL), lambda i, t: (i, t, 0))
    vec = pl.BlockSpec((1, D_MODEL), lambda i, t: (0, 0))
    resident = pl.Buffered(1)
    return pl.pallas_call(
        _swiglu_kernel,
        grid=(b, n // TM),
        in_specs=[xspec,
                  pl.BlockSpec(win.shape, lambda i, t: (0, 0), pipeline_mode=resident),
                  pl.BlockSpec(wout.shape, lambda i, t: (0, 0), pipeline_mode=resident), vec, vec],
        out_specs=xspec,
        out_shape=jax.ShapeDtypeStruct((b, n, D_MODEL), F32),
        compiler_params=_cparams("parallel", "parallel"),
        name="swiglu_ln",
    )(x, win, wout, g.reshape(1, D_MODEL), bias.reshape(1, D_MODEL))


N_QK_HEADS = C_Q_HEADS + C_KV_HEADS
QK_ROWS = N_QK_HEADS * HEAD_DIM


def _proj_cd_kernel(x_ref, wt_ref, wu_ref, gain_ref, cos_ref, sin_ref,
                    qt_ref, k_ref, vt_ref, u_ref, kn_ref, *, tm):
    xb = x_ref[0].astype(BF16)
    u_ref[0] = _dot(xb, wu_ref[...])
    zt = _dot_nt(wt_ref[...], xb)
    z = zt[:QK_ROWS].reshape(N_QK_HEADS, HEAD_DIM, tm)
    ssq = jnp.sum(z * z, axis=1, keepdims=True)
    zn = z * lax.rsqrt(ssq * (1.0 / HEAD_DIM) + RMS_EPS) * gain_ref[...]
    half = HEAD_DIM // 2
    x1 = zn[:, :half]
    x2 = zn[:, half:]
    c = cos_ref[...][None]
    s = sin_ref[...][None]
    rot = jnp.concatenate([x1 * c - x2 * s, x1 * s + x2 * c], axis=1).reshape(QK_ROWS, tm)
    qt_ref[0] = rot[:C_WIDTH].astype(BF16)
    kb = rot[C_WIDTH:].astype(BF16)
    k_ref[0] = kb.astype(F32).T.astype(BF16)
    kf = kb.astype(F32).reshape(C_KV_HEADS, HEAD_DIM, tm)
    kn_ref[0] = jnp.sum(kf * kf, axis=1, keepdims=True)
    vt = zt[QK_ROWS:].astype(BF16)
    for c0 in range(tm // GQA_TKC):
        vt_ref[0, c0] = vt[:, c0 * GQA_TKC:(c0 + 1) * GQA_TKC]


def _rope_tables_t(n):
    rows = n // GRID_W
    row_id = jnp.broadcast_to(jnp.arange(rows)[:, None], (rows, GRID_W)).reshape(n)
    col_id = jnp.broadcast_to(jnp.arange(GRID_W)[None, :], (rows, GRID_W)).reshape(n)
    axis_dim = HEAD_DIM // 2
    freqs = ROPE_THETA ** (-jnp.arange(0, axis_dim, 2, dtype=F32) / axis_dim)
    ang = jnp.concatenate([row_id[:, None] * freqs, col_id[:, None] * freqs], axis=-1)
    return jnp.cos(ang).T, jnp.sin(ang).T


def _proj_cd(x, w_in, q_norm, k_norm):
    b, n, _ = x.shape
    tm = TM
    half = HEAD_DIM // 2
    wqk = w_in[:, :QK_ROWS].reshape(D_MODEL, N_QK_HEADS, half, 2)
    wqk = wqk.transpose(1, 3, 2, 0).reshape(QK_ROWS, D_MODEL)
    wt = jnp.concatenate([wqk, w_in[:, QK_ROWS:QK_ROWS + C_KV_WIDTH].T], axis=0).astype(BF16)
    wu = w_in[:, QK_ROWS + C_KV_WIDTH:].astype(BF16)
    qg = q_norm.reshape(half, 2).T.reshape(HEAD_DIM) * (HEAD_DIM ** -0.5 * LOG2E)
    kg = k_norm.reshape(half, 2).T.reshape(HEAD_DIM)
    gain = jnp.concatenate([jnp.tile(qg[None], (C_Q_HEADS, 1)), jnp.tile(kg[None], (C_KV_HEADS, 1))])
    gain = gain.reshape(N_QK_HEADS, HEAD_DIM, 1).astype(F32)
    cos_t, sin_t = _rope_tables_t(n)
    nc = n // GQA_TKC
    return pl.pallas_call(
        functools.partial(_proj_cd_kernel, tm=tm),
        grid=(b, n // tm),
        in_specs=[pl.BlockSpec((1, tm, D_MODEL), lambda i, j: (i, j, 0)),
                  pl.BlockSpec(wt.shape, lambda i, j: (0, 0)),
                  pl.BlockSpec(wu.shape, lambda i, j: (0, 0)),
                  pl.BlockSpec(gain.shape, lambda i, j: (0, 0, 0)),
                  pl.BlockSpec((HEAD_DIM // 2, tm), lambda i, j: (0, j)),
                  pl.BlockSpec((HEAD_DIM // 2, tm), lambda i, j: (0, j))],
        out_specs=[pl.BlockSpec((1, C_WIDTH, tm), lambda i, j: (i, 0, j)),
                   pl.BlockSpec((1, tm, C_KV_WIDTH), lambda i, j: (i, j, 0)),
                   pl.BlockSpec((1, tm // GQA_TKC, C_KV_WIDTH, GQA_TKC), lambda i, j: (i, j, 0, 0)),
                   pl.BlockSpec((1, tm, D_WIDTH), lambda i, j: (i, j, 0)),
                   pl.BlockSpec((1, C_KV_HEADS, 1, tm), lambda i, j: (i, 0, 0, j))],
        out_shape=[jax.ShapeDtypeStruct((b, C_WIDTH, n), BF16),
                   jax.ShapeDtypeStruct((b, n, C_KV_WIDTH), BF16),
                   jax.ShapeDtypeStruct((b, nc, C_KV_WIDTH, GQA_TKC), BF16),
                   jax.ShapeDtypeStruct((b, n, D_WIDTH), F32),
                   jax.ShapeDtypeStruct((b, C_KV_HEADS, 1, n), F32)],
        compiler_params=_cparams("parallel", "parallel"),
        name="proj_cd",
    )(x, wt, wu, gain, cos_t, sin_t)


def _gqa_kernel(qt_ref, k_ref, vt_ref, kn_ref, ot_ref, m_ref, l_ref, acc_ref, qpad_ref, *, tq, n):
    g = pl.program_id(1)
    row_half = lax.broadcasted_iota(jnp.int32, (128, 1), 0) // HEAD_DIM
    mine = row_half == (g % 2)
    k_max2 = jnp.max(kn_ref[0, 0], axis=1, keepdims=True)
    bound_max = jnp.zeros((1, 1), F32)
    for j in range(C_REP):
        qj = qt_ref[0, j * HEAD_DIM:(j + 1) * HEAD_DIM, :]
        q2 = jnp.concatenate([qj, qj], axis=0)
        qpad_ref[j] = jnp.where(mine, q2, jnp.zeros_like(q2))
        qf = qj.astype(F32)
        bound = jnp.sqrt(jnp.sum(qf * qf, axis=0, keepdims=True) * k_max2) * GQA_BOUND_SLACK
        m_ref[j] = bound
        bound_max = jnp.maximum(bound_max, jnp.max(bound, axis=1, keepdims=True))
    fixed_shift = bound_max[0, 0] <= GQA_BOUND_LIMIT
    acc_ref[...] = jnp.zeros(acc_ref.shape, F32)
    ones = jnp.ones((GQA_VROWS - HEAD_DIM, GQA_TKC), BF16)

    def body(c, carry, *, online):
        kchs, vchs = [], []
        for u in range(GQA_UNROLL):
            cc = c * GQA_UNROLL + u
            start = pl.multiple_of(cc * GQA_TKC, GQA_TKC)
            kchs.append(k_ref[0, pl.ds(start, GQA_TKC), :])
            vchs.append(jnp.concatenate([vt_ref[0, cc], ones], axis=0))
        pieces = [(u, j, slice(s * GQA_TW, (s + 1) * GQA_TW))
                  for u in range(GQA_UNROLL) for j in range(C_REP) for s in range(tq // GQA_TW)]

        def scores(i):
            u, j, cols = pieces[i]
            return _dot(kchs[u], qpad_ref[j, :, cols])

        pending = [scores(i) for i in range(GQA_AHEAD)]
        for i, (u, j, cols) in enumerate(pieces):
            st = pending.pop(0)
            if i + GQA_AHEAD < len(pieces):
                pending.append(scores(i + GQA_AHEAD))
            m_old = m_ref[j, :, cols]
            if not online:
                p = jnp.exp2(st - m_old)
                l_ref[j, :, cols] = l_ref[j, :, cols] + jnp.sum(p.reshape(GQA_TKC // 8, 8, GQA_TW), axis=0)
                acc_ref[j, :HEAD_DIM, cols] = acc_ref[j, :HEAD_DIM, cols] + _dot(
                    vchs[u][:HEAD_DIM], p.astype(BF16))
                continue
            m_new = jnp.maximum(m_old, jnp.max(st, axis=0, keepdims=True))
            alpha = jnp.exp2(m_old - m_new)
            p = jnp.exp2(st - m_new)
            acc_ref[j, :, cols] = alpha * acc_ref[j, :, cols] + _dot(vchs[u], p.astype(BF16))
            m_ref[j, :, cols] = m_new
        return carry

    trips = n // (GQA_TKC * GQA_UNROLL)

    @pl.when(fixed_shift)
    def _():
        l_ref[...] = jnp.zeros(l_ref.shape, F32)
        lax.fori_loop(0, trips, functools.partial(body, online=False), 0)
        for j in range(C_REP):
            acc_ref[j, HEAD_DIM:HEAD_DIM + 1, :] = jnp.sum(l_ref[j], axis=0, keepdims=True)

    @pl.when(jnp.logical_not(fixed_shift))
    def _():
        m_ref[...] = jnp.full(m_ref.shape, NEG_INF, F32)
        lax.fori_loop(0, trips, functools.partial(body, online=True), 0)

    for j in range(C_REP):
        l = acc_ref[j, HEAD_DIM:HEAD_DIM + 1, :]
        ot_ref[0, j * HEAD_DIM:(j + 1) * HEAD_DIM, :] = (acc_ref[j, :HEAD_DIM, :] / l).astype(BF16)


def _gqa(qt, k, vt, kn):
    b, _, n = qt.shape
    tq = min(GQA_TQ, n)
    nc = n // GQA_TKC
    rows = C_REP * HEAD_DIM
    return pl.pallas_call(
        functools.partial(_gqa_kernel, tq=tq, n=n),
        grid=(b, C_KV_HEADS, n // tq),
        in_specs=[pl.BlockSpec((1, rows, tq), lambda i, g, t: (i, g, t)),
                  pl.BlockSpec((1, n, 128), lambda i, g, t: (i, 0, g // 2)),
                  pl.BlockSpec((1, nc, HEAD_DIM, GQA_TKC), lambda i, g, t: (i, 0, g, 0)),
                  pl.BlockSpec((1, 1, 1, n), lambda i, g, t: (i, g, 0, 0))],
        out_specs=pl.BlockSpec((1, rows, tq), lambda i, g, t: (i, g, t)),
        out_shape=jax.ShapeDtypeStruct((b, C_WIDTH, n), BF16),
        scratch_shapes=[pltpu.VMEM((C_REP, 1, tq), F32), pltpu.VMEM((C_REP, 8, tq), F32),
                        pltpu.VMEM((C_REP, GQA_VROWS, tq), F32), pltpu.VMEM((C_REP, 128, tq), BF16)],
        compiler_params=_cparams("parallel", "parallel", "parallel"),
        name="gqa_flash",
    )(qt, k, vt, kn)


def _pool_kernel(up_ref, uc_ref, un_ref, w_ref, scale_ref, o_ref, buf, *, tm, n):
    i = pl.program_id(1)
    cur = uc_ref[0]
    buf[0:POOL_HALO, :] = jnp.where(i > 0, up_ref[0], 0.0)
    buf[POOL_HALO:POOL_HALO + tm, :] = cur
    buf[POOL_HALO + tm:, :] = jnp.where(i < pl.num_programs(1) - 1, un_ref[0], 0.0)
    lane_group = lax.broadcasted_iota(jnp.int32, (1, D_WIDTH), 1) // 64
    half_w = jnp.left_shift(1, lane_group)
    acc = jnp.zeros((tm, D_WIDTH), F32)
    for j in range(-POOL_HALO, POOL_HALO):
        inside = (j >= -half_w) & (j < half_w)
        acc = acc + jnp.where(inside, buf[POOL_HALO + j:POOL_HALO + j + tm, :], 0.0)
    t = i * tm + lax.broadcasted_iota(jnp.int32, (tm, 1), 0)
    cnt = jnp.minimum(t + half_w, n) - jnp.maximum(t - half_w, 0)
    mixed = (acc / cnt.astype(F32) - cur).astype(BF16)
    o_ref[0] = (_dot(mixed, w_ref[...]) * scale_ref[...]).astype(BF16)


def _pool(u, pool_w, pool_scale):
    b, n, _ = u.shape
    tm = TM
    per = tm // POOL_HALO
    last = n // POOL_HALO - 1
    wbd = jax.scipy.linalg.block_diag(*[pool_w[g] for g in range(4)]).astype(BF16)
    return pl.pallas_call(
        functools.partial(_pool_kernel, tm=tm, n=n),
        grid=(b, n // tm),
        in_specs=[pl.BlockSpec((1, POOL_HALO, D_WIDTH), lambda i, j: (i, jnp.maximum(j * per - 1, 0), 0)),
                  pl.BlockSpec((1, tm, D_WIDTH), lambda i, j: (i, j, 0)),
                  pl.BlockSpec((1, POOL_HALO, D_WIDTH), lambda i, j: (i, jnp.minimum((j + 1) * per, last), 0)),
                  pl.BlockSpec((D_WIDTH, D_WIDTH), lambda i, j: (0, 0)),
                  pl.BlockSpec((1, D_WIDTH), lambda i, j: (0, 0))],
        out_specs=pl.BlockSpec((1, tm, D_WIDTH), lambda i, j: (i, j, 0)),
        out_shape=jax.ShapeDtypeStruct((b, n, D_WIDTH), BF16),
        scratch_shapes=[pltpu.VMEM((tm + 2 * POOL_HALO, D_WIDTH), F32)],
        compiler_params=_cparams("parallel", "parallel"),
        name="pool",
    )(u, u, u, wbd, pool_scale.reshape(1, D_WIDTH))


def _trunk(x, mem, rel_bias, ab_w_in, ab_fnet_g, ab_fnet_w, ab_w_out,
           cd_w_in, cd_q_norm, cd_k_norm, cd_pool_w, cd_pool_scale, cd_w_out,
           xa_w_q, xa_w_kv, xa_w_o, ffn_w_in, ffn_w_out, ln_g, ln_b):
    for layer in range(DEPTH):
        i = layer // 2
        if layer % 2 == 0:
            w_in = ab_w_in[i]
            w_in = jnp.concatenate([w_in[:, :A_WIDTH] * (HEAD_DIM ** -0.5 * LOG2E), w_in[:, A_WIDTH:]], axis=1)
            *views, u = _proj_ab(x, w_in.astype(BF16))
            outs = [_dilated(view, rel_bias, d) for view, (_, d) in zip(views, A_PATTERNS)]
            o_a = _mixture([o for o, _ in outs], [l for _, l in outs])
            o_b = _fnet(u, ab_fnet_g[i], ab_fnet_w[i])
            mixed, w_out, a_transposed = (o_a, o_b), ab_w_out[i], False
        else:
            qt, k, vt, u, kn = _proj_cd(x, cd_w_in[i], cd_q_norm[i], cd_k_norm[i])
            o_c = _gqa(qt, k, vt, kn)
            o_d = _pool(u, cd_pool_w[i], cd_pool_scale[i])
            mixed, w_out, a_transposed = (o_c, o_d), cd_w_out[i], True
        x = _mix_xattn(*mixed, w_out, x, ln_g[layer, 0], ln_b[layer, 0], mem, xa_w_q[layer], xa_w_kv[layer],
                       xa_w_o[layer], ln_g[layer, 1], ln_b[layer, 1], a_transposed=a_transposed)
        x = _swiglu(x, ffn_w_in[layer], ffn_w_out[layer], ln_g[layer, 2], ln_b[layer, 2])
    return x


def kernel(x_prompt, x_sample, mem_prompt, mem_sample, rel_bias, ab_w_in, ab_fnet_g, ab_fnet_w, ab_w_out, cd_w_in, cd_q_norm, cd_k_norm, cd_pool_w, cd_pool_scale, cd_w_out, xa_w_q, xa_w_kv, xa_w_o, ffn_w_in, ffn_w_out, ln_g, ln_b):
    params = (rel_bias, ab_w_in, ab_fnet_g, ab_fnet_w, ab_w_out,
              cd_w_in, cd_q_norm, cd_k_norm, cd_pool_w, cd_pool_scale, cd_w_out,
              xa_w_q, xa_w_kv, xa_w_o, ffn_w_in, ffn_w_out, ln_g, ln_b)
    return (_trunk(x_prompt, mem_prompt, *params), _trunk(x_sample, mem_sample, *params))
```

```python
import functools
import math

import numpy as np
import jax
import jax.numpy as jnp
from jax import lax
from jax.experimental import pallas as pl
from jax.experimental.pallas import tpu as pltpu

F32 = jnp.float32
BF16 = jnp.bfloat16

D_MODEL = 1024
HEAD_DIM = 64
GRID_W = 64
LN_EPS = 1e-5
RMS_EPS = 1e-6
NEG_INF = -1e30
DEPTH = 2
A_HEADS = 12
A_WIDTH = A_HEADS * HEAD_DIM
A_PATTERNS = ((128, 1), (512, 4), (2048, 16))
A_HALF = 64
QKV_WIDTH = 3 * A_WIDTH
N_BUCKETS = 32
REL_MAX_DIST = 1024
B_WIDTH = 256
C_Q_HEADS = 12
C_KV_HEADS = 4
C_REP = C_Q_HEADS // C_KV_HEADS
C_WIDTH = C_Q_HEADS * HEAD_DIM
C_KV_WIDTH = C_KV_HEADS * HEAD_DIM
ROPE_THETA = 10000.0
POOL_WINDOWS = (2, 4, 8, 16)
POOL_HALO = 8
D_WIDTH = 256
XA_HEADS = 4
XA_HEAD_DIM = D_MODEL // XA_HEADS
FFN_HIDDEN = 2816
DN_ALPHA = (2 * DEPTH) ** 0.25
LOG2E = 1.4426950408889634

VMEM_LIMIT = 56 * 1024 * 1024
TM = 512
FFT_N2 = 128
FFT_ROWS = 1024
FFT_KC = 16
FFT_AHEAD = 3
DIL_AHEAD = 3
DIL_TQ = 512
DIL_SQ = 128
DIL_SK = DIL_SQ + 2 * A_HALF
DIL_PIECES = 48
GQA_TQ = 1024
GQA_TW = 256
GQA_TKC = 256
GQA_VROWS = HEAD_DIM + 16
GQA_UNROLL = 8
GQA_AHEAD = 5
GQA_BOUND_SLACK = 1.0 + 2.0 ** -10
GQA_BOUND_LIMIT = 60.0


def _cparams(*sem):
    return pltpu.CompilerParams(dimension_semantics=sem, vmem_limit_bytes=VMEM_LIMIT)


def _dot(a, b):
    return jnp.dot(a, b, preferred_element_type=F32)


def _dot_nt(a, b):
    return lax.dot_general(a, b, (((1,), (1,)), ((), ())), preferred_element_type=F32)


def _dot_tn(a, b):
    return lax.dot_general(a, b, (((0,), (0,)), ((), ())), preferred_element_type=F32)


def _split(x):
    hi = x.astype(BF16)
    lo = (x - hi.astype(F32)).astype(BF16)
    return hi, lo


def _dot3(ah, al, bh, bl):
    return _dot(ah, bh) + _dot(al, bh) + _dot(ah, bl)


def _np_split(x):
    x = np.asarray(x, np.float32)
    hi = x.astype(BF16)
    lo = (x - hi.astype(np.float32)).astype(BF16)
    return jnp.asarray(hi), jnp.asarray(lo)


def _layer_norm(h, g, b):
    mu = jnp.mean(h, axis=-1, keepdims=True)
    xc = h - mu
    var = jnp.mean(xc * xc, axis=-1, keepdims=True)
    return xc * lax.rsqrt(var + LN_EPS) * g + b


def _proj_ab_kernel(x_ref, w_ref, qkv1_ref, qkv4_ref, qkv16_ref, u_ref, zbuf):
    xb = x_ref[0].astype(BF16)
    chunks = list(range(0, QKV_WIDTH, 256))

    for c in chunks:
        z = _dot(xb, w_ref[:, c:c + 256])
        zbuf[c // 128] = z[:, :128]
        zbuf[c // 128 + 1] = z[:, 128:]
        qkv1_ref[0, :, c:c + 256] = z.astype(BF16)
    u_ref[0] = _dot(xb, w_ref[:, QKV_WIDTH:])
    for (_, d), ref in zip(A_PATTERNS[1:], (qkv4_ref, qkv16_ref)):
        rows = TM // d
        for r in range(d):
            for ct in range(QKV_WIDTH // 128):
                col = r * QKV_WIDTH + ct * 128
                ref[0, :, col:col + 128] = zbuf[ct, pl.ds(r, rows, stride=d), :].astype(BF16)


def _proj_ab(x, w):
    b, n, _ = x.shape
    dils = [d for _, d in A_PATTERNS]
    return pl.pallas_call(
        _proj_ab_kernel,
        grid=(b, n // TM),
        in_specs=[pl.BlockSpec((1, TM, D_MODEL), lambda i, j: (i, j, 0)),
                  pl.BlockSpec(w.shape, lambda i, j: (0, 0))],
        out_specs=[pl.BlockSpec((1, TM // d, d * QKV_WIDTH), lambda i, j: (i, j, 0)) for d in dils]
        + [pl.BlockSpec((1, TM, B_WIDTH), lambda i, j: (i, j, 0))],
        out_shape=[jax.ShapeDtypeStruct((b, n // d, d * QKV_WIDTH), BF16) for d in dils]
        + [jax.ShapeDtypeStruct((b, n, B_WIDTH), F32)],
        scratch_shapes=[pltpu.VMEM((QKV_WIDTH // 128, TM, 128), F32)],
        compiler_params=_cparams("parallel", "parallel"),
        name="proj_ab",
    )(x, w)


def _t5_bucket_np(rel):
    nb = N_BUCKETS // 2
    max_exact = nb // 2
    ret = np.where(rel > 0, nb, 0)
    n = np.abs(rel)
    nf = np.maximum(n, 1).astype(np.float32)
    large = max_exact + (np.log(nf / max_exact) / math.log(REL_MAX_DIST / max_exact)
                         * (nb - max_exact)).astype(np.int32)
    large = np.minimum(large, nb - 1)
    return ret + np.where(n < max_exact, n, large)


def _band_bias(rel_bias, dilation):
    tq, tk = DIL_SQ, DIL_SK
    band = 2 * A_HALF + 1
    bucket = _t5_bucket_np((np.arange(band) - A_HALF) * dilation)
    row = (rel_bias[jnp.asarray(bucket)].T * LOG2E).astype(F32)
    row = jnp.concatenate([row, jnp.full((A_HEADS, tk + 1 - band), NEG_INF, F32)], axis=1)
    bias = jnp.tile(row, (1, tq))[:, :tq * tk].reshape(A_HEADS, tq, tk)
    bias = bias.transpose(0, 2, 1)
    bias = bias.reshape(A_HEADS // 2, 2, tk, tq).transpose(0, 2, 1, 3).reshape(A_HEADS // 2, tk, 2 * tq)
    key = np.arange(tk)[None, :, None]
    before = jnp.asarray(key < A_HALF)
    after = jnp.asarray(key >= A_HALF + tq)
    first = jnp.where(before, NEG_INF, bias)
    return jnp.stack([bias, first, jnp.where(after, NEG_INF, bias), jnp.where(after, NEG_INF, first)])


def _dilated_kernel(prev_ref, cur_ref, next_ref, bias_ref, o_ref, lse_ref, kbuf, vbuf, *, tq, n_res):
    for rr in range(n_res):
        for buf, c0 in ((kbuf, rr * QKV_WIDTH + A_WIDTH), (vbuf, rr * QKV_WIDTH + 2 * A_WIDTH)):
            buf[rr, 0:A_HALF, :] = prev_ref[0, :, c0:c0 + A_WIDTH]
            buf[rr, A_HALF:A_HALF + tq, :] = cur_ref[0, :, c0:c0 + A_WIDTH]
            buf[rr, A_HALF + tq:, :] = next_ref[0, :, c0:c0 + A_WIDTH]

    i = pl.program_id(1)
    n_sub = tq // DIL_SQ
    row_low = lax.broadcasted_iota(jnp.int32, (128, 1), 0) < HEAD_DIM
    row16 = lax.broadcasted_iota(jnp.int32, (16, 1), 0)
    ones = jnp.ones((16, DIL_SK), BF16)
    qts, vts = {}, {}
    pieces = [(rr, pair, sub) for rr in range(n_res) for pair in range(A_HEADS // 2) for sub in range(n_sub)]

    def variant(sub):
        v = 0
        if sub == 0:
            v = v + jnp.where(i == 0, 1, 0)
        if sub == n_sub - 1:
            v = v + jnp.where(i == pl.num_programs(1) - 1, 2, 0)
        return v

    def scores(idx):
        rr, pair, sub = pieces[idx]
        cols = slice(pair * 128, (pair + 1) * 128)
        if (rr, pair) not in qts:
            qcols = slice(rr * QKV_WIDTH + pair * 128, rr * QKV_WIDTH + (pair + 1) * 128)
            qts[rr, pair] = cur_ref[0, :, qcols].astype(F32).T
        qsub = qts[rr, pair][:, sub * DIL_SQ:(sub + 1) * DIL_SQ]
        rhs = jnp.concatenate([jnp.where(row_low, qsub, 0.0), jnp.where(row_low, 0.0, qsub)], axis=1)
        keys = kbuf[rr, sub * DIL_SQ:sub * DIL_SQ + DIL_SK, cols]
        return _dot(keys, rhs.astype(BF16)) + bias_ref[variant(sub), pair]

    pending = [scores(idx) for idx in range(DIL_AHEAD)]
    lse_t = {(rr, sub): jnp.zeros((16, DIL_SQ), F32) for rr in range(n_res) for sub in range(n_sub)}
    for idx, (rr, pair, sub) in enumerate(pieces):
        st = pending.pop(0)
        if idx + DIL_AHEAD < len(pieces):
            pending.append(scores(idx + DIL_AHEAD))
        cols = slice(pair * 128, (pair + 1) * 128)
        m = jnp.max(st, axis=0, keepdims=True)
        p = jnp.exp2(st - m).astype(BF16)
        if (rr, pair) not in vts:
            vts[rr, pair] = vbuf[rr, :, cols].astype(F32).T.astype(BF16)
        vaug = jnp.concatenate([vts[rr, pair][:, sub * DIL_SQ:sub * DIL_SQ + DIL_SK], ones], axis=0)
        ot = _dot(vaug, p)
        l = ot[128:129]
        lse2 = m + jnp.log2(l)
        o_pair = jnp.concatenate([ot[:HEAD_DIM, :DIL_SQ] / l[:, :DIL_SQ],
                                  ot[HEAD_DIM:128, DIL_SQ:] / l[:, DIL_SQ:]], axis=0)
        ocols = slice(rr * A_WIDTH + pair * 128, rr * A_WIDTH + (pair + 1) * 128)
        o_ref[0, sub * DIL_SQ:(sub + 1) * DIL_SQ, ocols] = o_pair.T.astype(BF16)
        lse_t[rr, sub] = jnp.where(row16 == 2 * pair, lse2[:, :DIL_SQ],
                                   jnp.where(row16 == 2 * pair + 1, lse2[:, DIL_SQ:], lse_t[rr, sub]))
    for (rr, sub), rows16 in lse_t.items():
        full = jnp.concatenate([rows16, jnp.zeros((128 - 16, DIL_SQ), F32)], axis=0)
        lse_ref[0, sub * DIL_SQ:(sub + 1) * DIL_SQ, rr * 128:(rr + 1) * 128] = full.T


def _dilated(view, rel_bias, dilation):
    b, seq, _ = view.shape
    tq = min(DIL_TQ, seq)
    tk = tq + 2 * A_HALF
    n_res = min(dilation, max(1, DIL_PIECES // (A_HEADS // 2 * (tq // DIL_SQ))))
    bias = _band_bias(rel_bias, dilation)
    per = tq // A_HALF
    last = seq // A_HALF - 1
    width = n_res * QKV_WIDTH

    o, lse = pl.pallas_call(
        functools.partial(_dilated_kernel, tq=tq, n_res=n_res),
        grid=(b, seq // tq, dilation // n_res),
        in_specs=[pl.BlockSpec((1, A_HALF, width), lambda bi, i, r: (bi, jnp.maximum(i * per - 1, 0), r)),
                  pl.BlockSpec((1, tq, width), lambda bi, i, r: (bi, i, r)),
                  pl.BlockSpec((1, A_HALF, width), lambda bi, i, r: (bi, jnp.minimum((i + 1) * per, last), r)),
                  pl.BlockSpec(bias.shape, lambda bi, i, r: (0, 0, 0, 0))],
        out_specs=[pl.BlockSpec((1, tq, n_res * A_WIDTH), lambda bi, i, r: (bi, i, r)),
                   pl.BlockSpec((1, tq, n_res * 128), lambda bi, i, r: (bi, i, r))],
        out_shape=[jax.ShapeDtypeStruct((b, seq, dilation * A_WIDTH), BF16),
                   jax.ShapeDtypeStruct((b, seq, dilation * 128), F32)],
        scratch_shapes=[pltpu.VMEM((n_res, tk, A_WIDTH), BF16), pltpu.VMEM((n_res, tk, A_WIDTH), BF16)],
        compiler_params=_cparams("parallel", "parallel", "parallel"),
        name=f"dilated_d{dilation}",
    )(view, view, view, bias)
    return o, lse


def _mixture_tile(o1_ref, o4_ref, o16_ref, l1_ref, l4_ref, l16_ref, expand_ref, obuf, lbuf):
    for idx, (d, o_ref, l_ref) in enumerate(((A_PATTERNS[1][1], o4_ref, l4_ref), (A_PATTERNS[2][1], o16_ref, l16_ref))):
        rows = TM // d
        for r in range(d):
            for ct in range(A_WIDTH // 128):
                col = r * A_WIDTH + ct * 128
                obuf[idx, ct, pl.ds(r, rows, stride=d), :] = o_ref[0, :, col:col + 128].astype(F32)
            lbuf[idx, pl.ds(r, rows, stride=d), :] = l_ref[0, :, r * 128:(r + 1) * 128]
    ls = [l1_ref[0], lbuf[0], lbuf[1]]
    mx = jnp.maximum(jnp.maximum(ls[0], ls[1]), ls[2])
    es = [jnp.exp2(l - mx) for l in ls]
    inv = 1.0 / (es[0] + es[1] + es[2])
    ws = []
    for e in es:
        wh, wl = _split(e * inv)
        ws.append(_dot(wh, expand_ref[...]) + _dot(wl, expand_ref[...]))
    pairs = []
    for pair in range(A_HEADS // 2):
        cols = slice(pair * 128, (pair + 1) * 128)
        os_ = [o1_ref[0, :, cols].astype(F32), obuf[0, pair], obuf[1, pair]]
        acc = None
        for g in range(3):
            t = ws[g][:, cols] * os_[g]
            acc = t if acc is None else acc + t
        pairs.append(acc.astype(BF16))
    return jnp.concatenate(pairs, axis=1)


def _group_mean_matrix():
    g = np.kron(np.eye(4), np.full((64, 64), 1.0 / 64))
    return jnp.asarray(g, BF16)


def _fnet_front_kernel(u_ref, gm_ref, gain_ref, ch_ref, cl_ref, sh_ref, sl_ref, m1h_ref, m1l_ref,
                       zr_ref, zi_ref, ybuf, zbuf, *, n1, tcn):
    rows = n1 * tcn
    u = u_ref[0].reshape(rows, B_WIDTH)
    gm = gm_ref[...]
    uh, ul = _split(u)
    mean = _dot(uh, gm) + _dot(ul, gm)
    xc = u - mean
    qh, ql = _split(xc * xc)
    var = _dot(qh, gm) + _dot(ql, gm)
    un = xc * lax.rsqrt(var + LN_EPS) * gain_ref[...]
    nh, nl = _split(un)
    yr = _dot3(nh, nl, ch_ref[...], cl_ref[...])
    yi = -_dot3(nh, nl, sh_ref[...], sl_ref[...])
    for q, val in enumerate((yr[:, :128], yr[:, 128:], yi[:, :128], yi[:, 128:])):
        ybuf[q] = val
    def first_stage(n2):
        col = [ybuf[q, pl.ds(n2, n1, stride=tcn), :] for q in range(4)]
        y = jnp.concatenate([jnp.concatenate(col[:2], axis=1), jnp.concatenate(col[2:], axis=1)], axis=0)
        yh, yl = _split(y)
        return _dot3(m1h_ref[...], m1l_ref[...], yh, yl)

    pending = [first_stage(n2) for n2 in range(min(FFT_AHEAD, tcn))]
    for n2 in range(tcn):
        z = pending.pop(0)
        if n2 + FFT_AHEAD < tcn:
            pending.append(first_stage(n2 + FFT_AHEAD))
        for q, val in enumerate((z[:n1, :128], z[:n1, 128:], z[n1:, :128], z[n1:, 128:])):
            zbuf[q, pl.ds(n2, n1, stride=tcn), :] = val
    zr_ref[0] = jnp.concatenate([zbuf[0], zbuf[1]], axis=1).reshape(n1, tcn, B_WIDTH)
    zi_ref[0] = jnp.concatenate([zbuf[2], zbuf[3]], axis=1).reshape(n1, tcn, B_WIDTH)


def _fnet_back_kernel(zr_ref, zi_ref, tch_ref, tcl_ref, tsh_ref, tsl_ref, w_ref, o_ref, obuf, *, kc, scale):
    n2 = FFT_N2

    def second_stage(kk):
        rh, rl = _split(zr_ref[0, kk])
        ih, il = _split(zi_ref[0, kk])
        return _dot3(tch_ref[kk], tcl_ref[kk], rh, rl) + _dot3(tsh_ref[kk], tsl_ref[kk], ih, il)

    pending = [second_stage(kk) for kk in range(FFT_AHEAD)]
    for kk in range(kc):
        f = pending.pop(0)
        if kk + FFT_AHEAD < kc:
            pending.append(second_stage(kk + FFT_AHEAD))
        o = _dot((f * scale).astype(BF16), w_ref[...])
        obuf[0, pl.ds(kk, n2, stride=kc), :] = o[:, :128]
        obuf[1, pl.ds(kk, n2, stride=kc), :] = o[:, 128:]
    out = jnp.concatenate([obuf[0], obuf[1]], axis=1).reshape(n2, kc, B_WIDTH)
    o_ref[0] = out.astype(BF16)


def _fnet_tables(n):
    n2 = FFT_N2
    n1 = n // n2
    c = np.arange(64)
    ang = 2 * np.pi * np.outer(c, c) / 64
    cbd = np.kron(np.eye(4), np.cos(ang))
    sbd = np.kron(np.eye(4), np.sin(ang))
    k1 = np.arange(n1)
    ang1 = 2 * np.pi * np.outer(k1, k1) / n1
    c1, s1 = np.cos(ang1), np.sin(ang1)
    m1 = np.block([[c1, s1], [-s1, c1]])
    k2 = np.arange(n2)
    npr = k1[:, None, None] + n1 * k2[None, :, None]
    prod = (npr * k2[None, None, :]) % n
    ang2 = 2 * np.pi * prod / n
    return (_np_split(cbd), _np_split(sbd), _np_split(m1), _np_split(np.cos(ang2)), _np_split(np.sin(ang2)))


def _fnet(u, fnet_g, fnet_w):
    b, n, _ = u.shape
    n2 = FFT_N2
    n1 = n // n2
    tcn = FFT_ROWS // n1
    (ch, cl), (sh, sl), (m1h, m1l), (tch, tcl), (tsh, tsl) = _fnet_tables(n)
    gain = fnet_g.reshape(1, B_WIDTH)
    wbd = jax.scipy.linalg.block_diag(*[fnet_w[g] for g in range(4)]).astype(BF16)

    mat = pl.BlockSpec((B_WIDTH, B_WIDTH), lambda i, j: (0, 0))
    m1spec = pl.BlockSpec((2 * n1, 2 * n1), lambda i, j: (0, 0))
    tile = pl.BlockSpec((1, n1, tcn, B_WIDTH), lambda i, j: (i, 0, j, 0))
    zr, zi = pl.pallas_call(
        functools.partial(_fnet_front_kernel, n1=n1, tcn=tcn),
        grid=(b, n2 // tcn),
        in_specs=[tile, mat, pl.BlockSpec((1, B_WIDTH), lambda i, j: (0, 0)), mat, mat, mat, mat, m1spec, m1spec],
        out_specs=[tile, tile],
        out_shape=[jax.ShapeDtypeStruct((b, n1, n2, B_WIDTH), F32)] * 2,
        scratch_shapes=[pltpu.VMEM((4, FFT_ROWS, 128), F32), pltpu.VMEM((4, FFT_ROWS, 128), F32)],
        compiler_params=_cparams("parallel", "parallel"),
        name="fnet_front",
    )(u.reshape(b, n1, n2, B_WIDTH), _group_mean_matrix(), gain, ch, cl, sh, sl, m1h, m1l)

    kc = FFT_KC
    zspec = pl.BlockSpec((1, kc, n2, B_WIDTH), lambda i, j: (i, j, 0, 0))
    tspec = pl.BlockSpec((kc, n2, n2), lambda i, j: (j, 0, 0))
    out = pl.pallas_call(
        functools.partial(_fnet_back_kernel, kc=kc, scale=1.0 / math.sqrt(64.0 * n)),
        grid=(b, n1 // kc),
        in_specs=[zspec, zspec, tspec, tspec, tspec, tspec,
                  pl.BlockSpec((B_WIDTH, B_WIDTH), lambda i, j: (0, 0))],
        out_specs=pl.BlockSpec((1, n2, kc, B_WIDTH), lambda i, j: (i, 0, j, 0)),
        out_shape=jax.ShapeDtypeStruct((b, n2, n1, B_WIDTH), BF16),
        scratch_shapes=[pltpu.VMEM((2, n2 * kc, 128), F32)],
        compiler_params=_cparams("parallel", "parallel"),
        name="fnet_back",
    )(zr, zi, tch, tcl, tsh, tsl, wbd)
    return out.reshape(b, n, B_WIDTH)


def _mem_kv_kernel(mem_ref, wk_ref, wvt_ref, k_ref, vt_ref):
    m = mem_ref[0].astype(BF16)
    k_ref[0] = _dot(m, wk_ref[...]).astype(BF16)
    vt_ref[0] = _dot_nt(wvt_ref[...], m).astype(BF16)


def _mem_kv(mem, w_kv):
    b, m, _ = mem.shape
    wk = w_kv[:, :D_MODEL].astype(BF16)
    wvt = w_kv[:, D_MODEL:].T.astype(BF16)
    wspec = pl.BlockSpec((D_MODEL, D_MODEL), lambda i: (0, 0))
    return pl.pallas_call(
        _mem_kv_kernel,
        grid=(b,),
        in_specs=[pl.BlockSpec((1, m, D_MODEL), lambda i: (i, 0, 0)), wspec, wspec],
        out_specs=[pl.BlockSpec((1, m, D_MODEL), lambda i: (i, 0, 0)),
                   pl.BlockSpec((1, D_MODEL, m), lambda i: (i, 0, 0))],
        out_shape=[jax.ShapeDtypeStruct((b, m, D_MODEL), BF16),
                   jax.ShapeDtypeStruct((b, D_MODEL, m), BF16)],
        compiler_params=_cparams("parallel"),
        name="mem_kv",
    )(mem, wk, wvt)


HALVES = (slice(0, TM // 2), slice(TM // 2, TM))


def _tail_after_mixer(hs, x_ref, g0_ref, b0_ref, wqt_ref, k_ref, vt_ref, wo_ref, g1_ref, b1_ref, o_ref):
    halves = HALVES
    x1s, qts = [], []
    for rows, h in zip(halves, hs):
        x1 = _layer_norm(DN_ALPHA * x_ref[0, rows, :] + h, g0_ref[...], b0_ref[...])
        x1s.append(x1)
        qts.append(_dot_nt(wqt_ref[...], x1.astype(BF16)).astype(BF16))
    qt = jnp.concatenate(qts, axis=1)
    heads = [slice(h * XA_HEAD_DIM, (h + 1) * XA_HEAD_DIM) for h in range(XA_HEADS)]
    scores = [_dot(k_ref[0, :, hd], qt[hd]) for hd in heads]
    ones = jnp.ones((16, k_ref.shape[1]), BF16)
    outs = []
    for hd, st in zip(heads, scores):
        p = jnp.exp2(st - jnp.max(st, axis=0, keepdims=True)).astype(BF16)
        ot = _dot(jnp.concatenate([vt_ref[0, hd, :], ones], axis=0), p)
        outs.append((ot[:XA_HEAD_DIM] / ot[XA_HEAD_DIM:XA_HEAD_DIM + 1]).astype(BF16))
    ot_all = jnp.concatenate(outs, axis=0)
    hs = [_dot_tn(ot_all[:, rows], wo_ref[...]) for rows in halves]
    for rows, x1, h in zip(halves, x1s, hs):
        o_ref[0, rows, :] = _layer_norm(DN_ALPHA * x1 + h, g1_ref[...], b1_ref[...])


def _tail_ab_kernel(o1_ref, o4_ref, o16_ref, l1_ref, l4_ref, l16_ref, expand_ref, c_ref, wa_ref, wc_ref,
                    x_ref, g0_ref, b0_ref, wqt_ref, k_ref, vt_ref, wo_ref, g1_ref, b1_ref, o_ref, obuf, lbuf):
    o_a = _mixture_tile(o1_ref, o4_ref, o16_ref, l1_ref, l4_ref, l16_ref, expand_ref, obuf, lbuf)
    hs = [_dot(o_a[rows], wa_ref[...]) + _dot(c_ref[0, rows, :], wc_ref[...]) for rows in HALVES]
    _tail_after_mixer(hs, x_ref, g0_ref, b0_ref, wqt_ref, k_ref, vt_ref, wo_ref, g1_ref, b1_ref, o_ref)


def _tail_cd_kernel(at_ref, up_ref, uc_ref, un_ref, pw_ref, ps_ref, wa_ref, wc_ref,
                    x_ref, g0_ref, b0_ref, wqt_ref, k_ref, vt_ref, wo_ref, g1_ref, b1_ref, o_ref, pool_buf, *, n):
    o_d = _pool_tile(up_ref, uc_ref, un_ref, pw_ref, ps_ref, pool_buf, tm=TM, n=n)
    hs = [_dot_tn(at_ref[0, :, rows], wa_ref[...]) + _dot(o_d[rows], wc_ref[...]) for rows in HALVES]
    _tail_after_mixer(hs, x_ref, g0_ref, b0_ref, wqt_ref, k_ref, vt_ref, wo_ref, g1_ref, b1_ref, o_ref)


def _tail_specs(x, mem):
    m = mem.shape[1]
    xspec = pl.BlockSpec((1, TM, D_MODEL), lambda i, j: (i, j, 0))
    wspec = pl.BlockSpec((D_MODEL, D_MODEL), lambda i, j: (0, 0))
    vec = pl.BlockSpec((1, D_MODEL), lambda i, j: (0, 0))
    return [xspec, vec, vec, wspec, pl.BlockSpec((1, m, D_MODEL), lambda i, j: (i, 0, 0)),
            pl.BlockSpec((1, D_MODEL, m), lambda i, j: (i, 0, 0)), wspec, vec, vec], xspec


def _tail_operands(x, g0, b0, mem, w_q, w_kv, w_o, g1, b1):
    k, vt = _mem_kv(mem, w_kv)
    wqt = (w_q * (XA_HEAD_DIM ** -0.5 * LOG2E)).T.astype(BF16)
    row = lambda v: v.reshape(1, D_MODEL)
    return x, row(g0), row(b0), wqt, k, vt, w_o.astype(BF16), row(g1), row(b1)


def _tail_ab(os_, ls_, o_b, w_out, x, g0, b0, mem, w_q, w_kv, w_o, g1, b1):
    b, n, _ = x.shape
    dils = [d for _, d in A_PATTERNS]
    expand = jnp.asarray(np.arange(128)[:, None] == np.arange(A_WIDTH)[None, :] // HEAD_DIM, BF16)
    wa = w_out[:A_WIDTH].astype(BF16)
    wc = w_out[A_WIDTH:].astype(BF16)
    tail_specs, xspec = _tail_specs(x, mem)
    return pl.pallas_call(
        _tail_ab_kernel,
        grid=(b, n // TM),
        in_specs=[pl.BlockSpec((1, TM // d, d * A_WIDTH), lambda i, j: (i, j, 0)) for d in dils]
        + [pl.BlockSpec((1, TM // d, d * 128), lambda i, j: (i, j, 0)) for d in dils]
        + [pl.BlockSpec((128, A_WIDTH), lambda i, j: (0, 0)),
           pl.BlockSpec((1, TM, B_WIDTH), lambda i, j: (i, j, 0)),
           pl.BlockSpec(wa.shape, lambda i, j: (0, 0)), pl.BlockSpec(wc.shape, lambda i, j: (0, 0))]
        + tail_specs,
        out_specs=xspec,
        out_shape=jax.ShapeDtypeStruct((b, n, D_MODEL), F32),
        scratch_shapes=[pltpu.VMEM((2, A_WIDTH // 128, TM, 128), F32), pltpu.VMEM((2, TM, 128), F32)],
        compiler_params=_cparams("parallel", "parallel"),
        name="tail_ab",
    )(*os_, *ls_, expand, o_b, wa, wc, *_tail_operands(x, g0, b0, mem, w_q, w_kv, w_o, g1, b1))


def _tail_cd(o_ct, u, pool_w, pool_scale, w_out, x, g0, b0, mem, w_q, w_kv, w_o, g1, b1):
    b, n, _ = x.shape
    per = TM // POOL_HALO
    last = n // POOL_HALO - 1
    pool_wbd = jax.scipy.linalg.block_diag(*[pool_w[g] for g in range(4)]).astype(BF16)
    wa = w_out[:C_WIDTH].astype(BF16)
    wc = w_out[C_WIDTH:].astype(BF16)
    tail_specs, xspec = _tail_specs(x, mem)
    return pl.pallas_call(
        functools.partial(_tail_cd_kernel, n=n),
        grid=(b, n // TM),
        in_specs=[pl.BlockSpec((1, C_WIDTH, TM), lambda i, j: (i, 0, j)),
                  pl.BlockSpec((1, POOL_HALO, D_WIDTH), lambda i, j: (i, jnp.maximum(j * per - 1, 0), 0)),
                  pl.BlockSpec((1, TM, D_WIDTH), lambda i, j: (i, j, 0)),
                  pl.BlockSpec((1, POOL_HALO, D_WIDTH), lambda i, j: (i, jnp.minimum((j + 1) * per, last), 0)),
                  pl.BlockSpec((D_WIDTH, D_WIDTH), lambda i, j: (0, 0)),
                  pl.BlockSpec((1, D_WIDTH), lambda i, j: (0, 0)),
                  pl.BlockSpec(wa.shape, lambda i, j: (0, 0)), pl.BlockSpec(wc.shape, lambda i, j: (0, 0))]
        + tail_specs,
        out_specs=xspec,
        out_shape=jax.ShapeDtypeStruct((b, n, D_MODEL), F32),
        scratch_shapes=[pltpu.VMEM((TM + 2 * POOL_HALO, D_WIDTH), F32)],
        compiler_params=_cparams("parallel", "parallel"),
        name="tail_cd",
    )(o_ct, u, u, u, pool_wbd, pool_scale.reshape(1, D_WIDTH), wa, wc,
      *_tail_operands(x, g0, b0, mem, w_q, w_kv, w_o, g1, b1))


def _swiglu_kernel(x_ref, win_ref, wo_ref, g_ref, b_ref, o_ref):
    halves = [slice(0, TM // 2), slice(TM // 2, TM)]
    gate_up = []
    for rows in halves:
        xb = x_ref[0, rows, :].astype(BF16)
        gate_up.append((_dot(xb, win_ref[:, :FFN_HIDDEN]), _dot(xb, win_ref[:, FFN_HIDDEN:])))
    parts = []
    for gate, up in gate_up:
        hid = (gate * (1.0 / (1.0 + jnp.exp(-gate))) * up).astype(BF16)
        parts.append(_dot(hid, wo_ref[...]))
    for rows, part in zip(halves, parts):
        o_ref[0, rows, :] = _layer_norm(DN_ALPHA * x_ref[0, rows, :] + part, g_ref[...], b_ref[...])


def _swiglu(x, w_in, w_out, g, bias):
    b, n, _ = x.shape
    win = w_in.astype(BF16)
    wout = w_out.astype(BF16)
    xspec = pl.BlockSpec((1, TM, D_MODEL), lambda i, t: (i, t, 0))
    vec = pl.BlockSpec((1, D_MODEL), lambda i, t: (0, 0))
    resident = pl.Buffered(1)
    return pl.pallas_call(
        _swiglu_kernel,
        grid=(b, n // TM),
        in_specs=[xspec,
                  pl.BlockSpec(win.shape, lambda i, t: (0, 0), pipeline_mode=resident),
                  pl.BlockSpec(wout.shape, lambda i, t: (0, 0), pipeline_mode=resident), vec, vec],
        out_specs=xspec,
        out_shape=jax.ShapeDtypeStruct((b, n, D_MODEL), F32),
        compiler_params=_cparams("parallel", "parallel"),
        name="swiglu_ln",
    )(x, win, wout, g.reshape(1, D_MODEL), bias.reshape(1, D_MODEL))


N_QK_HEADS = C_Q_HEADS + C_KV_HEADS
QK_ROWS = N_QK_HEADS * HEAD_DIM


def _proj_cd_kernel(x_ref, wt_ref, wu_ref, gain_ref, cos_ref, sin_ref,
                    qt_ref, k_ref, vt_ref, u_ref, kn_ref, *, tm):
    xb = x_ref[0].astype(BF16)
    u_ref[0] = _dot(xb, wu_ref[...])
    zt = _dot_nt(wt_ref[...], xb)
    z = zt[:QK_ROWS].reshape(N_QK_HEADS, HEAD_DIM, tm)
    ssq = jnp.sum(z * z, axis=1, keepdims=True)
    zn = z * lax.rsqrt(ssq * (1.0 / HEAD_DIM) + RMS_EPS) * gain_ref[...]
    half = HEAD_DIM // 2
    x1 = zn[:, :half]
    x2 = zn[:, half:]
    c = cos_ref[...][None]
    s = sin_ref[...][None]
    rot = jnp.concatenate([x1 * c - x2 * s, x1 * s + x2 * c], axis=1).reshape(QK_ROWS, tm)
    qt_ref[0] = rot[:C_WIDTH].astype(BF16)
    kb = rot[C_WIDTH:].astype(BF16)
    k_ref[0] = kb.astype(F32).T.astype(BF16)
    kf = kb.astype(F32).reshape(C_KV_HEADS, HEAD_DIM, tm)
    kn_ref[0] = jnp.sum(kf * kf, axis=1, keepdims=True)
    vt = zt[QK_ROWS:].astype(BF16)
    for c0 in range(tm // GQA_TKC):
        vt_ref[0, c0] = vt[:, c0 * GQA_TKC:(c0 + 1) * GQA_TKC]


def _rope_tables_t(n):
    rows = n // GRID_W
    row_id = jnp.broadcast_to(jnp.arange(rows)[:, None], (rows, GRID_W)).reshape(n)
    col_id = jnp.broadcast_to(jnp.arange(GRID_W)[None, :], (rows, GRID_W)).reshape(n)
    axis_dim = HEAD_DIM // 2
    freqs = ROPE_THETA ** (-jnp.arange(0, axis_dim, 2, dtype=F32) / axis_dim)
    ang = jnp.concatenate([row_id[:, None] * freqs, col_id[:, None] * freqs], axis=-1)
    return jnp.cos(ang).T, jnp.sin(ang).T


def _proj_cd(x, w_in, q_norm, k_norm):
    b, n, _ = x.shape
    tm = TM
    half = HEAD_DIM // 2
    wqk = w_in[:, :QK_ROWS].reshape(D_MODEL, N_QK_HEADS, half, 2)
    wqk = wqk.transpose(1, 3, 2, 0).reshape(QK_ROWS, D_MODEL)
    wt = jnp.concatenate([wqk, w_in[:, QK_ROWS:QK_ROWS + C_KV_WIDTH].T], axis=0).astype(BF16)
    wu = w_in[:, QK_ROWS + C_KV_WIDTH:].astype(BF16)
    qg = q_norm.reshape(half, 2).T.reshape(HEAD_DIM) * (HEAD_DIM ** -0.5 * LOG2E)
    kg = k_norm.reshape(half, 2).T.reshape(HEAD_DIM)
    gain = jnp.concatenate([jnp.tile(qg[None], (C_Q_HEADS, 1)), jnp.tile(kg[None], (C_KV_HEADS, 1))])
    gain = gain.reshape(N_QK_HEADS, HEAD_DIM, 1).astype(F32)
    cos_t, sin_t = _rope_tables_t(n)
    nc = n // GQA_TKC
    return pl.pallas_call(
        functools.partial(_proj_cd_kernel, tm=tm),
        grid=(b, n // tm),
        in_specs=[pl.BlockSpec((1, tm, D_MODEL), lambda i, j: (i, j, 0)),
                  pl.BlockSpec(wt.shape, lambda i, j: (0, 0)),
                  pl.BlockSpec(wu.shape, lambda i, j: (0, 0)),
                  pl.BlockSpec(gain.shape, lambda i, j: (0, 0, 0)),
                  pl.BlockSpec((HEAD_DIM // 2, tm), lambda i, j: (0, j)),
                  pl.BlockSpec((HEAD_DIM // 2, tm), lambda i, j: (0, j))],
        out_specs=[pl.BlockSpec((1, C_WIDTH, tm), lambda i, j: (i, 0, j)),
                   pl.BlockSpec((1, tm, C_KV_WIDTH), lambda i, j: (i, j, 0)),
                   pl.BlockSpec((1, tm // GQA_TKC, C_KV_WIDTH, GQA_TKC), lambda i, j: (i, j, 0, 0)),
                   pl.BlockSpec((1, tm, D_WIDTH), lambda i, j: (i, j, 0)),
                   pl.BlockSpec((1, C_KV_HEADS, 1, tm), lambda i, j: (i, 0, 0, j))],
        out_shape=[jax.ShapeDtypeStruct((b, C_WIDTH, n), BF16),
                   jax.ShapeDtypeStruct((b, n, C_KV_WIDTH), BF16),
                   jax.ShapeDtypeStruct((b, nc, C_KV_WIDTH, GQA_TKC), BF16),
                   jax.ShapeDtypeStruct((b, n, D_WIDTH), F32),
                   jax.ShapeDtypeStruct((b, C_KV_HEADS, 1, n), F32)],
        compiler_params=_cparams("parallel", "parallel"),
        name="proj_cd",
    )(x, wt, wu, gain, cos_t, sin_t)


def _gqa_kernel(qt_ref, k_ref, vt_ref, kn_ref, ot_ref, m_ref, l_ref, acc_ref, qpad_ref, *, tq, n):
    g = pl.program_id(1)
    row_half = lax.broadcasted_iota(jnp.int32, (128, 1), 0) // HEAD_DIM
    mine = row_half == (g % 2)
    k_max2 = jnp.max(kn_ref[0, 0], axis=1, keepdims=True)
    bound_max = jnp.zeros((1, 1), F32)
    for j in range(C_REP):
        qj = qt_ref[0, j * HEAD_DIM:(j + 1) * HEAD_DIM, :]
        q2 = jnp.concatenate([qj, qj], axis=0)
        qpad_ref[j] = jnp.where(mine, q2, jnp.zeros_like(q2))
        qf = qj.astype(F32)
        bound = jnp.sqrt(jnp.sum(qf * qf, axis=0, keepdims=True) * k_max2) * GQA_BOUND_SLACK
        m_ref[j] = bound
        bound_max = jnp.maximum(bound_max, jnp.max(bound, axis=1, keepdims=True))
    fixed_shift = bound_max[0, 0] <= GQA_BOUND_LIMIT
    acc_ref[...] = jnp.zeros(acc_ref.shape, F32)
    ones = jnp.ones((GQA_VROWS - HEAD_DIM, GQA_TKC), BF16)

    def body(c, carry, *, online):
        kchs, vchs = [], []
        for u in range(GQA_UNROLL):
            cc = c * GQA_UNROLL + u
            start = pl.multiple_of(cc * GQA_TKC, GQA_TKC)
            kchs.append(k_ref[0, pl.ds(start, GQA_TKC), :])
            vchs.append(jnp.concatenate([vt_ref[0, cc], ones], axis=0))
        pieces = [(u, j, slice(s * GQA_TW, (s + 1) * GQA_TW))
                  for u in range(GQA_UNROLL) for j in range(C_REP) for s in range(tq // GQA_TW)]

        def scores(i):
            u, j, cols = pieces[i]
            return _dot(kchs[u], qpad_ref[j, :, cols])

        pending = [scores(i) for i in range(GQA_AHEAD)]
        for i, (u, j, cols) in enumerate(pieces):
            st = pending.pop(0)
            if i + GQA_AHEAD < len(pieces):
                pending.append(scores(i + GQA_AHEAD))
            m_old = m_ref[j, :, cols]
            if not online:
                p = jnp.exp2(st - m_old)
                l_ref[j, :, cols] = l_ref[j, :, cols] + jnp.sum(p.reshape(GQA_TKC // 8, 8, GQA_TW), axis=0)
                acc_ref[j, :HEAD_DIM, cols] = acc_ref[j, :HEAD_DIM, cols] + _dot(
                    vchs[u][:HEAD_DIM], p.astype(BF16))
                continue
            m_new = jnp.maximum(m_old, jnp.max(st, axis=0, keepdims=True))
            alpha = jnp.exp2(m_old - m_new)
            p = jnp.exp2(st - m_new)
            acc_ref[j, :, cols] = alpha * acc_ref[j, :, cols] + _dot(vchs[u], p.astype(BF16))
            m_ref[j, :, cols] = m_new
        return carry

    trips = n // (GQA_TKC * GQA_UNROLL)

    @pl.when(fixed_shift)
    def _():
        l_ref[...] = jnp.zeros(l_ref.shape, F32)
        lax.fori_loop(0, trips, functools.partial(body, online=False), 0)
        for j in range(C_REP):
            acc_ref[j, HEAD_DIM:HEAD_DIM + 1, :] = jnp.sum(l_ref[j], axis=0, keepdims=True)

    @pl.when(jnp.logical_not(fixed_shift))
    def _():
        m_ref[...] = jnp.full(m_ref.shape, NEG_INF, F32)
        lax.fori_loop(0, trips, functools.partial(body, online=True), 0)

    for j in range(C_REP):
        l = acc_ref[j, HEAD_DIM:HEAD_DIM + 1, :]
        ot_ref[0, j * HEAD_DIM:(j + 1) * HEAD_DIM, :] = (acc_ref[j, :HEAD_DIM, :] / l).astype(BF16)


def _gqa(qt, k, vt, kn):
    b, _, n = qt.shape
    tq = min(GQA_TQ, n)
    nc = n // GQA_TKC
    rows = C_REP * HEAD_DIM
    return pl.pallas_call(
        functools.partial(_gqa_kernel, tq=tq, n=n),
        grid=(b, C_KV_HEADS, n // tq),
        in_specs=[pl.BlockSpec((1, rows, tq), lambda i, g, t: (i, g, t)),
                  pl.BlockSpec((1, n, 128), lambda i, g, t: (i, 0, g // 2)),
                  pl.BlockSpec((1, nc, HEAD_DIM, GQA_TKC), lambda i, g, t: (i, 0, g, 0)),
                  pl.BlockSpec((1, 1, 1, n), lambda i, g, t: (i, g, 0, 0))],
        out_specs=pl.BlockSpec((1, rows, tq), lambda i, g, t: (i, g, t)),
        out_shape=jax.ShapeDtypeStruct((b, C_WIDTH, n), BF16),
        scratch_shapes=[pltpu.VMEM((C_REP, 1, tq), F32), pltpu.VMEM((C_REP, 8, tq), F32),
                        pltpu.VMEM((C_REP, GQA_VROWS, tq), F32), pltpu.VMEM((C_REP, 128, tq), BF16)],
        compiler_params=_cparams("parallel", "parallel", "parallel"),
        name="gqa_flash",
    )(qt, k, vt, kn)


def _pool_tile(up_ref, uc_ref, un_ref, w_ref, scale_ref, buf, *, tm, n):
    i = pl.program_id(1)
    cur = uc_ref[0]
    buf[0:POOL_HALO, :] = jnp.where(i > 0, up_ref[0], 0.0)
    buf[POOL_HALO:POOL_HALO + tm, :] = cur
    buf[POOL_HALO + tm:, :] = jnp.where(i < pl.num_programs(1) - 1, un_ref[0], 0.0)
    lane_group = lax.broadcasted_iota(jnp.int32, (1, D_WIDTH), 1) // 64
    half_w = jnp.left_shift(1, lane_group)
    acc = jnp.zeros((tm, D_WIDTH), F32)
    for j in range(-POOL_HALO, POOL_HALO):
        inside = (j >= -half_w) & (j < half_w)
        acc = acc + jnp.where(inside, buf[POOL_HALO + j:POOL_HALO + j + tm, :], 0.0)
    t = i * tm + lax.broadcasted_iota(jnp.int32, (tm, 1), 0)
    cnt = jnp.minimum(t + half_w, n) - jnp.maximum(t - half_w, 0)
    mixed = (acc / cnt.astype(F32) - cur).astype(BF16)
    return (_dot(mixed, w_ref[...]) * scale_ref[...]).astype(BF16)


def _trunk(x, mem, rel_bias, ab_w_in, ab_fnet_g, ab_fnet_w, ab_w_out,
           cd_w_in, cd_q_norm, cd_k_norm, cd_pool_w, cd_pool_scale, cd_w_out,
           xa_w_q, xa_w_kv, xa_w_o, ffn_w_in, ffn_w_out, ln_g, ln_b):
    def tail_args(layer):
        return (ln_g[layer, 0], ln_b[layer, 0], mem, xa_w_q[layer], xa_w_kv[layer], xa_w_o[layer],
                ln_g[layer, 1], ln_b[layer, 1])

    for layer in range(DEPTH):
        i = layer // 2
        if layer % 2 == 0:
            w_in = ab_w_in[i]
            w_in = jnp.concatenate([w_in[:, :A_WIDTH] * (HEAD_DIM ** -0.5 * LOG2E), w_in[:, A_WIDTH:]], axis=1)
            *views, u = _proj_ab(x, w_in.astype(BF16))
            outs = [_dilated(view, rel_bias, d) for view, (_, d) in zip(views, A_PATTERNS)]
            o_b = _fnet(u, ab_fnet_g[i], ab_fnet_w[i])
            x = _tail_ab([o for o, _ in outs], [l for _, l in outs], o_b, ab_w_out[i], x, *tail_args(layer))
        else:
            qt, k, vt, u, kn = _proj_cd(x, cd_w_in[i], cd_q_norm[i], cd_k_norm[i])
            o_c = _gqa(qt, k, vt, kn)
            x = _tail_cd(o_c, u, cd_pool_w[i], cd_pool_scale[i], cd_w_out[i], x, *tail_args(layer))
        x = _swiglu(x, ffn_w_in[layer], ffn_w_out[layer], ln_g[layer, 2], ln_b[layer, 2])
    return x


def kernel(x_prompt, x_sample, mem_prompt, mem_sample, rel_bias, ab_w_in, ab_fnet_g, ab_fnet_w, ab_w_out, cd_w_in, cd_q_norm, cd_k_norm, cd_pool_w, cd_pool_scale, cd_w_out, xa_w_q, xa_w_kv, xa_w_o, ffn_w_in, ffn_w_out, ln_g, ln_b):
    params = (rel_bias, ab_w_in, ab_fnet_g, ab_fnet_w, ab_w_out,
              cd_w_in, cd_q_norm, cd_k_norm, cd_pool_w, cd_pool_scale, cd_w_out,
              xa_w_q, xa_w_kv, xa_w_o, ffn_w_in, ffn_w_out, ln_g, ln_b)
    return (_trunk(x_prompt, mem_prompt, *params), _trunk(x_sample, mem_sample, *params))
```

```python
import functools
import math

import numpy as np
import jax
import jax.numpy as jnp
from jax import lax
from jax.experimental import pallas as pl
from jax.experimental.pallas import tpu as pltpu

F32 = jnp.float32
BF16 = jnp.bfloat16

D_MODEL = 1024
HEAD_DIM = 64
GRID_W = 64
LN_EPS = 1e-5
RMS_EPS = 1e-6
NEG_INF = -1e30
DEPTH = 2
A_HEADS = 12
A_WIDTH = A_HEADS * HEAD_DIM
A_PATTERNS = ((128, 1), (512, 4), (2048, 16))
A_HALF = 64
QKV_WIDTH = 3 * A_WIDTH
N_BUCKETS = 32
REL_MAX_DIST = 1024
B_WIDTH = 256
C_Q_HEADS = 12
C_KV_HEADS = 4
C_REP = C_Q_HEADS // C_KV_HEADS
C_WIDTH = C_Q_HEADS * HEAD_DIM
C_KV_WIDTH = C_KV_HEADS * HEAD_DIM
ROPE_THETA = 10000.0
POOL_WINDOWS = (2, 4, 8, 16)
POOL_HALO = 8
D_WIDTH = 256
XA_HEADS = 4
XA_HEAD_DIM = D_MODEL // XA_HEADS
FFN_HIDDEN = 2816
DN_ALPHA = (2 * DEPTH) ** 0.25
LOG2E = 1.4426950408889634

LANES = 128
MXU_TILE = 256
BF16_ROWS = 16
VMEM_LIMIT = 56 * 1024 * 1024
TM = 512
FFT_N2 = 128
FFT_ROWS = 1024
FFT_KC = 16
FFT_AHEAD = 3
DIL_AHEAD = 3
DIL_TQ = 512
DIL_SQ = MXU_TILE // 2
DIL_SK = DIL_SQ + 2 * A_HALF
DIL_PIECES = 48
GQA_TQ = 2048
GQA_TW = MXU_TILE
GQA_TKC = 256
GQA_VROWS = HEAD_DIM + BF16_ROWS
GQA_UNROLL = 4
GQA_AHEAD = 5
GQA_BOUND_SLACK = 1.0 + 2.0 ** -10
GQA_BOUND_LIMIT = 60.0


def _cparams(*sem):
    return pltpu.CompilerParams(dimension_semantics=sem, vmem_limit_bytes=VMEM_LIMIT)


def _dot(a, b):
    return jnp.dot(a, b, preferred_element_type=F32)


def _dot_nt(a, b):
    return lax.dot_general(a, b, (((1,), (1,)), ((), ())), preferred_element_type=F32)


def _dot_tn(a, b):
    return lax.dot_general(a, b, (((0,), (0,)), ((), ())), preferred_element_type=F32)


def _split(x):
    hi = x.astype(BF16)
    lo = (x - hi.astype(F32)).astype(BF16)
    return hi, lo


def _dot3(ah, al, bh, bl):
    return _dot(ah, bh) + _dot(al, bh) + _dot(ah, bl)


def _np_split(x):
    x = np.asarray(x, np.float32)
    hi = x.astype(BF16)
    lo = (x - hi.astype(np.float32)).astype(BF16)
    return jnp.asarray(hi), jnp.asarray(lo)


def _layer_norm(h, g, b):
    mu = jnp.mean(h, axis=-1, keepdims=True)
    xc = h - mu
    var = jnp.mean(xc * xc, axis=-1, keepdims=True)
    return xc * lax.rsqrt(var + LN_EPS) * g + b


def _proj_ab_kernel(x_ref, w_ref, qkv1_ref, qkv4_ref, qkv16_ref, u_ref, zbuf):
    xb = x_ref[0].astype(BF16)
    chunks = list(range(0, QKV_WIDTH, MXU_TILE))

    for c in chunks:
        z = _dot(xb, w_ref[:, c:c + MXU_TILE])
        zbuf[c // LANES] = z[:, :LANES]
        zbuf[c // LANES + 1] = z[:, LANES:]
        qkv1_ref[0, :, c:c + MXU_TILE] = z.astype(BF16)
    u_ref[0] = _dot(xb, w_ref[:, QKV_WIDTH:])
    for (_, d), ref in zip(A_PATTERNS[1:], (qkv4_ref, qkv16_ref)):
        rows = TM // d
        for r in range(d):
            for ct in range(QKV_WIDTH // LANES):
                col = r * QKV_WIDTH + ct * LANES
                ref[0, :, col:col + LANES] = zbuf[ct, pl.ds(r, rows, stride=d), :].astype(BF16)


def _proj_ab(x, w):
    b, n, _ = x.shape
    dils = [d for _, d in A_PATTERNS]
    return pl.pallas_call(
        _proj_ab_kernel,
        grid=(b, n // TM),
        in_specs=[pl.BlockSpec((1, TM, D_MODEL), lambda i, j: (i, j, 0)),
                  pl.BlockSpec(w.shape, lambda i, j: (0, 0))],
        out_specs=[pl.BlockSpec((1, TM // d, d * QKV_WIDTH), lambda i, j: (i, j, 0)) for d in dils]
        + [pl.BlockSpec((1, TM, B_WIDTH), lambda i, j: (i, j, 0))],
        out_shape=[jax.ShapeDtypeStruct((b, n // d, d * QKV_WIDTH), BF16) for d in dils]
        + [jax.ShapeDtypeStruct((b, n, B_WIDTH), F32)],
        scratch_shapes=[pltpu.VMEM((QKV_WIDTH // LANES, TM, LANES), F32)],
        compiler_params=_cparams("parallel", "parallel"),
        name="proj_ab",
    )(x, w)


def _t5_bucket_np(rel):
    nb = N_BUCKETS // 2
    max_exact = nb // 2
    ret = np.where(rel > 0, nb, 0)
    n = np.abs(rel)
    nf = np.maximum(n, 1).astype(np.float32)
    large = max_exact + (np.log(nf / max_exact) / math.log(REL_MAX_DIST / max_exact)
                         * (nb - max_exact)).astype(np.int32)
    large = np.minimum(large, nb - 1)
    return ret + np.where(n < max_exact, n, large)


def _band_bias(rel_bias, dilation):
    tq, tk = DIL_SQ, DIL_SK
    band = 2 * A_HALF + 1
    bucket = _t5_bucket_np((np.arange(band) - A_HALF) * dilation)
    row = (rel_bias[jnp.asarray(bucket)].T * LOG2E).astype(F32)
    row = jnp.concatenate([row, jnp.full((A_HEADS, tk + 1 - band), NEG_INF, F32)], axis=1)
    bias = jnp.tile(row, (1, tq))[:, :tq * tk].reshape(A_HEADS, tq, tk)
    bias = bias.transpose(0, 2, 1)
    bias = bias.reshape(A_HEADS // 2, 2, tk, tq).transpose(0, 2, 1, 3).reshape(A_HEADS // 2, tk, 2 * tq)
    key = np.arange(tk)[None, :, None]
    before = jnp.asarray(key < A_HALF)
    after = jnp.asarray(key >= A_HALF + tq)
    first = jnp.where(before, NEG_INF, bias)
    return jnp.stack([bias, first, jnp.where(after, NEG_INF, bias), jnp.where(after, NEG_INF, first)])


def _dilated_kernel(prev_ref, cur_ref, next_ref, bias_ref, o_ref, lse_ref, kbuf, vbuf, *, tq, n_res):
    for rr in range(n_res):
        for buf, c0 in ((kbuf, rr * QKV_WIDTH + A_WIDTH), (vbuf, rr * QKV_WIDTH + 2 * A_WIDTH)):
            buf[rr, 0:A_HALF, :] = prev_ref[0, :, c0:c0 + A_WIDTH]
            buf[rr, A_HALF:A_HALF + tq, :] = cur_ref[0, :, c0:c0 + A_WIDTH]
            buf[rr, A_HALF + tq:, :] = next_ref[0, :, c0:c0 + A_WIDTH]

    i = pl.program_id(1)
    n_sub = tq // DIL_SQ
    row_low = lax.broadcasted_iota(jnp.int32, (LANES, 1), 0) < HEAD_DIM
    row16 = lax.broadcasted_iota(jnp.int32, (BF16_ROWS, 1), 0)
    ones = jnp.ones((BF16_ROWS, DIL_SK), BF16)
    qts, vts = {}, {}
    pieces = [(rr, pair, sub) for rr in range(n_res) for pair in range(A_HEADS // 2) for sub in range(n_sub)]

    def variant(sub):
        v = 0
        if sub == 0:
            v = v + jnp.where(i == 0, 1, 0)
        if sub == n_sub - 1:
            v = v + jnp.where(i == pl.num_programs(1) - 1, 2, 0)
        return v

    def scores(idx):
        rr, pair, sub = pieces[idx]
        cols = slice(pair * LANES, (pair + 1) * LANES)
        if (rr, pair) not in qts:
            qcols = slice(rr * QKV_WIDTH + pair * LANES, rr * QKV_WIDTH + (pair + 1) * LANES)
            qts[rr, pair] = cur_ref[0, :, qcols].astype(F32).T
        qsub = qts[rr, pair][:, sub * DIL_SQ:(sub + 1) * DIL_SQ]
        rhs = jnp.concatenate([jnp.where(row_low, qsub, 0.0), jnp.where(row_low, 0.0, qsub)], axis=1)
        keys = kbuf[rr, sub * DIL_SQ:sub * DIL_SQ + DIL_SK, cols]
        return _dot(keys, rhs.astype(BF16)) + bias_ref[variant(sub), pair]

    pending = [scores(idx) for idx in range(DIL_AHEAD)]
    lse_t = {(rr, sub): jnp.zeros((BF16_ROWS, DIL_SQ), F32) for rr in range(n_res) for sub in range(n_sub)}
    for idx, (rr, pair, sub) in enumerate(pieces):
        st = pending.pop(0)
        if idx + DIL_AHEAD < len(pieces):
            pending.append(scores(idx + DIL_AHEAD))
        cols = slice(pair * LANES, (pair + 1) * LANES)
        m = jnp.max(st, axis=0, keepdims=True)
        p = jnp.exp2(st - m).astype(BF16)
        if (rr, pair) not in vts:
            vts[rr, pair] = vbuf[rr, :, cols].astype(F32).T.astype(BF16)
        vaug = jnp.concatenate([vts[rr, pair][:, sub * DIL_SQ:sub * DIL_SQ + DIL_SK], ones], axis=0)
        ot = _dot(vaug, p)
        l = ot[LANES:LANES + 1]
        lse2 = m + jnp.log2(l)
        o_pair = jnp.concatenate([ot[:HEAD_DIM, :DIL_SQ] / l[:, :DIL_SQ],
                                  ot[HEAD_DIM:LANES, DIL_SQ:] / l[:, DIL_SQ:]], axis=0)
        ocols = slice(rr * A_WIDTH + pair * LANES, rr * A_WIDTH + (pair + 1) * LANES)
        o_ref[0, sub * DIL_SQ:(sub + 1) * DIL_SQ, ocols] = o_pair.T.astype(BF16)
        lse_t[rr, sub] = jnp.where(row16 == 2 * pair, lse2[:, :DIL_SQ],
                                   jnp.where(row16 == 2 * pair + 1, lse2[:, DIL_SQ:], lse_t[rr, sub]))
    for (rr, sub), rows16 in lse_t.items():
        full = jnp.concatenate([rows16, jnp.zeros((LANES - BF16_ROWS, DIL_SQ), F32)], axis=0)
        lse_ref[0, sub * DIL_SQ:(sub + 1) * DIL_SQ, rr * LANES:(rr + 1) * LANES] = full.T


def _dilated(view, rel_bias, dilation):
    b, seq, _ = view.shape
    tq = min(DIL_TQ, seq)
    tk = tq + 2 * A_HALF
    n_res = min(dilation, max(1, DIL_PIECES // (A_HEADS // 2 * (tq // DIL_SQ))))
    bias = _band_bias(rel_bias, dilation)
    per = tq // A_HALF
    last = seq // A_HALF - 1
    width = n_res * QKV_WIDTH

    o, lse = pl.pallas_call(
        functools.partial(_dilated_kernel, tq=tq, n_res=n_res),
        grid=(b, seq // tq, dilation // n_res),
        in_specs=[pl.BlockSpec((1, A_HALF, width), lambda bi, i, r: (bi, jnp.maximum(i * per - 1, 0), r)),
                  pl.BlockSpec((1, tq, width), lambda bi, i, r: (bi, i, r)),
                  pl.BlockSpec((1, A_HALF, width), lambda bi, i, r: (bi, jnp.minimum((i + 1) * per, last), r)),
                  pl.BlockSpec(bias.shape, lambda bi, i, r: (0, 0, 0, 0))],
        out_specs=[pl.BlockSpec((1, tq, n_res * A_WIDTH), lambda bi, i, r: (bi, i, r)),
                   pl.BlockSpec((1, tq, n_res * LANES), lambda bi, i, r: (bi, i, r))],
        out_shape=[jax.ShapeDtypeStruct((b, seq, dilation * A_WIDTH), BF16),
                   jax.ShapeDtypeStruct((b, seq, dilation * LANES), F32)],
        scratch_shapes=[pltpu.VMEM((n_res, tk, A_WIDTH), BF16), pltpu.VMEM((n_res, tk, A_WIDTH), BF16)],
        compiler_params=_cparams("parallel", "parallel", "parallel"),
        name=f"dilated_d{dilation}",
    )(view, view, view, bias)
    return o, lse


def _mixture_tile(o1_ref, o4_ref, o16_ref, l1_ref, l4_ref, l16_ref, expand_ref, obuf, lbuf):
    for idx, (d, o_ref, l_ref) in enumerate(((A_PATTERNS[1][1], o4_ref, l4_ref), (A_PATTERNS[2][1], o16_ref, l16_ref))):
        rows = TM // d
        for r in range(d):
            for ct in range(A_WIDTH // LANES):
                col = r * A_WIDTH + ct * LANES
                obuf[idx, ct, pl.ds(r, rows, stride=d), :] = o_ref[0, :, col:col + LANES].astype(F32)
            lbuf[idx, pl.ds(r, rows, stride=d), :] = l_ref[0, :, r * LANES:(r + 1) * LANES]
    ls = [l1_ref[0], lbuf[0], lbuf[1]]
    mx = jnp.maximum(jnp.maximum(ls[0], ls[1]), ls[2])
    es = [jnp.exp2(l - mx) for l in ls]
    inv = 1.0 / (es[0] + es[1] + es[2])
    ws = []
    for e in es:
        wh, wl = _split(e * inv)
        ws.append(_dot(wh, expand_ref[...]) + _dot(wl, expand_ref[...]))
    pairs = []
    for pair in range(A_HEADS // 2):
        cols = slice(pair * LANES, (pair + 1) * LANES)
        os_ = [o1_ref[0, :, cols].astype(F32), obuf[0, pair], obuf[1, pair]]
        acc = None
        for g in range(3):
            t = ws[g][:, cols] * os_[g]
            acc = t if acc is None else acc + t
        pairs.append(acc.astype(BF16))
    return jnp.concatenate(pairs, axis=1)


def _group_mean_matrix():
    g = np.kron(np.eye(4), np.full((64, 64), 1.0 / 64))
    return jnp.asarray(g, BF16)


def _fnet_front_kernel(u_ref, gm_ref, gain_ref, ch_ref, cl_ref, sh_ref, sl_ref, m1h_ref, m1l_ref,
                       zr_ref, zi_ref, ybuf, zbuf, *, n1, tcn):
    rows = n1 * tcn
    u = u_ref[0].reshape(rows, B_WIDTH)
    gm = gm_ref[...]
    uh, ul = _split(u)
    mean = _dot(uh, gm) + _dot(ul, gm)
    xc = u - mean
    qh, ql = _split(xc * xc)
    var = _dot(qh, gm) + _dot(ql, gm)
    un = xc * lax.rsqrt(var + LN_EPS) * gain_ref[...]
    nh, nl = _split(un)
    yr = _dot3(nh, nl, ch_ref[...], cl_ref[...])
    yi = -_dot3(nh, nl, sh_ref[...], sl_ref[...])
    for q, val in enumerate((yr[:, :LANES], yr[:, LANES:], yi[:, :LANES], yi[:, LANES:])):
        ybuf[q] = val
    def first_stage(n2):
        col = [ybuf[q, pl.ds(n2, n1, stride=tcn), :] for q in range(4)]
        y = jnp.concatenate([jnp.concatenate(col[:2], axis=1), jnp.concatenate(col[2:], axis=1)], axis=0)
        yh, yl = _split(y)
        return _dot3(m1h_ref[...], m1l_ref[...], yh, yl)

    pending = [first_stage(n2) for n2 in range(min(FFT_AHEAD, tcn))]
    for n2 in range(tcn):
        z = pending.pop(0)
        if n2 + FFT_AHEAD < tcn:
            pending.append(first_stage(n2 + FFT_AHEAD))
        for q, val in enumerate((z[:n1, :LANES], z[:n1, LANES:], z[n1:, :LANES], z[n1:, LANES:])):
            zbuf[q, pl.ds(n2, n1, stride=tcn), :] = val
    zr_ref[0] = jnp.concatenate([zbuf[0], zbuf[1]], axis=1).reshape(n1, tcn, B_WIDTH)
    zi_ref[0] = jnp.concatenate([zbuf[2], zbuf[3]], axis=1).reshape(n1, tcn, B_WIDTH)


def _fnet_back_kernel(zr_ref, zi_ref, tch_ref, tcl_ref, tsh_ref, tsl_ref, w_ref, o_ref, obuf, *, kc, scale):
    n2 = FFT_N2

    def second_stage(kk):
        rh, rl = _split(zr_ref[0, kk])
        ih, il = _split(zi_ref[0, kk])
        return _dot3(tch_ref[kk], tcl_ref[kk], rh, rl) + _dot3(tsh_ref[kk], tsl_ref[kk], ih, il)

    pending = [second_stage(kk) for kk in range(FFT_AHEAD)]
    for kk in range(kc):
        f = pending.pop(0)
        if kk + FFT_AHEAD < kc:
            pending.append(second_stage(kk + FFT_AHEAD))
        o = _dot((f * scale).astype(BF16), w_ref[...])
        obuf[0, pl.ds(kk, n2, stride=kc), :] = o[:, :LANES]
        obuf[1, pl.ds(kk, n2, stride=kc), :] = o[:, LANES:]
    out = jnp.concatenate([obuf[0], obuf[1]], axis=1).reshape(n2, kc, B_WIDTH)
    o_ref[0] = out.astype(BF16)


def _fnet_tables(n):
    n2 = FFT_N2
    n1 = n // n2
    c = np.arange(64)
    ang = 2 * np.pi * np.outer(c, c) / 64
    cbd = np.kron(np.eye(4), np.cos(ang))
    sbd = np.kron(np.eye(4), np.sin(ang))
    k1 = np.arange(n1)
    ang1 = 2 * np.pi * np.outer(k1, k1) / n1
    c1, s1 = np.cos(ang1), np.sin(ang1)
    m1 = np.block([[c1, s1], [-s1, c1]])
    k2 = np.arange(n2)
    npr = k1[:, None, None] + n1 * k2[None, :, None]
    prod = (npr * k2[None, None, :]) % n
    ang2 = 2 * np.pi * prod / n
    return (_np_split(cbd), _np_split(sbd), _np_split(m1), _np_split(np.cos(ang2)), _np_split(np.sin(ang2)))


def _fnet(u, fnet_g, fnet_w):
    b, n, _ = u.shape
    n2 = FFT_N2
    n1 = n // n2
    tcn = FFT_ROWS // n1
    (ch, cl), (sh, sl), (m1h, m1l), (tch, tcl), (tsh, tsl) = _fnet_tables(n)
    gain = fnet_g.reshape(1, B_WIDTH)
    wbd = jax.scipy.linalg.block_diag(*[fnet_w[g] for g in range(4)]).astype(BF16)

    mat = pl.BlockSpec((B_WIDTH, B_WIDTH), lambda i, j: (0, 0))
    m1spec = pl.BlockSpec((2 * n1, 2 * n1), lambda i, j: (0, 0))
    tile = pl.BlockSpec((1, n1, tcn, B_WIDTH), lambda i, j: (i, 0, j, 0))
    zr, zi = pl.pallas_call(
        functools.partial(_fnet_front_kernel, n1=n1, tcn=tcn),
        grid=(b, n2 // tcn),
        in_specs=[tile, mat, pl.BlockSpec((1, B_WIDTH), lambda i, j: (0, 0)), mat, mat, mat, mat, m1spec, m1spec],
        out_specs=[tile, tile],
        out_shape=[jax.ShapeDtypeStruct((b, n1, n2, B_WIDTH), F32)] * 2,
        scratch_shapes=[pltpu.VMEM((4, FFT_ROWS, LANES), F32), pltpu.VMEM((4, FFT_ROWS, LANES), F32)],
        compiler_params=_cparams("parallel", "parallel"),
        name="fnet_front",
    )(u.reshape(b, n1, n2, B_WIDTH), _group_mean_matrix(), gain, ch, cl, sh, sl, m1h, m1l)

    kc = FFT_KC
    zspec = pl.BlockSpec((1, kc, n2, B_WIDTH), lambda i, j: (i, j, 0, 0))
    tspec = pl.BlockSpec((kc, n2, n2), lambda i, j: (j, 0, 0))
    out = pl.pallas_call(
        functools.partial(_fnet_back_kernel, kc=kc, scale=1.0 / math.sqrt(64.0 * n)),
        grid=(b, n1 // kc),
        in_specs=[zspec, zspec, tspec, tspec, tspec, tspec,
                  pl.BlockSpec((B_WIDTH, B_WIDTH), lambda i, j: (0, 0))],
        out_specs=pl.BlockSpec((1, n2, kc, B_WIDTH), lambda i, j: (i, 0, j, 0)),
        out_shape=jax.ShapeDtypeStruct((b, n2, n1, B_WIDTH), BF16),
        scratch_shapes=[pltpu.VMEM((2, n2 * kc, LANES), F32)],
        compiler_params=_cparams("parallel", "parallel"),
        name="fnet_back",
    )(zr, zi, tch, tcl, tsh, tsl, wbd)
    return out.reshape(b, n, B_WIDTH)


def _mem_kv_kernel(mem_ref, wk_ref, wvt_ref, k_ref, vt_ref):
    m = mem_ref[0].astype(BF16)
    k_ref[0] = _dot(m, wk_ref[...]).astype(BF16)
    vt_ref[0] = _dot_nt(wvt_ref[...], m).astype(BF16)


def _mem_kv(mem, w_kv):
    b, m, _ = mem.shape
    wk = w_kv[:, :D_MODEL].astype(BF16)
    wvt = w_kv[:, D_MODEL:].T.astype(BF16)
    wspec = pl.BlockSpec((D_MODEL, D_MODEL), lambda i: (0, 0))
    return pl.pallas_call(
        _mem_kv_kernel,
        grid=(b,),
        in_specs=[pl.BlockSpec((1, m, D_MODEL), lambda i: (i, 0, 0)), wspec, wspec],
        out_specs=[pl.BlockSpec((1, m, D_MODEL), lambda i: (i, 0, 0)),
                   pl.BlockSpec((1, D_MODEL, m), lambda i: (i, 0, 0))],
        out_shape=[jax.ShapeDtypeStruct((b, m, D_MODEL), BF16),
                   jax.ShapeDtypeStruct((b, D_MODEL, m), BF16)],
        compiler_params=_cparams("parallel"),
        name="mem_kv",
    )(mem, wk, wvt)


HALVES = (slice(0, TM // 2), slice(TM // 2, TM))


def _tail_after_mixer(hs, x_ref, g0_ref, b0_ref, wqt_ref, k_ref, vt_ref, wo_ref, g1_ref, b1_ref, o_ref):
    halves = HALVES
    x1s, qts = [], []
    for rows, h in zip(halves, hs):
        x1 = _layer_norm(DN_ALPHA * x_ref[0, rows, :] + h, g0_ref[...], b0_ref[...])
        x1s.append(x1)
        qts.append(_dot_nt(wqt_ref[...], x1.astype(BF16)).astype(BF16))
    qt = jnp.concatenate(qts, axis=1)
    heads = [slice(h * XA_HEAD_DIM, (h + 1) * XA_HEAD_DIM) for h in range(XA_HEADS)]
    scores = [_dot(k_ref[0, :, hd], qt[hd]) for hd in heads]
    ones = jnp.ones((BF16_ROWS, k_ref.shape[1]), BF16)
    outs = []
    for hd, st in zip(heads, scores):
        p = jnp.exp2(st - jnp.max(st, axis=0, keepdims=True)).astype(BF16)
        ot = _dot(jnp.concatenate([vt_ref[0, hd, :], ones], axis=0), p)
        outs.append((ot[:XA_HEAD_DIM] / ot[XA_HEAD_DIM:XA_HEAD_DIM + 1]).astype(BF16))
    ot_all = jnp.concatenate(outs, axis=0)
    hs = [_dot_tn(ot_all[:, rows], wo_ref[...]) for rows in halves]
    for rows, x1, h in zip(halves, x1s, hs):
        o_ref[0, rows, :] = _layer_norm(DN_ALPHA * x1 + h, g1_ref[...], b1_ref[...])


def _tail_ab_kernel(o1_ref, o4_ref, o16_ref, l1_ref, l4_ref, l16_ref, expand_ref, c_ref, wa_ref, wc_ref,
                    x_ref, g0_ref, b0_ref, wqt_ref, k_ref, vt_ref, wo_ref, g1_ref, b1_ref, o_ref, obuf, lbuf):
    o_a = _mixture_tile(o1_ref, o4_ref, o16_ref, l1_ref, l4_ref, l16_ref, expand_ref, obuf, lbuf)
    hs = [_dot(o_a[rows], wa_ref[...]) + _dot(c_ref[0, rows, :], wc_ref[...]) for rows in HALVES]
    _tail_after_mixer(hs, x_ref, g0_ref, b0_ref, wqt_ref, k_ref, vt_ref, wo_ref, g1_ref, b1_ref, o_ref)


def _tail_cd_kernel(at_ref, up_ref, uc_ref, un_ref, pw_ref, ps_ref, wa_ref, wc_ref,
                    x_ref, g0_ref, b0_ref, wqt_ref, k_ref, vt_ref, wo_ref, g1_ref, b1_ref, o_ref, pool_buf, *, n):
    o_d = _pool_tile(up_ref, uc_ref, un_ref, pw_ref, ps_ref, pool_buf, tm=TM, n=n)
    hs = [_dot_tn(at_ref[0, :, rows], wa_ref[...]) + _dot(o_d[rows], wc_ref[...]) for rows in HALVES]
    _tail_after_mixer(hs, x_ref, g0_ref, b0_ref, wqt_ref, k_ref, vt_ref, wo_ref, g1_ref, b1_ref, o_ref)


def _tail_specs(x, mem):
    m = mem.shape[1]
    xspec = pl.BlockSpec((1, TM, D_MODEL), lambda i, j: (i, j, 0))
    wspec = pl.BlockSpec((D_MODEL, D_MODEL), lambda i, j: (0, 0))
    vec = pl.BlockSpec((1, D_MODEL), lambda i, j: (0, 0))
    return [xspec, vec, vec, wspec, pl.BlockSpec((1, m, D_MODEL), lambda i, j: (i, 0, 0)),
            pl.BlockSpec((1, D_MODEL, m), lambda i, j: (i, 0, 0)), wspec, vec, vec], xspec


def _tail_operands(x, g0, b0, mem, w_q, w_kv, w_o, g1, b1):
    k, vt = _mem_kv(mem, w_kv)
    wqt = (w_q * (XA_HEAD_DIM ** -0.5 * LOG2E)).T.astype(BF16)
    row = lambda v: v.reshape(1, D_MODEL)
    return x, row(g0), row(b0), wqt, k, vt, w_o.astype(BF16), row(g1), row(b1)


def _tail_ab(os_, ls_, o_b, w_out, x, g0, b0, mem, w_q, w_kv, w_o, g1, b1):
    b, n, _ = x.shape
    dils = [d for _, d in A_PATTERNS]
    expand = jnp.asarray(np.arange(LANES)[:, None] == np.arange(A_WIDTH)[None, :] // HEAD_DIM, BF16)
    wa = w_out[:A_WIDTH].astype(BF16)
    wc = w_out[A_WIDTH:].astype(BF16)
    tail_specs, xspec = _tail_specs(x, mem)
    return pl.pallas_call(
        _tail_ab_kernel,
        grid=(b, n // TM),
        in_specs=[pl.BlockSpec((1, TM // d, d * A_WIDTH), lambda i, j: (i, j, 0)) for d in dils]
        + [pl.BlockSpec((1, TM // d, d * LANES), lambda i, j: (i, j, 0)) for d in dils]
        + [pl.BlockSpec((LANES, A_WIDTH), lambda i, j: (0, 0)),
           pl.BlockSpec((1, TM, B_WIDTH), lambda i, j: (i, j, 0)),
           pl.BlockSpec(wa.shape, lambda i, j: (0, 0)), pl.BlockSpec(wc.shape, lambda i, j: (0, 0))]
        + tail_specs,
        out_specs=xspec,
        out_shape=jax.ShapeDtypeStruct((b, n, D_MODEL), F32),
        scratch_shapes=[pltpu.VMEM((2, A_WIDTH // LANES, TM, LANES), F32), pltpu.VMEM((2, TM, LANES), F32)],
        compiler_params=_cparams("parallel", "parallel"),
        name="tail_ab",
    )(*os_, *ls_, expand, o_b, wa, wc, *_tail_operands(x, g0, b0, mem, w_q, w_kv, w_o, g1, b1))


def _tail_cd(o_ct, u, pool_w, pool_scale, w_out, x, g0, b0, mem, w_q, w_kv, w_o, g1, b1):
    b, n, _ = x.shape
    per = TM // POOL_HALO
    last = n // POOL_HALO - 1
    pool_wbd = jax.scipy.linalg.block_diag(*[pool_w[g] for g in range(4)]).astype(BF16)
    wa = w_out[:C_WIDTH].astype(BF16)
    wc = w_out[C_WIDTH:].astype(BF16)
    tail_specs, xspec = _tail_specs(x, mem)
    return pl.pallas_call(
        functools.partial(_tail_cd_kernel, n=n),
        grid=(b, n // TM),
        in_specs=[pl.BlockSpec((1, C_WIDTH, TM), lambda i, j: (i, 0, j)),
                  pl.BlockSpec((1, POOL_HALO, D_WIDTH), lambda i, j: (i, jnp.maximum(j * per - 1, 0), 0)),
                  pl.BlockSpec((1, TM, D_WIDTH), lambda i, j: (i, j, 0)),
                  pl.BlockSpec((1, POOL_HALO, D_WIDTH), lambda i, j: (i, jnp.minimum((j + 1) * per, last), 0)),
                  pl.BlockSpec((D_WIDTH, D_WIDTH), lambda i, j: (0, 0)),
                  pl.BlockSpec((1, D_WIDTH), lambda i, j: (0, 0)),
                  pl.BlockSpec(wa.shape, lambda i, j: (0, 0)), pl.BlockSpec(wc.shape, lambda i, j: (0, 0))]
        + tail_specs,
        out_specs=xspec,
        out_shape=jax.ShapeDtypeStruct((b, n, D_MODEL), F32),
        scratch_shapes=[pltpu.VMEM((TM + 2 * POOL_HALO, D_WIDTH), F32)],
        compiler_params=_cparams("parallel", "parallel"),
        name="tail_cd",
    )(o_ct, u, u, u, pool_wbd, pool_scale.reshape(1, D_WIDTH), wa, wc,
      *_tail_operands(x, g0, b0, mem, w_q, w_kv, w_o, g1, b1))


def _swiglu_kernel(x_ref, win_ref, wo_ref, g_ref, b_ref, o_ref):
    halves = [slice(0, TM // 2), slice(TM // 2, TM)]
    gate_up = []
    for rows in halves:
        xb = x_ref[0, rows, :].astype(BF16)
        gate_up.append((_dot(xb, win_ref[:, :FFN_HIDDEN]), _dot(xb, win_ref[:, FFN_HIDDEN:])))
    parts = []
    for gate, up in gate_up:
        hid = (gate * (1.0 / (1.0 + jnp.exp(-gate))) * up).astype(BF16)
        parts.append(_dot(hid, wo_ref[...]))
    for rows, part in zip(halves, parts):
        o_ref[0, rows, :] = _layer_norm(DN_ALPHA * x_ref[0, rows, :] + part, g_ref[...], b_ref[...])


def _swiglu(x, w_in, w_out, g, bias):
    b, n, _ = x.shape
    win = w_in.astype(BF16)
    wout = w_out.astype(BF16)
    xspec = pl.BlockSpec((1, TM, D_MODEL), lambda i, t: (i, t, 0))
    vec = pl.BlockSpec((1, D_MODEL), lambda i, t: (0, 0))
    resident = pl.Buffered(1)
    return pl.pallas_call(
        _swiglu_kernel,
        grid=(b, n // TM),
        in_specs=[xspec,
                  pl.BlockSpec(win.shape, lambda i, t: (0, 0), pipeline_mode=resident),
                  pl.BlockSpec(wout.shape, lambda i, t: (0, 0), pipeline_mode=resident), vec, vec],
        out_specs=xspec,
        out_shape=jax.ShapeDtypeStruct((b, n, D_MODEL), F32),
        compiler_params=_cparams("parallel", "parallel"),
        name="swiglu_ln",
    )(x, win, wout, g.reshape(1, D_MODEL), bias.reshape(1, D_MODEL))


N_QK_HEADS = C_Q_HEADS + C_KV_HEADS
QK_ROWS = N_QK_HEADS * HEAD_DIM


def _proj_cd_kernel(x_ref, wt_ref, wu_ref, gain_ref, cos_ref, sin_ref,
                    qt_ref, k_ref, vt_ref, u_ref, kn_ref, *, tm):
    xb = x_ref[0].astype(BF16)
    u_ref[0] = _dot(xb, wu_ref[...])
    zt = _dot_nt(wt_ref[...], xb)
    z = zt[:QK_ROWS].reshape(N_QK_HEADS, HEAD_DIM, tm)
    ssq = jnp.sum(z * z, axis=1, keepdims=True)
    zn = z * lax.rsqrt(ssq * (1.0 / HEAD_DIM) + RMS_EPS) * gain_ref[...]
    half = HEAD_DIM // 2
    x1 = zn[:, :half]
    x2 = zn[:, half:]
    c = cos_ref[...][None]
    s = sin_ref[...][None]
    rot = jnp.concatenate([x1 * c - x2 * s, x1 * s + x2 * c], axis=1).reshape(QK_ROWS, tm)
    qt_ref[0] = rot[:C_WIDTH].astype(BF16)
    kb = rot[C_WIDTH:].astype(BF16)
    k_ref[0] = kb.astype(F32).T.astype(BF16)
    kf = kb.astype(F32).reshape(C_KV_HEADS, HEAD_DIM, tm)
    kn_ref[0] = jnp.sum(kf * kf, axis=1, keepdims=True)
    vt = zt[QK_ROWS:].astype(BF16)
    for c0 in range(tm // GQA_TKC):
        vt_ref[0, c0] = vt[:, c0 * GQA_TKC:(c0 + 1) * GQA_TKC]


def _rope_tables_t(n):
    rows = n // GRID_W
    row_id = jnp.broadcast_to(jnp.arange(rows)[:, None], (rows, GRID_W)).reshape(n)
    col_id = jnp.broadcast_to(jnp.arange(GRID_W)[None, :], (rows, GRID_W)).reshape(n)
    axis_dim = HEAD_DIM // 2
    freqs = ROPE_THETA ** (-jnp.arange(0, axis_dim, 2, dtype=F32) / axis_dim)
    ang = jnp.concatenate([row_id[:, None] * freqs, col_id[:, None] * freqs], axis=-1)
    return jnp.cos(ang).T, jnp.sin(ang).T


def _proj_cd(x, w_in, q_norm, k_norm):
    b, n, _ = x.shape
    tm = TM
    half = HEAD_DIM // 2
    wqk = w_in[:, :QK_ROWS].reshape(D_MODEL, N_QK_HEADS, half, 2)
    wqk = wqk.transpose(1, 3, 2, 0).reshape(QK_ROWS, D_MODEL)
    wt = jnp.concatenate([wqk, w_in[:, QK_ROWS:QK_ROWS + C_KV_WIDTH].T], axis=0).astype(BF16)
    wu = w_in[:, QK_ROWS + C_KV_WIDTH:].astype(BF16)
    qg = q_norm.reshape(half, 2).T.reshape(HEAD_DIM) * (HEAD_DIM ** -0.5 * LOG2E)
    kg = k_norm.reshape(half, 2).T.reshape(HEAD_DIM)
    gain = jnp.concatenate([jnp.tile(qg[None], (C_Q_HEADS, 1)), jnp.tile(kg[None], (C_KV_HEADS, 1))])
    gain = gain.reshape(N_QK_HEADS, HEAD_DIM, 1).astype(F32)
    cos_t, sin_t = _rope_tables_t(n)
    nc = n // GQA_TKC
    return pl.pallas_call(
        functools.partial(_proj_cd_kernel, tm=tm),
        grid=(b, n // tm),
        in_specs=[pl.BlockSpec((1, tm, D_MODEL), lambda i, j: (i, j, 0)),
                  pl.BlockSpec(wt.shape, lambda i, j: (0, 0)),
                  pl.BlockSpec(wu.shape, lambda i, j: (0, 0)),
                  pl.BlockSpec(gain.shape, lambda i, j: (0, 0, 0)),
                  pl.BlockSpec((HEAD_DIM // 2, tm), lambda i, j: (0, j)),
                  pl.BlockSpec((HEAD_DIM // 2, tm), lambda i, j: (0, j))],
        out_specs=[pl.BlockSpec((1, C_WIDTH, tm), lambda i, j: (i, 0, j)),
                   pl.BlockSpec((1, tm, C_KV_WIDTH), lambda i, j: (i, j, 0)),
                   pl.BlockSpec((1, tm // GQA_TKC, C_KV_WIDTH, GQA_TKC), lambda i, j: (i, j, 0, 0)),
                   pl.BlockSpec((1, tm, D_WIDTH), lambda i, j: (i, j, 0)),
                   pl.BlockSpec((1, C_KV_HEADS, 1, tm), lambda i, j: (i, 0, 0, j))],
        out_shape=[jax.ShapeDtypeStruct((b, C_WIDTH, n), BF16),
                   jax.ShapeDtypeStruct((b, n, C_KV_WIDTH), BF16),
                   jax.ShapeDtypeStruct((b, nc, C_KV_WIDTH, GQA_TKC), BF16),
                   jax.ShapeDtypeStruct((b, n, D_WIDTH), F32),
                   jax.ShapeDtypeStruct((b, C_KV_HEADS, 1, n), F32)],
        compiler_params=_cparams("parallel", "parallel"),
        name="proj_cd",
    )(x, wt, wu, gain, cos_t, sin_t)


def _gqa_kernel(qt_ref, k_ref, vt_ref, kn_ref, ot_ref, m_ref, l_ref, acc_ref, qpad_ref, *, tq, n):
    g = pl.program_id(1)
    row_half = lax.broadcasted_iota(jnp.int32, (LANES, 1), 0) // HEAD_DIM
    mine = row_half == (g % 2)
    k_max2 = jnp.max(kn_ref[0, 0], axis=1, keepdims=True)
    bound_max = jnp.zeros((1, 1), F32)
    for j in range(C_REP):
        qj = qt_ref[0, j * HEAD_DIM:(j + 1) * HEAD_DIM, :]
        q2 = jnp.concatenate([qj, qj], axis=0)
        qpad_ref[j] = jnp.where(mine, q2, jnp.zeros_like(q2))
        qf = qj.astype(F32)
        bound = jnp.sqrt(jnp.sum(qf * qf, axis=0, keepdims=True) * k_max2) * GQA_BOUND_SLACK
        m_ref[j] = bound
        bound_max = jnp.maximum(bound_max, jnp.max(bound, axis=1, keepdims=True))
    fixed_shift = bound_max[0, 0] <= GQA_BOUND_LIMIT
    acc_ref[...] = jnp.zeros(acc_ref.shape, F32)
    ones = jnp.ones((GQA_VROWS - HEAD_DIM, GQA_TKC), BF16)

    def body(c, carry, *, online):
        kchs, vchs = [], []
        for u in range(GQA_UNROLL):
            cc = c * GQA_UNROLL + u
            start = pl.multiple_of(cc * GQA_TKC, GQA_TKC)
            kchs.append(k_ref[0, pl.ds(start, GQA_TKC), :])
            vchs.append(jnp.concatenate([vt_ref[0, cc], ones], axis=0))
        pieces = [(u, j, slice(s * GQA_TW, (s + 1) * GQA_TW))
                  for u in range(GQA_UNROLL) for j in range(C_REP) for s in range(tq // GQA_TW)]

        def scores(i):
            u, j, cols = pieces[i]
            return _dot(kchs[u], qpad_ref[j, :, cols])

        pending = [scores(i) for i in range(GQA_AHEAD)]
        for i, (u, j, cols) in enumerate(pieces):
            st = pending.pop(0)
            if i + GQA_AHEAD < len(pieces):
                pending.append(scores(i + GQA_AHEAD))
            m_old = m_ref[j, :, cols]
            if not online:
                p = jnp.exp2(st - m_old)
                l_ref[j, :, cols] = l_ref[j, :, cols] + jnp.sum(p.reshape(GQA_TKC // 8, 8, GQA_TW), axis=0)
                acc_ref[j, :HEAD_DIM, cols] = acc_ref[j, :HEAD_DIM, cols] + _dot(
                    vchs[u][:HEAD_DIM], p.astype(BF16))
                continue
            m_new = jnp.maximum(m_old, jnp.max(st, axis=0, keepdims=True))
            alpha = jnp.exp2(m_old - m_new)
            p = jnp.exp2(st - m_new)
            acc_ref[j, :, cols] = alpha * acc_ref[j, :, cols] + _dot(vchs[u], p.astype(BF16))
            m_ref[j, :, cols] = m_new
        return carry

    trips = n // (GQA_TKC * GQA_UNROLL)

    @pl.when(fixed_shift)
    def _():
        l_ref[...] = jnp.zeros(l_ref.shape, F32)
        lax.fori_loop(0, trips, functools.partial(body, online=False), 0)
        for j in range(C_REP):
            acc_ref[j, HEAD_DIM:HEAD_DIM + 1, :] = jnp.sum(l_ref[j], axis=0, keepdims=True)

    @pl.when(jnp.logical_not(fixed_shift))
    def _():
        m_ref[...] = jnp.full(m_ref.shape, NEG_INF, F32)
        lax.fori_loop(0, trips, functools.partial(body, online=True), 0)

    for j in range(C_REP):
        l = acc_ref[j, HEAD_DIM:HEAD_DIM + 1, :]
        ot_ref[0, j * HEAD_DIM:(j + 1) * HEAD_DIM, :] = (acc_ref[j, :HEAD_DIM, :] / l).astype(BF16)


def _gqa(qt, k, vt, kn):
    b, _, n = qt.shape
    tq = min(GQA_TQ, n)
    nc = n // GQA_TKC
    rows = C_REP * HEAD_DIM
    return pl.pallas_call(
        functools.partial(_gqa_kernel, tq=tq, n=n),
        grid=(b, C_KV_HEADS, n // tq),
        in_specs=[pl.BlockSpec((1, rows, tq), lambda i, g, t: (i, g, t)),
                  pl.BlockSpec((1, n, LANES), lambda i, g, t: (i, 0, g // 2)),
                  pl.BlockSpec((1, nc, HEAD_DIM, GQA_TKC), lambda i, g, t: (i, 0, g, 0)),
                  pl.BlockSpec((1, 1, 1, n), lambda i, g, t: (i, g, 0, 0))],
        out_specs=pl.BlockSpec((1, rows, tq), lambda i, g, t: (i, g, t)),
        out_shape=jax.ShapeDtypeStruct((b, C_WIDTH, n), BF16),
        scratch_shapes=[pltpu.VMEM((C_REP, 1, tq), F32), pltpu.VMEM((C_REP, 8, tq), F32),
                        pltpu.VMEM((C_REP, GQA_VROWS, tq), F32), pltpu.VMEM((C_REP, LANES, tq), BF16)],
        compiler_params=_cparams("parallel", "parallel", "parallel"),
        name="gqa_flash",
    )(qt, k, vt, kn)


def _pool_tile(up_ref, uc_ref, un_ref, w_ref, scale_ref, buf, *, tm, n):
    i = pl.program_id(1)
    cur = uc_ref[0]
    buf[0:POOL_HALO, :] = jnp.where(i > 0, up_ref[0], 0.0)
    buf[POOL_HALO:POOL_HALO + tm, :] = cur
    buf[POOL_HALO + tm:, :] = jnp.where(i < pl.num_programs(1) - 1, un_ref[0], 0.0)
    lane_group = lax.broadcasted_iota(jnp.int32, (1, D_WIDTH), 1) // 64
    half_w = jnp.left_shift(1, lane_group)
    acc = jnp.zeros((tm, D_WIDTH), F32)
    for j in range(-POOL_HALO, POOL_HALO):
        inside = (j >= -half_w) & (j < half_w)
        acc = acc + jnp.where(inside, buf[POOL_HALO + j:POOL_HALO + j + tm, :], 0.0)
    t = i * tm + lax.broadcasted_iota(jnp.int32, (tm, 1), 0)
    cnt = jnp.minimum(t + half_w, n) - jnp.maximum(t - half_w, 0)
    mixed = (acc / cnt.astype(F32) - cur).astype(BF16)
    return (_dot(mixed, w_ref[...]) * scale_ref[...]).astype(BF16)


def _trunk(x, mem, rel_bias, ab_w_in, ab_fnet_g, ab_fnet_w, ab_w_out,
           cd_w_in, cd_q_norm, cd_k_norm, cd_pool_w, cd_pool_scale, cd_w_out,
           xa_w_q, xa_w_kv, xa_w_o, ffn_w_in, ffn_w_out, ln_g, ln_b):
    def tail_args(layer):
        return (ln_g[layer, 0], ln_b[layer, 0], mem, xa_w_q[layer], xa_w_kv[layer], xa_w_o[layer],
                ln_g[layer, 1], ln_b[layer, 1])

    for layer in range(DEPTH):
        i = layer // 2
        if layer % 2 == 0:
            w_in = ab_w_in[i]
            w_in = jnp.concatenate([w_in[:, :A_WIDTH] * (HEAD_DIM ** -0.5 * LOG2E), w_in[:, A_WIDTH:]], axis=1)
            *views, u = _proj_ab(x, w_in.astype(BF16))
            outs = [_dilated(view, rel_bias, d) for view, (_, d) in zip(views, A_PATTERNS)]
            o_b = _fnet(u, ab_fnet_g[i], ab_fnet_w[i])
            x = _tail_ab([o for o, _ in outs], [l for _, l in outs], o_b, ab_w_out[i], x, *tail_args(layer))
        else:
            qt, k, vt, u, kn = _proj_cd(x, cd_w_in[i], cd_q_norm[i], cd_k_norm[i])
            o_c = _gqa(qt, k, vt, kn)
            x = _tail_cd(o_c, u, cd_pool_w[i], cd_pool_scale[i], cd_w_out[i], x, *tail_args(layer))
        x = _swiglu(x, ffn_w_in[layer], ffn_w_out[layer], ln_g[layer, 2], ln_b[layer, 2])
    return x


def kernel(x_prompt, x_sample, mem_prompt, mem_sample, rel_bias, ab_w_in, ab_fnet_g, ab_fnet_w, ab_w_out, cd_w_in, cd_q_norm, cd_k_norm, cd_pool_w, cd_pool_scale, cd_w_out, xa_w_q, xa_w_kv, xa_w_o, ffn_w_in, ffn_w_out, ln_g, ln_b):
    params = (rel_bias, ab_w_in, ab_fnet_g, ab_fnet_w, ab_w_out,
              cd_w_in, cd_q_norm, cd_k_norm, cd_pool_w, cd_pool_scale, cd_w_out,
              xa_w_q, xa_w_kv, xa_w_o, ffn_w_in, ffn_w_out, ln_g, ln_b)
    return (_trunk(x_prompt, mem_prompt, *params), _trunk(x_sample, mem_sample, *params))
```

```python
import functools
import math

import numpy as np
import jax
import jax.numpy as jnp
from jax import lax
from jax.experimental import pallas as pl
from jax.experimental.pallas import tpu as pltpu

F32 = jnp.float32
BF16 = jnp.bfloat16

D_MODEL = 1024
HEAD_DIM = 64
GRID_W = 64
LN_EPS = 1e-5
RMS_EPS = 1e-6
NEG_INF = -1e30
DEPTH = 2
A_HEADS = 12
A_WIDTH = A_HEADS * HEAD_DIM
A_PATTERNS = ((128, 1), (512, 4), (2048, 16))
A_HALF = 64
QKV_WIDTH = 3 * A_WIDTH
N_BUCKETS = 32
REL_MAX_DIST = 1024
B_WIDTH = 256
C_Q_HEADS = 12
C_KV_HEADS = 4
C_REP = C_Q_HEADS // C_KV_HEADS
C_WIDTH = C_Q_HEADS * HEAD_DIM
C_KV_WIDTH = C_KV_HEADS * HEAD_DIM
ROPE_THETA = 10000.0
POOL_WINDOWS = (2, 4, 8, 16)
POOL_HALO = 8
D_WIDTH = 256
XA_HEADS = 4
XA_HEAD_DIM = D_MODEL // XA_HEADS
FFN_HIDDEN = 2816
DN_ALPHA = (2 * DEPTH) ** 0.25
LOG2E = 1.4426950408889634

LANES = 128
MXU_TILE = 256
BF16_ROWS = 16
VMEM_LIMIT = 56 * 1024 * 1024
TM = 512
FFT_N2 = 128
FFT_ROWS = 1024
FFT_KC = 16
FFT_AHEAD = 3
DIL_AHEAD = 3
DIL_TQ = 512
DIL_SQ = MXU_TILE // 2
DIL_SK = DIL_SQ + 2 * A_HALF
DIL_PIECES = 48
GQA_TQ = 4096
GQA_TW = MXU_TILE
GQA_TKC = 256
GQA_VROWS = HEAD_DIM + BF16_ROWS
GQA_PIECES = 96
GQA_AHEAD = 5
GQA_BOUND_SLACK = 1.0 + 2.0 ** -10
GQA_BOUND_LIMIT = 60.0


def _cparams(*sem):
    return pltpu.CompilerParams(dimension_semantics=sem, vmem_limit_bytes=VMEM_LIMIT)


def _dot(a, b):
    return jnp.dot(a, b, preferred_element_type=F32)


def _dot_nt(a, b):
    return lax.dot_general(a, b, (((1,), (1,)), ((), ())), preferred_element_type=F32)


def _dot_tn(a, b):
    return lax.dot_general(a, b, (((0,), (0,)), ((), ())), preferred_element_type=F32)


def _split(x):
    hi = x.astype(BF16)
    lo = (x - hi.astype(F32)).astype(BF16)
    return hi, lo


def _dot3(ah, al, bh, bl):
    return _dot(ah, bh) + _dot(al, bh) + _dot(ah, bl)


def _np_split(x):
    x = np.asarray(x, np.float32)
    hi = x.astype(BF16)
    lo = (x - hi.astype(np.float32)).astype(BF16)
    return jnp.asarray(hi), jnp.asarray(lo)


def _layer_norm(h, g, b):
    mu = jnp.mean(h, axis=-1, keepdims=True)
    xc = h - mu
    var = jnp.mean(xc * xc, axis=-1, keepdims=True)
    return xc * lax.rsqrt(var + LN_EPS) * g + b


def _proj_ab_kernel(x_ref, w_ref, qkv1_ref, qkv4_ref, qkv16_ref, u_ref, zbuf):
    xb = x_ref[0].astype(BF16)
    chunks = list(range(0, QKV_WIDTH, MXU_TILE))

    for c in chunks:
        z = _dot(xb, w_ref[:, c:c + MXU_TILE])
        zbuf[c // LANES] = z[:, :LANES]
        zbuf[c // LANES + 1] = z[:, LANES:]
        qkv1_ref[0, :, c:c + MXU_TILE] = z.astype(BF16)
    u_ref[0] = _dot(xb, w_ref[:, QKV_WIDTH:])
    for (_, d), ref in zip(A_PATTERNS[1:], (qkv4_ref, qkv16_ref)):
        rows = TM // d
        for r in range(d):
            for ct in range(QKV_WIDTH // LANES):
                col = r * QKV_WIDTH + ct * LANES
                ref[0, :, col:col + LANES] = zbuf[ct, pl.ds(r, rows, stride=d), :].astype(BF16)


def _proj_ab(x, w):
    b, n, _ = x.shape
    dils = [d for _, d in A_PATTERNS]
    return pl.pallas_call(
        _proj_ab_kernel,
        grid=(b, n // TM),
        in_specs=[pl.BlockSpec((1, TM, D_MODEL), lambda i, j: (i, j, 0)),
                  pl.BlockSpec(w.shape, lambda i, j: (0, 0))],
        out_specs=[pl.BlockSpec((1, TM // d, d * QKV_WIDTH), lambda i, j: (i, j, 0)) for d in dils]
        + [pl.BlockSpec((1, TM, B_WIDTH), lambda i, j: (i, j, 0))],
        out_shape=[jax.ShapeDtypeStruct((b, n // d, d * QKV_WIDTH), BF16) for d in dils]
        + [jax.ShapeDtypeStruct((b, n, B_WIDTH), F32)],
        scratch_shapes=[pltpu.VMEM((QKV_WIDTH // LANES, TM, LANES), F32)],
        compiler_params=_cparams("parallel", "parallel"),
        name="proj_ab",
    )(x, w)


def _t5_bucket_np(rel):
    nb = N_BUCKETS // 2
    max_exact = nb // 2
    ret = np.where(rel > 0, nb, 0)
    n = np.abs(rel)
    nf = np.maximum(n, 1).astype(np.float32)
    large = max_exact + (np.log(nf / max_exact) / math.log(REL_MAX_DIST / max_exact)
                         * (nb - max_exact)).astype(np.int32)
    large = np.minimum(large, nb - 1)
    return ret + np.where(n < max_exact, n, large)


def _band_bias(rel_bias, dilation):
    tq, tk = DIL_SQ, DIL_SK
    band = 2 * A_HALF + 1
    bucket = _t5_bucket_np((np.arange(band) - A_HALF) * dilation)
    row = (rel_bias[jnp.asarray(bucket)].T * LOG2E).astype(F32)
    row = jnp.concatenate([row, jnp.full((A_HEADS, tk + 1 - band), NEG_INF, F32)], axis=1)
    bias = jnp.tile(row, (1, tq))[:, :tq * tk].reshape(A_HEADS, tq, tk)
    bias = bias.transpose(0, 2, 1)
    bias = bias.reshape(A_HEADS // 2, 2, tk, tq).transpose(0, 2, 1, 3).reshape(A_HEADS // 2, tk, 2 * tq)
    key = np.arange(tk)[None, :, None]
    before = jnp.asarray(key < A_HALF)
    after = jnp.asarray(key >= A_HALF + tq)
    first = jnp.where(before, NEG_INF, bias)
    return jnp.stack([bias, first, jnp.where(after, NEG_INF, bias), jnp.where(after, NEG_INF, first)])


def _dilated_kernel(prev_ref, cur_ref, next_ref, bias_ref, o_ref, lse_ref, kbuf, vbuf, *, tq, n_res):
    for rr in range(n_res):
        for buf, c0 in ((kbuf, rr * QKV_WIDTH + A_WIDTH), (vbuf, rr * QKV_WIDTH + 2 * A_WIDTH)):
            buf[rr, 0:A_HALF, :] = prev_ref[0, :, c0:c0 + A_WIDTH]
            buf[rr, A_HALF:A_HALF + tq, :] = cur_ref[0, :, c0:c0 + A_WIDTH]
            buf[rr, A_HALF + tq:, :] = next_ref[0, :, c0:c0 + A_WIDTH]

    i = pl.program_id(1)
    n_sub = tq // DIL_SQ
    row_low = lax.broadcasted_iota(jnp.int32, (LANES, 1), 0) < HEAD_DIM
    row16 = lax.broadcasted_iota(jnp.int32, (BF16_ROWS, 1), 0)
    ones = jnp.ones((BF16_ROWS, DIL_SK), BF16)
    qts, vts = {}, {}
    pieces = [(rr, pair, sub) for rr in range(n_res) for pair in range(A_HEADS // 2) for sub in range(n_sub)]

    def variant(sub):
        v = 0
        if sub == 0:
            v = v + jnp.where(i == 0, 1, 0)
        if sub == n_sub - 1:
            v = v + jnp.where(i == pl.num_programs(1) - 1, 2, 0)
        return v

    def scores(idx):
        rr, pair, sub = pieces[idx]
        cols = slice(pair * LANES, (pair + 1) * LANES)
        if (rr, pair) not in qts:
            qcols = slice(rr * QKV_WIDTH + pair * LANES, rr * QKV_WIDTH + (pair + 1) * LANES)
            qts[rr, pair] = cur_ref[0, :, qcols].astype(F32).T
        qsub = qts[rr, pair][:, sub * DIL_SQ:(sub + 1) * DIL_SQ]
        rhs = jnp.concatenate([jnp.where(row_low, qsub, 0.0), jnp.where(row_low, 0.0, qsub)], axis=1)
        keys = kbuf[rr, sub * DIL_SQ:sub * DIL_SQ + DIL_SK, cols]
        return _dot(keys, rhs.astype(BF16)) + bias_ref[variant(sub), pair]

    pending = [scores(idx) for idx in range(DIL_AHEAD)]
    lse_t = {(rr, sub): jnp.zeros((BF16_ROWS, DIL_SQ), F32) for rr in range(n_res) for sub in range(n_sub)}
    for idx, (rr, pair, sub) in enumerate(pieces):
        st = pending.pop(0)
        if idx + DIL_AHEAD < len(pieces):
            pending.append(scores(idx + DIL_AHEAD))
        cols = slice(pair * LANES, (pair + 1) * LANES)
        m = jnp.max(st, axis=0, keepdims=True)
        p = jnp.exp2(st - m).astype(BF16)
        if (rr, pair) not in vts:
            vts[rr, pair] = vbuf[rr, :, cols].astype(F32).T.astype(BF16)
        vaug = jnp.concatenate([vts[rr, pair][:, sub * DIL_SQ:sub * DIL_SQ + DIL_SK], ones], axis=0)
        ot = _dot(vaug, p)
        l = ot[LANES:LANES + 1]
        lse2 = m + jnp.log2(l)
        o_pair = jnp.concatenate([ot[:HEAD_DIM, :DIL_SQ] / l[:, :DIL_SQ],
                                  ot[HEAD_DIM:LANES, DIL_SQ:] / l[:, DIL_SQ:]], axis=0)
        ocols = slice(rr * A_WIDTH + pair * LANES, rr * A_WIDTH + (pair + 1) * LANES)
        o_ref[0, sub * DIL_SQ:(sub + 1) * DIL_SQ, ocols] = o_pair.T.astype(BF16)
        lse_t[rr, sub] = jnp.where(row16 == 2 * pair, lse2[:, :DIL_SQ],
                                   jnp.where(row16 == 2 * pair + 1, lse2[:, DIL_SQ:], lse_t[rr, sub]))
    for (rr, sub), rows16 in lse_t.items():
        full = jnp.concatenate([rows16, jnp.zeros((LANES - BF16_ROWS, DIL_SQ), F32)], axis=0)
        lse_ref[0, sub * DIL_SQ:(sub + 1) * DIL_SQ, rr * LANES:(rr + 1) * LANES] = full.T


def _dilated(view, rel_bias, dilation):
    b, seq, _ = view.shape
    tq = min(DIL_TQ, seq)
    tk = tq + 2 * A_HALF
    n_res = min(dilation, max(1, DIL_PIECES // (A_HEADS // 2 * (tq // DIL_SQ))))
    bias = _band_bias(rel_bias, dilation)
    per = tq // A_HALF
    last = seq // A_HALF - 1
    width = n_res * QKV_WIDTH

    o, lse = pl.pallas_call(
        functools.partial(_dilated_kernel, tq=tq, n_res=n_res),
        grid=(b, seq // tq, dilation // n_res),
        in_specs=[pl.BlockSpec((1, A_HALF, width), lambda bi, i, r: (bi, jnp.maximum(i * per - 1, 0), r)),
                  pl.BlockSpec((1, tq, width), lambda bi, i, r: (bi, i, r)),
                  pl.BlockSpec((1, A_HALF, width), lambda bi, i, r: (bi, jnp.minimum((i + 1) * per, last), r)),
                  pl.BlockSpec(bias.shape, lambda bi, i, r: (0, 0, 0, 0))],
        out_specs=[pl.BlockSpec((1, tq, n_res * A_WIDTH), lambda bi, i, r: (bi, i, r)),
                   pl.BlockSpec((1, tq, n_res * LANES), lambda bi, i, r: (bi, i, r))],
        out_shape=[jax.ShapeDtypeStruct((b, seq, dilation * A_WIDTH), BF16),
                   jax.ShapeDtypeStruct((b, seq, dilation * LANES), F32)],
        scratch_shapes=[pltpu.VMEM((n_res, tk, A_WIDTH), BF16), pltpu.VMEM((n_res, tk, A_WIDTH), BF16)],
        compiler_params=_cparams("parallel", "parallel", "parallel"),
        name=f"dilated_d{dilation}",
    )(view, view, view, bias)
    return o, lse


def _mixture_tile(o1_ref, o4_ref, o16_ref, l1_ref, l4_ref, l16_ref, expand_ref, obuf, lbuf):
    for idx, (d, o_ref, l_ref) in enumerate(((A_PATTERNS[1][1], o4_ref, l4_ref), (A_PATTERNS[2][1], o16_ref, l16_ref))):
        rows = TM // d
        for r in range(d):
            for ct in range(A_WIDTH // LANES):
                col = r * A_WIDTH + ct * LANES
                obuf[idx, ct, pl.ds(r, rows, stride=d), :] = o_ref[0, :, col:col + LANES].astype(F32)
            lbuf[idx, pl.ds(r, rows, stride=d), :] = l_ref[0, :, r * LANES:(r + 1) * LANES]
    ls = [l1_ref[0], lbuf[0], lbuf[1]]
    mx = jnp.maximum(jnp.maximum(ls[0], ls[1]), ls[2])
    es = [jnp.exp2(l - mx) for l in ls]
    inv = 1.0 / (es[0] + es[1] + es[2])
    ws = []
    for e in es:
        wh, wl = _split(e * inv)
        ws.append(_dot(wh, expand_ref[...]) + _dot(wl, expand_ref[...]))
    pairs = []
    for pair in range(A_HEADS // 2):
        cols = slice(pair * LANES, (pair + 1) * LANES)
        os_ = [o1_ref[0, :, cols].astype(F32), obuf[0, pair], obuf[1, pair]]
        acc = None
        for g in range(3):
            t = ws[g][:, cols] * os_[g]
            acc = t if acc is None else acc + t
        pairs.append(acc.astype(BF16))
    return jnp.concatenate(pairs, axis=1)


def _group_mean_matrix():
    g = np.kron(np.eye(4), np.full((64, 64), 1.0 / 64))
    return jnp.asarray(g, BF16)


def _fnet_front_kernel(u_ref, gm_ref, gain_ref, ch_ref, cl_ref, sh_ref, sl_ref, m1h_ref, m1l_ref,
                       zr_ref, zi_ref, ybuf, zbuf, *, n1, tcn):
    rows = n1 * tcn
    u = u_ref[0].reshape(rows, B_WIDTH)
    gm = gm_ref[...]
    uh, ul = _split(u)
    mean = _dot(uh, gm) + _dot(ul, gm)
    xc = u - mean
    qh, ql = _split(xc * xc)
    var = _dot(qh, gm) + _dot(ql, gm)
    un = xc * lax.rsqrt(var + LN_EPS) * gain_ref[...]
    nh, nl = _split(un)
    yr = _dot3(nh, nl, ch_ref[...], cl_ref[...])
    yi = -_dot3(nh, nl, sh_ref[...], sl_ref[...])
    for q, val in enumerate((yr[:, :LANES], yr[:, LANES:], yi[:, :LANES], yi[:, LANES:])):
        ybuf[q] = val
    def first_stage(n2):
        col = [ybuf[q, pl.ds(n2, n1, stride=tcn), :] for q in range(4)]
        y = jnp.concatenate([jnp.concatenate(col[:2], axis=1), jnp.concatenate(col[2:], axis=1)], axis=0)
        yh, yl = _split(y)
        return _dot3(m1h_ref[...], m1l_ref[...], yh, yl)

    pending = [first_stage(n2) for n2 in range(min(FFT_AHEAD, tcn))]
    for n2 in range(tcn):
        z = pending.pop(0)
        if n2 + FFT_AHEAD < tcn:
            pending.append(first_stage(n2 + FFT_AHEAD))
        for q, val in enumerate((z[:n1, :LANES], z[:n1, LANES:], z[n1:, :LANES], z[n1:, LANES:])):
            zbuf[q, pl.ds(n2, n1, stride=tcn), :] = val
    zr_ref[0] = jnp.concatenate([zbuf[0], zbuf[1]], axis=1).reshape(n1, tcn, B_WIDTH)
    zi_ref[0] = jnp.concatenate([zbuf[2], zbuf[3]], axis=1).reshape(n1, tcn, B_WIDTH)


def _fnet_back_kernel(zr_ref, zi_ref, tch_ref, tcl_ref, tsh_ref, tsl_ref, w_ref, o_ref, obuf, *, kc, scale):
    n2 = FFT_N2

    def second_stage(kk):
        rh, rl = _split(zr_ref[0, kk])
        ih, il = _split(zi_ref[0, kk])
        return _dot3(tch_ref[kk], tcl_ref[kk], rh, rl) + _dot3(tsh_ref[kk], tsl_ref[kk], ih, il)

    pending = [second_stage(kk) for kk in range(FFT_AHEAD)]
    for kk in range(kc):
        f = pending.pop(0)
        if kk + FFT_AHEAD < kc:
            pending.append(second_stage(kk + FFT_AHEAD))
        o = _dot((f * scale).astype(BF16), w_ref[...])
        obuf[0, pl.ds(kk, n2, stride=kc), :] = o[:, :LANES]
        obuf[1, pl.ds(kk, n2, stride=kc), :] = o[:, LANES:]
    out = jnp.concatenate([obuf[0], obuf[1]], axis=1).reshape(n2, kc, B_WIDTH)
    o_ref[0] = out.astype(BF16)


def _fnet_tables(n):
    n2 = FFT_N2
    n1 = n // n2
    c = np.arange(64)
    ang = 2 * np.pi * np.outer(c, c) / 64
    cbd = np.kron(np.eye(4), np.cos(ang))
    sbd = np.kron(np.eye(4), np.sin(ang))
    k1 = np.arange(n1)
    ang1 = 2 * np.pi * np.outer(k1, k1) / n1
    c1, s1 = np.cos(ang1), np.sin(ang1)
    m1 = np.block([[c1, s1], [-s1, c1]])
    k2 = np.arange(n2)
    npr = k1[:, None, None] + n1 * k2[None, :, None]
    prod = (npr * k2[None, None, :]) % n
    ang2 = 2 * np.pi * prod / n
    return (_np_split(cbd), _np_split(sbd), _np_split(m1), _np_split(np.cos(ang2)), _np_split(np.sin(ang2)))


def _fnet(u, fnet_g, fnet_w):
    b, n, _ = u.shape
    n2 = FFT_N2
    n1 = n // n2
    tcn = FFT_ROWS // n1
    (ch, cl), (sh, sl), (m1h, m1l), (tch, tcl), (tsh, tsl) = _fnet_tables(n)
    gain = fnet_g.reshape(1, B_WIDTH)
    wbd = jax.scipy.linalg.block_diag(*[fnet_w[g] for g in range(4)]).astype(BF16)

    mat = pl.BlockSpec((B_WIDTH, B_WIDTH), lambda i, j: (0, 0))
    m1spec = pl.BlockSpec((2 * n1, 2 * n1), lambda i, j: (0, 0))
    tile = pl.BlockSpec((1, n1, tcn, B_WIDTH), lambda i, j: (i, 0, j, 0))
    zr, zi = pl.pallas_call(
        functools.partial(_fnet_front_kernel, n1=n1, tcn=tcn),
        grid=(b, n2 // tcn),
        in_specs=[tile, mat, pl.BlockSpec((1, B_WIDTH), lambda i, j: (0, 0)), mat, mat, mat, mat, m1spec, m1spec],
        out_specs=[tile, tile],
        out_shape=[jax.ShapeDtypeStruct((b, n1, n2, B_WIDTH), F32)] * 2,
        scratch_shapes=[pltpu.VMEM((4, FFT_ROWS, LANES), F32), pltpu.VMEM((4, FFT_ROWS, LANES), F32)],
        compiler_params=_cparams("parallel", "parallel"),
        name="fnet_front",
    )(u.reshape(b, n1, n2, B_WIDTH), _group_mean_matrix(), gain, ch, cl, sh, sl, m1h, m1l)

    kc = FFT_KC
    zspec = pl.BlockSpec((1, kc, n2, B_WIDTH), lambda i, j: (i, j, 0, 0))
    tspec = pl.BlockSpec((kc, n2, n2), lambda i, j: (j, 0, 0))
    out = pl.pallas_call(
        functools.partial(_fnet_back_kernel, kc=kc, scale=1.0 / math.sqrt(64.0 * n)),
        grid=(b, n1 // kc),
        in_specs=[zspec, zspec, tspec, tspec, tspec, tspec,
                  pl.BlockSpec((B_WIDTH, B_WIDTH), lambda i, j: (0, 0))],
        out_specs=pl.BlockSpec((1, n2, kc, B_WIDTH), lambda i, j: (i, 0, j, 0)),
        out_shape=jax.ShapeDtypeStruct((b, n2, n1, B_WIDTH), BF16),
        scratch_shapes=[pltpu.VMEM((2, n2 * kc, LANES), F32)],
        compiler_params=_cparams("parallel", "parallel"),
        name="fnet_back",
    )(zr, zi, tch, tcl, tsh, tsl, wbd)
    return out.reshape(b, n, B_WIDTH)


def _mem_kv_kernel(mem_ref, wk_ref, wvt_ref, k_ref, vt_ref):
    m = mem_ref[0].astype(BF16)
    k_ref[0] = _dot(m, wk_ref[...]).astype(BF16)
    vt_ref[0] = _dot_nt(wvt_ref[...], m).astype(BF16)


def _mem_kv(mem, w_kv):
    b, m, _ = mem.shape
    wk = w_kv[:, :D_MODEL].astype(BF16)
    wvt = w_kv[:, D_MODEL:].T.astype(BF16)
    wspec = pl.BlockSpec((D_MODEL, D_MODEL), lambda i: (0, 0))
    return pl.pallas_call(
        _mem_kv_kernel,
        grid=(b,),
        in_specs=[pl.BlockSpec((1, m, D_MODEL), lambda i: (i, 0, 0)), wspec, wspec],
        out_specs=[pl.BlockSpec((1, m, D_MODEL), lambda i: (i, 0, 0)),
                   pl.BlockSpec((1, D_MODEL, m), lambda i: (i, 0, 0))],
        out_shape=[jax.ShapeDtypeStruct((b, m, D_MODEL), BF16),
                   jax.ShapeDtypeStruct((b, D_MODEL, m), BF16)],
        compiler_params=_cparams("parallel"),
        name="mem_kv",
    )(mem, wk, wvt)


HALVES = (slice(0, TM // 2), slice(TM // 2, TM))


def _tail_after_mixer(hs, x_ref, g0_ref, b0_ref, wqt_ref, k_ref, vt_ref, wo_ref, g1_ref, b1_ref, o_ref):
    halves = HALVES
    x1s, qts = [], []
    for rows, h in zip(halves, hs):
        x1 = _layer_norm(DN_ALPHA * x_ref[0, rows, :] + h, g0_ref[...], b0_ref[...])
        x1s.append(x1)
        qts.append(_dot_nt(wqt_ref[...], x1.astype(BF16)).astype(BF16))
    qt = jnp.concatenate(qts, axis=1)
    heads = [slice(h * XA_HEAD_DIM, (h + 1) * XA_HEAD_DIM) for h in range(XA_HEADS)]
    scores = [_dot(k_ref[0, :, hd], qt[hd]) for hd in heads]
    ones = jnp.ones((BF16_ROWS, k_ref.shape[1]), BF16)
    outs = []
    for hd, st in zip(heads, scores):
        p = jnp.exp2(st - jnp.max(st, axis=0, keepdims=True)).astype(BF16)
        ot = _dot(jnp.concatenate([vt_ref[0, hd, :], ones], axis=0), p)
        outs.append((ot[:XA_HEAD_DIM] / ot[XA_HEAD_DIM:XA_HEAD_DIM + 1]).astype(BF16))
    ot_all = jnp.concatenate(outs, axis=0)
    hs = [_dot_tn(ot_all[:, rows], wo_ref[...]) for rows in halves]
    for rows, x1, h in zip(halves, x1s, hs):
        o_ref[0, rows, :] = _layer_norm(DN_ALPHA * x1 + h, g1_ref[...], b1_ref[...])


def _tail_ab_kernel(o1_ref, o4_ref, o16_ref, l1_ref, l4_ref, l16_ref, expand_ref, c_ref, wa_ref, wc_ref,
                    x_ref, g0_ref, b0_ref, wqt_ref, k_ref, vt_ref, wo_ref, g1_ref, b1_ref, o_ref, obuf, lbuf):
    o_a = _mixture_tile(o1_ref, o4_ref, o16_ref, l1_ref, l4_ref, l16_ref, expand_ref, obuf, lbuf)
    hs = [_dot(o_a[rows], wa_ref[...]) + _dot(c_ref[0, rows, :], wc_ref[...]) for rows in HALVES]
    _tail_after_mixer(hs, x_ref, g0_ref, b0_ref, wqt_ref, k_ref, vt_ref, wo_ref, g1_ref, b1_ref, o_ref)


def _tail_cd_kernel(at_ref, up_ref, uc_ref, un_ref, pw_ref, ps_ref, wa_ref, wc_ref,
                    x_ref, g0_ref, b0_ref, wqt_ref, k_ref, vt_ref, wo_ref, g1_ref, b1_ref, o_ref, pool_buf, *, n):
    o_d = _pool_tile(up_ref, uc_ref, un_ref, pw_ref, ps_ref, pool_buf, tm=TM, n=n)
    hs = [_dot_tn(at_ref[0, :, rows], wa_ref[...]) + _dot(o_d[rows], wc_ref[...]) for rows in HALVES]
    _tail_after_mixer(hs, x_ref, g0_ref, b0_ref, wqt_ref, k_ref, vt_ref, wo_ref, g1_ref, b1_ref, o_ref)


def _tail_specs(x, mem):
    m = mem.shape[1]
    xspec = pl.BlockSpec((1, TM, D_MODEL), lambda i, j: (i, j, 0))
    wspec = pl.BlockSpec((D_MODEL, D_MODEL), lambda i, j: (0, 0))
    vec = pl.BlockSpec((1, D_MODEL), lambda i, j: (0, 0))
    return [xspec, vec, vec, wspec, pl.BlockSpec((1, m, D_MODEL), lambda i, j: (i, 0, 0)),
            pl.BlockSpec((1, D_MODEL, m), lambda i, j: (i, 0, 0)), wspec, vec, vec], xspec


def _tail_operands(x, g0, b0, mem, w_q, w_kv, w_o, g1, b1):
    k, vt = _mem_kv(mem, w_kv)
    wqt = (w_q * (XA_HEAD_DIM ** -0.5 * LOG2E)).T.astype(BF16)
    row = lambda v: v.reshape(1, D_MODEL)
    return x, row(g0), row(b0), wqt, k, vt, w_o.astype(BF16), row(g1), row(b1)


def _tail_ab(os_, ls_, o_b, w_out, x, g0, b0, mem, w_q, w_kv, w_o, g1, b1):
    b, n, _ = x.shape
    dils = [d for _, d in A_PATTERNS]
    expand = jnp.asarray(np.arange(LANES)[:, None] == np.arange(A_WIDTH)[None, :] // HEAD_DIM, BF16)
    wa = w_out[:A_WIDTH].astype(BF16)
    wc = w_out[A_WIDTH:].astype(BF16)
    tail_specs, xspec = _tail_specs(x, mem)
    return pl.pallas_call(
        _tail_ab_kernel,
        grid=(b, n // TM),
        in_specs=[pl.BlockSpec((1, TM // d, d * A_WIDTH), lambda i, j: (i, j, 0)) for d in dils]
        + [pl.BlockSpec((1, TM // d, d * LANES), lambda i, j: (i, j, 0)) for d in dils]
        + [pl.BlockSpec((LANES, A_WIDTH), lambda i, j: (0, 0)),
           pl.BlockSpec((1, TM, B_WIDTH), lambda i, j: (i, j, 0)),
           pl.BlockSpec(wa.shape, lambda i, j: (0, 0)), pl.BlockSpec(wc.shape, lambda i, j: (0, 0))]
        + tail_specs,
        out_specs=xspec,
        out_shape=jax.ShapeDtypeStruct((b, n, D_MODEL), F32),
        scratch_shapes=[pltpu.VMEM((2, A_WIDTH // LANES, TM, LANES), F32), pltpu.VMEM((2, TM, LANES), F32)],
        compiler_params=_cparams("parallel", "parallel"),
        name="tail_ab",
    )(*os_, *ls_, expand, o_b, wa, wc, *_tail_operands(x, g0, b0, mem, w_q, w_kv, w_o, g1, b1))


def _tail_cd(o_ct, u, pool_w, pool_scale, w_out, x, g0, b0, mem, w_q, w_kv, w_o, g1, b1):
    b, n, _ = x.shape
    per = TM // POOL_HALO
    last = n // POOL_HALO - 1
    pool_wbd = jax.scipy.linalg.block_diag(*[pool_w[g] for g in range(4)]).astype(BF16)
    wa = w_out[:C_WIDTH].astype(BF16)
    wc = w_out[C_WIDTH:].astype(BF16)
    tail_specs, xspec = _tail_specs(x, mem)
    return pl.pallas_call(
        functools.partial(_tail_cd_kernel, n=n),
        grid=(b, n // TM),
        in_specs=[pl.BlockSpec((1, C_WIDTH, TM), lambda i, j: (i, 0, j)),
                  pl.BlockSpec((1, POOL_HALO, D_WIDTH), lambda i, j: (i, jnp.maximum(j * per - 1, 0), 0)),
                  pl.BlockSpec((1, TM, D_WIDTH), lambda i, j: (i, j, 0)),
                  pl.BlockSpec((1, POOL_HALO, D_WIDTH), lambda i, j: (i, jnp.minimum((j + 1) * per, last), 0)),
                  pl.BlockSpec((D_WIDTH, D_WIDTH), lambda i, j: (0, 0)),
                  pl.BlockSpec((1, D_WIDTH), lambda i, j: (0, 0)),
                  pl.BlockSpec(wa.shape, lambda i, j: (0, 0)), pl.BlockSpec(wc.shape, lambda i, j: (0, 0))]
        + tail_specs,
        out_specs=xspec,
        out_shape=jax.ShapeDtypeStruct((b, n, D_MODEL), F32),
        scratch_shapes=[pltpu.VMEM((TM + 2 * POOL_HALO, D_WIDTH), F32)],
        compiler_params=_cparams("parallel", "parallel"),
        name="tail_cd",
    )(o_ct, u, u, u, pool_wbd, pool_scale.reshape(1, D_WIDTH), wa, wc,
      *_tail_operands(x, g0, b0, mem, w_q, w_kv, w_o, g1, b1))


def _swiglu_kernel(x_ref, win_ref, wo_ref, g_ref, b_ref, o_ref):
    halves = [slice(0, TM // 2), slice(TM // 2, TM)]
    gate_up = []
    for rows in halves:
        xb = x_ref[0, rows, :].astype(BF16)
        gate_up.append((_dot(xb, win_ref[:, :FFN_HIDDEN]), _dot(xb, win_ref[:, FFN_HIDDEN:])))
    parts = []
    for gate, up in gate_up:
        hid = (gate * (1.0 / (1.0 + jnp.exp(-gate))) * up).astype(BF16)
        parts.append(_dot(hid, wo_ref[...]))
    for rows, part in zip(halves, parts):
        o_ref[0, rows, :] = _layer_norm(DN_ALPHA * x_ref[0, rows, :] + part, g_ref[...], b_ref[...])


def _swiglu(x, w_in, w_out, g, bias):
    b, n, _ = x.shape
    win = w_in.astype(BF16)
    wout = w_out.astype(BF16)
    xspec = pl.BlockSpec((1, TM, D_MODEL), lambda i, t: (i, t, 0))
    vec = pl.BlockSpec((1, D_MODEL), lambda i, t: (0, 0))
    resident = pl.Buffered(1)
    return pl.pallas_call(
        _swiglu_kernel,
        grid=(b, n // TM),
        in_specs=[xspec,
                  pl.BlockSpec(win.shape, lambda i, t: (0, 0), pipeline_mode=resident),
                  pl.BlockSpec(wout.shape, lambda i, t: (0, 0), pipeline_mode=resident), vec, vec],
        out_specs=xspec,
        out_shape=jax.ShapeDtypeStruct((b, n, D_MODEL), F32),
        compiler_params=_cparams("parallel", "parallel"),
        name="swiglu_ln",
    )(x, win, wout, g.reshape(1, D_MODEL), bias.reshape(1, D_MODEL))


N_QK_HEADS = C_Q_HEADS + C_KV_HEADS
QK_ROWS = N_QK_HEADS * HEAD_DIM


def _proj_cd_kernel(x_ref, wt_ref, wu_ref, gain_ref, cos_ref, sin_ref,
                    qt_ref, k_ref, vt_ref, u_ref, kn_ref, *, tm):
    xb = x_ref[0].astype(BF16)
    u_ref[0] = _dot(xb, wu_ref[...])
    zt = _dot_nt(wt_ref[...], xb)
    z = zt[:QK_ROWS].reshape(N_QK_HEADS, HEAD_DIM, tm)
    ssq = jnp.sum(z * z, axis=1, keepdims=True)
    zn = z * lax.rsqrt(ssq * (1.0 / HEAD_DIM) + RMS_EPS) * gain_ref[...]
    half = HEAD_DIM // 2
    x1 = zn[:, :half]
    x2 = zn[:, half:]
    c = cos_ref[...][None]
    s = sin_ref[...][None]
    rot = jnp.concatenate([x1 * c - x2 * s, x1 * s + x2 * c], axis=1).reshape(QK_ROWS, tm)
    qt_ref[0] = rot[:C_WIDTH].astype(BF16)
    kb = rot[C_WIDTH:].astype(BF16)
    k_ref[0] = kb.astype(F32).T.astype(BF16)
    kf = kb.astype(F32).reshape(C_KV_HEADS, HEAD_DIM, tm)
    kn_ref[0] = jnp.sum(kf * kf, axis=1, keepdims=True)
    vt = zt[QK_ROWS:].astype(BF16)
    for c0 in range(tm // GQA_TKC):
        vt_ref[0, c0] = vt[:, c0 * GQA_TKC:(c0 + 1) * GQA_TKC]


def _rope_tables_t(n):
    rows = n // GRID_W
    row_id = jnp.broadcast_to(jnp.arange(rows)[:, None], (rows, GRID_W)).reshape(n)
    col_id = jnp.broadcast_to(jnp.arange(GRID_W)[None, :], (rows, GRID_W)).reshape(n)
    axis_dim = HEAD_DIM // 2
    freqs = ROPE_THETA ** (-jnp.arange(0, axis_dim, 2, dtype=F32) / axis_dim)
    ang = jnp.concatenate([row_id[:, None] * freqs, col_id[:, None] * freqs], axis=-1)
    return jnp.cos(ang).T, jnp.sin(ang).T


def _proj_cd(x, w_in, q_norm, k_norm):
    b, n, _ = x.shape
    tm = TM
    half = HEAD_DIM // 2
    wqk = w_in[:, :QK_ROWS].reshape(D_MODEL, N_QK_HEADS, half, 2)
    wqk = wqk.transpose(1, 3, 2, 0).reshape(QK_ROWS, D_MODEL)
    wt = jnp.concatenate([wqk, w_in[:, QK_ROWS:QK_ROWS + C_KV_WIDTH].T], axis=0).astype(BF16)
    wu = w_in[:, QK_ROWS + C_KV_WIDTH:].astype(BF16)
    qg = q_norm.reshape(half, 2).T.reshape(HEAD_DIM) * (HEAD_DIM ** -0.5 * LOG2E)
    kg = k_norm.reshape(half, 2).T.reshape(HEAD_DIM)
    gain = jnp.concatenate([jnp.tile(qg[None], (C_Q_HEADS, 1)), jnp.tile(kg[None], (C_KV_HEADS, 1))])
    gain = gain.reshape(N_QK_HEADS, HEAD_DIM, 1).astype(F32)
    cos_t, sin_t = _rope_tables_t(n)
    nc = n // GQA_TKC
    return pl.pallas_call(
        functools.partial(_proj_cd_kernel, tm=tm),
        grid=(b, n // tm),
        in_specs=[pl.BlockSpec((1, tm, D_MODEL), lambda i, j: (i, j, 0)),
                  pl.BlockSpec(wt.shape, lambda i, j: (0, 0)),
                  pl.BlockSpec(wu.shape, lambda i, j: (0, 0)),
                  pl.BlockSpec(gain.shape, lambda i, j: (0, 0, 0)),
                  pl.BlockSpec((HEAD_DIM // 2, tm), lambda i, j: (0, j)),
                  pl.BlockSpec((HEAD_DIM // 2, tm), lambda i, j: (0, j))],
        out_specs=[pl.BlockSpec((1, C_WIDTH, tm), lambda i, j: (i, 0, j)),
                   pl.BlockSpec((1, tm, C_KV_WIDTH), lambda i, j: (i, j, 0)),
                   pl.BlockSpec((1, tm // GQA_TKC, C_KV_WIDTH, GQA_TKC), lambda i, j: (i, j, 0, 0)),
                   pl.BlockSpec((1, tm, D_WIDTH), lambda i, j: (i, j, 0)),
                   pl.BlockSpec((1, C_KV_HEADS, 1, tm), lambda i, j: (i, 0, 0, j))],
        out_shape=[jax.ShapeDtypeStruct((b, C_WIDTH, n), BF16),
                   jax.ShapeDtypeStruct((b, n, C_KV_WIDTH), BF16),
                   jax.ShapeDtypeStruct((b, nc, C_KV_WIDTH, GQA_TKC), BF16),
                   jax.ShapeDtypeStruct((b, n, D_WIDTH), F32),
                   jax.ShapeDtypeStruct((b, C_KV_HEADS, 1, n), F32)],
        compiler_params=_cparams("parallel", "parallel"),
        name="proj_cd",
    )(x, wt, wu, gain, cos_t, sin_t)


def _gqa_kernel(qt_ref, k_ref, vt_ref, kn_ref, ot_ref, m_ref, l_ref, acc_ref, qpad_ref, *, tq, n):
    g = pl.program_id(1)
    row_half = lax.broadcasted_iota(jnp.int32, (LANES, 1), 0) // HEAD_DIM
    mine = row_half == (g % 2)
    k_max2 = jnp.max(kn_ref[0, 0], axis=1, keepdims=True)
    bound_max = jnp.zeros((1, 1), F32)
    for j in range(C_REP):
        qj = qt_ref[0, j * HEAD_DIM:(j + 1) * HEAD_DIM, :]
        q2 = jnp.concatenate([qj, qj], axis=0)
        qpad_ref[j] = jnp.where(mine, q2, jnp.zeros_like(q2))
        qf = qj.astype(F32)
        bound = jnp.sqrt(jnp.sum(qf * qf, axis=0, keepdims=True) * k_max2) * GQA_BOUND_SLACK
        m_ref[j] = bound
        bound_max = jnp.maximum(bound_max, jnp.max(bound, axis=1, keepdims=True))
    fixed_shift = bound_max[0, 0] <= GQA_BOUND_LIMIT
    acc_ref[...] = jnp.zeros(acc_ref.shape, F32)
    ones = jnp.ones((GQA_VROWS - HEAD_DIM, GQA_TKC), BF16)
    unroll = GQA_PIECES // (C_REP * (tq // GQA_TW))

    def body(c, carry, *, online):
        kchs, vchs = [], []
        for u in range(unroll):
            cc = c * unroll + u
            start = pl.multiple_of(cc * GQA_TKC, GQA_TKC)
            kchs.append(k_ref[0, pl.ds(start, GQA_TKC), :])
            vchs.append(jnp.concatenate([vt_ref[0, cc], ones], axis=0))
        pieces = [(u, j, slice(s * GQA_TW, (s + 1) * GQA_TW))
                  for u in range(unroll) for j in range(C_REP) for s in range(tq // GQA_TW)]

        def scores(i):
            u, j, cols = pieces[i]
            return _dot(kchs[u], qpad_ref[j, :, cols])

        pending = [scores(i) for i in range(GQA_AHEAD)]
        for i, (u, j, cols) in enumerate(pieces):
            st = pending.pop(0)
            if i + GQA_AHEAD < len(pieces):
                pending.append(scores(i + GQA_AHEAD))
            m_old = m_ref[j, :, cols]
            if not online:
                p = jnp.exp2(st - m_old)
                l_ref[j, :, cols] = l_ref[j, :, cols] + jnp.sum(p.reshape(GQA_TKC // 8, 8, GQA_TW), axis=0)
                acc_ref[j, :HEAD_DIM, cols] = acc_ref[j, :HEAD_DIM, cols] + _dot(
                    vchs[u][:HEAD_DIM], p.astype(BF16))
                continue
            m_new = jnp.maximum(m_old, jnp.max(st, axis=0, keepdims=True))
            alpha = jnp.exp2(m_old - m_new)
            p = jnp.exp2(st - m_new)
            acc_ref[j, :, cols] = alpha * acc_ref[j, :, cols] + _dot(vchs[u], p.astype(BF16))
            m_ref[j, :, cols] = m_new
        return carry

    trips = n // (GQA_TKC * unroll)

    @pl.when(fixed_shift)
    def _():
        l_ref[...] = jnp.zeros(l_ref.shape, F32)
        lax.fori_loop(0, trips, functools.partial(body, online=False), 0)
        for j in range(C_REP):
            acc_ref[j, HEAD_DIM:HEAD_DIM + 1, :] = jnp.sum(l_ref[j], axis=0, keepdims=True)

    @pl.when(jnp.logical_not(fixed_shift))
    def _():
        m_ref[...] = jnp.full(m_ref.shape, NEG_INF, F32)
        lax.fori_loop(0, trips, functools.partial(body, online=True), 0)

    for j in range(C_REP):
        l = acc_ref[j, HEAD_DIM:HEAD_DIM + 1, :]
        ot_ref[0, j * HEAD_DIM:(j + 1) * HEAD_DIM, :] = (acc_ref[j, :HEAD_DIM, :] / l).astype(BF16)


def _gqa(qt, k, vt, kn):
    b, _, n = qt.shape
    tq = min(GQA_TQ, n)
    nc = n // GQA_TKC
    rows = C_REP * HEAD_DIM
    return pl.pallas_call(
        functools.partial(_gqa_kernel, tq=tq, n=n),
        grid=(b, C_KV_HEADS, n // tq),
        in_specs=[pl.BlockSpec((1, rows, tq), lambda i, g, t: (i, g, t)),
                  pl.BlockSpec((1, n, LANES), lambda i, g, t: (i, 0, g // 2)),
                  pl.BlockSpec((1, nc, HEAD_DIM, GQA_TKC), lambda i, g, t: (i, 0, g, 0)),
                  pl.BlockSpec((1, 1, 1, n), lambda i, g, t: (i, g, 0, 0))],
        out_specs=pl.BlockSpec((1, rows, tq), lambda i, g, t: (i, g, t)),
        out_shape=jax.ShapeDtypeStruct((b, C_WIDTH, n), BF16),
        scratch_shapes=[pltpu.VMEM((C_REP, 1, tq), F32), pltpu.VMEM((C_REP, 8, tq), F32),
                        pltpu.VMEM((C_REP, GQA_VROWS, tq), F32), pltpu.VMEM((C_REP, LANES, tq), BF16)],
        compiler_params=_cparams("parallel", "parallel", "parallel"),
        name="gqa_flash",
    )(qt, k, vt, kn)


def _pool_tile(up_ref, uc_ref, un_ref, w_ref, scale_ref, buf, *, tm, n):
    i = pl.program_id(1)
    cur = uc_ref[0]
    buf[0:POOL_HALO, :] = jnp.where(i > 0, up_ref[0], 0.0)
    buf[POOL_HALO:POOL_HALO + tm, :] = cur
    buf[POOL_HALO + tm:, :] = jnp.where(i < pl.num_programs(1) - 1, un_ref[0], 0.0)
    lane_group = lax.broadcasted_iota(jnp.int32, (1, D_WIDTH), 1) // 64
    half_w = jnp.left_shift(1, lane_group)
    acc = jnp.zeros((tm, D_WIDTH), F32)
    for j in range(-POOL_HALO, POOL_HALO):
        inside = (j >= -half_w) & (j < half_w)
        acc = acc + jnp.where(inside, buf[POOL_HALO + j:POOL_HALO + j + tm, :], 0.0)
    t = i * tm + lax.broadcasted_iota(jnp.int32, (tm, 1), 0)
    cnt = jnp.minimum(t + half_w, n) - jnp.maximum(t - half_w, 0)
    mixed = (acc / cnt.astype(F32) - cur).astype(BF16)
    return (_dot(mixed, w_ref[...]) * scale_ref[...]).astype(BF16)


def _trunk(x, mem, rel_bias, ab_w_in, ab_fnet_g, ab_fnet_w, ab_w_out,
           cd_w_in, cd_q_norm, cd_k_norm, cd_pool_w, cd_pool_scale, cd_w_out,
           xa_w_q, xa_w_kv, xa_w_o, ffn_w_in, ffn_w_out, ln_g, ln_b):
    def tail_args(layer):
        return (ln_g[layer, 0], ln_b[layer, 0], mem, xa_w_q[layer], xa_w_kv[layer], xa_w_o[layer],
                ln_g[layer, 1], ln_b[layer, 1])

    for layer in range(DEPTH):
        i = layer // 2
        if layer % 2 == 0:
            w_in = ab_w_in[i]
            w_in = jnp.concatenate([w_in[:, :A_WIDTH] * (HEAD_DIM ** -0.5 * LOG2E), w_in[:, A_WIDTH:]], axis=1)
            *views, u = _proj_ab(x, w_in.astype(BF16))
            outs = [_dilated(view, rel_bias, d) for view, (_, d) in zip(views, A_PATTERNS)]
            o_b = _fnet(u, ab_fnet_g[i], ab_fnet_w[i])
            x = _tail_ab([o for o, _ in outs], [l for _, l in outs], o_b, ab_w_out[i], x, *tail_args(layer))
        else:
            qt, k, vt, u, kn = _proj_cd(x, cd_w_in[i], cd_q_norm[i], cd_k_norm[i])
            o_c = _gqa(qt, k, vt, kn)
            x = _tail_cd(o_c, u, cd_pool_w[i], cd_pool_scale[i], cd_w_out[i], x, *tail_args(layer))
        x = _swiglu(x, ffn_w_in[layer], ffn_w_out[layer], ln_g[layer, 2], ln_b[layer, 2])
    return x


def kernel(x_prompt, x_sample, mem_prompt, mem_sample, rel_bias, ab_w_in, ab_fnet_g, ab_fnet_w, ab_w_out, cd_w_in, cd_q_norm, cd_k_norm, cd_pool_w, cd_pool_scale, cd_w_out, xa_w_q, xa_w_kv, xa_w_o, ffn_w_in, ffn_w_out, ln_g, ln_b):
    params = (rel_bias, ab_w_in, ab_fnet_g, ab_fnet_w, ab_w_out,
              cd_w_in, cd_q_norm, cd_k_norm, cd_pool_w, cd_pool_scale, cd_w_out,
              xa_w_q, xa_w_kv, xa_w_o, ffn_w_in, ffn_w_out, ln_g, ln_b)
    return (_trunk(x_prompt, mem_prompt, *params), _trunk(x_sample, mem_sample, *params))
```

```python
import functools
import math

import numpy as np
import jax
import jax.numpy as jnp
from jax import lax
from jax.experimental import pallas as pl
from jax.experimental.pallas import tpu as pltpu

F32 = jnp.float32
BF16 = jnp.bfloat16

D_MODEL = 1024
HEAD_DIM = 64
GRID_W = 64
LN_EPS = 1e-5
RMS_EPS = 1e-6
NEG_INF = -1e30
DEPTH = 2
A_HEADS = 12
A_WIDTH = A_HEADS * HEAD_DIM
A_PATTERNS = ((128, 1), (512, 4), (2048, 16))
A_HALF = 64
QKV_WIDTH = 3 * A_WIDTH
N_BUCKETS = 32
REL_MAX_DIST = 1024
B_WIDTH = 256
C_Q_HEADS = 12
C_KV_HEADS = 4
C_REP = C_Q_HEADS // C_KV_HEADS
C_WIDTH = C_Q_HEADS * HEAD_DIM
C_KV_WIDTH = C_KV_HEADS * HEAD_DIM
ROPE_THETA = 10000.0
POOL_WINDOWS = (2, 4, 8, 16)
POOL_HALO = 8
D_WIDTH = 256
XA_HEADS = 4
XA_HEAD_DIM = D_MODEL // XA_HEADS
FFN_HIDDEN = 2816
DN_ALPHA = (2 * DEPTH) ** 0.25
LOG2E = 1.4426950408889634

LANES = 128
MXU_TILE = 256
BF16_ROWS = 16
VMEM_LIMIT = 56 * 1024 * 1024
TM = 512
FFT_N2 = 128
FFT_ROWS = 1024
FFT_KC = 16
FFT_AHEAD = 3
DIL_AHEAD = 3
DIL_TQ = 512
DIL_SQ = MXU_TILE // 2
DIL_SK = DIL_SQ + 2 * A_HALF
DIL_PIECES = 48
GQA_TQ = 8192
GQA_TW = MXU_TILE
GQA_TKC = 256
GQA_VROWS = HEAD_DIM + BF16_ROWS
GQA_PIECES = 96
GQA_AHEAD = 5
GQA_BOUND_SLACK = 1.0 + 2.0 ** -10
GQA_BOUND_LIMIT = 60.0


def _cparams(*sem):
    return pltpu.CompilerParams(dimension_semantics=sem, vmem_limit_bytes=VMEM_LIMIT)


def _dot(a, b):
    return jnp.dot(a, b, preferred_element_type=F32)


def _dot_nt(a, b):
    return lax.dot_general(a, b, (((1,), (1,)), ((), ())), preferred_element_type=F32)


def _dot_tn(a, b):
    return lax.dot_general(a, b, (((0,), (0,)), ((), ())), preferred_element_type=F32)


def _split(x):
    hi = x.astype(BF16)
    lo = (x - hi.astype(F32)).astype(BF16)
    return hi, lo


def _dot3(ah, al, bh, bl):
    return _dot(ah, bh) + _dot(al, bh) + _dot(ah, bl)


def _np_split(x):
    x = np.asarray(x, np.float32)
    hi = x.astype(BF16)
    lo = (x - hi.astype(np.float32)).astype(BF16)
    return jnp.asarray(hi), jnp.asarray(lo)


def _layer_norm(h, g, b):
    mu = jnp.mean(h, axis=-1, keepdims=True)
    xc = h - mu
    var = jnp.mean(xc * xc, axis=-1, keepdims=True)
    return xc * lax.rsqrt(var + LN_EPS) * g + b


def _proj_ab_kernel(x_ref, w_ref, qkv1_ref, qkv4_ref, qkv16_ref, u_ref, zbuf):
    xb = x_ref[0].astype(BF16)
    chunks = list(range(0, QKV_WIDTH, MXU_TILE))

    for c in chunks:
        z = _dot(xb, w_ref[:, c:c + MXU_TILE])
        zbuf[c // LANES] = z[:, :LANES]
        zbuf[c // LANES + 1] = z[:, LANES:]
        qkv1_ref[0, :, c:c + MXU_TILE] = z.astype(BF16)
    u_ref[0] = _dot(xb, w_ref[:, QKV_WIDTH:])
    for (_, d), ref in zip(A_PATTERNS[1:], (qkv4_ref, qkv16_ref)):
        rows = TM // d
        for r in range(d):
            for ct in range(QKV_WIDTH // LANES):
                col = r * QKV_WIDTH + ct * LANES
                ref[0, :, col:col + LANES] = zbuf[ct, pl.ds(r, rows, stride=d), :].astype(BF16)


def _proj_ab(x, w):
    b, n, _ = x.shape
    dils = [d for _, d in A_PATTERNS]
    return pl.pallas_call(
        _proj_ab_kernel,
        grid=(b, n // TM),
        in_specs=[pl.BlockSpec((1, TM, D_MODEL), lambda i, j: (i, j, 0)),
                  pl.BlockSpec(w.shape, lambda i, j: (0, 0))],
        out_specs=[pl.BlockSpec((1, TM // d, d * QKV_WIDTH), lambda i, j: (i, j, 0)) for d in dils]
        + [pl.BlockSpec((1, TM, B_WIDTH), lambda i, j: (i, j, 0))],
        out_shape=[jax.ShapeDtypeStruct((b, n // d, d * QKV_WIDTH), BF16) for d in dils]
        + [jax.ShapeDtypeStruct((b, n, B_WIDTH), F32)],
        scratch_shapes=[pltpu.VMEM((QKV_WIDTH // LANES, TM, LANES), F32)],
        compiler_params=_cparams("parallel", "parallel"),
        name="proj_ab",
    )(x, w)


def _t5_bucket_np(rel):
    nb = N_BUCKETS // 2
    max_exact = nb // 2
    ret = np.where(rel > 0, nb, 0)
    n = np.abs(rel)
    nf = np.maximum(n, 1).astype(np.float32)
    large = max_exact + (np.log(nf / max_exact) / math.log(REL_MAX_DIST / max_exact)
                         * (nb - max_exact)).astype(np.int32)
    large = np.minimum(large, nb - 1)
    return ret + np.where(n < max_exact, n, large)


def _band_bias(rel_bias, dilation):
    tq, tk = DIL_SQ, DIL_SK
    band = 2 * A_HALF + 1
    bucket = _t5_bucket_np((np.arange(band) - A_HALF) * dilation)
    row = (rel_bias[jnp.asarray(bucket)].T * LOG2E).astype(F32)
    row = jnp.concatenate([row, jnp.full((A_HEADS, tk + 1 - band), NEG_INF, F32)], axis=1)
    bias = jnp.tile(row, (1, tq))[:, :tq * tk].reshape(A_HEADS, tq, tk)
    bias = bias.transpose(0, 2, 1)
    bias = bias.reshape(A_HEADS // 2, 2, tk, tq).transpose(0, 2, 1, 3).reshape(A_HEADS // 2, tk, 2 * tq)
    key = np.arange(tk)[None, :, None]
    before = jnp.asarray(key < A_HALF)
    after = jnp.asarray(key >= A_HALF + tq)
    first = jnp.where(before, NEG_INF, bias)
    return jnp.stack([bias, first, jnp.where(after, NEG_INF, bias), jnp.where(after, NEG_INF, first)])


def _dilated_kernel(prev_ref, cur_ref, next_ref, bias_ref, o_ref, lse_ref, kbuf, vbuf, *, tq, n_res):
    for rr in range(n_res):
        for buf, c0 in ((kbuf, rr * QKV_WIDTH + A_WIDTH), (vbuf, rr * QKV_WIDTH + 2 * A_WIDTH)):
            buf[rr, 0:A_HALF, :] = prev_ref[0, :, c0:c0 + A_WIDTH]
            buf[rr, A_HALF:A_HALF + tq, :] = cur_ref[0, :, c0:c0 + A_WIDTH]
            buf[rr, A_HALF + tq:, :] = next_ref[0, :, c0:c0 + A_WIDTH]

    i = pl.program_id(1)
    n_sub = tq // DIL_SQ
    row_low = lax.broadcasted_iota(jnp.int32, (LANES, 1), 0) < HEAD_DIM
    row16 = lax.broadcasted_iota(jnp.int32, (BF16_ROWS, 1), 0)
    ones = jnp.ones((BF16_ROWS, DIL_SK), BF16)
    qts, vts = {}, {}
    pieces = [(rr, pair, sub) for rr in range(n_res) for pair in range(A_HEADS // 2) for sub in range(n_sub)]

    def variant(sub):
        v = 0
        if sub == 0:
            v = v + jnp.where(i == 0, 1, 0)
        if sub == n_sub - 1:
            v = v + jnp.where(i == pl.num_programs(1) - 1, 2, 0)
        return v

    def scores(idx):
        rr, pair, sub = pieces[idx]
        cols = slice(pair * LANES, (pair + 1) * LANES)
        if (rr, pair) not in qts:
            qcols = slice(rr * QKV_WIDTH + pair * LANES, rr * QKV_WIDTH + (pair + 1) * LANES)
            qts[rr, pair] = cur_ref[0, :, qcols].astype(F32).T
        qsub = qts[rr, pair][:, sub * DIL_SQ:(sub + 1) * DIL_SQ]
        rhs = jnp.concatenate([jnp.where(row_low, qsub, 0.0), jnp.where(row_low, 0.0, qsub)], axis=1)
        keys = kbuf[rr, sub * DIL_SQ:sub * DIL_SQ + DIL_SK, cols]
        return _dot(keys, rhs.astype(BF16)) + bias_ref[variant(sub), pair]

    pending = [scores(idx) for idx in range(DIL_AHEAD)]
    lse_t = {(rr, sub): jnp.zeros((BF16_ROWS, DIL_SQ), F32) for rr in range(n_res) for sub in range(n_sub)}
    for idx, (rr, pair, sub) in enumerate(pieces):
        st = pending.pop(0)
        if idx + DIL_AHEAD < len(pieces):
            pending.append(scores(idx + DIL_AHEAD))
        cols = slice(pair * LANES, (pair + 1) * LANES)
        m = jnp.max(st, axis=0, keepdims=True)
        p = jnp.exp2(st - m).astype(BF16)
        if (rr, pair) not in vts:
            vts[rr, pair] = vbuf[rr, :, cols].astype(F32).T.astype(BF16)
        vaug = jnp.concatenate([vts[rr, pair][:, sub * DIL_SQ:sub * DIL_SQ + DIL_SK], ones], axis=0)
        ot = _dot(vaug, p)
        l = ot[LANES:LANES + 1]
        lse2 = m + jnp.log2(l)
        o_pair = jnp.concatenate([ot[:HEAD_DIM, :DIL_SQ] / l[:, :DIL_SQ],
                                  ot[HEAD_DIM:LANES, DIL_SQ:] / l[:, DIL_SQ:]], axis=0)
        ocols = slice(rr * A_WIDTH + pair * LANES, rr * A_WIDTH + (pair + 1) * LANES)
        o_ref[0, sub * DIL_SQ:(sub + 1) * DIL_SQ, ocols] = o_pair.T.astype(BF16)
        lse_t[rr, sub] = jnp.where(row16 == 2 * pair, lse2[:, :DIL_SQ],
                                   jnp.where(row16 == 2 * pair + 1, lse2[:, DIL_SQ:], lse_t[rr, sub]))
    for (rr, sub), rows16 in lse_t.items():
        full = jnp.concatenate([rows16, jnp.zeros((LANES - BF16_ROWS, DIL_SQ), F32)], axis=0)
        lse_ref[0, sub * DIL_SQ:(sub + 1) * DIL_SQ, rr * LANES:(rr + 1) * LANES] = full.T


def _dilated(view, rel_bias, dilation):
    b, seq, _ = view.shape
    tq = min(DIL_TQ, seq)
    tk = tq + 2 * A_HALF
    n_res = min(dilation, max(1, DIL_PIECES // (A_HEADS // 2 * (tq // DIL_SQ))))
    bias = _band_bias(rel_bias, dilation)
    per = tq // A_HALF
    last = seq // A_HALF - 1
    width = n_res * QKV_WIDTH

    o, lse = pl.pallas_call(
        functools.partial(_dilated_kernel, tq=tq, n_res=n_res),
        grid=(b, seq // tq, dilation // n_res),
        in_specs=[pl.BlockSpec((1, A_HALF, width), lambda bi, i, r: (bi, jnp.maximum(i * per - 1, 0), r)),
                  pl.BlockSpec((1, tq, width), lambda bi, i, r: (bi, i, r)),
                  pl.BlockSpec((1, A_HALF, width), lambda bi, i, r: (bi, jnp.minimum((i + 1) * per, last), r)),
                  pl.BlockSpec(bias.shape, lambda bi, i, r: (0, 0, 0, 0))],
        out_specs=[pl.BlockSpec((1, tq, n_res * A_WIDTH), lambda bi, i, r: (bi, i, r)),
                   pl.BlockSpec((1, tq, n_res * LANES), lambda bi, i, r: (bi, i, r))],
        out_shape=[jax.ShapeDtypeStruct((b, seq, dilation * A_WIDTH), BF16),
                   jax.ShapeDtypeStruct((b, seq, dilation * LANES), F32)],
        scratch_shapes=[pltpu.VMEM((n_res, tk, A_WIDTH), BF16), pltpu.VMEM((n_res, tk, A_WIDTH), BF16)],
        compiler_params=_cparams("parallel", "parallel", "parallel"),
        name=f"dilated_d{dilation}",
    )(view, view, view, bias)
    return o, lse


def _mixture_tile(o1_ref, o4_ref, o16_ref, l1_ref, l4_ref, l16_ref, expand_ref, obuf, lbuf):
    for idx, (d, o_ref, l_ref) in enumerate(((A_PATTERNS[1][1], o4_ref, l4_ref), (A_PATTERNS[2][1], o16_ref, l16_ref))):
        rows = TM // d
        for r in range(d):
            for ct in range(A_WIDTH // LANES):
                col = r * A_WIDTH + ct * LANES
                obuf[idx, ct, pl.ds(r, rows, stride=d), :] = o_ref[0, :, col:col + LANES].astype(F32)
            lbuf[idx, pl.ds(r, rows, stride=d), :] = l_ref[0, :, r * LANES:(r + 1) * LANES]
    ls = [l1_ref[0], lbuf[0], lbuf[1]]
    mx = jnp.maximum(jnp.maximum(ls[0], ls[1]), ls[2])
    es = [jnp.exp2(l - mx) for l in ls]
    inv = 1.0 / (es[0] + es[1] + es[2])
    ws = []
    for e in es:
        wh, wl = _split(e * inv)
        ws.append(_dot(wh, expand_ref[...]) + _dot(wl, expand_ref[...]))
    pairs = []
    for pair in range(A_HEADS // 2):
        cols = slice(pair * LANES, (pair + 1) * LANES)
        os_ = [o1_ref[0, :, cols].astype(F32), obuf[0, pair], obuf[1, pair]]
        acc = None
        for g in range(3):
            t = ws[g][:, cols] * os_[g]
            acc = t if acc is None else acc + t
        pairs.append(acc.astype(BF16))
    return jnp.concatenate(pairs, axis=1)


def _group_mean_matrix():
    g = np.kron(np.eye(4), np.full((64, 64), 1.0 / 64))
    return jnp.asarray(g, BF16)


def _fnet_front_kernel(u_ref, gm_ref, gain_ref, ch_ref, cl_ref, sh_ref, sl_ref, m1h_ref, m1l_ref,
                       zr_ref, zi_ref, ybuf, zbuf, *, n1, tcn):
    rows = n1 * tcn
    u = u_ref[0].reshape(rows, B_WIDTH)
    gm = gm_ref[...]
    uh, ul = _split(u)
    mean = _dot(uh, gm) + _dot(ul, gm)
    xc = u - mean
    qh, ql = _split(xc * xc)
    var = _dot(qh, gm) + _dot(ql, gm)
    un = xc * lax.rsqrt(var + LN_EPS) * gain_ref[...]
    nh, nl = _split(un)
    yr = _dot3(nh, nl, ch_ref[...], cl_ref[...])
    yi = -_dot3(nh, nl, sh_ref[...], sl_ref[...])
    for q, val in enumerate((yr[:, :LANES], yr[:, LANES:], yi[:, :LANES], yi[:, LANES:])):
        ybuf[q] = val
    def first_stage(n2):
        col = [ybuf[q, pl.ds(n2, n1, stride=tcn), :] for q in range(4)]
        y = jnp.concatenate([jnp.concatenate(col[:2], axis=1), jnp.concatenate(col[2:], axis=1)], axis=0)
        yh, yl = _split(y)
        return _dot3(m1h_ref[...], m1l_ref[...], yh, yl)

    pending = [first_stage(n2) for n2 in range(min(FFT_AHEAD, tcn))]
    for n2 in range(tcn):
        z = pending.pop(0)
        if n2 + FFT_AHEAD < tcn:
            pending.append(first_stage(n2 + FFT_AHEAD))
        for q, val in enumerate((z[:n1, :LANES], z[:n1, LANES:], z[n1:, :LANES], z[n1:, LANES:])):
            zbuf[q, pl.ds(n2, n1, stride=tcn), :] = val
    zr_ref[0] = jnp.concatenate([zbuf[0], zbuf[1]], axis=1).reshape(n1, tcn, B_WIDTH)
    zi_ref[0] = jnp.concatenate([zbuf[2], zbuf[3]], axis=1).reshape(n1, tcn, B_WIDTH)


def _fnet_back_kernel(zr_ref, zi_ref, tch_ref, tcl_ref, tsh_ref, tsl_ref, w_ref, o_ref, obuf, *, kc, scale):
    n2 = FFT_N2

    def second_stage(kk):
        rh, rl = _split(zr_ref[0, kk])
        ih, il = _split(zi_ref[0, kk])
        return _dot3(tch_ref[kk], tcl_ref[kk], rh, rl) + _dot3(tsh_ref[kk], tsl_ref[kk], ih, il)

    pending = [second_stage(kk) for kk in range(FFT_AHEAD)]
    for kk in range(kc):
        f = pending.pop(0)
        if kk + FFT_AHEAD < kc:
            pending.append(second_stage(kk + FFT_AHEAD))
        o = _dot((f * scale).astype(BF16), w_ref[...])
        obuf[0, pl.ds(kk, n2, stride=kc), :] = o[:, :LANES]
        obuf[1, pl.ds(kk, n2, stride=kc), :] = o[:, LANES:]
    out = jnp.concatenate([obuf[0], obuf[1]], axis=1).reshape(n2, kc, B_WIDTH)
    o_ref[0] = out.astype(BF16)


def _fnet_tables(n):
    n2 = FFT_N2
    n1 = n // n2
    c = np.arange(64)
    ang = 2 * np.pi * np.outer(c, c) / 64
    cbd = np.kron(np.eye(4), np.cos(ang))
    sbd = np.kron(np.eye(4), np.sin(ang))
    k1 = np.arange(n1)
    ang1 = 2 * np.pi * np.outer(k1, k1) / n1
    c1, s1 = np.cos(ang1), np.sin(ang1)
    m1 = np.block([[c1, s1], [-s1, c1]])
    k2 = np.arange(n2)
    npr = k1[:, None, None] + n1 * k2[None, :, None]
    prod = (npr * k2[None, None, :]) % n
    ang2 = 2 * np.pi * prod / n
    return (_np_split(cbd), _np_split(sbd), _np_split(m1), _np_split(np.cos(ang2)), _np_split(np.sin(ang2)))


def _fnet(u, fnet_g, fnet_w):
    b, n, _ = u.shape
    n2 = FFT_N2
    n1 = n // n2
    tcn = FFT_ROWS // n1
    (ch, cl), (sh, sl), (m1h, m1l), (tch, tcl), (tsh, tsl) = _fnet_tables(n)
    gain = fnet_g.reshape(1, B_WIDTH)
    wbd = jax.scipy.linalg.block_diag(*[fnet_w[g] for g in range(4)]).astype(BF16)

    mat = pl.BlockSpec((B_WIDTH, B_WIDTH), lambda i, j: (0, 0))
    m1spec = pl.BlockSpec((2 * n1, 2 * n1), lambda i, j: (0, 0))
    tile = pl.BlockSpec((1, n1, tcn, B_WIDTH), lambda i, j: (i, 0, j, 0))
    zr, zi = pl.pallas_call(
        functools.partial(_fnet_front_kernel, n1=n1, tcn=tcn),
        grid=(b, n2 // tcn),
        in_specs=[tile, mat, pl.BlockSpec((1, B_WIDTH), lambda i, j: (0, 0)), mat, mat, mat, mat, m1spec, m1spec],
        out_specs=[tile, tile],
        out_shape=[jax.ShapeDtypeStruct((b, n1, n2, B_WIDTH), F32)] * 2,
        scratch_shapes=[pltpu.VMEM((4, FFT_ROWS, LANES), F32), pltpu.VMEM((4, FFT_ROWS, LANES), F32)],
        compiler_params=_cparams("parallel", "parallel"),
        name="fnet_front",
    )(u.reshape(b, n1, n2, B_WIDTH), _group_mean_matrix(), gain, ch, cl, sh, sl, m1h, m1l)

    kc = FFT_KC
    zspec = pl.BlockSpec((1, kc, n2, B_WIDTH), lambda i, j: (i, j, 0, 0))
    tspec = pl.BlockSpec((kc, n2, n2), lambda i, j: (j, 0, 0))
    out = pl.pallas_call(
        functools.partial(_fnet_back_kernel, kc=kc, scale=1.0 / math.sqrt(64.0 * n)),
        grid=(b, n1 // kc),
        in_specs=[zspec, zspec, tspec, tspec, tspec, tspec,
                  pl.BlockSpec((B_WIDTH, B_WIDTH), lambda i, j: (0, 0))],
        out_specs=pl.BlockSpec((1, n2, kc, B_WIDTH), lambda i, j: (i, 0, j, 0)),
        out_shape=jax.ShapeDtypeStruct((b, n2, n1, B_WIDTH), BF16),
        scratch_shapes=[pltpu.VMEM((2, n2 * kc, LANES), F32)],
        compiler_params=_cparams("parallel", "parallel"),
        name="fnet_back",
    )(zr, zi, tch, tcl, tsh, tsl, wbd)
    return out.reshape(b, n, B_WIDTH)


def _mem_kv_kernel(mem_ref, wk_ref, wvt_ref, k_ref, vt_ref):
    m = mem_ref[0].astype(BF16)
    k_ref[0] = _dot(m, wk_ref[...]).astype(BF16)
    vt_ref[0] = _dot_nt(wvt_ref[...], m).astype(BF16)


def _mem_kv(mem, w_kv):
    b, m, _ = mem.shape
    wk = w_kv[:, :D_MODEL].astype(BF16)
    wvt = w_kv[:, D_MODEL:].T.astype(BF16)
    wspec = pl.BlockSpec((D_MODEL, D_MODEL), lambda i: (0, 0))
    return pl.pallas_call(
        _mem_kv_kernel,
        grid=(b,),
        in_specs=[pl.BlockSpec((1, m, D_MODEL), lambda i: (i, 0, 0)), wspec, wspec],
        out_specs=[pl.BlockSpec((1, m, D_MODEL), lambda i: (i, 0, 0)),
                   pl.BlockSpec((1, D_MODEL, m), lambda i: (i, 0, 0))],
        out_shape=[jax.ShapeDtypeStruct((b, m, D_MODEL), BF16),
                   jax.ShapeDtypeStruct((b, D_MODEL, m), BF16)],
        compiler_params=_cparams("parallel"),
        name="mem_kv",
    )(mem, wk, wvt)


HALVES = (slice(0, TM // 2), slice(TM // 2, TM))


def _tail_after_mixer(hs, x_ref, g0_ref, b0_ref, wqt_ref, k_ref, vt_ref, wo_ref, g1_ref, b1_ref, o_ref):
    halves = HALVES
    x1s, qts = [], []
    for rows, h in zip(halves, hs):
        x1 = _layer_norm(DN_ALPHA * x_ref[0, rows, :] + h, g0_ref[...], b0_ref[...])
        x1s.append(x1)
        qts.append(_dot_nt(wqt_ref[...], x1.astype(BF16)).astype(BF16))
    qt = jnp.concatenate(qts, axis=1)
    heads = [slice(h * XA_HEAD_DIM, (h + 1) * XA_HEAD_DIM) for h in range(XA_HEADS)]
    scores = [_dot(k_ref[0, :, hd], qt[hd]) for hd in heads]
    ones = jnp.ones((BF16_ROWS, k_ref.shape[1]), BF16)
    outs = []
    for hd, st in zip(heads, scores):
        p = jnp.exp2(st - jnp.max(st, axis=0, keepdims=True)).astype(BF16)
        ot = _dot(jnp.concatenate([vt_ref[0, hd, :], ones], axis=0), p)
        outs.append((ot[:XA_HEAD_DIM] / ot[XA_HEAD_DIM:XA_HEAD_DIM + 1]).astype(BF16))
    ot_all = jnp.concatenate(outs, axis=0)
    hs = [_dot_tn(ot_all[:, rows], wo_ref[...]) for rows in halves]
    for rows, x1, h in zip(halves, x1s, hs):
        o_ref[0, rows, :] = _layer_norm(DN_ALPHA * x1 + h, g1_ref[...], b1_ref[...])


def _tail_ab_kernel(o1_ref, o4_ref, o16_ref, l1_ref, l4_ref, l16_ref, expand_ref, c_ref, wa_ref, wc_ref,
                    x_ref, g0_ref, b0_ref, wqt_ref, k_ref, vt_ref, wo_ref, g1_ref, b1_ref, o_ref, obuf, lbuf):
    o_a = _mixture_tile(o1_ref, o4_ref, o16_ref, l1_ref, l4_ref, l16_ref, expand_ref, obuf, lbuf)
    hs = [_dot(o_a[rows], wa_ref[...]) + _dot(c_ref[0, rows, :], wc_ref[...]) for rows in HALVES]
    _tail_after_mixer(hs, x_ref, g0_ref, b0_ref, wqt_ref, k_ref, vt_ref, wo_ref, g1_ref, b1_ref, o_ref)


def _tail_cd_kernel(at_ref, up_ref, uc_ref, un_ref, pw_ref, ps_ref, wa_ref, wc_ref,
                    x_ref, g0_ref, b0_ref, wqt_ref, k_ref, vt_ref, wo_ref, g1_ref, b1_ref, o_ref, pool_buf, *, n):
    o_d = _pool_tile(up_ref, uc_ref, un_ref, pw_ref, ps_ref, pool_buf, tm=TM, n=n)
    hs = [_dot_tn(at_ref[0, :, rows], wa_ref[...]) + _dot(o_d[rows], wc_ref[...]) for rows in HALVES]
    _tail_after_mixer(hs, x_ref, g0_ref, b0_ref, wqt_ref, k_ref, vt_ref, wo_ref, g1_ref, b1_ref, o_ref)


def _tail_specs(x, mem):
    m = mem.shape[1]
    xspec = pl.BlockSpec((1, TM, D_MODEL), lambda i, j: (i, j, 0))
    wspec = pl.BlockSpec((D_MODEL, D_MODEL), lambda i, j: (0, 0))
    vec = pl.BlockSpec((1, D_MODEL), lambda i, j: (0, 0))
    return [xspec, vec, vec, wspec, pl.BlockSpec((1, m, D_MODEL), lambda i, j: (i, 0, 0)),
            pl.BlockSpec((1, D_MODEL, m), lambda i, j: (i, 0, 0)), wspec, vec, vec], xspec


def _tail_operands(x, g0, b0, mem, w_q, w_kv, w_o, g1, b1):
    k, vt = _mem_kv(mem, w_kv)
    wqt = (w_q * (XA_HEAD_DIM ** -0.5 * LOG2E)).T.astype(BF16)
    row = lambda v: v.reshape(1, D_MODEL)
    return x, row(g0), row(b0), wqt, k, vt, w_o.astype(BF16), row(g1), row(b1)


def _tail_ab(os_, ls_, o_b, w_out, x, g0, b0, mem, w_q, w_kv, w_o, g1, b1):
    b, n, _ = x.shape
    dils = [d for _, d in A_PATTERNS]
    expand = jnp.asarray(np.arange(LANES)[:, None] == np.arange(A_WIDTH)[None, :] // HEAD_DIM, BF16)
    wa = w_out[:A_WIDTH].astype(BF16)
    wc = w_out[A_WIDTH:].astype(BF16)
    tail_specs, xspec = _tail_specs(x, mem)
    return pl.pallas_call(
        _tail_ab_kernel,
        grid=(b, n // TM),
        in_specs=[pl.BlockSpec((1, TM // d, d * A_WIDTH), lambda i, j: (i, j, 0)) for d in dils]
        + [pl.BlockSpec((1, TM // d, d * LANES), lambda i, j: (i, j, 0)) for d in dils]
        + [pl.BlockSpec((LANES, A_WIDTH), lambda i, j: (0, 0)),
           pl.BlockSpec((1, TM, B_WIDTH), lambda i, j: (i, j, 0)),
           pl.BlockSpec(wa.shape, lambda i, j: (0, 0)), pl.BlockSpec(wc.shape, lambda i, j: (0, 0))]
        + tail_specs,
        out_specs=xspec,
        out_shape=jax.ShapeDtypeStruct((b, n, D_MODEL), F32),
        scratch_shapes=[pltpu.VMEM((2, A_WIDTH // LANES, TM, LANES), F32), pltpu.VMEM((2, TM, LANES), F32)],
        compiler_params=_cparams("parallel", "parallel"),
        name="tail_ab",
    )(*os_, *ls_, expand, o_b, wa, wc, *_tail_operands(x, g0, b0, mem, w_q, w_kv, w_o, g1, b1))


def _tail_cd(o_ct, u, pool_w, pool_scale, w_out, x, g0, b0, mem, w_q, w_kv, w_o, g1, b1):
    b, n, _ = x.shape
    per = TM // POOL_HALO
    last = n // POOL_HALO - 1
    pool_wbd = jax.scipy.linalg.block_diag(*[pool_w[g] for g in range(4)]).astype(BF16)
    wa = w_out[:C_WIDTH].astype(BF16)
    wc = w_out[C_WIDTH:].astype(BF16)
    tail_specs, xspec = _tail_specs(x, mem)
    return pl.pallas_call(
        functools.partial(_tail_cd_kernel, n=n),
        grid=(b, n // TM),
        in_specs=[pl.BlockSpec((1, C_WIDTH, TM), lambda i, j: (i, 0, j)),
                  pl.BlockSpec((1, POOL_HALO, D_WIDTH), lambda i, j: (i, jnp.maximum(j * per - 1, 0), 0)),
                  pl.BlockSpec((1, TM, D_WIDTH), lambda i, j: (i, j, 0)),
                  pl.BlockSpec((1, POOL_HALO, D_WIDTH), lambda i, j: (i, jnp.minimum((j + 1) * per, last), 0)),
                  pl.BlockSpec((D_WIDTH, D_WIDTH), lambda i, j: (0, 0)),
                  pl.BlockSpec((1, D_WIDTH), lambda i, j: (0, 0)),
                  pl.BlockSpec(wa.shape, lambda i, j: (0, 0)), pl.BlockSpec(wc.shape, lambda i, j: (0, 0))]
        + tail_specs,
        out_specs=xspec,
        out_shape=jax.ShapeDtypeStruct((b, n, D_MODEL), F32),
        scratch_shapes=[pltpu.VMEM((TM + 2 * POOL_HALO, D_WIDTH), F32)],
        compiler_params=_cparams("parallel", "parallel"),
        name="tail_cd",
    )(o_ct, u, u, u, pool_wbd, pool_scale.reshape(1, D_WIDTH), wa, wc,
      *_tail_operands(x, g0, b0, mem, w_q, w_kv, w_o, g1, b1))


def _swiglu_kernel(x_ref, win_ref, wo_ref, g_ref, b_ref, o_ref):
    halves = [slice(0, TM // 2), slice(TM // 2, TM)]
    gate_up = []
    for rows in halves:
        xb = x_ref[0, rows, :].astype(BF16)
        gate_up.append((_dot(xb, win_ref[:, :FFN_HIDDEN]), _dot(xb, win_ref[:, FFN_HIDDEN:])))
    parts = []
    for gate, up in gate_up:
        hid = (gate * (1.0 / (1.0 + jnp.exp(-gate))) * up).astype(BF16)
        parts.append(_dot(hid, wo_ref[...]))
    for rows, part in zip(halves, parts):
        o_ref[0, rows, :] = _layer_norm(DN_ALPHA * x_ref[0, rows, :] + part, g_ref[...], b_ref[...])


def _swiglu(x, w_in, w_out, g, bias):
    b, n, _ = x.shape
    win = w_in.astype(BF16)
    wout = w_out.astype(BF16)
    xspec = pl.BlockSpec((1, TM, D_MODEL), lambda i, t: (i, t, 0))
    vec = pl.BlockSpec((1, D_MODEL), lambda i, t: (0, 0))
    resident = pl.Buffered(1)
    return pl.pallas_call(
        _swiglu_kernel,
        grid=(b, n // TM),
        in_specs=[xspec,
                  pl.BlockSpec(win.shape, lambda i, t: (0, 0), pipeline_mode=resident),
                  pl.BlockSpec(wout.shape, lambda i, t: (0, 0), pipeline_mode=resident), vec, vec],
        out_specs=xspec,
        out_shape=jax.ShapeDtypeStruct((b, n, D_MODEL), F32),
        compiler_params=_cparams("parallel", "parallel"),
        name="swiglu_ln",
    )(x, win, wout, g.reshape(1, D_MODEL), bias.reshape(1, D_MODEL))


N_QK_HEADS = C_Q_HEADS + C_KV_HEADS
QK_ROWS = N_QK_HEADS * HEAD_DIM


def _proj_cd_kernel(x_ref, wt_ref, wu_ref, gain_ref, cos_ref, sin_ref,
                    qt_ref, k_ref, vt_ref, u_ref, kn_ref, *, tm):
    xb = x_ref[0].astype(BF16)
    u_ref[0] = _dot(xb, wu_ref[...])
    zt = _dot_nt(wt_ref[...], xb)
    z = zt[:QK_ROWS].reshape(N_QK_HEADS, HEAD_DIM, tm)
    ssq = jnp.sum(z * z, axis=1, keepdims=True)
    zn = z * lax.rsqrt(ssq * (1.0 / HEAD_DIM) + RMS_EPS) * gain_ref[...]
    half = HEAD_DIM // 2
    x1 = zn[:, :half]
    x2 = zn[:, half:]
    c = cos_ref[...][None]
    s = sin_ref[...][None]
    rot = jnp.concatenate([x1 * c - x2 * s, x1 * s + x2 * c], axis=1).reshape(QK_ROWS, tm)
    qt_ref[0] = rot[:C_WIDTH].astype(BF16)
    kb = rot[C_WIDTH:].astype(BF16)
    k_ref[0] = kb.astype(F32).T.astype(BF16)
    kf = kb.astype(F32).reshape(C_KV_HEADS, HEAD_DIM, tm)
    kn_ref[0] = jnp.sum(kf * kf, axis=1, keepdims=True)
    vt = zt[QK_ROWS:].astype(BF16)
    for c0 in range(tm // GQA_TKC):
        vt_ref[0, c0] = vt[:, c0 * GQA_TKC:(c0 + 1) * GQA_TKC]


def _rope_tables_t(n):
    rows = n // GRID_W
    row_id = jnp.broadcast_to(jnp.arange(rows)[:, None], (rows, GRID_W)).reshape(n)
    col_id = jnp.broadcast_to(jnp.arange(GRID_W)[None, :], (rows, GRID_W)).reshape(n)
    axis_dim = HEAD_DIM // 2
    freqs = ROPE_THETA ** (-jnp.arange(0, axis_dim, 2, dtype=F32) / axis_dim)
    ang = jnp.concatenate([row_id[:, None] * freqs, col_id[:, None] * freqs], axis=-1)
    return jnp.cos(ang).T, jnp.sin(ang).T


def _proj_cd(x, w_in, q_norm, k_norm):
    b, n, _ = x.shape
    tm = TM
    half = HEAD_DIM // 2
    wqk = w_in[:, :QK_ROWS].reshape(D_MODEL, N_QK_HEADS, half, 2)
    wqk = wqk.transpose(1, 3, 2, 0).reshape(QK_ROWS, D_MODEL)
    wt = jnp.concatenate([wqk, w_in[:, QK_ROWS:QK_ROWS + C_KV_WIDTH].T], axis=0).astype(BF16)
    wu = w_in[:, QK_ROWS + C_KV_WIDTH:].astype(BF16)
    qg = q_norm.reshape(half, 2).T.reshape(HEAD_DIM) * (HEAD_DIM ** -0.5 * LOG2E)
    kg = k_norm.reshape(half, 2).T.reshape(HEAD_DIM)
    gain = jnp.concatenate([jnp.tile(qg[None], (C_Q_HEADS, 1)), jnp.tile(kg[None], (C_KV_HEADS, 1))])
    gain = gain.reshape(N_QK_HEADS, HEAD_DIM, 1).astype(F32)
    cos_t, sin_t = _rope_tables_t(n)
    nc = n // GQA_TKC
    return pl.pallas_call(
        functools.partial(_proj_cd_kernel, tm=tm),
        grid=(b, n // tm),
        in_specs=[pl.BlockSpec((1, tm, D_MODEL), lambda i, j: (i, j, 0)),
                  pl.BlockSpec(wt.shape, lambda i, j: (0, 0)),
                  pl.BlockSpec(wu.shape, lambda i, j: (0, 0)),
                  pl.BlockSpec(gain.shape, lambda i, j: (0, 0, 0)),
                  pl.BlockSpec((HEAD_DIM // 2, tm), lambda i, j: (0, j)),
                  pl.BlockSpec((HEAD_DIM // 2, tm), lambda i, j: (0, j))],
        out_specs=[pl.BlockSpec((1, C_WIDTH, tm), lambda i, j: (i, 0, j)),
                   pl.BlockSpec((1, tm, C_KV_WIDTH), lambda i, j: (i, j, 0)),
                   pl.BlockSpec((1, tm // GQA_TKC, C_KV_WIDTH, GQA_TKC), lambda i, j: (i, j, 0, 0)),
                   pl.BlockSpec((1, tm, D_WIDTH), lambda i, j: (i, j, 0)),
                   pl.BlockSpec((1, C_KV_HEADS, 1, tm), lambda i, j: (i, 0, 0, j))],
        out_shape=[jax.ShapeDtypeStruct((b, C_WIDTH, n), BF16),
                   jax.ShapeDtypeStruct((b, n, C_KV_WIDTH), BF16),
                   jax.ShapeDtypeStruct((b, nc, C_KV_WIDTH, GQA_TKC), BF16),
                   jax.ShapeDtypeStruct((b, n, D_WIDTH), F32),
                   jax.ShapeDtypeStruct((b, C_KV_HEADS, 1, n), F32)],
        compiler_params=_cparams("parallel", "parallel"),
        name="proj_cd",
    )(x, wt, wu, gain, cos_t, sin_t)


def _gqa_kernel(qt_ref, k_ref, vt_ref, kn_ref, ot_ref, m_ref, l_ref, acc_ref, qpad_ref, *, tq, n):
    g = pl.program_id(1)
    row_half = lax.broadcasted_iota(jnp.int32, (LANES, 1), 0) // HEAD_DIM
    mine = row_half == (g % 2)
    k_max2 = jnp.max(kn_ref[0, 0], axis=1, keepdims=True)
    bound_max = jnp.zeros((1, 1), F32)
    for j in range(C_REP):
        qj = qt_ref[0, j * HEAD_DIM:(j + 1) * HEAD_DIM, :]
        q2 = jnp.concatenate([qj, qj], axis=0)
        qpad_ref[j] = jnp.where(mine, q2, jnp.zeros_like(q2))
        qf = qj.astype(F32)
        bound = jnp.sqrt(jnp.sum(qf * qf, axis=0, keepdims=True) * k_max2) * GQA_BOUND_SLACK
        m_ref[j] = bound
        bound_max = jnp.maximum(bound_max, jnp.max(bound, axis=1, keepdims=True))
    fixed_shift = bound_max[0, 0] <= GQA_BOUND_LIMIT
    acc_ref[...] = jnp.zeros(acc_ref.shape, F32)
    ones = jnp.ones((GQA_VROWS - HEAD_DIM, GQA_TKC), BF16)
    unroll = GQA_PIECES // (C_REP * (tq // GQA_TW))

    def body(c, carry, *, online):
        kchs, vchs = [], []
        for u in range(unroll):
            cc = c * unroll + u
            start = pl.multiple_of(cc * GQA_TKC, GQA_TKC)
            kchs.append(k_ref[0, pl.ds(start, GQA_TKC), :])
            vchs.append(jnp.concatenate([vt_ref[0, cc], ones], axis=0))
        pieces = [(u, j, slice(s * GQA_TW, (s + 1) * GQA_TW))
                  for u in range(unroll) for j in range(C_REP) for s in range(tq // GQA_TW)]

        def scores(i):
            u, j, cols = pieces[i]
            return _dot(kchs[u], qpad_ref[j, :, cols])

        pending = [scores(i) for i in range(GQA_AHEAD)]
        for i, (u, j, cols) in enumerate(pieces):
            st = pending.pop(0)
            if i + GQA_AHEAD < len(pieces):
                pending.append(scores(i + GQA_AHEAD))
            m_old = m_ref[j, :, cols]
            if not online:
                p = jnp.exp2(st - m_old)
                l_ref[j, :, cols] = l_ref[j, :, cols] + jnp.sum(p.reshape(GQA_TKC // 8, 8, GQA_TW), axis=0)
                acc_ref[j, :HEAD_DIM, cols] = acc_ref[j, :HEAD_DIM, cols] + _dot(
                    vchs[u][:HEAD_DIM], p.astype(BF16))
                continue
            m_new = jnp.maximum(m_old, jnp.max(st, axis=0, keepdims=True))
            alpha = jnp.exp2(m_old - m_new)
            p = jnp.exp2(st - m_new)
            acc_ref[j, :, cols] = alpha * acc_ref[j, :, cols] + _dot(vchs[u], p.astype(BF16))
            m_ref[j, :, cols] = m_new
        return carry

    trips = n // (GQA_TKC * unroll)

    @pl.when(fixed_shift)
    def _():
        l_ref[...] = jnp.zeros(l_ref.shape, F32)
        lax.fori_loop(0, trips, functools.partial(body, online=False), 0)
        for j in range(C_REP):
            acc_ref[j, HEAD_DIM:HEAD_DIM + 1, :] = jnp.sum(l_ref[j], axis=0, keepdims=True)

    @pl.when(jnp.logical_not(fixed_shift))
    def _():
        m_ref[...] = jnp.full(m_ref.shape, NEG_INF, F32)
        lax.fori_loop(0, trips, functools.partial(body, online=True), 0)

    for j in range(C_REP):
        l = acc_ref[j, HEAD_DIM:HEAD_DIM + 1, :]
        ot_ref[0, j * HEAD_DIM:(j + 1) * HEAD_DIM, :] = (acc_ref[j, :HEAD_DIM, :] / l).astype(BF16)


def _gqa(qt, k, vt, kn):
    b, _, n = qt.shape
    tq = min(GQA_TQ, n)
    nc = n // GQA_TKC
    rows = C_REP * HEAD_DIM
    return pl.pallas_call(
        functools.partial(_gqa_kernel, tq=tq, n=n),
        grid=(b, C_KV_HEADS, n // tq),
        in_specs=[pl.BlockSpec((1, rows, tq), lambda i, g, t: (i, g, t)),
                  pl.BlockSpec((1, n, LANES), lambda i, g, t: (i, 0, g // 2)),
                  pl.BlockSpec((1, nc, HEAD_DIM, GQA_TKC), lambda i, g, t: (i, 0, g, 0)),
                  pl.BlockSpec((1, 1, 1, n), lambda i, g, t: (i, g, 0, 0))],
        out_specs=pl.BlockSpec((1, rows, tq), lambda i, g, t: (i, g, t)),
        out_shape=jax.ShapeDtypeStruct((b, C_WIDTH, n), BF16),
        scratch_shapes=[pltpu.VMEM((C_REP, 1, tq), F32), pltpu.VMEM((C_REP, 8, tq), F32),
                        pltpu.VMEM((C_REP, GQA_VROWS, tq), F32), pltpu.VMEM((C_REP, LANES, tq), BF16)],
        compiler_params=_cparams("parallel", "parallel", "parallel"),
        name="gqa_flash",
    )(qt, k, vt, kn)


def _pool_tile(up_ref, uc_ref, un_ref, w_ref, scale_ref, buf, *, tm, n):
    i = pl.program_id(1)
    cur = uc_ref[0]
    buf[0:POOL_HALO, :] = jnp.where(i > 0, up_ref[0], 0.0)
    buf[POOL_HALO:POOL_HALO + tm, :] = cur
    buf[POOL_HALO + tm:, :] = jnp.where(i < pl.num_programs(1) - 1, un_ref[0], 0.0)
    lane_group = lax.broadcasted_iota(jnp.int32, (1, D_WIDTH), 1) // 64
    half_w = jnp.left_shift(1, lane_group)
    acc = jnp.zeros((tm, D_WIDTH), F32)
    for j in range(-POOL_HALO, POOL_HALO):
        inside = (j >= -half_w) & (j < half_w)
        acc = acc + jnp.where(inside, buf[POOL_HALO + j:POOL_HALO + j + tm, :], 0.0)
    t = i * tm + lax.broadcasted_iota(jnp.int32, (tm, 1), 0)
    cnt = jnp.minimum(t + half_w, n) - jnp.maximum(t - half_w, 0)
    mixed = (acc / cnt.astype(F32) - cur).astype(BF16)
    return (_dot(mixed, w_ref[...]) * scale_ref[...]).astype(BF16)


def _trunk(x, mem, rel_bias, ab_w_in, ab_fnet_g, ab_fnet_w, ab_w_out,
           cd_w_in, cd_q_norm, cd_k_norm, cd_pool_w, cd_pool_scale, cd_w_out,
           xa_w_q, xa_w_kv, xa_w_o, ffn_w_in, ffn_w_out, ln_g, ln_b):
    def tail_args(layer):
        return (ln_g[layer, 0], ln_b[layer, 0], mem, xa_w_q[layer], xa_w_kv[layer], xa_w_o[layer],
                ln_g[layer, 1], ln_b[layer, 1])

    for layer in range(DEPTH):
        i = layer // 2
        if layer % 2 == 0:
            w_in = ab_w_in[i]
            w_in = jnp.concatenate([w_in[:, :A_WIDTH] * (HEAD_DIM ** -0.5 * LOG2E), w_in[:, A_WIDTH:]], axis=1)
            *views, u = _proj_ab(x, w_in.astype(BF16))
            outs = [_dilated(view, rel_bias, d) for view, (_, d) in zip(views, A_PATTERNS)]
            o_b = _fnet(u, ab_fnet_g[i], ab_fnet_w[i])
            x = _tail_ab([o for o, _ in outs], [l for _, l in outs], o_b, ab_w_out[i], x, *tail_args(layer))
        else:
            qt, k, vt, u, kn = _proj_cd(x, cd_w_in[i], cd_q_norm[i], cd_k_norm[i])
            o_c = _gqa(qt, k, vt, kn)
            x = _tail_cd(o_c, u, cd_pool_w[i], cd_pool_scale[i], cd_w_out[i], x, *tail_args(layer))
        x = _swiglu(x, ffn_w_in[layer], ffn_w_out[layer], ln_g[layer, 2], ln_b[layer, 2])
    return x


def kernel(x_prompt, x_sample, mem_prompt, mem_sample, rel_bias, ab_w_in, ab_fnet_g, ab_fnet_w, ab_w_out, cd_w_in, cd_q_norm, cd_k_norm, cd_pool_w, cd_pool_scale, cd_w_out, xa_w_q, xa_w_kv, xa_w_o, ffn_w_in, ffn_w_out, ln_g, ln_b):
    params = (rel_bias, ab_w_in, ab_fnet_g, ab_fnet_w, ab_w_out,
              cd_w_in, cd_q_norm, cd_k_norm, cd_pool_w, cd_pool_scale, cd_w_out,
              xa_w_q, xa_w_kv, xa_w_o, ffn_w_in, ffn_w_out, ln_g, ln_b)
    return (_trunk(x_prompt, mem_prompt, *params), _trunk(x_sample, mem_sample, *params))
```

```python
import functools
import math

import numpy as np
import jax
import jax.numpy as jnp
from jax import lax
from jax.experimental import pallas as pl
from jax.experimental.pallas import tpu as pltpu

F32 = jnp.float32
BF16 = jnp.bfloat16

D_MODEL = 1024
HEAD_DIM = 64
GRID_W = 64
LN_EPS = 1e-5
RMS_EPS = 1e-6
NEG_INF = -1e30
DEPTH = 2
A_HEADS = 12
A_WIDTH = A_HEADS * HEAD_DIM
A_PATTERNS = ((128, 1), (512, 4), (2048, 16))
A_HALF = 64
QKV_WIDTH = 3 * A_WIDTH
N_BUCKETS = 32
REL_MAX_DIST = 1024
B_WIDTH = 256
C_Q_HEADS = 12
C_KV_HEADS = 4
C_REP = C_Q_HEADS // C_KV_HEADS
C_WIDTH = C_Q_HEADS * HEAD_DIM
C_KV_WIDTH = C_KV_HEADS * HEAD_DIM
ROPE_THETA = 10000.0
POOL_WINDOWS = (2, 4, 8, 16)
POOL_HALO = 8
D_WIDTH = 256
XA_HEADS = 4
XA_HEAD_DIM = D_MODEL // XA_HEADS
FFN_HIDDEN = 2816
DN_ALPHA = (2 * DEPTH) ** 0.25
LOG2E = 1.4426950408889634

LANES = 128
MXU_TILE = 256
BF16_ROWS = 16
VMEM_LIMIT = 56 * 1024 * 1024
TM = 512
FFT_N2 = 128
FFT_ROWS = 1024
FFT_KC = 16
FFT_AHEAD = 3
DIL_AHEAD = 3
DIL_TQ = 512
DIL_SQ = MXU_TILE // 2
DIL_SK = DIL_SQ + 2 * A_HALF
DIL_PIECES = 48
GQA_TQ = 8192
GQA_TW = MXU_TILE
GQA_TKC = 256
GQA_VROWS = HEAD_DIM + BF16_ROWS
GQA_PIECES = 192
GQA_AHEAD = 5
GQA_BOUND_SLACK = 1.0 + 2.0 ** -10
GQA_BOUND_LIMIT = 60.0


def _cparams(*sem):
    return pltpu.CompilerParams(dimension_semantics=sem, vmem_limit_bytes=VMEM_LIMIT)


def _dot(a, b):
    return jnp.dot(a, b, preferred_element_type=F32)


def _dot_nt(a, b):
    return lax.dot_general(a, b, (((1,), (1,)), ((), ())), preferred_element_type=F32)


def _dot_tn(a, b):
    return lax.dot_general(a, b, (((0,), (0,)), ((), ())), preferred_element_type=F32)


def _split(x):
    hi = x.astype(BF16)
    lo = (x - hi.astype(F32)).astype(BF16)
    return hi, lo


def _dot3(ah, al, bh, bl):
    return _dot(ah, bh) + _dot(al, bh) + _dot(ah, bl)


def _np_split(x):
    x = np.asarray(x, np.float32)
    hi = x.astype(BF16)
    lo = (x - hi.astype(np.float32)).astype(BF16)
    return jnp.asarray(hi), jnp.asarray(lo)


def _layer_norm(h, g, b):
    mu = jnp.mean(h, axis=-1, keepdims=True)
    xc = h - mu
    var = jnp.mean(xc * xc, axis=-1, keepdims=True)
    return xc * lax.rsqrt(var + LN_EPS) * g + b


def _proj_ab_kernel(x_ref, w_ref, qkv1_ref, qkv4_ref, qkv16_ref, u_ref, zbuf):
    xb = x_ref[0].astype(BF16)
    chunks = list(range(0, QKV_WIDTH, MXU_TILE))

    for c in chunks:
        z = _dot(xb, w_ref[:, c:c + MXU_TILE])
        zbuf[c // LANES] = z[:, :LANES]
        zbuf[c // LANES + 1] = z[:, LANES:]
        qkv1_ref[0, :, c:c + MXU_TILE] = z.astype(BF16)
    u_ref[0] = _dot(xb, w_ref[:, QKV_WIDTH:])
    for (_, d), ref in zip(A_PATTERNS[1:], (qkv4_ref, qkv16_ref)):
        rows = TM // d
        for r in range(d):
            for ct in range(QKV_WIDTH // LANES):
                col = r * QKV_WIDTH + ct * LANES
                ref[0, :, col:col + LANES] = zbuf[ct, pl.ds(r, rows, stride=d), :].astype(BF16)


def _proj_ab(x, w):
    b, n, _ = x.shape
    dils = [d for _, d in A_PATTERNS]
    return pl.pallas_call(
        _proj_ab_kernel,
        grid=(b, n // TM),
        in_specs=[pl.BlockSpec((1, TM, D_MODEL), lambda i, j: (i, j, 0)),
                  pl.BlockSpec(w.shape, lambda i, j: (0, 0))],
        out_specs=[pl.BlockSpec((1, TM // d, d * QKV_WIDTH), lambda i, j: (i, j, 0)) for d in dils]
        + [pl.BlockSpec((1, TM, B_WIDTH), lambda i, j: (i, j, 0))],
        out_shape=[jax.ShapeDtypeStruct((b, n // d, d * QKV_WIDTH), BF16) for d in dils]
        + [jax.ShapeDtypeStruct((b, n, B_WIDTH), F32)],
        scratch_shapes=[pltpu.VMEM((QKV_WIDTH // LANES, TM, LANES), F32)],
        compiler_params=_cparams("parallel", "parallel"),
        name="proj_ab",
    )(x, w)


def _t5_bucket_np(rel):
    nb = N_BUCKETS // 2
    max_exact = nb // 2
    ret = np.where(rel > 0, nb, 0)
    n = np.abs(rel)
    nf = np.maximum(n, 1).astype(np.float32)
    large = max_exact + (np.log(nf / max_exact) / math.log(REL_MAX_DIST / max_exact)
                         * (nb - max_exact)).astype(np.int32)
    large = np.minimum(large, nb - 1)
    return ret + np.where(n < max_exact, n, large)


def _band_bias(rel_bias, dilation):
    tq, tk = DIL_SQ, DIL_SK
    band = 2 * A_HALF + 1
    bucket = _t5_bucket_np((np.arange(band) - A_HALF) * dilation)
    row = (rel_bias[jnp.asarray(bucket)].T * LOG2E).astype(F32)
    row = jnp.concatenate([row, jnp.full((A_HEADS, tk + 1 - band), NEG_INF, F32)], axis=1)
    bias = jnp.tile(row, (1, tq))[:, :tq * tk].reshape(A_HEADS, tq, tk)
    bias = bias.transpose(0, 2, 1)
    bias = bias.reshape(A_HEADS // 2, 2, tk, tq).transpose(0, 2, 1, 3).reshape(A_HEADS // 2, tk, 2 * tq)
    key = np.arange(tk)[None, :, None]
    before = jnp.asarray(key < A_HALF)
    after = jnp.asarray(key >= A_HALF + tq)
    first = jnp.where(before, NEG_INF, bias)
    return jnp.stack([bias, first, jnp.where(after, NEG_INF, bias), jnp.where(after, NEG_INF, first)])


def _dilated_kernel(prev_ref, cur_ref, next_ref, bias_ref, o_ref, lse_ref, kbuf, vbuf, *, tq, n_res):
    for rr in range(n_res):
        for buf, c0 in ((kbuf, rr * QKV_WIDTH + A_WIDTH), (vbuf, rr * QKV_WIDTH + 2 * A_WIDTH)):
            buf[rr, 0:A_HALF, :] = prev_ref[0, :, c0:c0 + A_WIDTH]
            buf[rr, A_HALF:A_HALF + tq, :] = cur_ref[0, :, c0:c0 + A_WIDTH]
            buf[rr, A_HALF + tq:, :] = next_ref[0, :, c0:c0 + A_WIDTH]

    i = pl.program_id(1)
    n_sub = tq // DIL_SQ
    row_low = lax.broadcasted_iota(jnp.int32, (LANES, 1), 0) < HEAD_DIM
    row16 = lax.broadcasted_iota(jnp.int32, (BF16_ROWS, 1), 0)
    ones = jnp.ones((BF16_ROWS, DIL_SK), BF16)
    qts, vts = {}, {}
    pieces = [(rr, pair, sub) for rr in range(n_res) for pair in range(A_HEADS // 2) for sub in range(n_sub)]

    def variant(sub):
        v = 0
        if sub == 0:
            v = v + jnp.where(i == 0, 1, 0)
        if sub == n_sub - 1:
            v = v + jnp.where(i == pl.num_programs(1) - 1, 2, 0)
        return v

    def scores(idx):
        rr, pair, sub = pieces[idx]
        cols = slice(pair * LANES, (pair + 1) * LANES)
        if (rr, pair) not in qts:
            qcols = slice(rr * QKV_WIDTH + pair * LANES, rr * QKV_WIDTH + (pair + 1) * LANES)
            qts[rr, pair] = cur_ref[0, :, qcols].astype(F32).T
        qsub = qts[rr, pair][:, sub * DIL_SQ:(sub + 1) * DIL_SQ]
        rhs = jnp.concatenate([jnp.where(row_low, qsub, 0.0), jnp.where(row_low, 0.0, qsub)], axis=1)
        keys = kbuf[rr, sub * DIL_SQ:sub * DIL_SQ + DIL_SK, cols]
        return _dot(keys, rhs.astype(BF16)) + bias_ref[variant(sub), pair]

    pending = [scores(idx) for idx in range(DIL_AHEAD)]
    lse_t = {(rr, sub): jnp.zeros((BF16_ROWS, DIL_SQ), F32) for rr in range(n_res) for sub in range(n_sub)}
    for idx, (rr, pair, sub) in enumerate(pieces):
        st = pending.pop(0)
        if idx + DIL_AHEAD < len(pieces):
            pending.append(scores(idx + DIL_AHEAD))
        cols = slice(pair * LANES, (pair + 1) * LANES)
        m = jnp.max(st, axis=0, keepdims=True)
        p = jnp.exp2(st - m).astype(BF16)
        if (rr, pair) not in vts:
            vts[rr, pair] = vbuf[rr, :, cols].astype(F32).T.astype(BF16)
        vaug = jnp.concatenate([vts[rr, pair][:, sub * DIL_SQ:sub * DIL_SQ + DIL_SK], ones], axis=0)
        ot = _dot(vaug, p)
        l = ot[LANES:LANES + 1]
        lse2 = m + jnp.log2(l)
        o_pair = jnp.concatenate([ot[:HEAD_DIM, :DIL_SQ] / l[:, :DIL_SQ],
                                  ot[HEAD_DIM:LANES, DIL_SQ:] / l[:, DIL_SQ:]], axis=0)
        ocols = slice(rr * A_WIDTH + pair * LANES, rr * A_WIDTH + (pair + 1) * LANES)
        o_ref[0, sub * DIL_SQ:(sub + 1) * DIL_SQ, ocols] = o_pair.T.astype(BF16)
        lse_t[rr, sub] = jnp.where(row16 == 2 * pair, lse2[:, :DIL_SQ],
                                   jnp.where(row16 == 2 * pair + 1, lse2[:, DIL_SQ:], lse_t[rr, sub]))
    for (rr, sub), rows16 in lse_t.items():
        full = jnp.concatenate([rows16, jnp.zeros((LANES - BF16_ROWS, DIL_SQ), F32)], axis=0)
        lse_ref[0, sub * DIL_SQ:(sub + 1) * DIL_SQ, rr * LANES:(rr + 1) * LANES] = full.T


def _dilated(view, rel_bias, dilation):
    b, seq, _ = view.shape
    tq = min(DIL_TQ, seq)
    tk = tq + 2 * A_HALF
    n_res = min(dilation, max(1, DIL_PIECES // (A_HEADS // 2 * (tq // DIL_SQ))))
    bias = _band_bias(rel_bias, dilation)
    per = tq // A_HALF
    last = seq // A_HALF - 1
    width = n_res * QKV_WIDTH

    o, lse = pl.pallas_call(
        functools.partial(_dilated_kernel, tq=tq, n_res=n_res),
        grid=(b, seq // tq, dilation // n_res),
        in_specs=[pl.BlockSpec((1, A_HALF, width), lambda bi, i, r: (bi, jnp.maximum(i * per - 1, 0), r)),
                  pl.BlockSpec((1, tq, width), lambda bi, i, r: (bi, i, r)),
                  pl.BlockSpec((1, A_HALF, width), lambda bi, i, r: (bi, jnp.minimum((i + 1) * per, last), r)),
                  pl.BlockSpec(bias.shape, lambda bi, i, r: (0, 0, 0, 0))],
        out_specs=[pl.BlockSpec((1, tq, n_res * A_WIDTH), lambda bi, i, r: (bi, i, r)),
                   pl.BlockSpec((1, tq, n_res * LANES), lambda bi, i, r: (bi, i, r))],
        out_shape=[jax.ShapeDtypeStruct((b, seq, dilation * A_WIDTH), BF16),
                   jax.ShapeDtypeStruct((b, seq, dilation * LANES), F32)],
        scratch_shapes=[pltpu.VMEM((n_res, tk, A_WIDTH), BF16), pltpu.VMEM((n_res, tk, A_WIDTH), BF16)],
        compiler_params=_cparams("parallel", "parallel", "parallel"),
        name=f"dilated_d{dilation}",
    )(view, view, view, bias)
    return o, lse


def _mixture_tile(o1_ref, o4_ref, o16_ref, l1_ref, l4_ref, l16_ref, expand_ref, obuf, lbuf):
    for idx, (d, o_ref, l_ref) in enumerate(((A_PATTERNS[1][1], o4_ref, l4_ref), (A_PATTERNS[2][1], o16_ref, l16_ref))):
        rows = TM // d
        for r in range(d):
            for ct in range(A_WIDTH // LANES):
                col = r * A_WIDTH + ct * LANES
                obuf[idx, ct, pl.ds(r, rows, stride=d), :] = o_ref[0, :, col:col + LANES].astype(F32)
            lbuf[idx, pl.ds(r, rows, stride=d), :] = l_ref[0, :, r * LANES:(r + 1) * LANES]
    ls = [l1_ref[0], lbuf[0], lbuf[1]]
    mx = jnp.maximum(jnp.maximum(ls[0], ls[1]), ls[2])
    es = [jnp.exp2(l - mx) for l in ls]
    inv = 1.0 / (es[0] + es[1] + es[2])
    ws = []
    for e in es:
        wh, wl = _split(e * inv)
        ws.append(_dot(wh, expand_ref[...]) + _dot(wl, expand_ref[...]))
    pairs = []
    for pair in range(A_HEADS // 2):
        cols = slice(pair * LANES, (pair + 1) * LANES)
        os_ = [o1_ref[0, :, cols].astype(F32), obuf[0, pair], obuf[1, pair]]
        acc = None
        for g in range(3):
            t = ws[g][:, cols] * os_[g]
            acc = t if acc is None else acc + t
        pairs.append(acc.astype(BF16))
    return jnp.concatenate(pairs, axis=1)


def _group_mean_matrix():
    g = np.kron(np.eye(4), np.full((64, 64), 1.0 / 64))
    return jnp.asarray(g, BF16)


def _fnet_front_kernel(u_ref, gm_ref, gain_ref, ch_ref, cl_ref, sh_ref, sl_ref, m1h_ref, m1l_ref,
                       zr_ref, zi_ref, ybuf, zbuf, *, n1, tcn):
    rows = n1 * tcn
    u = u_ref[0].reshape(rows, B_WIDTH)
    gm = gm_ref[...]
    uh, ul = _split(u)
    mean = _dot(uh, gm) + _dot(ul, gm)
    xc = u - mean
    qh, ql = _split(xc * xc)
    var = _dot(qh, gm) + _dot(ql, gm)
    un = xc * lax.rsqrt(var + LN_EPS) * gain_ref[...]
    nh, nl = _split(un)
    yr = _dot3(nh, nl, ch_ref[...], cl_ref[...])
    yi = -_dot3(nh, nl, sh_ref[...], sl_ref[...])
    for q, val in enumerate((yr[:, :LANES], yr[:, LANES:], yi[:, :LANES], yi[:, LANES:])):
        ybuf[q] = val
    def first_stage(n2):
        col = [ybuf[q, pl.ds(n2, n1, stride=tcn), :] for q in range(4)]
        y = jnp.concatenate([jnp.concatenate(col[:2], axis=1), jnp.concatenate(col[2:], axis=1)], axis=0)
        yh, yl = _split(y)
        return _dot3(m1h_ref[...], m1l_ref[...], yh, yl)

    pending = [first_stage(n2) for n2 in range(min(FFT_AHEAD, tcn))]
    for n2 in range(tcn):
        z = pending.pop(0)
        if n2 + FFT_AHEAD < tcn:
            pending.append(first_stage(n2 + FFT_AHEAD))
        for q, val in enumerate((z[:n1, :LANES], z[:n1, LANES:], z[n1:, :LANES], z[n1:, LANES:])):
            zbuf[q, pl.ds(n2, n1, stride=tcn), :] = val
    zr_ref[0] = jnp.concatenate([zbuf[0], zbuf[1]], axis=1).reshape(n1, tcn, B_WIDTH)
    zi_ref[0] = jnp.concatenate([zbuf[2], zbuf[3]], axis=1).reshape(n1, tcn, B_WIDTH)


def _fnet_back_kernel(zr_ref, zi_ref, tch_ref, tcl_ref, tsh_ref, tsl_ref, w_ref, o_ref, obuf, *, kc, scale):
    n2 = FFT_N2

    def second_stage(kk):
        rh, rl = _split(zr_ref[0, kk])
        ih, il = _split(zi_ref[0, kk])
        return _dot3(tch_ref[kk], tcl_ref[kk], rh, rl) + _dot3(tsh_ref[kk], tsl_ref[kk], ih, il)

    pending = [second_stage(kk) for kk in range(FFT_AHEAD)]
    for kk in range(kc):
        f = pending.pop(0)
        if kk + FFT_AHEAD < kc:
            pending.append(second_stage(kk + FFT_AHEAD))
        o = _dot((f * scale).astype(BF16), w_ref[...])
        obuf[0, pl.ds(kk, n2, stride=kc), :] = o[:, :LANES]
        obuf[1, pl.ds(kk, n2, stride=kc), :] = o[:, LANES:]
    out = jnp.concatenate([obuf[0], obuf[1]], axis=1).reshape(n2, kc, B_WIDTH)
    o_ref[0] = out.astype(BF16)


def _fnet_tables(n):
    n2 = FFT_N2
    n1 = n // n2
    c = np.arange(64)
    ang = 2 * np.pi * np.outer(c, c) / 64
    cbd = np.kron(np.eye(4), np.cos(ang))
    sbd = np.kron(np.eye(4), np.sin(ang))
    k1 = np.arange(n1)
    ang1 = 2 * np.pi * np.outer(k1, k1) / n1
    c1, s1 = np.cos(ang1), np.sin(ang1)
    m1 = np.block([[c1, s1], [-s1, c1]])
    k2 = np.arange(n2)
    npr = k1[:, None, None] + n1 * k2[None, :, None]
    prod = (npr * k2[None, None, :]) % n
    ang2 = 2 * np.pi * prod / n
    return (_np_split(cbd), _np_split(sbd), _np_split(m1), _np_split(np.cos(ang2)), _np_split(np.sin(ang2)))


def _fnet(u, fnet_g, fnet_w):
    b, n, _ = u.shape
    n2 = FFT_N2
    n1 = n // n2
    tcn = FFT_ROWS // n1
    (ch, cl), (sh, sl), (m1h, m1l), (tch, tcl), (tsh, tsl) = _fnet_tables(n)
    gain = fnet_g.reshape(1, B_WIDTH)
    wbd = jax.scipy.linalg.block_diag(*[fnet_w[g] for g in range(4)]).astype(BF16)

    mat = pl.BlockSpec((B_WIDTH, B_WIDTH), lambda i, j: (0, 0))
    m1spec = pl.BlockSpec((2 * n1, 2 * n1), lambda i, j: (0, 0))
    tile = pl.BlockSpec((1, n1, tcn, B_WIDTH), lambda i, j: (i, 0, j, 0))
    zr, zi = pl.pallas_call(
        functools.partial(_fnet_front_kernel, n1=n1, tcn=tcn),
        grid=(b, n2 // tcn),
        in_specs=[tile, mat, pl.BlockSpec((1, B_WIDTH), lambda i, j: (0, 0)), mat, mat, mat, mat, m1spec, m1spec],
        out_specs=[tile, tile],
        out_shape=[jax.ShapeDtypeStruct((b, n1, n2, B_WIDTH), F32)] * 2,
        scratch_shapes=[pltpu.VMEM((4, FFT_ROWS, LANES), F32), pltpu.VMEM((4, FFT_ROWS, LANES), F32)],
        compiler_params=_cparams("parallel", "parallel"),
        name="fnet_front",
    )(u.reshape(b, n1, n2, B_WIDTH), _group_mean_matrix(), gain, ch, cl, sh, sl, m1h, m1l)

    kc = FFT_KC
    zspec = pl.BlockSpec((1, kc, n2, B_WIDTH), lambda i, j: (i, j, 0, 0))
    tspec = pl.BlockSpec((kc, n2, n2), lambda i, j: (j, 0, 0))
    out = pl.pallas_call(
        functools.partial(_fnet_back_kernel, kc=kc, scale=1.0 / math.sqrt(64.0 * n)),
        grid=(b, n1 // kc),
        in_specs=[zspec, zspec, tspec, tspec, tspec, tspec,
                  pl.BlockSpec((B_WIDTH, B_WIDTH), lambda i, j: (0, 0))],
        out_specs=pl.BlockSpec((1, n2, kc, B_WIDTH), lambda i, j: (i, 0, j, 0)),
        out_shape=jax.ShapeDtypeStruct((b, n2, n1, B_WIDTH), BF16),
        scratch_shapes=[pltpu.VMEM((2, n2 * kc, LANES), F32)],
        compiler_params=_cparams("parallel", "parallel"),
        name="fnet_back",
    )(zr, zi, tch, tcl, tsh, tsl, wbd)
    return out.reshape(b, n, B_WIDTH)


def _mem_kv_kernel(mem_ref, wk_ref, wvt_ref, k_ref, vt_ref):
    m = mem_ref[0].astype(BF16)
    k_ref[0] = _dot(m, wk_ref[...]).astype(BF16)
    vt_ref[0] = _dot_nt(wvt_ref[...], m).astype(BF16)


def _mem_kv(mem, w_kv):
    b, m, _ = mem.shape
    wk = w_kv[:, :D_MODEL].astype(BF16)
    wvt = w_kv[:, D_MODEL:].T.astype(BF16)
    wspec = pl.BlockSpec((D_MODEL, D_MODEL), lambda i: (0, 0))
    return pl.pallas_call(
        _mem_kv_kernel,
        grid=(b,),
        in_specs=[pl.BlockSpec((1, m, D_MODEL), lambda i: (i, 0, 0)), wspec, wspec],
        out_specs=[pl.BlockSpec((1, m, D_MODEL), lambda i: (i, 0, 0)),
                   pl.BlockSpec((1, D_MODEL, m), lambda i: (i, 0, 0))],
        out_shape=[jax.ShapeDtypeStruct((b, m, D_MODEL), BF16),
                   jax.ShapeDtypeStruct((b, D_MODEL, m), BF16)],
        compiler_params=_cparams("parallel"),
        name="mem_kv",
    )(mem, wk, wvt)


HALVES = (slice(0, TM // 2), slice(TM // 2, TM))


def _tail_after_mixer(hs, x_ref, g0_ref, b0_ref, wqt_ref, k_ref, vt_ref, wo_ref, g1_ref, b1_ref, o_ref):
    halves = HALVES
    x1s, qts = [], []
    for rows, h in zip(halves, hs):
        x1 = _layer_norm(DN_ALPHA * x_ref[0, rows, :] + h, g0_ref[...], b0_ref[...])
        x1s.append(x1)
        qts.append(_dot_nt(wqt_ref[...], x1.astype(BF16)).astype(BF16))
    qt = jnp.concatenate(qts, axis=1)
    heads = [slice(h * XA_HEAD_DIM, (h + 1) * XA_HEAD_DIM) for h in range(XA_HEADS)]
    scores = [_dot(k_ref[0, :, hd], qt[hd]) for hd in heads]
    ones = jnp.ones((BF16_ROWS, k_ref.shape[1]), BF16)
    outs = []
    for hd, st in zip(heads, scores):
        p = jnp.exp2(st - jnp.max(st, axis=0, keepdims=True)).astype(BF16)
        ot = _dot(jnp.concatenate([vt_ref[0, hd, :], ones], axis=0), p)
        outs.append((ot[:XA_HEAD_DIM] / ot[XA_HEAD_DIM:XA_HEAD_DIM + 1]).astype(BF16))
    ot_all = jnp.concatenate(outs, axis=0)
    hs = [_dot_tn(ot_all[:, rows], wo_ref[...]) for rows in halves]
    for rows, x1, h in zip(halves, x1s, hs):
        o_ref[0, rows, :] = _layer_norm(DN_ALPHA * x1 + h, g1_ref[...], b1_ref[...])


def _tail_ab_kernel(o1_ref, o4_ref, o16_ref, l1_ref, l4_ref, l16_ref, expand_ref, c_ref, wa_ref, wc_ref,
                    x_ref, g0_ref, b0_ref, wqt_ref, k_ref, vt_ref, wo_ref, g1_ref, b1_ref, o_ref, obuf, lbuf):
    o_a = _mixture_tile(o1_ref, o4_ref, o16_ref, l1_ref, l4_ref, l16_ref, expand_ref, obuf, lbuf)
    hs = [_dot(o_a[rows], wa_ref[...]) + _dot(c_ref[0, rows, :], wc_ref[...]) for rows in HALVES]
    _tail_after_mixer(hs, x_ref, g0_ref, b0_ref, wqt_ref, k_ref, vt_ref, wo_ref, g1_ref, b1_ref, o_ref)


def _tail_cd_kernel(at_ref, up_ref, uc_ref, un_ref, pw_ref, ps_ref, wa_ref, wc_ref,
                    x_ref, g0_ref, b0_ref, wqt_ref, k_ref, vt_ref, wo_ref, g1_ref, b1_ref, o_ref, pool_buf, *, n):
    o_d = _pool_tile(up_ref, uc_ref, un_ref, pw_ref, ps_ref, pool_buf, tm=TM, n=n)
    hs = [_dot_tn(at_ref[0, :, rows], wa_ref[...]) + _dot(o_d[rows], wc_ref[...]) for rows in HALVES]
    _tail_after_mixer(hs, x_ref, g0_ref, b0_ref, wqt_ref, k_ref, vt_ref, wo_ref, g1_ref, b1_ref, o_ref)


def _tail_specs(x, mem):
    m = mem.shape[1]
    xspec = pl.BlockSpec((1, TM, D_MODEL), lambda i, j: (i, j, 0))
    wspec = pl.BlockSpec((D_MODEL, D_MODEL), lambda i, j: (0, 0))
    vec = pl.BlockSpec((1, D_MODEL), lambda i, j: (0, 0))
    return [xspec, vec, vec, wspec, pl.BlockSpec((1, m, D_MODEL), lambda i, j: (i, 0, 0)),
            pl.BlockSpec((1, D_MODEL, m), lambda i, j: (i, 0, 0)), wspec, vec, vec], xspec


def _tail_operands(x, g0, b0, mem, w_q, w_kv, w_o, g1, b1):
    k, vt = _mem_kv(mem, w_kv)
    wqt = (w_q * (XA_HEAD_DIM ** -0.5 * LOG2E)).T.astype(BF16)
    row = lambda v: v.reshape(1, D_MODEL)
    return x, row(g0), row(b0), wqt, k, vt, w_o.astype(BF16), row(g1), row(b1)


def _tail_ab(os_, ls_, o_b, w_out, x, g0, b0, mem, w_q, w_kv, w_o, g1, b1):
    b, n, _ = x.shape
    dils = [d for _, d in A_PATTERNS]
    expand = jnp.asarray(np.arange(LANES)[:, None] == np.arange(A_WIDTH)[None, :] // HEAD_DIM, BF16)
    wa = w_out[:A_WIDTH].astype(BF16)
    wc = w_out[A_WIDTH:].astype(BF16)
    tail_specs, xspec = _tail_specs(x, mem)
    return pl.pallas_call(
        _tail_ab_kernel,
        grid=(b, n // TM),
        in_specs=[pl.BlockSpec((1, TM // d, d * A_WIDTH), lambda i, j: (i, j, 0)) for d in dils]
        + [pl.BlockSpec((1, TM // d, d * LANES), lambda i, j: (i, j, 0)) for d in dils]
        + [pl.BlockSpec((LANES, A_WIDTH), lambda i, j: (0, 0)),
           pl.BlockSpec((1, TM, B_WIDTH), lambda i, j: (i, j, 0)),
           pl.BlockSpec(wa.shape, lambda i, j: (0, 0)), pl.BlockSpec(wc.shape, lambda i, j: (0, 0))]
        + tail_specs,
        out_specs=xspec,
        out_shape=jax.ShapeDtypeStruct((b, n, D_MODEL), F32),
        scratch_shapes=[pltpu.VMEM((2, A_WIDTH // LANES, TM, LANES), F32), pltpu.VMEM((2, TM, LANES), F32)],
        compiler_params=_cparams("parallel", "parallel"),
        name="tail_ab",
    )(*os_, *ls_, expand, o_b, wa, wc, *_tail_operands(x, g0, b0, mem, w_q, w_kv, w_o, g1, b1))


def _tail_cd(o_ct, u, pool_w, pool_scale, w_out, x, g0, b0, mem, w_q, w_kv, w_o, g1, b1):
    b, n, _ = x.shape
    per = TM // POOL_HALO
    last = n // POOL_HALO - 1
    pool_wbd = jax.scipy.linalg.block_diag(*[pool_w[g] for g in range(4)]).astype(BF16)
    wa = w_out[:C_WIDTH].astype(BF16)
    wc = w_out[C_WIDTH:].astype(BF16)
    tail_specs, xspec = _tail_specs(x, mem)
    return pl.pallas_call(
        functools.partial(_tail_cd_kernel, n=n),
        grid=(b, n // TM),
        in_specs=[pl.BlockSpec((1, C_WIDTH, TM), lambda i, j: (i, 0, j)),
                  pl.BlockSpec((1, POOL_HALO, D_WIDTH), lambda i, j: (i, jnp.maximum(j * per - 1, 0), 0)),
                  pl.BlockSpec((1, TM, D_WIDTH), lambda i, j: (i, j, 0)),
                  pl.BlockSpec((1, POOL_HALO, D_WIDTH), lambda i, j: (i, jnp.minimum((j + 1) * per, last), 0)),
                  pl.BlockSpec((D_WIDTH, D_WIDTH), lambda i, j: (0, 0)),
                  pl.BlockSpec((1, D_WIDTH), lambda i, j: (0, 0)),
                  pl.BlockSpec(wa.shape, lambda i, j: (0, 0)), pl.BlockSpec(wc.shape, lambda i, j: (0, 0))]
        + tail_specs,
        out_specs=xspec,
        out_shape=jax.ShapeDtypeStruct((b, n, D_MODEL), F32),
        scratch_shapes=[pltpu.VMEM((TM + 2 * POOL_HALO, D_WIDTH), F32)],
        compiler_params=_cparams("parallel", "parallel"),
        name="tail_cd",
    )(o_ct, u, u, u, pool_wbd, pool_scale.reshape(1, D_WIDTH), wa, wc,
      *_tail_operands(x, g0, b0, mem, w_q, w_kv, w_o, g1, b1))


def _swiglu_kernel(x_ref, win_ref, wo_ref, g_ref, b_ref, o_ref):
    halves = [slice(0, TM // 2), slice(TM // 2, TM)]
    gate_up = []
    for rows in halves:
        xb = x_ref[0, rows, :].astype(BF16)
        gate_up.append((_dot(xb, win_ref[:, :FFN_HIDDEN]), _dot(xb, win_ref[:, FFN_HIDDEN:])))
    parts = []
    for gate, up in gate_up:
        hid = (gate * (1.0 / (1.0 + jnp.exp(-gate))) * up).astype(BF16)
        parts.append(_dot(hid, wo_ref[...]))
    for rows, part in zip(halves, parts):
        o_ref[0, rows, :] = _layer_norm(DN_ALPHA * x_ref[0, rows, :] + part, g_ref[...], b_ref[...])


def _swiglu(x, w_in, w_out, g, bias):
    b, n, _ = x.shape
    win = w_in.astype(BF16)
    wout = w_out.astype(BF16)
    xspec = pl.BlockSpec((1, TM, D_MODEL), lambda i, t: (i, t, 0))
    vec = pl.BlockSpec((1, D_MODEL), lambda i, t: (0, 0))
    resident = pl.Buffered(1)
    return pl.pallas_call(
        _swiglu_kernel,
        grid=(b, n // TM),
        in_specs=[xspec,
                  pl.BlockSpec(win.shape, lambda i, t: (0, 0), pipeline_mode=resident),
                  pl.BlockSpec(wout.shape, lambda i, t: (0, 0), pipeline_mode=resident), vec, vec],
        out_specs=xspec,
        out_shape=jax.ShapeDtypeStruct((b, n, D_MODEL), F32),
        compiler_params=_cparams("parallel", "parallel"),
        name="swiglu_ln",
    )(x, win, wout, g.reshape(1, D_MODEL), bias.reshape(1, D_MODEL))


N_QK_HEADS = C_Q_HEADS + C_KV_HEADS
QK_ROWS = N_QK_HEADS * HEAD_DIM


def _proj_cd_kernel(x_ref, wt_ref, wu_ref, gain_ref, cos_ref, sin_ref,
                    qt_ref, k_ref, vt_ref, u_ref, kn_ref, *, tm):
    xb = x_ref[0].astype(BF16)
    u_ref[0] = _dot(xb, wu_ref[...])
    zt = _dot_nt(wt_ref[...], xb)
    z = zt[:QK_ROWS].reshape(N_QK_HEADS, HEAD_DIM, tm)
    ssq = jnp.sum(z * z, axis=1, keepdims=True)
    zn = z * lax.rsqrt(ssq * (1.0 / HEAD_DIM) + RMS_EPS) * gain_ref[...]
    half = HEAD_DIM // 2
    x1 = zn[:, :half]
    x2 = zn[:, half:]
    c = cos_ref[...][None]
    s = sin_ref[...][None]
    rot = jnp.concatenate([x1 * c - x2 * s, x1 * s + x2 * c], axis=1).reshape(QK_ROWS, tm)
    qt_ref[0] = rot[:C_WIDTH].astype(BF16)
    kb = rot[C_WIDTH:].astype(BF16)
    k_ref[0] = kb.astype(F32).T.astype(BF16)
    kf = kb.astype(F32).reshape(C_KV_HEADS, HEAD_DIM, tm)
    kn_ref[0] = jnp.sum(kf * kf, axis=1, keepdims=True)
    vt = zt[QK_ROWS:].astype(BF16)
    for c0 in range(tm // GQA_TKC):
        vt_ref[0, c0] = vt[:, c0 * GQA_TKC:(c0 + 1) * GQA_TKC]


def _rope_tables_t(n):
    rows = n // GRID_W
    row_id = jnp.broadcast_to(jnp.arange(rows)[:, None], (rows, GRID_W)).reshape(n)
    col_id = jnp.broadcast_to(jnp.arange(GRID_W)[None, :], (rows, GRID_W)).reshape(n)
    axis_dim = HEAD_DIM // 2
    freqs = ROPE_THETA ** (-jnp.arange(0, axis_dim, 2, dtype=F32) / axis_dim)
    ang = jnp.concatenate([row_id[:, None] * freqs, col_id[:, None] * freqs], axis=-1)
    return jnp.cos(ang).T, jnp.sin(ang).T


def _proj_cd(x, w_in, q_norm, k_norm):
    b, n, _ = x.shape
    tm = TM
    half = HEAD_DIM // 2
    wqk = w_in[:, :QK_ROWS].reshape(D_MODEL, N_QK_HEADS, half, 2)
    wqk = wqk.transpose(1, 3, 2, 0).reshape(QK_ROWS, D_MODEL)
    wt = jnp.concatenate([wqk, w_in[:, QK_ROWS:QK_ROWS + C_KV_WIDTH].T], axis=0).astype(BF16)
    wu = w_in[:, QK_ROWS + C_KV_WIDTH:].astype(BF16)
    qg = q_norm.reshape(half, 2).T.reshape(HEAD_DIM) * (HEAD_DIM ** -0.5 * LOG2E)
    kg = k_norm.reshape(half, 2).T.reshape(HEAD_DIM)
    gain = jnp.concatenate([jnp.tile(qg[None], (C_Q_HEADS, 1)), jnp.tile(kg[None], (C_KV_HEADS, 1))])
    gain = gain.reshape(N_QK_HEADS, HEAD_DIM, 1).astype(F32)
    cos_t, sin_t = _rope_tables_t(n)
    nc = n // GQA_TKC
    return pl.pallas_call(
        functools.partial(_proj_cd_kernel, tm=tm),
        grid=(b, n // tm),
        in_specs=[pl.BlockSpec((1, tm, D_MODEL), lambda i, j: (i, j, 0)),
                  pl.BlockSpec(wt.shape, lambda i, j: (0, 0)),
                  pl.BlockSpec(wu.shape, lambda i, j: (0, 0)),
                  pl.BlockSpec(gain.shape, lambda i, j: (0, 0, 0)),
                  pl.BlockSpec((HEAD_DIM // 2, tm), lambda i, j: (0, j)),
                  pl.BlockSpec((HEAD_DIM // 2, tm), lambda i, j: (0, j))],
        out_specs=[pl.BlockSpec((1, C_WIDTH, tm), lambda i, j: (i, 0, j)),
                   pl.BlockSpec((1, tm, C_KV_WIDTH), lambda i, j: (i, j, 0)),
                   pl.BlockSpec((1, tm // GQA_TKC, C_KV_WIDTH, GQA_TKC), lambda i, j: (i, j, 0, 0)),
                   pl.BlockSpec((1, tm, D_WIDTH), lambda i, j: (i, j, 0)),
                   pl.BlockSpec((1, C_KV_HEADS, 1, tm), lambda i, j: (i, 0, 0, j))],
        out_shape=[jax.ShapeDtypeStruct((b, C_WIDTH, n), BF16),
                   jax.ShapeDtypeStruct((b, n, C_KV_WIDTH), BF16),
                   jax.ShapeDtypeStruct((b, nc, C_KV_WIDTH, GQA_TKC), BF16),
                   jax.ShapeDtypeStruct((b, n, D_WIDTH), F32),
                   jax.ShapeDtypeStruct((b, C_KV_HEADS, 1, n), F32)],
        compiler_params=_cparams("parallel", "parallel"),
        name="proj_cd",
    )(x, wt, wu, gain, cos_t, sin_t)


def _gqa_kernel(qt_ref, k_ref, vt_ref, kn_ref, ot_ref, m_ref, l_ref, acc_ref, qpad_ref, *, tq, n):
    g = pl.program_id(1)
    row_half = lax.broadcasted_iota(jnp.int32, (LANES, 1), 0) // HEAD_DIM
    mine = row_half == (g % 2)
    k_max2 = jnp.max(kn_ref[0, 0], axis=1, keepdims=True)
    bound_max = jnp.zeros((1, 1), F32)
    for j in range(C_REP):
        qj = qt_ref[0, j * HEAD_DIM:(j + 1) * HEAD_DIM, :]
        q2 = jnp.concatenate([qj, qj], axis=0)
        qpad_ref[j] = jnp.where(mine, q2, jnp.zeros_like(q2))
        qf = qj.astype(F32)
        bound = jnp.sqrt(jnp.sum(qf * qf, axis=0, keepdims=True) * k_max2) * GQA_BOUND_SLACK
        m_ref[j] = bound
        bound_max = jnp.maximum(bound_max, jnp.max(bound, axis=1, keepdims=True))
    fixed_shift = bound_max[0, 0] <= GQA_BOUND_LIMIT
    acc_ref[...] = jnp.zeros(acc_ref.shape, F32)
    ones = jnp.ones((GQA_VROWS - HEAD_DIM, GQA_TKC), BF16)
    unroll = GQA_PIECES // (C_REP * (tq // GQA_TW))

    def body(c, carry, *, online):
        kchs, vchs = [], []
        for u in range(unroll):
            cc = c * unroll + u
            start = pl.multiple_of(cc * GQA_TKC, GQA_TKC)
            kchs.append(k_ref[0, pl.ds(start, GQA_TKC), :])
            vchs.append(jnp.concatenate([vt_ref[0, cc], ones], axis=0))
        pieces = [(u, j, slice(s * GQA_TW, (s + 1) * GQA_TW))
                  for u in range(unroll) for j in range(C_REP) for s in range(tq // GQA_TW)]

        def scores(i):
            u, j, cols = pieces[i]
            return _dot(kchs[u], qpad_ref[j, :, cols])

        pending = [scores(i) for i in range(GQA_AHEAD)]
        for i, (u, j, cols) in enumerate(pieces):
            st = pending.pop(0)
            if i + GQA_AHEAD < len(pieces):
                pending.append(scores(i + GQA_AHEAD))
            m_old = m_ref[j, :, cols]
            if not online:
                p = jnp.exp2(st - m_old)
                l_ref[j, :, cols] = l_ref[j, :, cols] + jnp.sum(p.reshape(GQA_TKC // 8, 8, GQA_TW), axis=0)
                acc_ref[j, :HEAD_DIM, cols] = acc_ref[j, :HEAD_DIM, cols] + _dot(
                    vchs[u][:HEAD_DIM], p.astype(BF16))
                continue
            m_new = jnp.maximum(m_old, jnp.max(st, axis=0, keepdims=True))
            alpha = jnp.exp2(m_old - m_new)
            p = jnp.exp2(st - m_new)
            acc_ref[j, :, cols] = alpha * acc_ref[j, :, cols] + _dot(vchs[u], p.astype(BF16))
            m_ref[j, :, cols] = m_new
        return carry

    trips = n // (GQA_TKC * unroll)

    @pl.when(fixed_shift)
    def _():
        l_ref[...] = jnp.zeros(l_ref.shape, F32)
        lax.fori_loop(0, trips, functools.partial(body, online=False), 0)
        for j in range(C_REP):
            acc_ref[j, HEAD_DIM:HEAD_DIM + 1, :] = jnp.sum(l_ref[j], axis=0, keepdims=True)

    @pl.when(jnp.logical_not(fixed_shift))
    def _():
        m_ref[...] = jnp.full(m_ref.shape, NEG_INF, F32)
        lax.fori_loop(0, trips, functools.partial(body, online=True), 0)

    for j in range(C_REP):
        l = acc_ref[j, HEAD_DIM:HEAD_DIM + 1, :]
        ot_ref[0, j * HEAD_DIM:(j + 1) * HEAD_DIM, :] = (acc_ref[j, :HEAD_DIM, :] / l).astype(BF16)


def _gqa(qt, k, vt, kn):
    b, _, n = qt.shape
    tq = min(GQA_TQ, n)
    nc = n // GQA_TKC
    rows = C_REP * HEAD_DIM
    return pl.pallas_call(
        functools.partial(_gqa_kernel, tq=tq, n=n),
        grid=(b, C_KV_HEADS, n // tq),
        in_specs=[pl.BlockSpec((1, rows, tq), lambda i, g, t: (i, g, t)),
                  pl.BlockSpec((1, n, LANES), lambda i, g, t: (i, 0, g // 2)),
                  pl.BlockSpec((1, nc, HEAD_DIM, GQA_TKC), lambda i, g, t: (i, 0, g, 0)),
                  pl.BlockSpec((1, 1, 1, n), lambda i, g, t: (i, g, 0, 0))],
        out_specs=pl.BlockSpec((1, rows, tq), lambda i, g, t: (i, g, t)),
        out_shape=jax.ShapeDtypeStruct((b, C_WIDTH, n), BF16),
        scratch_shapes=[pltpu.VMEM((C_REP, 1, tq), F32), pltpu.VMEM((C_REP, 8, tq), F32),
                        pltpu.VMEM((C_REP, GQA_VROWS, tq), F32), pltpu.VMEM((C_REP, LANES, tq), BF16)],
        compiler_params=_cparams("parallel", "parallel", "parallel"),
        name="gqa_flash",
    )(qt, k, vt, kn)


def _pool_tile(up_ref, uc_ref, un_ref, w_ref, scale_ref, buf, *, tm, n):
    i = pl.program_id(1)
    cur = uc_ref[0]
    buf[0:POOL_HALO, :] = jnp.where(i > 0, up_ref[0], 0.0)
    buf[POOL_HALO:POOL_HALO + tm, :] = cur
    buf[POOL_HALO + tm:, :] = jnp.where(i < pl.num_programs(1) - 1, un_ref[0], 0.0)
    lane_group = lax.broadcasted_iota(jnp.int32, (1, D_WIDTH), 1) // 64
    half_w = jnp.left_shift(1, lane_group)
    acc = jnp.zeros((tm, D_WIDTH), F32)
    for j in range(-POOL_HALO, POOL_HALO):
        inside = (j >= -half_w) & (j < half_w)
        acc = acc + jnp.where(inside, buf[POOL_HALO + j:POOL_HALO + j + tm, :], 0.0)
    t = i * tm + lax.broadcasted_iota(jnp.int32, (tm, 1), 0)
    cnt = jnp.minimum(t + half_w, n) - jnp.maximum(t - half_w, 0)
    mixed = (acc / cnt.astype(F32) - cur).astype(BF16)
    return (_dot(mixed, w_ref[...]) * scale_ref[...]).astype(BF16)


def _trunk(x, mem, rel_bias, ab_w_in, ab_fnet_g, ab_fnet_w, ab_w_out,
           cd_w_in, cd_q_norm, cd_k_norm, cd_pool_w, cd_pool_scale, cd_w_out,
           xa_w_q, xa_w_kv, xa_w_o, ffn_w_in, ffn_w_out, ln_g, ln_b):
    def tail_args(layer):
        return (ln_g[layer, 0], ln_b[layer, 0], mem, xa_w_q[layer], xa_w_kv[layer], xa_w_o[layer],
                ln_g[layer, 1], ln_b[layer, 1])

    for layer in range(DEPTH):
        i = layer // 2
        if layer % 2 == 0:
            w_in = ab_w_in[i]
            w_in = jnp.concatenate([w_in[:, :A_WIDTH] * (HEAD_DIM ** -0.5 * LOG2E), w_in[:, A_WIDTH:]], axis=1)
            *views, u = _proj_ab(x, w_in.astype(BF16))
            outs = [_dilated(view, rel_bias, d) for view, (_, d) in zip(views, A_PATTERNS)]
            o_b = _fnet(u, ab_fnet_g[i], ab_fnet_w[i])
            x = _tail_ab([o for o, _ in outs], [l for _, l in outs], o_b, ab_w_out[i], x, *tail_args(layer))
        else:
            qt, k, vt, u, kn = _proj_cd(x, cd_w_in[i], cd_q_norm[i], cd_k_norm[i])
            o_c = _gqa(qt, k, vt, kn)
            x = _tail_cd(o_c, u, cd_pool_w[i], cd_pool_scale[i], cd_w_out[i], x, *tail_args(layer))
        x = _swiglu(x, ffn_w_in[layer], ffn_w_out[layer], ln_g[layer, 2], ln_b[layer, 2])
    return x


def kernel(x_prompt, x_sample, mem_prompt, mem_sample, rel_bias, ab_w_in, ab_fnet_g, ab_fnet_w, ab_w_out, cd_w_in, cd_q_norm, cd_k_norm, cd_pool_w, cd_pool_scale, cd_w_out, xa_w_q, xa_w_kv, xa_w_o, ffn_w_in, ffn_w_out, ln_g, ln_b):
    params = (rel_bias, ab_w_in, ab_fnet_g, ab_fnet_w, ab_w_out,
              cd_w_in, cd_q_norm, cd_k_norm, cd_pool_w, cd_pool_scale, cd_w_out,
              xa_w_q, xa_w_kv, xa_w_o, ffn_w_in, ffn_w_out, ln_g, ln_b)
    return (_trunk(x_prompt, mem_prompt, *params), _trunk(x_sample, mem_sample, *params))
```

```python
import functools
import math

import numpy as np
import jax
import jax.numpy as jnp
from jax import lax
from jax.experimental import pallas as pl
from jax.experimental.pallas import tpu as pltpu

F32 = jnp.float32
BF16 = jnp.bfloat16

D_MODEL = 1024
HEAD_DIM = 64
GRID_W = 64
LN_EPS = 1e-5
RMS_EPS = 1e-6
NEG_INF = -1e30
DEPTH = 2
A_HEADS = 12
A_WIDTH = A_HEADS * HEAD_DIM
A_PATTERNS = ((128, 1), (512, 4), (2048, 16))
A_HALF = 64
QKV_WIDTH = 3 * A_WIDTH
N_BUCKETS = 32
REL_MAX_DIST = 1024
B_WIDTH = 256
C_Q_HEADS = 12
C_KV_HEADS = 4
C_REP = C_Q_HEADS // C_KV_HEADS
C_WIDTH = C_Q_HEADS * HEAD_DIM
C_KV_WIDTH = C_KV_HEADS * HEAD_DIM
ROPE_THETA = 10000.0
POOL_WINDOWS = (2, 4, 8, 16)
POOL_HALO = 8
D_WIDTH = 256
XA_HEADS = 4
XA_HEAD_DIM = D_MODEL // XA_HEADS
FFN_HIDDEN = 2816
DN_ALPHA = (2 * DEPTH) ** 0.25
LOG2E = 1.4426950408889634

LANES = 128
MXU_TILE = 256
BF16_ROWS = 16
VMEM_LIMIT = 56 * 1024 * 1024
TM = 512
FFT_N2 = 128
FFT_ROWS = 1024
FFT_KC = 16
FFT_AHEAD = 3
DIL_AHEAD = 3
DIL_TQ = 512
DIL_SQ = MXU_TILE // 2
DIL_SK = DIL_SQ + 2 * A_HALF
DIL_PIECES = 48
GQA_TQ = 8192
GQA_TW = MXU_TILE
GQA_TKC = 256
GQA_VROWS = HEAD_DIM + BF16_ROWS
GQA_PIECES = 384
GQA_AHEAD = 5
GQA_BOUND_SLACK = 1.0 + 2.0 ** -10
GQA_BOUND_LIMIT = 60.0


def _cparams(*sem):
    return pltpu.CompilerParams(dimension_semantics=sem, vmem_limit_bytes=VMEM_LIMIT)


def _dot(a, b):
    return jnp.dot(a, b, preferred_element_type=F32)


def _dot_nt(a, b):
    return lax.dot_general(a, b, (((1,), (1,)), ((), ())), preferred_element_type=F32)


def _dot_tn(a, b):
    return lax.dot_general(a, b, (((0,), (0,)), ((), ())), preferred_element_type=F32)


def _split(x):
    hi = x.astype(BF16)
    lo = (x - hi.astype(F32)).astype(BF16)
    return hi, lo


def _dot3(ah, al, bh, bl):
    return _dot(ah, bh) + _dot(al, bh) + _dot(ah, bl)


def _np_split(x):
    x = np.asarray(x, np.float32)
    hi = x.astype(BF16)
    lo = (x - hi.astype(np.float32)).astype(BF16)
    return jnp.asarray(hi), jnp.asarray(lo)


def _layer_norm(h, g, b):
    mu = jnp.mean(h, axis=-1, keepdims=True)
    xc = h - mu
    var = jnp.mean(xc * xc, axis=-1, keepdims=True)
    return xc * lax.rsqrt(var + LN_EPS) * g + b


def _proj_ab_kernel(x_ref, w_ref, qkv1_ref, qkv4_ref, qkv16_ref, u_ref, zbuf):
    xb = x_ref[0].astype(BF16)
    chunks = list(range(0, QKV_WIDTH, MXU_TILE))

    for c in chunks:
        z = _dot(xb, w_ref[:, c:c + MXU_TILE])
        zbuf[c // LANES] = z[:, :LANES]
        zbuf[c // LANES + 1] = z[:, LANES:]
        qkv1_ref[0, :, c:c + MXU_TILE] = z.astype(BF16)
    u_ref[0] = _dot(xb, w_ref[:, QKV_WIDTH:])
    for (_, d), ref in zip(A_PATTERNS[1:], (qkv4_ref, qkv16_ref)):
        rows = TM // d
        for r in range(d):
            for ct in range(QKV_WIDTH // LANES):
                col = r * QKV_WIDTH + ct * LANES
                ref[0, :, col:col + LANES] = zbuf[ct, pl.ds(r, rows, stride=d), :].astype(BF16)


def _proj_ab(x, w):
    b, n, _ = x.shape
    dils = [d for _, d in A_PATTERNS]
    return pl.pallas_call(
        _proj_ab_kernel,
        grid=(b, n // TM),
        in_specs=[pl.BlockSpec((1, TM, D_MODEL), lambda i, j: (i, j, 0)),
                  pl.BlockSpec(w.shape, lambda i, j: (0, 0))],
        out_specs=[pl.BlockSpec((1, TM // d, d * QKV_WIDTH), lambda i, j: (i, j, 0)) for d in dils]
        + [pl.BlockSpec((1, TM, B_WIDTH), lambda i, j: (i, j, 0))],
        out_shape=[jax.ShapeDtypeStruct((b, n // d, d * QKV_WIDTH), BF16) for d in dils]
        + [jax.ShapeDtypeStruct((b, n, B_WIDTH), F32)],
        scratch_shapes=[pltpu.VMEM((QKV_WIDTH // LANES, TM, LANES), F32)],
        compiler_params=_cparams("parallel", "parallel"),
        name="proj_ab",
    )(x, w)


def _t5_bucket_np(rel):
    nb = N_BUCKETS // 2
    max_exact = nb // 2
    ret = np.where(rel > 0, nb, 0)
    n = np.abs(rel)
    nf = np.maximum(n, 1).astype(np.float32)
    large = max_exact + (np.log(nf / max_exact) / math.log(REL_MAX_DIST / max_exact)
                         * (nb - max_exact)).astype(np.int32)
    large = np.minimum(large, nb - 1)
    return ret + np.where(n < max_exact, n, large)


def _band_bias(rel_bias, dilation):
    tq, tk = DIL_SQ, DIL_SK
    band = 2 * A_HALF + 1
    bucket = _t5_bucket_np((np.arange(band) - A_HALF) * dilation)
    row = (rel_bias[jnp.asarray(bucket)].T * LOG2E).astype(F32)
    row = jnp.concatenate([row, jnp.full((A_HEADS, tk + 1 - band), NEG_INF, F32)], axis=1)
    bias = jnp.tile(row, (1, tq))[:, :tq * tk].reshape(A_HEADS, tq, tk)
    bias = bias.transpose(0, 2, 1)
    bias = bias.reshape(A_HEADS // 2, 2, tk, tq).transpose(0, 2, 1, 3).reshape(A_HEADS // 2, tk, 2 * tq)
    key = np.arange(tk)[None, :, None]
    before = jnp.asarray(key < A_HALF)
    after = jnp.asarray(key >= A_HALF + tq)
    first = jnp.where(before, NEG_INF, bias)
    return jnp.stack([bias, first, jnp.where(after, NEG_INF, bias), jnp.where(after, NEG_INF, first)])


def _dilated_kernel(prev_ref, cur_ref, next_ref, bias_ref, o_ref, lse_ref, kbuf, vbuf, *, tq, n_res):
    for rr in range(n_res):
        for buf, c0 in ((kbuf, rr * QKV_WIDTH + A_WIDTH), (vbuf, rr * QKV_WIDTH + 2 * A_WIDTH)):
            buf[rr, 0:A_HALF, :] = prev_ref[0, :, c0:c0 + A_WIDTH]
            buf[rr, A_HALF:A_HALF + tq, :] = cur_ref[0, :, c0:c0 + A_WIDTH]
            buf[rr, A_HALF + tq:, :] = next_ref[0, :, c0:c0 + A_WIDTH]

    i = pl.program_id(1)
    n_sub = tq // DIL_SQ
    row_low = lax.broadcasted_iota(jnp.int32, (LANES, 1), 0) < HEAD_DIM
    row16 = lax.broadcasted_iota(jnp.int32, (BF16_ROWS, 1), 0)
    ones = jnp.ones((BF16_ROWS, DIL_SK), BF16)
    qts, vts = {}, {}
    pieces = [(rr, pair, sub) for rr in range(n_res) for pair in range(A_HEADS // 2) for sub in range(n_sub)]

    def variant(sub):
        v = 0
        if sub == 0:
            v = v + jnp.where(i == 0, 1, 0)
        if sub == n_sub - 1:
            v = v + jnp.where(i == pl.num_programs(1) - 1, 2, 0)
        return v

    def scores(idx):
        rr, pair, sub = pieces[idx]
        cols = slice(pair * LANES, (pair + 1) * LANES)
        if (rr, pair) not in qts:
            qcols = slice(rr * QKV_WIDTH + pair * LANES, rr * QKV_WIDTH + (pair + 1) * LANES)
            qts[rr, pair] = cur_ref[0, :, qcols].astype(F32).T
        qsub = qts[rr, pair][:, sub * DIL_SQ:(sub + 1) * DIL_SQ]
        rhs = jnp.concatenate([jnp.where(row_low, qsub, 0.0), jnp.where(row_low, 0.0, qsub)], axis=1)
        keys = kbuf[rr, sub * DIL_SQ:sub * DIL_SQ + DIL_SK, cols]
        return _dot(keys, rhs.astype(BF16)) + bias_ref[variant(sub), pair]

    pending = [scores(idx) for idx in range(DIL_AHEAD)]
    lse_t = {(rr, sub): jnp.zeros((BF16_ROWS, DIL_SQ), F32) for rr in range(n_res) for sub in range(n_sub)}
    for idx, (rr, pair, sub) in enumerate(pieces):
        st = pending.pop(0)
        if idx + DIL_AHEAD < len(pieces):
            pending.append(scores(idx + DIL_AHEAD))
        cols = slice(pair * LANES, (pair + 1) * LANES)
        m = jnp.max(st, axis=0, keepdims=True)
        p = jnp.exp2(st - m).astype(BF16)
        if (rr, pair) not in vts:
            vts[rr, pair] = vbuf[rr, :, cols].astype(F32).T.astype(BF16)
        vaug = jnp.concatenate([vts[rr, pair][:, sub * DIL_SQ:sub * DIL_SQ + DIL_SK], ones], axis=0)
        ot = _dot(vaug, p)
        l = ot[LANES:LANES + 1]
        lse2 = m + jnp.log2(l)
        o_pair = jnp.concatenate([ot[:HEAD_DIM, :DIL_SQ] / l[:, :DIL_SQ],
                                  ot[HEAD_DIM:LANES, DIL_SQ:] / l[:, DIL_SQ:]], axis=0)
        ocols = slice(rr * A_WIDTH + pair * LANES, rr * A_WIDTH + (pair + 1) * LANES)
        o_ref[0, sub * DIL_SQ:(sub + 1) * DIL_SQ, ocols] = o_pair.T.astype(BF16)
        lse_t[rr, sub] = jnp.where(row16 == 2 * pair, lse2[:, :DIL_SQ],
                                   jnp.where(row16 == 2 * pair + 1, lse2[:, DIL_SQ:], lse_t[rr, sub]))
    for (rr, sub), rows16 in lse_t.items():
        full = jnp.concatenate([rows16, jnp.zeros((LANES - BF16_ROWS, DIL_SQ), F32)], axis=0)
        lse_ref[0, sub * DIL_SQ:(sub + 1) * DIL_SQ, rr * LANES:(rr + 1) * LANES] = full.T


def _dilated(view, rel_bias, dilation):
    b, seq, _ = view.shape
    tq = min(DIL_TQ, seq)
    tk = tq + 2 * A_HALF
    n_res = min(dilation, max(1, DIL_PIECES // (A_HEADS // 2 * (tq // DIL_SQ))))
    bias = _band_bias(rel_bias, dilation)
    per = tq // A_HALF
    last = seq // A_HALF - 1
    width = n_res * QKV_WIDTH

    o, lse = pl.pallas_call(
        functools.partial(_dilated_kernel, tq=tq, n_res=n_res),
        grid=(b, seq // tq, dilation // n_res),
        in_specs=[pl.BlockSpec((1, A_HALF, width), lambda bi, i, r: (bi, jnp.maximum(i * per - 1, 0), r)),
                  pl.BlockSpec((1, tq, width), lambda bi, i, r: (bi, i, r)),
                  pl.BlockSpec((1, A_HALF, width), lambda bi, i, r: (bi, jnp.minimum((i + 1) * per, last), r)),
                  pl.BlockSpec(bias.shape, lambda bi, i, r: (0, 0, 0, 0))],
        out_specs=[pl.BlockSpec((1, tq, n_res * A_WIDTH), lambda bi, i, r: (bi, i, r)),
                   pl.BlockSpec((1, tq, n_res * LANES), lambda bi, i, r: (bi, i, r))],
        out_shape=[jax.ShapeDtypeStruct((b, seq, dilation * A_WIDTH), BF16),
                   jax.ShapeDtypeStruct((b, seq, dilation * LANES), F32)],
        scratch_shapes=[pltpu.VMEM((n_res, tk, A_WIDTH), BF16), pltpu.VMEM((n_res, tk, A_WIDTH), BF16)],
        compiler_params=_cparams("parallel", "parallel", "parallel"),
        name=f"dilated_d{dilation}",
    )(view, view, view, bias)
    return o, lse


def _mixture_tile(o1_ref, o4_ref, o16_ref, l1_ref, l4_ref, l16_ref, expand_ref, obuf, lbuf):
    for idx, (d, o_ref, l_ref) in enumerate(((A_PATTERNS[1][1], o4_ref, l4_ref), (A_PATTERNS[2][1], o16_ref, l16_ref))):
        rows = TM // d
        for r in range(d):
            for ct in range(A_WIDTH // LANES):
                col = r * A_WIDTH + ct * LANES
                obuf[idx, ct, pl.ds(r, rows, stride=d), :] = o_ref[0, :, col:col + LANES].astype(F32)
            lbuf[idx, pl.ds(r, rows, stride=d), :] = l_ref[0, :, r * LANES:(r + 1) * LANES]
    ls = [l1_ref[0], lbuf[0], lbuf[1]]
    mx = jnp.maximum(jnp.maximum(ls[0], ls[1]), ls[2])
    es = [jnp.exp2(l - mx) for l in ls]
    inv = 1.0 / (es[0] + es[1] + es[2])
    ws = []
    for e in es:
        wh, wl = _split(e * inv)
        ws.append(_dot(wh, expand_ref[...]) + _dot(wl, expand_ref[...]))
    pairs = []
    for pair in range(A_HEADS // 2):
        cols = slice(pair * LANES, (pair + 1) * LANES)
        os_ = [o1_ref[0, :, cols].astype(F32), obuf[0, pair], obuf[1, pair]]
        acc = None
        for g in range(3):
            t = ws[g][:, cols] * os_[g]
            acc = t if acc is None else acc + t
        pairs.append(acc.astype(BF16))
    return jnp.concatenate(pairs, axis=1)


def _group_mean_matrix():
    g = np.kron(np.eye(4), np.full((64, 64), 1.0 / 64))
    return jnp.asarray(g, BF16)


def _fnet_front_kernel(u_ref, gm_ref, gain_ref, ch_ref, cl_ref, sh_ref, sl_ref, m1h_ref, m1l_ref,
                       zr_ref, zi_ref, ybuf, zbuf, *, n1, tcn):
    rows = n1 * tcn
    u = u_ref[0].reshape(rows, B_WIDTH)
    gm = gm_ref[...]
    uh, ul = _split(u)
    mean = _dot(uh, gm) + _dot(ul, gm)
    xc = u - mean
    qh, ql = _split(xc * xc)
    var = _dot(qh, gm) + _dot(ql, gm)
    un = xc * lax.rsqrt(var + LN_EPS) * gain_ref[...]
    nh, nl = _split(un)
    yr = _dot3(nh, nl, ch_ref[...], cl_ref[...])
    yi = -_dot3(nh, nl, sh_ref[...], sl_ref[...])
    for q, val in enumerate((yr[:, :LANES], yr[:, LANES:], yi[:, :LANES], yi[:, LANES:])):
        ybuf[q] = val
    def first_stage(n2):
        col = [ybuf[q, pl.ds(n2, n1, stride=tcn), :] for q in range(4)]
        y = jnp.concatenate([jnp.concatenate(col[:2], axis=1), jnp.concatenate(col[2:], axis=1)], axis=0)
        yh, yl = _split(y)
        return _dot3(m1h_ref[...], m1l_ref[...], yh, yl)

    pending = [first_stage(n2) for n2 in range(min(FFT_AHEAD, tcn))]
    for n2 in range(tcn):
        z = pending.pop(0)
        if n2 + FFT_AHEAD < tcn:
            pending.append(first_stage(n2 + FFT_AHEAD))
        for q, val in enumerate((z[:n1, :LANES], z[:n1, LANES:], z[n1:, :LANES], z[n1:, LANES:])):
            zbuf[q, pl.ds(n2, n1, stride=tcn), :] = val
    zr_ref[0] = jnp.concatenate([zbuf[0], zbuf[1]], axis=1).reshape(n1, tcn, B_WIDTH)
    zi_ref[0] = jnp.concatenate([zbuf[2], zbuf[3]], axis=1).reshape(n1, tcn, B_WIDTH)


def _fnet_back_kernel(zr_ref, zi_ref, tch_ref, tcl_ref, tsh_ref, tsl_ref, w_ref, o_ref, obuf, *, kc, scale):
    n2 = FFT_N2

    def second_stage(kk):
        rh, rl = _split(zr_ref[0, kk])
        ih, il = _split(zi_ref[0, kk])
        return _dot3(tch_ref[kk], tcl_ref[kk], rh, rl) + _dot3(tsh_ref[kk], tsl_ref[kk], ih, il)

    pending = [second_stage(kk) for kk in range(FFT_AHEAD)]
    for kk in range(kc):
        f = pending.pop(0)
        if kk + FFT_AHEAD < kc:
            pending.append(second_stage(kk + FFT_AHEAD))
        o = _dot((f * scale).astype(BF16), w_ref[...])
        obuf[0, pl.ds(kk, n2, stride=kc), :] = o[:, :LANES]
        obuf[1, pl.ds(kk, n2, stride=kc), :] = o[:, LANES:]
    out = jnp.concatenate([obuf[0], obuf[1]], axis=1).reshape(n2, kc, B_WIDTH)
    o_ref[0] = out.astype(BF16)


def _fnet_tables(n):
    n2 = FFT_N2
    n1 = n // n2
    c = np.arange(64)
    ang = 2 * np.pi * np.outer(c, c) / 64
    cbd = np.kron(np.eye(4), np.cos(ang))
    sbd = np.kron(np.eye(4), np.sin(ang))
    k1 = np.arange(n1)
    ang1 = 2 * np.pi * np.outer(k1, k1) / n1
    c1, s1 = np.cos(ang1), np.sin(ang1)
    m1 = np.block([[c1, s1], [-s1, c1]])
    k2 = np.arange(n2)
    npr = k1[:, None, None] + n1 * k2[None, :, None]
    prod = (npr * k2[None, None, :]) % n
    ang2 = 2 * np.pi * prod / n
    return (_np_split(cbd), _np_split(sbd), _np_split(m1), _np_split(np.cos(ang2)), _np_split(np.sin(ang2)))


def _fnet(u, fnet_g, fnet_w):
    b, n, _ = u.shape
    n2 = FFT_N2
    n1 = n // n2
    tcn = FFT_ROWS // n1
    (ch, cl), (sh, sl), (m1h, m1l), (tch, tcl), (tsh, tsl) = _fnet_tables(n)
    gain = fnet_g.reshape(1, B_WIDTH)
    wbd = jax.scipy.linalg.block_diag(*[fnet_w[g] for g in range(4)]).astype(BF16)

    mat = pl.BlockSpec((B_WIDTH, B_WIDTH), lambda i, j: (0, 0))
    m1spec = pl.BlockSpec((2 * n1, 2 * n1), lambda i, j: (0, 0))
    tile = pl.BlockSpec((1, n1, tcn, B_WIDTH), lambda i, j: (i, 0, j, 0))
    zr, zi = pl.pallas_call(
        functools.partial(_fnet_front_kernel, n1=n1, tcn=tcn),
        grid=(b, n2 // tcn),
        in_specs=[tile, mat, pl.BlockSpec((1, B_WIDTH), lambda i, j: (0, 0)), mat, mat, mat, mat, m1spec, m1spec],
        out_specs=[tile, tile],
        out_shape=[jax.ShapeDtypeStruct((b, n1, n2, B_WIDTH), F32)] * 2,
        scratch_shapes=[pltpu.VMEM((4, FFT_ROWS, LANES), F32), pltpu.VMEM((4, FFT_ROWS, LANES), F32)],
        compiler_params=_cparams("parallel", "parallel"),
        name="fnet_front",
    )(u.reshape(b, n1, n2, B_WIDTH), _group_mean_matrix(), gain, ch, cl, sh, sl, m1h, m1l)

    kc = FFT_KC
    zspec = pl.BlockSpec((1, kc, n2, B_WIDTH), lambda i, j: (i, j, 0, 0))
    tspec = pl.BlockSpec((kc, n2, n2), lambda i, j: (j, 0, 0))
    out = pl.pallas_call(
        functools.partial(_fnet_back_kernel, kc=kc, scale=1.0 / math.sqrt(64.0 * n)),
        grid=(b, n1 // kc),
        in_specs=[zspec, zspec, tspec, tspec, tspec, tspec,
                  pl.BlockSpec((B_WIDTH, B_WIDTH), lambda i, j: (0, 0))],
        out_specs=pl.BlockSpec((1, n2, kc, B_WIDTH), lambda i, j: (i, 0, j, 0)),
        out_shape=jax.ShapeDtypeStruct((b, n2, n1, B_WIDTH), BF16),
        scratch_shapes=[pltpu.VMEM((2, n2 * kc, LANES), F32)],
        compiler_params=_cparams("parallel", "parallel"),
        name="fnet_back",
    )(zr, zi, tch, tcl, tsh, tsl, wbd)
    return out.reshape(b, n, B_WIDTH)


def _mem_kv_kernel(mem_ref, wk_ref, wvt_ref, k_ref, vt_ref):
    m = mem_ref[0].astype(BF16)
    k_ref[0] = _dot(m, wk_ref[...]).astype(BF16)
    vt_ref[0] = _dot_nt(wvt_ref[...], m).astype(BF16)


def _mem_kv(mem, w_kv):
    b, m, _ = mem.shape
    wk = w_kv[:, :D_MODEL].astype(BF16)
    wvt = w_kv[:, D_MODEL:].T.astype(BF16)
    wspec = pl.BlockSpec((D_MODEL, D_MODEL), lambda i: (0, 0))
    return pl.pallas_call(
        _mem_kv_kernel,
        grid=(b,),
        in_specs=[pl.BlockSpec((1, m, D_MODEL), lambda i: (i, 0, 0)), wspec, wspec],
        out_specs=[pl.BlockSpec((1, m, D_MODEL), lambda i: (i, 0, 0)),
                   pl.BlockSpec((1, D_MODEL, m), lambda i: (i, 0, 0))],
        out_shape=[jax.ShapeDtypeStruct((b, m, D_MODEL), BF16),
                   jax.ShapeDtypeStruct((b, D_MODEL, m), BF16)],
        compiler_params=_cparams("parallel"),
        name="mem_kv",
    )(mem, wk, wvt)


HALVES = (slice(0, TM // 2), slice(TM // 2, TM))


def _tail_after_mixer(hs, x_ref, g0_ref, b0_ref, wqt_ref, k_ref, vt_ref, wo_ref, g1_ref, b1_ref, o_ref):
    halves = HALVES
    x1s, qts = [], []
    for rows, h in zip(halves, hs):
        x1 = _layer_norm(DN_ALPHA * x_ref[0, rows, :] + h, g0_ref[...], b0_ref[...])
        x1s.append(x1)
        qts.append(_dot_nt(wqt_ref[...], x1.astype(BF16)).astype(BF16))
    qt = jnp.concatenate(qts, axis=1)
    heads = [slice(h * XA_HEAD_DIM, (h + 1) * XA_HEAD_DIM) for h in range(XA_HEADS)]
    scores = [_dot(k_ref[0, :, hd], qt[hd]) for hd in heads]
    ones = jnp.ones((BF16_ROWS, k_ref.shape[1]), BF16)
    outs = []
    for hd, st in zip(heads, scores):
        p = jnp.exp2(st - jnp.max(st, axis=0, keepdims=True)).astype(BF16)
        ot = _dot(jnp.concatenate([vt_ref[0, hd, :], ones], axis=0), p)
        outs.append((ot[:XA_HEAD_DIM] / ot[XA_HEAD_DIM:XA_HEAD_DIM + 1]).astype(BF16))
    ot_all = jnp.concatenate(outs, axis=0)
    hs = [_dot_tn(ot_all[:, rows], wo_ref[...]) for rows in halves]
    for rows, x1, h in zip(halves, x1s, hs):
        o_ref[0, rows, :] = _layer_norm(DN_ALPHA * x1 + h, g1_ref[...], b1_ref[...])


def _tail_ab_kernel(o1_ref, o4_ref, o16_ref, l1_ref, l4_ref, l16_ref, expand_ref, c_ref, wa_ref, wc_ref,
                    x_ref, g0_ref, b0_ref, wqt_ref, k_ref, vt_ref, wo_ref, g1_ref, b1_ref, o_ref, obuf, lbuf):
    o_a = _mixture_tile(o1_ref, o4_ref, o16_ref, l1_ref, l4_ref, l16_ref, expand_ref, obuf, lbuf)
    hs = [_dot(o_a[rows], wa_ref[...]) + _dot(c_ref[0, rows, :], wc_ref[...]) for rows in HALVES]
    _tail_after_mixer(hs, x_ref, g0_ref, b0_ref, wqt_ref, k_ref, vt_ref, wo_ref, g1_ref, b1_ref, o_ref)


def _tail_cd_kernel(at_ref, up_ref, uc_ref, un_ref, pw_ref, ps_ref, wa_ref, wc_ref,
                    x_ref, g0_ref, b0_ref, wqt_ref, k_ref, vt_ref, wo_ref, g1_ref, b1_ref, o_ref, pool_buf, *, n):
    o_d = _pool_tile(up_ref, uc_ref, un_ref, pw_ref, ps_ref, pool_buf, tm=TM, n=n)
    hs = [_dot_tn(at_ref[0, :, rows], wa_ref[...]) + _dot(o_d[rows], wc_ref[...]) for rows in HALVES]
    _tail_after_mixer(hs, x_ref, g0_ref, b0_ref, wqt_ref, k_ref, vt_ref, wo_ref, g1_ref, b1_ref, o_ref)


def _tail_specs(x, mem):
    m = mem.shape[1]
    xspec = pl.BlockSpec((1, TM, D_MODEL), lambda i, j: (i, j, 0))
    wspec = pl.BlockSpec((D_MODEL, D_MODEL), lambda i, j: (0, 0))
    vec = pl.BlockSpec((1, D_MODEL), lambda i, j: (0, 0))
    return [xspec, vec, vec, wspec, pl.BlockSpec((1, m, D_MODEL), lambda i, j: (i, 0, 0)),
            pl.BlockSpec((1, D_MODEL, m), lambda i, j: (i, 0, 0)), wspec, vec, vec], xspec


def _tail_operands(x, g0, b0, mem, w_q, w_kv, w_o, g1, b1):
    k, vt = _mem_kv(mem, w_kv)
    wqt = (w_q * (XA_HEAD_DIM ** -0.5 * LOG2E)).T.astype(BF16)
    row = lambda v: v.reshape(1, D_MODEL)
    return x, row(g0), row(b0), wqt, k, vt, w_o.astype(BF16), row(g1), row(b1)


def _tail_ab(os_, ls_, o_b, w_out, x, g0, b0, mem, w_q, w_kv, w_o, g1, b1):
    b, n, _ = x.shape
    dils = [d for _, d in A_PATTERNS]
    expand = jnp.asarray(np.arange(LANES)[:, None] == np.arange(A_WIDTH)[None, :] // HEAD_DIM, BF16)
    wa = w_out[:A_WIDTH].astype(BF16)
    wc = w_out[A_WIDTH:].astype(BF16)
    tail_specs, xspec = _tail_specs(x, mem)
    return pl.pallas_call(
        _tail_ab_kernel,
        grid=(b, n // TM),
        in_specs=[pl.BlockSpec((1, TM // d, d * A_WIDTH), lambda i, j: (i, j, 0)) for d in dils]
        + [pl.BlockSpec((1, TM // d, d * LANES), lambda i, j: (i, j, 0)) for d in dils]
        + [pl.BlockSpec((LANES, A_WIDTH), lambda i, j: (0, 0)),
           pl.BlockSpec((1, TM, B_WIDTH), lambda i, j: (i, j, 0)),
           pl.BlockSpec(wa.shape, lambda i, j: (0, 0)), pl.BlockSpec(wc.shape, lambda i, j: (0, 0))]
        + tail_specs,
        out_specs=xspec,
        out_shape=jax.ShapeDtypeStruct((b, n, D_MODEL), F32),
        scratch_shapes=[pltpu.VMEM((2, A_WIDTH // LANES, TM, LANES), F32), pltpu.VMEM((2, TM, LANES), F32)],
        compiler_params=_cparams("parallel", "parallel"),
        name="tail_ab",
    )(*os_, *ls_, expand, o_b, wa, wc, *_tail_operands(x, g0, b0, mem, w_q, w_kv, w_o, g1, b1))


def _tail_cd(o_ct, u, pool_w, pool_scale, w_out, x, g0, b0, mem, w_q, w_kv, w_o, g1, b1):
    b, n, _ = x.shape
    per = TM // POOL_HALO
    last = n // POOL_HALO - 1
    pool_wbd = jax.scipy.linalg.block_diag(*[pool_w[g] for g in range(4)]).astype(BF16)
    wa = w_out[:C_WIDTH].astype(BF16)
    wc = w_out[C_WIDTH:].astype(BF16)
    tail_specs, xspec = _tail_specs(x, mem)
    return pl.pallas_call(
        functools.partial(_tail_cd_kernel, n=n),
        grid=(b, n // TM),
        in_specs=[pl.BlockSpec((1, C_WIDTH, TM), lambda i, j: (i, 0, j)),
                  pl.BlockSpec((1, POOL_HALO, D_WIDTH), lambda i, j: (i, jnp.maximum(j * per - 1, 0), 0)),
                  pl.BlockSpec((1, TM, D_WIDTH), lambda i, j: (i, j, 0)),
                  pl.BlockSpec((1, POOL_HALO, D_WIDTH), lambda i, j: (i, jnp.minimum((j + 1) * per, last), 0)),
                  pl.BlockSpec((D_WIDTH, D_WIDTH), lambda i, j: (0, 0)),
                  pl.BlockSpec((1, D_WIDTH), lambda i, j: (0, 0)),
                  pl.BlockSpec(wa.shape, lambda i, j: (0, 0)), pl.BlockSpec(wc.shape, lambda i, j: (0, 0))]
        + tail_specs,
        out_specs=xspec,
        out_shape=jax.ShapeDtypeStruct((b, n, D_MODEL), F32),
        scratch_shapes=[pltpu.VMEM((TM + 2 * POOL_HALO, D_WIDTH), F32)],
        compiler_params=_cparams("parallel", "parallel"),
        name="tail_cd",
    )(o_ct, u, u, u, pool_wbd, pool_scale.reshape(1, D_WIDTH), wa, wc,
      *_tail_operands(x, g0, b0, mem, w_q, w_kv, w_o, g1, b1))


def _swiglu_kernel(x_ref, win_ref, wo_ref, g_ref, b_ref, o_ref):
    halves = [slice(0, TM // 2), slice(TM // 2, TM)]
    gate_up = []
    for rows in halves:
        xb = x_ref[0, rows, :].astype(BF16)
        gate_up.append((_dot(xb, win_ref[:, :FFN_HIDDEN]), _dot(xb, win_ref[:, FFN_HIDDEN:])))
    parts = []
    for gate, up in gate_up:
        hid = (gate * (1.0 / (1.0 + jnp.exp(-gate))) * up).astype(BF16)
        parts.append(_dot(hid, wo_ref[...]))
    for rows, part in zip(halves, parts):
        o_ref[0, rows, :] = _layer_norm(DN_ALPHA * x_ref[0, rows, :] + part, g_ref[...], b_ref[...])


def _swiglu(x, w_in, w_out, g, bias):
    b, n, _ = x.shape
    win = w_in.astype(BF16)
    wout = w_out.astype(BF16)
    xspec = pl.BlockSpec((1, TM, D_MODEL), lambda i, t: (i, t, 0))
    vec = pl.BlockSpec((1, D_MODEL), lambda i, t: (0, 0))
    resident = pl.Buffered(1)
    return pl.pallas_call(
        _swiglu_kernel,
        grid=(b, n // TM),
        in_specs=[xspec,
                  pl.BlockSpec(win.shape, lambda i, t: (0, 0), pipeline_mode=resident),
                  pl.BlockSpec(wout.shape, lambda i, t: (0, 0), pipeline_mode=resident), vec, vec],
        out_specs=xspec,
        out_shape=jax.ShapeDtypeStruct((b, n, D_MODEL), F32),
        compiler_params=_cparams("parallel", "parallel"),
        name="swiglu_ln",
    )(x, win, wout, g.reshape(1, D_MODEL), bias.reshape(1, D_MODEL))


N_QK_HEADS = C_Q_HEADS + C_KV_HEADS
QK_ROWS = N_QK_HEADS * HEAD_DIM


def _proj_cd_kernel(x_ref, wt_ref, wu_ref, gain_ref, cos_ref, sin_ref,
                    qt_ref, k_ref, vt_ref, u_ref, kn_ref, *, tm):
    xb = x_ref[0].astype(BF16)
    u_ref[0] = _dot(xb, wu_ref[...])
    zt = _dot_nt(wt_ref[...], xb)
    z = zt[:QK_ROWS].reshape(N_QK_HEADS, HEAD_DIM, tm)
    ssq = jnp.sum(z * z, axis=1, keepdims=True)
    zn = z * lax.rsqrt(ssq * (1.0 / HEAD_DIM) + RMS_EPS) * gain_ref[...]
    half = HEAD_DIM // 2
    x1 = zn[:, :half]
    x2 = zn[:, half:]
    c = cos_ref[...][None]
    s = sin_ref[...][None]
    rot = jnp.concatenate([x1 * c - x2 * s, x1 * s + x2 * c], axis=1).reshape(QK_ROWS, tm)
    qt_ref[0] = rot[:C_WIDTH].astype(BF16)
    kb = rot[C_WIDTH:].astype(BF16)
    k_ref[0] = kb.astype(F32).T.astype(BF16)
    kf = kb.astype(F32).reshape(C_KV_HEADS, HEAD_DIM, tm)
    kn_ref[0] = jnp.sum(kf * kf, axis=1, keepdims=True)
    vt = zt[QK_ROWS:].astype(BF16)
    for c0 in range(tm // GQA_TKC):
        vt_ref[0, c0] = vt[:, c0 * GQA_TKC:(c0 + 1) * GQA_TKC]


def _rope_tables_t(n):
    rows = n // GRID_W
    row_id = jnp.broadcast_to(jnp.arange(rows)[:, None], (rows, GRID_W)).reshape(n)
    col_id = jnp.broadcast_to(jnp.arange(GRID_W)[None, :], (rows, GRID_W)).reshape(n)
    axis_dim = HEAD_DIM // 2
    freqs = ROPE_THETA ** (-jnp.arange(0, axis_dim, 2, dtype=F32) / axis_dim)
    ang = jnp.concatenate([row_id[:, None] * freqs, col_id[:, None] * freqs], axis=-1)
    return jnp.cos(ang).T, jnp.sin(ang).T


def _proj_cd(x, w_in, q_norm, k_norm):
    b, n, _ = x.shape
    tm = TM
    half = HEAD_DIM // 2
    wqk = w_in[:, :QK_ROWS].reshape(D_MODEL, N_QK_HEADS, half, 2)
    wqk = wqk.transpose(1, 3, 2, 0).reshape(QK_ROWS, D_MODEL)
    wt = jnp.concatenate([wqk, w_in[:, QK_ROWS:QK_ROWS + C_KV_WIDTH].T], axis=0).astype(BF16)
    wu = w_in[:, QK_ROWS + C_KV_WIDTH:].astype(BF16)
    qg = q_norm.reshape(half, 2).T.reshape(HEAD_DIM) * (HEAD_DIM ** -0.5 * LOG2E)
    kg = k_norm.reshape(half, 2).T.reshape(HEAD_DIM)
    gain = jnp.concatenate([jnp.tile(qg[None], (C_Q_HEADS, 1)), jnp.tile(kg[None], (C_KV_HEADS, 1))])
    gain = gain.reshape(N_QK_HEADS, HEAD_DIM, 1).astype(F32)
    cos_t, sin_t = _rope_tables_t(n)
    nc = n // GQA_TKC
    return pl.pallas_call(
        functools.partial(_proj_cd_kernel, tm=tm),
        grid=(b, n // tm),
        in_specs=[pl.BlockSpec((1, tm, D_MODEL), lambda i, j: (i, j, 0)),
                  pl.BlockSpec(wt.shape, lambda i, j: (0, 0)),
                  pl.BlockSpec(wu.shape, lambda i, j: (0, 0)),
                  pl.BlockSpec(gain.shape, lambda i, j: (0, 0, 0)),
                  pl.BlockSpec((HEAD_DIM // 2, tm), lambda i, j: (0, j)),
                  pl.BlockSpec((HEAD_DIM // 2, tm), lambda i, j: (0, j))],
        out_specs=[pl.BlockSpec((1, C_WIDTH, tm), lambda i, j: (i, 0, j)),
                   pl.BlockSpec((1, tm, C_KV_WIDTH), lambda i, j: (i, j, 0)),
                   pl.BlockSpec((1, tm // GQA_TKC, C_KV_WIDTH, GQA_TKC), lambda i, j: (i, j, 0, 0)),
                   pl.BlockSpec((1, tm, D_WIDTH), lambda i, j: (i, j, 0)),
                   pl.BlockSpec((1, C_KV_HEADS, 1, tm), lambda i, j: (i, 0, 0, j))],
        out_shape=[jax.ShapeDtypeStruct((b, C_WIDTH, n), BF16),
                   jax.ShapeDtypeStruct((b, n, C_KV_WIDTH), BF16),
                   jax.ShapeDtypeStruct((b, nc, C_KV_WIDTH, GQA_TKC), BF16),
                   jax.ShapeDtypeStruct((b, n, D_WIDTH), F32),
                   jax.ShapeDtypeStruct((b, C_KV_HEADS, 1, n), F32)],
        compiler_params=_cparams("parallel", "parallel"),
        name="proj_cd",
    )(x, wt, wu, gain, cos_t, sin_t)


def _gqa_kernel(qt_ref, k_ref, vt_ref, kn_ref, ot_ref, m_ref, l_ref, acc_ref, qpad_ref, *, tq, n):
    g = pl.program_id(1)
    row_half = lax.broadcasted_iota(jnp.int32, (LANES, 1), 0) // HEAD_DIM
    mine = row_half == (g % 2)
    k_max2 = jnp.max(kn_ref[0, 0], axis=1, keepdims=True)
    bound_max = jnp.zeros((1, 1), F32)
    for j in range(C_REP):
        qj = qt_ref[0, j * HEAD_DIM:(j + 1) * HEAD_DIM, :]
        q2 = jnp.concatenate([qj, qj], axis=0)
        qpad_ref[j] = jnp.where(mine, q2, jnp.zeros_like(q2))
        qf = qj.astype(F32)
        bound = jnp.sqrt(jnp.sum(qf * qf, axis=0, keepdims=True) * k_max2) * GQA_BOUND_SLACK
        m_ref[j] = bound
        bound_max = jnp.maximum(bound_max, jnp.max(bound, axis=1, keepdims=True))
    fixed_shift = bound_max[0, 0] <= GQA_BOUND_LIMIT
    acc_ref[...] = jnp.zeros(acc_ref.shape, F32)
    ones = jnp.ones((GQA_VROWS - HEAD_DIM, GQA_TKC), BF16)
    unroll = min(GQA_PIECES // (C_REP * (tq // GQA_TW)), n // GQA_TKC)

    def body(c, carry, *, online):
        kchs, vchs = [], []
        for u in range(unroll):
            cc = c * unroll + u
            start = pl.multiple_of(cc * GQA_TKC, GQA_TKC)
            kchs.append(k_ref[0, pl.ds(start, GQA_TKC), :])
            vchs.append(jnp.concatenate([vt_ref[0, cc], ones], axis=0))
        pieces = [(u, j, slice(s * GQA_TW, (s + 1) * GQA_TW))
                  for u in range(unroll) for j in range(C_REP) for s in range(tq // GQA_TW)]

        def scores(i):
            u, j, cols = pieces[i]
            return _dot(kchs[u], qpad_ref[j, :, cols])

        pending = [scores(i) for i in range(GQA_AHEAD)]
        for i, (u, j, cols) in enumerate(pieces):
            st = pending.pop(0)
            if i + GQA_AHEAD < len(pieces):
                pending.append(scores(i + GQA_AHEAD))
            m_old = m_ref[j, :, cols]
            if not online:
                p = jnp.exp2(st - m_old)
                l_ref[j, :, cols] = l_ref[j, :, cols] + jnp.sum(p.reshape(GQA_TKC // 8, 8, GQA_TW), axis=0)
                acc_ref[j, :HEAD_DIM, cols] = acc_ref[j, :HEAD_DIM, cols] + _dot(
                    vchs[u][:HEAD_DIM], p.astype(BF16))
                continue
            m_new = jnp.maximum(m_old, jnp.max(st, axis=0, keepdims=True))
            alpha = jnp.exp2(m_old - m_new)
            p = jnp.exp2(st - m_new)
            acc_ref[j, :, cols] = alpha * acc_ref[j, :, cols] + _dot(vchs[u], p.astype(BF16))
            m_ref[j, :, cols] = m_new
        return carry

    trips = n // (GQA_TKC * unroll)

    @pl.when(fixed_shift)
    def _():
        l_ref[...] = jnp.zeros(l_ref.shape, F32)
        lax.fori_loop(0, trips, functools.partial(body, online=False), 0)
        for j in range(C_REP):
            acc_ref[j, HEAD_DIM:HEAD_DIM + 1, :] = jnp.sum(l_ref[j], axis=0, keepdims=True)

    @pl.when(jnp.logical_not(fixed_shift))
    def _():
        m_ref[...] = jnp.full(m_ref.shape, NEG_INF, F32)
        lax.fori_loop(0, trips, functools.partial(body, online=True), 0)

    for j in range(C_REP):
        l = acc_ref[j, HEAD_DIM:HEAD_DIM + 1, :]
        ot_ref[0, j * HEAD_DIM:(j + 1) * HEAD_DIM, :] = (acc_ref[j, :HEAD_DIM, :] / l).astype(BF16)


def _gqa(qt, k, vt, kn):
    b, _, n = qt.shape
    tq = min(GQA_TQ, n)
    nc = n // GQA_TKC
    rows = C_REP * HEAD_DIM
    return pl.pallas_call(
        functools.partial(_gqa_kernel, tq=tq, n=n),
        grid=(b, C_KV_HEADS, n // tq),
        in_specs=[pl.BlockSpec((1, rows, tq), lambda i, g, t: (i, g, t)),
                  pl.BlockSpec((1, n, LANES), lambda i, g, t: (i, 0, g // 2)),
                  pl.BlockSpec((1, nc, HEAD_DIM, GQA_TKC), lambda i, g, t: (i, 0, g, 0)),
                  pl.BlockSpec((1, 1, 1, n), lambda i, g, t: (i, g, 0, 0))],
        out_specs=pl.BlockSpec((1, rows, tq), lambda i, g, t: (i, g, t)),
        out_shape=jax.ShapeDtypeStruct((b, C_WIDTH, n), BF16),
        scratch_shapes=[pltpu.VMEM((C_REP, 1, tq), F32), pltpu.VMEM((C_REP, 8, tq), F32),
                        pltpu.VMEM((C_REP, GQA_VROWS, tq), F32), pltpu.VMEM((C_REP, LANES, tq), BF16)],
        compiler_params=_cparams("parallel", "parallel", "parallel"),
        name="gqa_flash",
    )(qt, k, vt, kn)


def _pool_tile(up_ref, uc_ref, un_ref, w_ref, scale_ref, buf, *, tm, n):
    i = pl.program_id(1)
    cur = uc_ref[0]
    buf[0:POOL_HALO, :] = jnp.where(i > 0, up_ref[0], 0.0)
    buf[POOL_HALO:POOL_HALO + tm, :] = cur
    buf[POOL_HALO + tm:, :] = jnp.where(i < pl.num_programs(1) - 1, un_ref[0], 0.0)
    lane_group = lax.broadcasted_iota(jnp.int32, (1, D_WIDTH), 1) // 64
    half_w = jnp.left_shift(1, lane_group)
    acc = jnp.zeros((tm, D_WIDTH), F32)
    for j in range(-POOL_HALO, POOL_HALO):
        inside = (j >= -half_w) & (j < half_w)
        acc = acc + jnp.where(inside, buf[POOL_HALO + j:POOL_HALO + j + tm, :], 0.0)
    t = i * tm + lax.broadcasted_iota(jnp.int32, (tm, 1), 0)
    cnt = jnp.minimum(t + half_w, n) - jnp.maximum(t - half_w, 0)
    mixed = (acc / cnt.astype(F32) - cur).astype(BF16)
    return (_dot(mixed, w_ref[...]) * scale_ref[...]).astype(BF16)


def _trunk(x, mem, rel_bias, ab_w_in, ab_fnet_g, ab_fnet_w, ab_w_out,
           cd_w_in, cd_q_norm, cd_k_norm, cd_pool_w, cd_pool_scale, cd_w_out,
           xa_w_q, xa_w_kv, xa_w_o, ffn_w_in, ffn_w_out, ln_g, ln_b):
    def tail_args(layer):
        return (ln_g[layer, 0], ln_b[layer, 0], mem, xa_w_q[layer], xa_w_kv[layer], xa_w_o[layer],
                ln_g[layer, 1], ln_b[layer, 1])

    for layer in range(DEPTH):
        i = layer // 2
        if layer % 2 == 0:
            w_in = ab_w_in[i]
            w_in = jnp.concatenate([w_in[:, :A_WIDTH] * (HEAD_DIM ** -0.5 * LOG2E), w_in[:, A_WIDTH:]], axis=1)
            *views, u = _proj_ab(x, w_in.astype(BF16))
            outs = [_dilated(view, rel_bias, d) for view, (_, d) in zip(views, A_PATTERNS)]
            o_b = _fnet(u, ab_fnet_g[i], ab_fnet_w[i])
            x = _tail_ab([o for o, _ in outs], [l for _, l in outs], o_b, ab_w_out[i], x, *tail_args(layer))
        else:
            qt, k, vt, u, kn = _proj_cd(x, cd_w_in[i], cd_q_norm[i], cd_k_norm[i])
            o_c = _gqa(qt, k, vt, kn)
            x = _tail_cd(o_c, u, cd_pool_w[i], cd_pool_scale[i], cd_w_out[i], x, *tail_args(layer))
        x = _swiglu(x, ffn_w_in[layer], ffn_w_out[layer], ln_g[layer, 2], ln_b[layer, 2])
    return x


def kernel(x_prompt, x_sample, mem_prompt, mem_sample, rel_bias, ab_w_in, ab_fnet_g, ab_fnet_w, ab_w_out, cd_w_in, cd_q_norm, cd_k_norm, cd_pool_w, cd_pool_scale, cd_w_out, xa_w_q, xa_w_kv, xa_w_o, ffn_w_in, ffn_w_out, ln_g, ln_b):
    params = (rel_bias, ab_w_in, ab_fnet_g, ab_fnet_w, ab_w_out,
              cd_w_in, cd_q_norm, cd_k_norm, cd_pool_w, cd_pool_scale, cd_w_out,
              xa_w_q, xa_w_kv, xa_w_o, ffn_w_in, ffn_w_out, ln_g, ln_b)
    return (_trunk(x_prompt, mem_prompt, *params), _trunk(x_sample, mem_sample, *params))
```

```python
import functools
import math

import numpy as np
import jax
import jax.numpy as jnp
from jax import lax
from jax.experimental import pallas as pl
from jax.experimental.pallas import tpu as pltpu

F32 = jnp.float32
BF16 = jnp.bfloat16

D_MODEL = 1024
HEAD_DIM = 64
GRID_W = 64
LN_EPS = 1e-5
RMS_EPS = 1e-6
NEG_INF = -1e30
DEPTH = 2
A_HEADS = 12
A_WIDTH = A_HEADS * HEAD_DIM
A_PATTERNS = ((128, 1), (512, 4), (2048, 16))
A_HALF = 64
QKV_WIDTH = 3 * A_WIDTH
N_BUCKETS = 32
REL_MAX_DIST = 1024
B_WIDTH = 256
C_Q_HEADS = 12
C_KV_HEADS = 4
C_REP = C_Q_HEADS // C_KV_HEADS
C_WIDTH = C_Q_HEADS * HEAD_DIM
C_KV_WIDTH = C_KV_HEADS * HEAD_DIM
ROPE_THETA = 10000.0
POOL_WINDOWS = (2, 4, 8, 16)
POOL_HALO = 8
D_WIDTH = 256
XA_HEADS = 4
XA_HEAD_DIM = D_MODEL // XA_HEADS
FFN_HIDDEN = 2816
DN_ALPHA = (2 * DEPTH) ** 0.25
LOG2E = 1.4426950408889634

LANES = 128
MXU_TILE = 256
BF16_ROWS = 16
VMEM_LIMIT = 56 * 1024 * 1024
TM = 512
FFT_N2 = 128
FFT_ROWS = 1024
FFT_KC = 16
FFT_AHEAD = 3
DIL_AHEAD = 3
DIL_TQ = 512
DIL_SQ = MXU_TILE // 2
DIL_SK = DIL_SQ + 2 * A_HALF
DIL_PIECES = 48
GQA_TQ = 8192
GQA_TW = MXU_TILE
GQA_TKC = 256
GQA_VROWS = HEAD_DIM + BF16_ROWS
GQA_PIECES = 192
GQA_AHEAD = 4
GQA_BOUND_SLACK = 1.0 + 2.0 ** -10
GQA_BOUND_LIMIT = 60.0


def _cparams(*sem):
    return pltpu.CompilerParams(dimension_semantics=sem, vmem_limit_bytes=VMEM_LIMIT)


def _dot(a, b):
    return jnp.dot(a, b, preferred_element_type=F32)


def _dot_nt(a, b):
    return lax.dot_general(a, b, (((1,), (1,)), ((), ())), preferred_element_type=F32)


def _dot_tn(a, b):
    return lax.dot_general(a, b, (((0,), (0,)), ((), ())), preferred_element_type=F32)


def _split(x):
    hi = x.astype(BF16)
    lo = (x - hi.astype(F32)).astype(BF16)
    return hi, lo


def _dot3(ah, al, bh, bl):
    return _dot(ah, bh) + _dot(al, bh) + _dot(ah, bl)


def _np_split(x):
    x = np.asarray(x, np.float32)
    hi = x.astype(BF16)
    lo = (x - hi.astype(np.float32)).astype(BF16)
    return jnp.asarray(hi), jnp.asarray(lo)


def _layer_norm(h, g, b):
    mu = jnp.mean(h, axis=-1, keepdims=True)
    xc = h - mu
    var = jnp.mean(xc * xc, axis=-1, keepdims=True)
    return xc * lax.rsqrt(var + LN_EPS) * g + b


def _proj_ab_kernel(x_ref, w_ref, qkv1_ref, qkv4_ref, qkv16_ref, u_ref, zbuf):
    xb = x_ref[0].astype(BF16)
    chunks = list(range(0, QKV_WIDTH, MXU_TILE))

    for c in chunks:
        z = _dot(xb, w_ref[:, c:c + MXU_TILE])
        zbuf[c // LANES] = z[:, :LANES]
        zbuf[c // LANES + 1] = z[:, LANES:]
        qkv1_ref[0, :, c:c + MXU_TILE] = z.astype(BF16)
    u_ref[0] = _dot(xb, w_ref[:, QKV_WIDTH:])
    for (_, d), ref in zip(A_PATTERNS[1:], (qkv4_ref, qkv16_ref)):
        rows = TM // d
        for r in range(d):
            for ct in range(QKV_WIDTH // LANES):
                col = r * QKV_WIDTH + ct * LANES
                ref[0, :, col:col + LANES] = zbuf[ct, pl.ds(r, rows, stride=d), :].astype(BF16)


def _proj_ab(x, w):
    b, n, _ = x.shape
    dils = [d for _, d in A_PATTERNS]
    return pl.pallas_call(
        _proj_ab_kernel,
        grid=(b, n // TM),
        in_specs=[pl.BlockSpec((1, TM, D_MODEL), lambda i, j: (i, j, 0)),
                  pl.BlockSpec(w.shape, lambda i, j: (0, 0))],
        out_specs=[pl.BlockSpec((1, TM // d, d * QKV_WIDTH), lambda i, j: (i, j, 0)) for d in dils]
        + [pl.BlockSpec((1, TM, B_WIDTH), lambda i, j: (i, j, 0))],
        out_shape=[jax.ShapeDtypeStruct((b, n // d, d * QKV_WIDTH), BF16) for d in dils]
        + [jax.ShapeDtypeStruct((b, n, B_WIDTH), F32)],
        scratch_shapes=[pltpu.VMEM((QKV_WIDTH // LANES, TM, LANES), F32)],
        compiler_params=_cparams("parallel", "parallel"),
        name="proj_ab",
    )(x, w)


def _t5_bucket_np(rel):
    nb = N_BUCKETS // 2
    max_exact = nb // 2
    ret = np.where(rel > 0, nb, 0)
    n = np.abs(rel)
    nf = np.maximum(n, 1).astype(np.float32)
    large = max_exact + (np.log(nf / max_exact) / math.log(REL_MAX_DIST / max_exact)
                         * (nb - max_exact)).astype(np.int32)
    large = np.minimum(large, nb - 1)
    return ret + np.where(n < max_exact, n, large)


def _band_bias(rel_bias, dilation):
    tq, tk = DIL_SQ, DIL_SK
    band = 2 * A_HALF + 1
    bucket = _t5_bucket_np((np.arange(band) - A_HALF) * dilation)
    row = (rel_bias[jnp.asarray(bucket)].T * LOG2E).astype(F32)
    row = jnp.concatenate([row, jnp.full((A_HEADS, tk + 1 - band), NEG_INF, F32)], axis=1)
    bias = jnp.tile(row, (1, tq))[:, :tq * tk].reshape(A_HEADS, tq, tk)
    bias = bias.transpose(0, 2, 1)
    bias = bias.reshape(A_HEADS // 2, 2, tk, tq).transpose(0, 2, 1, 3).reshape(A_HEADS // 2, tk, 2 * tq)
    key = np.arange(tk)[None, :, None]
    before = jnp.asarray(key < A_HALF)
    after = jnp.asarray(key >= A_HALF + tq)
    first = jnp.where(before, NEG_INF, bias)
    return jnp.stack([bias, first, jnp.where(after, NEG_INF, bias), jnp.where(after, NEG_INF, first)])


def _dilated_kernel(prev_ref, cur_ref, next_ref, bias_ref, o_ref, lse_ref, kbuf, vbuf, *, tq, n_res):
    for rr in range(n_res):
        for buf, c0 in ((kbuf, rr * QKV_WIDTH + A_WIDTH), (vbuf, rr * QKV_WIDTH + 2 * A_WIDTH)):
            buf[rr, 0:A_HALF, :] = prev_ref[0, :, c0:c0 + A_WIDTH]
            buf[rr, A_HALF:A_HALF + tq, :] = cur_ref[0, :, c0:c0 + A_WIDTH]
            buf[rr, A_HALF + tq:, :] = next_ref[0, :, c0:c0 + A_WIDTH]

    i = pl.program_id(1)
    n_sub = tq // DIL_SQ
    row_low = lax.broadcasted_iota(jnp.int32, (LANES, 1), 0) < HEAD_DIM
    row16 = lax.broadcasted_iota(jnp.int32, (BF16_ROWS, 1), 0)
    ones = jnp.ones((BF16_ROWS, DIL_SK), BF16)
    qts, vts = {}, {}
    pieces = [(rr, pair, sub) for rr in range(n_res) for pair in range(A_HEADS // 2) for sub in range(n_sub)]

    def variant(sub):
        v = 0
        if sub == 0:
            v = v + jnp.where(i == 0, 1, 0)
        if sub == n_sub - 1:
            v = v + jnp.where(i == pl.num_programs(1) - 1, 2, 0)
        return v

    def scores(idx):
        rr, pair, sub = pieces[idx]
        cols = slice(pair * LANES, (pair + 1) * LANES)
        if (rr, pair) not in qts:
            qcols = slice(rr * QKV_WIDTH + pair * LANES, rr * QKV_WIDTH + (pair + 1) * LANES)
            qts[rr, pair] = cur_ref[0, :, qcols].astype(F32).T
        qsub = qts[rr, pair][:, sub * DIL_SQ:(sub + 1) * DIL_SQ]
        rhs = jnp.concatenate([jnp.where(row_low, qsub, 0.0), jnp.where(row_low, 0.0, qsub)], axis=1)
        keys = kbuf[rr, sub * DIL_SQ:sub * DIL_SQ + DIL_SK, cols]
        return _dot(keys, rhs.astype(BF16)) + bias_ref[variant(sub), pair]

    pending = [scores(idx) for idx in range(DIL_AHEAD)]
    lse_t = {(rr, sub): jnp.zeros((BF16_ROWS, DIL_SQ), F32) for rr in range(n_res) for sub in range(n_sub)}
    for idx, (rr, pair, sub) in enumerate(pieces):
        st = pending.pop(0)
        if idx + DIL_AHEAD < len(pieces):
            pending.append(scores(idx + DIL_AHEAD))
        cols = slice(pair * LANES, (pair + 1) * LANES)
        m = jnp.max(st, axis=0, keepdims=True)
        p = jnp.exp2(st - m).astype(BF16)
        if (rr, pair) not in vts:
            vts[rr, pair] = vbuf[rr, :, cols].astype(F32).T.astype(BF16)
        vaug = jnp.concatenate([vts[rr, pair][:, sub * DIL_SQ:sub * DIL_SQ + DIL_SK], ones], axis=0)
        ot = _dot(vaug, p)
        l = ot[LANES:LANES + 1]
        lse2 = m + jnp.log2(l)
        o_pair = jnp.concatenate([ot[:HEAD_DIM, :DIL_SQ] / l[:, :DIL_SQ],
                                  ot[HEAD_DIM:LANES, DIL_SQ:] / l[:, DIL_SQ:]], axis=0)
        ocols = slice(rr * A_WIDTH + pair * LANES, rr * A_WIDTH + (pair + 1) * LANES)
        o_ref[0, sub * DIL_SQ:(sub + 1) * DIL_SQ, ocols] = o_pair.T.astype(BF16)
        lse_t[rr, sub] = jnp.where(row16 == 2 * pair, lse2[:, :DIL_SQ],
                                   jnp.where(row16 == 2 * pair + 1, lse2[:, DIL_SQ:], lse_t[rr, sub]))
    for (rr, sub), rows16 in lse_t.items():
        full = jnp.concatenate([rows16, jnp.zeros((LANES - BF16_ROWS, DIL_SQ), F32)], axis=0)
        lse_ref[0, sub * DIL_SQ:(sub + 1) * DIL_SQ, rr * LANES:(rr + 1) * LANES] = full.T


def _dilated(view, rel_bias, dilation):
    b, seq, _ = view.shape
    tq = min(DIL_TQ, seq)
    tk = tq + 2 * A_HALF
    n_res = min(dilation, max(1, DIL_PIECES // (A_HEADS // 2 * (tq // DIL_SQ))))
    bias = _band_bias(rel_bias, dilation)
    per = tq // A_HALF
    last = seq // A_HALF - 1
    width = n_res * QKV_WIDTH

    o, lse = pl.pallas_call(
        functools.partial(_dilated_kernel, tq=tq, n_res=n_res),
        grid=(b, seq // tq, dilation // n_res),
        in_specs=[pl.BlockSpec((1, A_HALF, width), lambda bi, i, r: (bi, jnp.maximum(i * per - 1, 0), r)),
                  pl.BlockSpec((1, tq, width), lambda bi, i, r: (bi, i, r)),
                  pl.BlockSpec((1, A_HALF, width), lambda bi, i, r: (bi, jnp.minimum((i + 1) * per, last), r)),
                  pl.BlockSpec(bias.shape, lambda bi, i, r: (0, 0, 0, 0))],
        out_specs=[pl.BlockSpec((1, tq, n_res * A_WIDTH), lambda bi, i, r: (bi, i, r)),
                   pl.BlockSpec((1, tq, n_res * LANES), lambda bi, i, r: (bi, i, r))],
        out_shape=[jax.ShapeDtypeStruct((b, seq, dilation * A_WIDTH), BF16),
                   jax.ShapeDtypeStruct((b, seq, dilation * LANES), F32)],
        scratch_shapes=[pltpu.VMEM((n_res, tk, A_WIDTH), BF16), pltpu.VMEM((n_res, tk, A_WIDTH), BF16)],
        compiler_params=_cparams("parallel", "parallel", "parallel"),
        name=f"dilated_d{dilation}",
    )(view, view, view, bias)
    return o, lse


def _mixture_tile(o1_ref, o4_ref, o16_ref, l1_ref, l4_ref, l16_ref, expand_ref, obuf, lbuf):
    for idx, (d, o_ref, l_ref) in enumerate(((A_PATTERNS[1][1], o4_ref, l4_ref), (A_PATTERNS[2][1], o16_ref, l16_ref))):
        rows = TM // d
        for r in range(d):
            for ct in range(A_WIDTH // LANES):
                col = r * A_WIDTH + ct * LANES
                obuf[idx, ct, pl.ds(r, rows, stride=d), :] = o_ref[0, :, col:col + LANES].astype(F32)
            lbuf[idx, pl.ds(r, rows, stride=d), :] = l_ref[0, :, r * LANES:(r + 1) * LANES]
    ls = [l1_ref[0], lbuf[0], lbuf[1]]
    mx = jnp.maximum(jnp.maximum(ls[0], ls[1]), ls[2])
    es = [jnp.exp2(l - mx) for l in ls]
    inv = 1.0 / (es[0] + es[1] + es[2])
    ws = []
    for e in es:
        wh, wl = _split(e * inv)
        ws.append(_dot(wh, expand_ref[...]) + _dot(wl, expand_ref[...]))
    pairs = []
    for pair in range(A_HEADS // 2):
        cols = slice(pair * LANES, (pair + 1) * LANES)
        os_ = [o1_ref[0, :, cols].astype(F32), obuf[0, pair], obuf[1, pair]]
        acc = None
        for g in range(3):
            t = ws[g][:, cols] * os_[g]
            acc = t if acc is None else acc + t
        pairs.append(acc.astype(BF16))
    return jnp.concatenate(pairs, axis=1)


def _group_mean_matrix():
    g = np.kron(np.eye(4), np.full((64, 64), 1.0 / 64))
    return jnp.asarray(g, BF16)


def _fnet_front_kernel(u_ref, gm_ref, gain_ref, ch_ref, cl_ref, sh_ref, sl_ref, m1h_ref, m1l_ref,
                       zr_ref, zi_ref, ybuf, zbuf, *, n1, tcn):
    rows = n1 * tcn
    u = u_ref[0].reshape(rows, B_WIDTH)
    gm = gm_ref[...]
    uh, ul = _split(u)
    mean = _dot(uh, gm) + _dot(ul, gm)
    xc = u - mean
    qh, ql = _split(xc * xc)
    var = _dot(qh, gm) + _dot(ql, gm)
    un = xc * lax.rsqrt(var + LN_EPS) * gain_ref[...]
    nh, nl = _split(un)
    yr = _dot3(nh, nl, ch_ref[...], cl_ref[...])
    yi = -_dot3(nh, nl, sh_ref[...], sl_ref[...])
    for q, val in enumerate((yr[:, :LANES], yr[:, LANES:], yi[:, :LANES], yi[:, LANES:])):
        ybuf[q] = val
    def first_stage(n2):
        col = [ybuf[q, pl.ds(n2, n1, stride=tcn), :] for q in range(4)]
        y = jnp.concatenate([jnp.concatenate(col[:2], axis=1), jnp.concatenate(col[2:], axis=1)], axis=0)
        yh, yl = _split(y)
        return _dot3(m1h_ref[...], m1l_ref[...], yh, yl)

    pending = [first_stage(n2) for n2 in range(min(FFT_AHEAD, tcn))]
    for n2 in range(tcn):
        z = pending.pop(0)
        if n2 + FFT_AHEAD < tcn:
            pending.append(first_stage(n2 + FFT_AHEAD))
        for q, val in enumerate((z[:n1, :LANES], z[:n1, LANES:], z[n1:, :LANES], z[n1:, LANES:])):
            zbuf[q, pl.ds(n2, n1, stride=tcn), :] = val
    zr_ref[0] = jnp.concatenate([zbuf[0], zbuf[1]], axis=1).reshape(n1, tcn, B_WIDTH)
    zi_ref[0] = jnp.concatenate([zbuf[2], zbuf[3]], axis=1).reshape(n1, tcn, B_WIDTH)


def _fnet_back_kernel(zr_ref, zi_ref, tch_ref, tcl_ref, tsh_ref, tsl_ref, w_ref, o_ref, obuf, *, kc, scale):
    n2 = FFT_N2

    def second_stage(kk):
        rh, rl = _split(zr_ref[0, kk])
        ih, il = _split(zi_ref[0, kk])
        return _dot3(tch_ref[kk], tcl_ref[kk], rh, rl) + _dot3(tsh_ref[kk], tsl_ref[kk], ih, il)

    pending = [second_stage(kk) for kk in range(FFT_AHEAD)]
    for kk in range(kc):
        f = pending.pop(0)
        if kk + FFT_AHEAD < kc:
            pending.append(second_stage(kk + FFT_AHEAD))
        o = _dot((f * scale).astype(BF16), w_ref[...])
        obuf[0, pl.ds(kk, n2, stride=kc), :] = o[:, :LANES]
        obuf[1, pl.ds(kk, n2, stride=kc), :] = o[:, LANES:]
    out = jnp.concatenate([obuf[0], obuf[1]], axis=1).reshape(n2, kc, B_WIDTH)
    o_ref[0] = out.astype(BF16)


def _fnet_tables(n):
    n2 = FFT_N2
    n1 = n // n2
    c = np.arange(64)
    ang = 2 * np.pi * np.outer(c, c) / 64
    cbd = np.kron(np.eye(4), np.cos(ang))
    sbd = np.kron(np.eye(4), np.sin(ang))
    k1 = np.arange(n1)
    ang1 = 2 * np.pi * np.outer(k1, k1) / n1
    c1, s1 = np.cos(ang1), np.sin(ang1)
    m1 = np.block([[c1, s1], [-s1, c1]])
    k2 = np.arange(n2)
    npr = k1[:, None, None] + n1 * k2[None, :, None]
    prod = (npr * k2[None, None, :]) % n
    ang2 = 2 * np.pi * prod / n
    return (_np_split(cbd), _np_split(sbd), _np_split(m1), _np_split(np.cos(ang2)), _np_split(np.sin(ang2)))


def _fnet(u, fnet_g, fnet_w):
    b, n, _ = u.shape
    n2 = FFT_N2
    n1 = n // n2
    tcn = FFT_ROWS // n1
    (ch, cl), (sh, sl), (m1h, m1l), (tch, tcl), (tsh, tsl) = _fnet_tables(n)
    gain = fnet_g.reshape(1, B_WIDTH)
    wbd = jax.scipy.linalg.block_diag(*[fnet_w[g] for g in range(4)]).astype(BF16)

    mat = pl.BlockSpec((B_WIDTH, B_WIDTH), lambda i, j: (0, 0))
    m1spec = pl.BlockSpec((2 * n1, 2 * n1), lambda i, j: (0, 0))
    tile = pl.BlockSpec((1, n1, tcn, B_WIDTH), lambda i, j: (i, 0, j, 0))
    zr, zi = pl.pallas_call(
        functools.partial(_fnet_front_kernel, n1=n1, tcn=tcn),
        grid=(b, n2 // tcn),
        in_specs=[tile, mat, pl.BlockSpec((1, B_WIDTH), lambda i, j: (0, 0)), mat, mat, mat, mat, m1spec, m1spec],
        out_specs=[tile, tile],
        out_shape=[jax.ShapeDtypeStruct((b, n1, n2, B_WIDTH), F32)] * 2,
        scratch_shapes=[pltpu.VMEM((4, FFT_ROWS, LANES), F32), pltpu.VMEM((4, FFT_ROWS, LANES), F32)],
        compiler_params=_cparams("parallel", "parallel"),
        name="fnet_front",
    )(u.reshape(b, n1, n2, B_WIDTH), _group_mean_matrix(), gain, ch, cl, sh, sl, m1h, m1l)

    kc = FFT_KC
    zspec = pl.BlockSpec((1, kc, n2, B_WIDTH), lambda i, j: (i, j, 0, 0))
    tspec = pl.BlockSpec((kc, n2, n2), lambda i, j: (j, 0, 0))
    out = pl.pallas_call(
        functools.partial(_fnet_back_kernel, kc=kc, scale=1.0 / math.sqrt(64.0 * n)),
        grid=(b, n1 // kc),
        in_specs=[zspec, zspec, tspec, tspec, tspec, tspec,
                  pl.BlockSpec((B_WIDTH, B_WIDTH), lambda i, j: (0, 0))],
        out_specs=pl.BlockSpec((1, n2, kc, B_WIDTH), lambda i, j: (i, 0, j, 0)),
        out_shape=jax.ShapeDtypeStruct((b, n2, n1, B_WIDTH), BF16),
        scratch_shapes=[pltpu.VMEM((2, n2 * kc, LANES), F32)],
        compiler_params=_cparams("parallel", "parallel"),
        name="fnet_back",
    )(zr, zi, tch, tcl, tsh, tsl, wbd)
    return out.reshape(b, n, B_WIDTH)


def _mem_kv_kernel(mem_ref, wk_ref, wvt_ref, k_ref, vt_ref):
    m = mem_ref[0].astype(BF16)
    k_ref[0] = _dot(m, wk_ref[...]).astype(BF16)
    vt_ref[0] = _dot_nt(wvt_ref[...], m).astype(BF16)


def _mem_kv(mem, w_kv):
    b, m, _ = mem.shape
    wk = w_kv[:, :D_MODEL].astype(BF16)
    wvt = w_kv[:, D_MODEL:].T.astype(BF16)
    wspec = pl.BlockSpec((D_MODEL, D_MODEL), lambda i: (0, 0))
    return pl.pallas_call(
        _mem_kv_kernel,
        grid=(b,),
        in_specs=[pl.BlockSpec((1, m, D_MODEL), lambda i: (i, 0, 0)), wspec, wspec],
        out_specs=[pl.BlockSpec((1, m, D_MODEL), lambda i: (i, 0, 0)),
                   pl.BlockSpec((1, D_MODEL, m), lambda i: (i, 0, 0))],
        out_shape=[jax.ShapeDtypeStruct((b, m, D_MODEL), BF16),
                   jax.ShapeDtypeStruct((b, D_MODEL, m), BF16)],
        compiler_params=_cparams("parallel"),
        name="mem_kv",
    )(mem, wk, wvt)


HALVES = (slice(0, TM // 2), slice(TM // 2, TM))


def _tail_after_mixer(hs, x_ref, g0_ref, b0_ref, wqt_ref, k_ref, vt_ref, wo_ref, g1_ref, b1_ref, o_ref):
    halves = HALVES
    x1s, qts = [], []
    for rows, h in zip(halves, hs):
        x1 = _layer_norm(DN_ALPHA * x_ref[0, rows, :] + h, g0_ref[...], b0_ref[...])
        x1s.append(x1)
        qts.append(_dot_nt(wqt_ref[...], x1.astype(BF16)).astype(BF16))
    qt = jnp.concatenate(qts, axis=1)
    heads = [slice(h * XA_HEAD_DIM, (h + 1) * XA_HEAD_DIM) for h in range(XA_HEADS)]
    scores = [_dot(k_ref[0, :, hd], qt[hd]) for hd in heads]
    ones = jnp.ones((BF16_ROWS, k_ref.shape[1]), BF16)
    outs = []
    for hd, st in zip(heads, scores):
        p = jnp.exp2(st - jnp.max(st, axis=0, keepdims=True)).astype(BF16)
        ot = _dot(jnp.concatenate([vt_ref[0, hd, :], ones], axis=0), p)
        outs.append((ot[:XA_HEAD_DIM] / ot[XA_HEAD_DIM:XA_HEAD_DIM + 1]).astype(BF16))
    ot_all = jnp.concatenate(outs, axis=0)
    hs = [_dot_tn(ot_all[:, rows], wo_ref[...]) for rows in halves]
    for rows, x1, h in zip(halves, x1s, hs):
        o_ref[0, rows, :] = _layer_norm(DN_ALPHA * x1 + h, g1_ref[...], b1_ref[...])


def _tail_ab_kernel(o1_ref, o4_ref, o16_ref, l1_ref, l4_ref, l16_ref, expand_ref, c_ref, wa_ref, wc_ref,
                    x_ref, g0_ref, b0_ref, wqt_ref, k_ref, vt_ref, wo_ref, g1_ref, b1_ref, o_ref, obuf, lbuf):
    o_a = _mixture_tile(o1_ref, o4_ref, o16_ref, l1_ref, l4_ref, l16_ref, expand_ref, obuf, lbuf)
    hs = [_dot(o_a[rows], wa_ref[...]) + _dot(c_ref[0, rows, :], wc_ref[...]) for rows in HALVES]
    _tail_after_mixer(hs, x_ref, g0_ref, b0_ref, wqt_ref, k_ref, vt_ref, wo_ref, g1_ref, b1_ref, o_ref)


def _tail_cd_kernel(at_ref, up_ref, uc_ref, un_ref, pw_ref, ps_ref, wa_ref, wc_ref,
                    x_ref, g0_ref, b0_ref, wqt_ref, k_ref, vt_ref, wo_ref, g1_ref, b1_ref, o_ref, pool_buf, *, n):
    o_d = _pool_tile(up_ref, uc_ref, un_ref, pw_ref, ps_ref, pool_buf, tm=TM, n=n)
    hs = [_dot_tn(at_ref[0, :, rows], wa_ref[...]) + _dot(o_d[rows], wc_ref[...]) for rows in HALVES]
    _tail_after_mixer(hs, x_ref, g0_ref, b0_ref, wqt_ref, k_ref, vt_ref, wo_ref, g1_ref, b1_ref, o_ref)


def _tail_specs(x, mem):
    m = mem.shape[1]
    xspec = pl.BlockSpec((1, TM, D_MODEL), lambda i, j: (i, j, 0))
    wspec = pl.BlockSpec((D_MODEL, D_MODEL), lambda i, j: (0, 0))
    vec = pl.BlockSpec((1, D_MODEL), lambda i, j: (0, 0))
    return [xspec, vec, vec, wspec, pl.BlockSpec((1, m, D_MODEL), lambda i, j: (i, 0, 0)),
            pl.BlockSpec((1, D_MODEL, m), lambda i, j: (i, 0, 0)), wspec, vec, vec], xspec


def _tail_operands(x, g0, b0, mem, w_q, w_kv, w_o, g1, b1):
    k, vt = _mem_kv(mem, w_kv)
    wqt = (w_q * (XA_HEAD_DIM ** -0.5 * LOG2E)).T.astype(BF16)
    row = lambda v: v.reshape(1, D_MODEL)
    return x, row(g0), row(b0), wqt, k, vt, w_o.astype(BF16), row(g1), row(b1)


def _tail_ab(os_, ls_, o_b, w_out, x, g0, b0, mem, w_q, w_kv, w_o, g1, b1):
    b, n, _ = x.shape
    dils = [d for _, d in A_PATTERNS]
    expand = jnp.asarray(np.arange(LANES)[:, None] == np.arange(A_WIDTH)[None, :] // HEAD_DIM, BF16)
    wa = w_out[:A_WIDTH].astype(BF16)
    wc = w_out[A_WIDTH:].astype(BF16)
    tail_specs, xspec = _tail_specs(x, mem)
    return pl.pallas_call(
        _tail_ab_kernel,
        grid=(b, n // TM),
        in_specs=[pl.BlockSpec((1, TM // d, d * A_WIDTH), lambda i, j: (i, j, 0)) for d in dils]
        + [pl.BlockSpec((1, TM // d, d * LANES), lambda i, j: (i, j, 0)) for d in dils]
        + [pl.BlockSpec((LANES, A_WIDTH), lambda i, j: (0, 0)),
           pl.BlockSpec((1, TM, B_WIDTH), lambda i, j: (i, j, 0)),
           pl.BlockSpec(wa.shape, lambda i, j: (0, 0)), pl.BlockSpec(wc.shape, lambda i, j: (0, 0))]
        + tail_specs,
        out_specs=xspec,
        out_shape=jax.ShapeDtypeStruct((b, n, D_MODEL), F32),
        scratch_shapes=[pltpu.VMEM((2, A_WIDTH // LANES, TM, LANES), F32), pltpu.VMEM((2, TM, LANES), F32)],
        compiler_params=_cparams("parallel", "parallel"),
        name="tail_ab",
    )(*os_, *ls_, expand, o_b, wa, wc, *_tail_operands(x, g0, b0, mem, w_q, w_kv, w_o, g1, b1))


def _tail_cd(o_ct, u, pool_w, pool_scale, w_out, x, g0, b0, mem, w_q, w_kv, w_o, g1, b1):
    b, n, _ = x.shape
    per = TM // POOL_HALO
    last = n // POOL_HALO - 1
    pool_wbd = jax.scipy.linalg.block_diag(*[pool_w[g] for g in range(4)]).astype(BF16)
    wa = w_out[:C_WIDTH].astype(BF16)
    wc = w_out[C_WIDTH:].astype(BF16)
    tail_specs, xspec = _tail_specs(x, mem)
    return pl.pallas_call(
        functools.partial(_tail_cd_kernel, n=n),
        grid=(b, n // TM),
        in_specs=[pl.BlockSpec((1, C_WIDTH, TM), lambda i, j: (i, 0, j)),
                  pl.BlockSpec((1, POOL_HALO, D_WIDTH), lambda i, j: (i, jnp.maximum(j * per - 1, 0), 0)),
                  pl.BlockSpec((1, TM, D_WIDTH), lambda i, j: (i, j, 0)),
                  pl.BlockSpec((1, POOL_HALO, D_WIDTH), lambda i, j: (i, jnp.minimum((j + 1) * per, last), 0)),
                  pl.BlockSpec((D_WIDTH, D_WIDTH), lambda i, j: (0, 0)),
                  pl.BlockSpec((1, D_WIDTH), lambda i, j: (0, 0)),
                  pl.BlockSpec(wa.shape, lambda i, j: (0, 0)), pl.BlockSpec(wc.shape, lambda i, j: (0, 0))]
        + tail_specs,
        out_specs=xspec,
        out_shape=jax.ShapeDtypeStruct((b, n, D_MODEL), F32),
        scratch_shapes=[pltpu.VMEM((TM + 2 * POOL_HALO, D_WIDTH), F32)],
        compiler_params=_cparams("parallel", "parallel"),
        name="tail_cd",
    )(o_ct, u, u, u, pool_wbd, pool_scale.reshape(1, D_WIDTH), wa, wc,
      *_tail_operands(x, g0, b0, mem, w_q, w_kv, w_o, g1, b1))


def _swiglu_kernel(x_ref, win_ref, wo_ref, g_ref, b_ref, o_ref):
    halves = [slice(0, TM // 2), slice(TM // 2, TM)]
    gate_up = []
    for rows in halves:
        xb = x_ref[0, rows, :].astype(BF16)
        gate_up.append((_dot(xb, win_ref[:, :FFN_HIDDEN]), _dot(xb, win_ref[:, FFN_HIDDEN:])))
    parts = []
    for gate, up in gate_up:
        hid = (gate * (1.0 / (1.0 + jnp.exp(-gate))) * up).astype(BF16)
        parts.append(_dot(hid, wo_ref[...]))
    for rows, part in zip(halves, parts):
        o_ref[0, rows, :] = _layer_norm(DN_ALPHA * x_ref[0, rows, :] + part, g_ref[...], b_ref[...])


def _swiglu(x, w_in, w_out, g, bias):
    b, n, _ = x.shape
    win = w_in.astype(BF16)
    wout = w_out.astype(BF16)
    xspec = pl.BlockSpec((1, TM, D_MODEL), lambda i, t: (i, t, 0))
    vec = pl.BlockSpec((1, D_MODEL), lambda i, t: (0, 0))
    resident = pl.Buffered(1)
    return pl.pallas_call(
        _swiglu_kernel,
        grid=(b, n // TM),
        in_specs=[xspec,
                  pl.BlockSpec(win.shape, lambda i, t: (0, 0), pipeline_mode=resident),
                  pl.BlockSpec(wout.shape, lambda i, t: (0, 0), pipeline_mode=resident), vec, vec],
        out_specs=xspec,
        out_shape=jax.ShapeDtypeStruct((b, n, D_MODEL), F32),
        compiler_params=_cparams("parallel", "parallel"),
        name="swiglu_ln",
    )(x, win, wout, g.reshape(1, D_MODEL), bias.reshape(1, D_MODEL))


N_QK_HEADS = C_Q_HEADS + C_KV_HEADS
QK_ROWS = N_QK_HEADS * HEAD_DIM


def _proj_cd_kernel(x_ref, wt_ref, wu_ref, gain_ref, cos_ref, sin_ref,
                    qt_ref, k_ref, vt_ref, u_ref, kn_ref, *, tm):
    xb = x_ref[0].astype(BF16)
    u_ref[0] = _dot(xb, wu_ref[...])
    zt = _dot_nt(wt_ref[...], xb)
    z = zt[:QK_ROWS].reshape(N_QK_HEADS, HEAD_DIM, tm)
    ssq = jnp.sum(z * z, axis=1, keepdims=True)
    zn = z * lax.rsqrt(ssq * (1.0 / HEAD_DIM) + RMS_EPS) * gain_ref[...]
    half = HEAD_DIM // 2
    x1 = zn[:, :half]
    x2 = zn[:, half:]
    c = cos_ref[...][None]
    s = sin_ref[...][None]
    rot = jnp.concatenate([x1 * c - x2 * s, x1 * s + x2 * c], axis=1).reshape(QK_ROWS, tm)
    qt_ref[0] = rot[:C_WIDTH].astype(BF16)
    kb = rot[C_WIDTH:].astype(BF16)
    k_ref[0] = kb.astype(F32).T.astype(BF16)
    kf = kb.astype(F32).reshape(C_KV_HEADS, HEAD_DIM, tm)
    kn_ref[0] = jnp.sum(kf * kf, axis=1, keepdims=True)
    vt = zt[QK_ROWS:].astype(BF16)
    for c0 in range(tm // GQA_TKC):
        vt_ref[0, c0] = vt[:, c0 * GQA_TKC:(c0 + 1) * GQA_TKC]


def _rope_tables_t(n):
    rows = n // GRID_W
    row_id = jnp.broadcast_to(jnp.arange(rows)[:, None], (rows, GRID_W)).reshape(n)
    col_id = jnp.broadcast_to(jnp.arange(GRID_W)[None, :], (rows, GRID_W)).reshape(n)
    axis_dim = HEAD_DIM // 2
    freqs = ROPE_THETA ** (-jnp.arange(0, axis_dim, 2, dtype=F32) / axis_dim)
    ang = jnp.concatenate([row_id[:, None] * freqs, col_id[:, None] * freqs], axis=-1)
    return jnp.cos(ang).T, jnp.sin(ang).T


def _proj_cd(x, w_in, q_norm, k_norm):
    b, n, _ = x.shape
    tm = TM
    half = HEAD_DIM // 2
    wqk = w_in[:, :QK_ROWS].reshape(D_MODEL, N_QK_HEADS, half, 2)
    wqk = wqk.transpose(1, 3, 2, 0).reshape(QK_ROWS, D_MODEL)
    wt = jnp.concatenate([wqk, w_in[:, QK_ROWS:QK_ROWS + C_KV_WIDTH].T], axis=0).astype(BF16)
    wu = w_in[:, QK_ROWS + C_KV_WIDTH:].astype(BF16)
    qg = q_norm.reshape(half, 2).T.reshape(HEAD_DIM) * (HEAD_DIM ** -0.5 * LOG2E)
    kg = k_norm.reshape(half, 2).T.reshape(HEAD_DIM)
    gain = jnp.concatenate([jnp.tile(qg[None], (C_Q_HEADS, 1)), jnp.tile(kg[None], (C_KV_HEADS, 1))])
    gain = gain.reshape(N_QK_HEADS, HEAD_DIM, 1).astype(F32)
    cos_t, sin_t = _rope_tables_t(n)
    nc = n // GQA_TKC
    return pl.pallas_call(
        functools.partial(_proj_cd_kernel, tm=tm),
        grid=(b, n // tm),
        in_specs=[pl.BlockSpec((1, tm, D_MODEL), lambda i, j: (i, j, 0)),
                  pl.BlockSpec(wt.shape, lambda i, j: (0, 0)),
                  pl.BlockSpec(wu.shape, lambda i, j: (0, 0)),
                  pl.BlockSpec(gain.shape, lambda i, j: (0, 0, 0)),
                  pl.BlockSpec((HEAD_DIM // 2, tm), lambda i, j: (0, j)),
                  pl.BlockSpec((HEAD_DIM // 2, tm), lambda i, j: (0, j))],
        out_specs=[pl.BlockSpec((1, C_WIDTH, tm), lambda i, j: (i, 0, j)),
                   pl.BlockSpec((1, tm, C_KV_WIDTH), lambda i, j: (i, j, 0)),
                   pl.BlockSpec((1, tm // GQA_TKC, C_KV_WIDTH, GQA_TKC), lambda i, j: (i, j, 0, 0)),
                   pl.BlockSpec((1, tm, D_WIDTH), lambda i, j: (i, j, 0)),
                   pl.BlockSpec((1, C_KV_HEADS, 1, tm), lambda i, j: (i, 0, 0, j))],
        out_shape=[jax.ShapeDtypeStruct((b, C_WIDTH, n), BF16),
                   jax.ShapeDtypeStruct((b, n, C_KV_WIDTH), BF16),
                   jax.ShapeDtypeStruct((b, nc, C_KV_WIDTH, GQA_TKC), BF16),
                   jax.ShapeDtypeStruct((b, n, D_WIDTH), F32),
                   jax.ShapeDtypeStruct((b, C_KV_HEADS, 1, n), F32)],
        compiler_params=_cparams("parallel", "parallel"),
        name="proj_cd",
    )(x, wt, wu, gain, cos_t, sin_t)


def _gqa_kernel(qt_ref, k_ref, vt_ref, kn_ref, ot_ref, m_ref, l_ref, acc_ref, qpad_ref, *, tq, n):
    g = pl.program_id(1)
    row_half = lax.broadcasted_iota(jnp.int32, (LANES, 1), 0) // HEAD_DIM
    mine = row_half == (g % 2)
    k_max2 = jnp.max(kn_ref[0, 0], axis=1, keepdims=True)
    bound_max = jnp.zeros((1, 1), F32)
    for j in range(C_REP):
        qj = qt_ref[0, j * HEAD_DIM:(j + 1) * HEAD_DIM, :]
        q2 = jnp.concatenate([qj, qj], axis=0)
        qpad_ref[j] = jnp.where(mine, q2, jnp.zeros_like(q2))
        qf = qj.astype(F32)
        bound = jnp.sqrt(jnp.sum(qf * qf, axis=0, keepdims=True) * k_max2) * GQA_BOUND_SLACK
        m_ref[j] = bound
        bound_max = jnp.maximum(bound_max, jnp.max(bound, axis=1, keepdims=True))
    fixed_shift = bound_max[0, 0] <= GQA_BOUND_LIMIT
    acc_ref[...] = jnp.zeros(acc_ref.shape, F32)
    ones = jnp.ones((GQA_VROWS - HEAD_DIM, GQA_TKC), BF16)
    unroll = GQA_PIECES // (C_REP * (tq // GQA_TW))

    def body(c, carry, *, online):
        kchs, vchs = [], []
        for u in range(unroll):
            cc = c * unroll + u
            start = pl.multiple_of(cc * GQA_TKC, GQA_TKC)
            kchs.append(k_ref[0, pl.ds(start, GQA_TKC), :])
            vchs.append(jnp.concatenate([vt_ref[0, cc], ones], axis=0))
        pieces = [(u, j, slice(s * GQA_TW, (s + 1) * GQA_TW))
                  for u in range(unroll) for j in range(C_REP) for s in range(tq // GQA_TW)]

        def scores(i):
            u, j, cols = pieces[i]
            return _dot(kchs[u], qpad_ref[j, :, cols])

        pending = [scores(i) for i in range(GQA_AHEAD)]
        for i, (u, j, cols) in enumerate(pieces):
            st = pending.pop(0)
            if i + GQA_AHEAD < len(pieces):
                pending.append(scores(i + GQA_AHEAD))
            m_old = m_ref[j, :, cols]
            if not online:
                p = jnp.exp2(st - m_old)
                l_ref[j, :, cols] = l_ref[j, :, cols] + jnp.sum(p.reshape(GQA_TKC // 8, 8, GQA_TW), axis=0)
                acc_ref[j, :HEAD_DIM, cols] = acc_ref[j, :HEAD_DIM, cols] + _dot(
                    vchs[u][:HEAD_DIM], p.astype(BF16))
                continue
            m_new = jnp.maximum(m_old, jnp.max(st, axis=0, keepdims=True))
            alpha = jnp.exp2(m_old - m_new)
            p = jnp.exp2(st - m_new)
            acc_ref[j, :, cols] = alpha * acc_ref[j, :, cols] + _dot(vchs[u], p.astype(BF16))
            m_ref[j, :, cols] = m_new
        return carry

    trips = n // (GQA_TKC * unroll)

    @pl.when(fixed_shift)
    def _():
        l_ref[...] = jnp.zeros(l_ref.shape, F32)
        lax.fori_loop(0, trips, functools.partial(body, online=False), 0)
        for j in range(C_REP):
            acc_ref[j, HEAD_DIM:HEAD_DIM + 1, :] = jnp.sum(l_ref[j], axis=0, keepdims=True)

    @pl.when(jnp.logical_not(fixed_shift))
    def _():
        m_ref[...] = jnp.full(m_ref.shape, NEG_INF, F32)
        lax.fori_loop(0, trips, functools.partial(body, online=True), 0)

    for j in range(C_REP):
        l = acc_ref[j, HEAD_DIM:HEAD_DIM + 1, :]
        ot_ref[0, j * HEAD_DIM:(j + 1) * HEAD_DIM, :] = (acc_ref[j, :HEAD_DIM, :] / l).astype(BF16)


def _gqa(qt, k, vt, kn):
    b, _, n = qt.shape
    tq = min(GQA_TQ, n)
    nc = n // GQA_TKC
    rows = C_REP * HEAD_DIM
    return pl.pallas_call(
        functools.partial(_gqa_kernel, tq=tq, n=n),
        grid=(b, C_KV_HEADS, n // tq),
        in_specs=[pl.BlockSpec((1, rows, tq), lambda i, g, t: (i, g, t)),
                  pl.BlockSpec((1, n, LANES), lambda i, g, t: (i, 0, g // 2)),
                  pl.BlockSpec((1, nc, HEAD_DIM, GQA_TKC), lambda i, g, t: (i, 0, g, 0)),
                  pl.BlockSpec((1, 1, 1, n), lambda i, g, t: (i, g, 0, 0))],
        out_specs=pl.BlockSpec((1, rows, tq), lambda i, g, t: (i, g, t)),
        out_shape=jax.ShapeDtypeStruct((b, C_WIDTH, n), BF16),
        scratch_shapes=[pltpu.VMEM((C_REP, 1, tq), F32), pltpu.VMEM((C_REP, 8, tq), F32),
                        pltpu.VMEM((C_REP, GQA_VROWS, tq), F32), pltpu.VMEM((C_REP, LANES, tq), BF16)],
        compiler_params=_cparams("parallel", "parallel", "parallel"),
        name="gqa_flash",
    )(qt, k, vt, kn)


def _pool_tile(up_ref, uc_ref, un_ref, w_ref, scale_ref, buf, *, tm, n):
    i = pl.program_id(1)
    cur = uc_ref[0]
    buf[0:POOL_HALO, :] = jnp.where(i > 0, up_ref[0], 0.0)
    buf[POOL_HALO:POOL_HALO + tm, :] = cur
    buf[POOL_HALO + tm:, :] = jnp.where(i < pl.num_programs(1) - 1, un_ref[0], 0.0)
    lane_group = lax.broadcasted_iota(jnp.int32, (1, D_WIDTH), 1) // 64
    half_w = jnp.left_shift(1, lane_group)
    acc = jnp.zeros((tm, D_WIDTH), F32)
    for j in range(-POOL_HALO, POOL_HALO):
        inside = (j >= -half_w) & (j < half_w)
        acc = acc + jnp.where(inside, buf[POOL_HALO + j:POOL_HALO + j + tm, :], 0.0)
    t = i * tm + lax.broadcasted_iota(jnp.int32, (tm, 1), 0)
    cnt = jnp.minimum(t + half_w, n) - jnp.maximum(t - half_w, 0)
    mixed = (acc / cnt.astype(F32) - cur).astype(BF16)
    return (_dot(mixed, w_ref[...]) * scale_ref[...]).astype(BF16)


def _trunk(x, mem, rel_bias, ab_w_in, ab_fnet_g, ab_fnet_w, ab_w_out,
           cd_w_in, cd_q_norm, cd_k_norm, cd_pool_w, cd_pool_scale, cd_w_out,
           xa_w_q, xa_w_kv, xa_w_o, ffn_w_in, ffn_w_out, ln_g, ln_b):
    def tail_args(layer):
        return (ln_g[layer, 0], ln_b[layer, 0], mem, xa_w_q[layer], xa_w_kv[layer], xa_w_o[layer],
                ln_g[layer, 1], ln_b[layer, 1])

    for layer in range(DEPTH):
        i = layer // 2
        if layer % 2 == 0:
            w_in = ab_w_in[i]
            w_in = jnp.concatenate([w_in[:, :A_WIDTH] * (HEAD_DIM ** -0.5 * LOG2E), w_in[:, A_WIDTH:]], axis=1)
            *views, u = _proj_ab(x, w_in.astype(BF16))
            outs = [_dilated(view, rel_bias, d) for view, (_, d) in zip(views, A_PATTERNS)]
            o_b = _fnet(u, ab_fnet_g[i], ab_fnet_w[i])
            x = _tail_ab([o for o, _ in outs], [l for _, l in outs], o_b, ab_w_out[i], x, *tail_args(layer))
        else:
            qt, k, vt, u, kn = _proj_cd(x, cd_w_in[i], cd_q_norm[i], cd_k_norm[i])
            o_c = _gqa(qt, k, vt, kn)
            x = _tail_cd(o_c, u, cd_pool_w[i], cd_pool_scale[i], cd_w_out[i], x, *tail_args(layer))
        x = _swiglu(x, ffn_w_in[layer], ffn_w_out[layer], ln_g[layer, 2], ln_b[layer, 2])
    return x


def kernel(x_prompt, x_sample, mem_prompt, mem_sample, rel_bias, ab_w_in, ab_fnet_g, ab_fnet_w, ab_w_out, cd_w_in, cd_q_norm, cd_k_norm, cd_pool_w, cd_pool_scale, cd_w_out, xa_w_q, xa_w_kv, xa_w_o, ffn_w_in, ffn_w_out, ln_g, ln_b):
    params = (rel_bias, ab_w_in, ab_fnet_g, ab_fnet_w, ab_w_out,
              cd_w_in, cd_q_norm, cd_k_norm, cd_pool_w, cd_pool_scale, cd_w_out,
              xa_w_q, xa_w_kv, xa_w_o, ffn_w_in, ffn_w_out, ln_g, ln_b)
    return (_trunk(x_prompt, mem_prompt, *params), _trunk(x_sample, mem_sample, *params))
```

```python
import functools
import math

import numpy as np
import jax
import jax.numpy as jnp
from jax import lax
from jax.experimental import pallas as pl
from jax.experimental.pallas import tpu as pltpu

F32 = jnp.float32
BF16 = jnp.bfloat16

D_MODEL = 1024
HEAD_DIM = 64
GRID_W = 64
LN_EPS = 1e-5
RMS_EPS = 1e-6
NEG_INF = -1e30
DEPTH = 2
A_HEADS = 12
A_WIDTH = A_HEADS * HEAD_DIM
A_PATTERNS = ((128, 1), (512, 4), (2048, 16))
A_HALF = 64
QKV_WIDTH = 3 * A_WIDTH
N_BUCKETS = 32
REL_MAX_DIST = 1024
B_WIDTH = 256
C_Q_HEADS = 12
C_KV_HEADS = 4
C_REP = C_Q_HEADS // C_KV_HEADS
C_WIDTH = C_Q_HEADS * HEAD_DIM
C_KV_WIDTH = C_KV_HEADS * HEAD_DIM
ROPE_THETA = 10000.0
POOL_WINDOWS = (2, 4, 8, 16)
POOL_HALO = 8
D_WIDTH = 256
XA_HEADS = 4
XA_HEAD_DIM = D_MODEL // XA_HEADS
FFN_HIDDEN = 2816
DN_ALPHA = (2 * DEPTH) ** 0.25
LOG2E = 1.4426950408889634

LANES = 128
MXU_TILE = 256
BF16_ROWS = 16
VMEM_LIMIT = 56 * 1024 * 1024
TM = 512
FFT_N2 = 128
FFT_ROWS = 1024
FFT_KC = 16
FFT_AHEAD = 3
DIL_AHEAD = 3
DIL_TQ = 512
DIL_SQ = MXU_TILE // 2
DIL_SK = DIL_SQ + 2 * A_HALF
DIL_PIECES = 48
GQA_TQ = 8192
GQA_TW = MXU_TILE
GQA_TKC = 256
GQA_VROWS = HEAD_DIM + BF16_ROWS
GQA_PIECES = 192
GQA_AHEAD = 3
GQA_BOUND_SLACK = 1.0 + 2.0 ** -10
GQA_BOUND_LIMIT = 60.0


def _cparams(*sem):
    return pltpu.CompilerParams(dimension_semantics=sem, vmem_limit_bytes=VMEM_LIMIT)


def _dot(a, b):
    return jnp.dot(a, b, preferred_element_type=F32)


def _dot_nt(a, b):
    return lax.dot_general(a, b, (((1,), (1,)), ((), ())), preferred_element_type=F32)


def _dot_tn(a, b):
    return lax.dot_general(a, b, (((0,), (0,)), ((), ())), preferred_element_type=F32)


def _split(x):
    hi = x.astype(BF16)
    lo = (x - hi.astype(F32)).astype(BF16)
    return hi, lo


def _dot3(ah, al, bh, bl):
    return _dot(ah, bh) + _dot(al, bh) + _dot(ah, bl)


def _np_split(x):
    x = np.asarray(x, np.float32)
    hi = x.astype(BF16)
    lo = (x - hi.astype(np.float32)).astype(BF16)
    return jnp.asarray(hi), jnp.asarray(lo)


def _layer_norm(h, g, b):
    mu = jnp.mean(h, axis=-1, keepdims=True)
    xc = h - mu
    var = jnp.mean(xc * xc, axis=-1, keepdims=True)
    return xc * lax.rsqrt(var + LN_EPS) * g + b


def _proj_ab_kernel(x_ref, w_ref, qkv1_ref, qkv4_ref, qkv16_ref, u_ref, zbuf):
    xb = x_ref[0].astype(BF16)
    chunks = list(range(0, QKV_WIDTH, MXU_TILE))

    for c in chunks:
        z = _dot(xb, w_ref[:, c:c + MXU_TILE])
        zbuf[c // LANES] = z[:, :LANES]
        zbuf[c // LANES + 1] = z[:, LANES:]
        qkv1_ref[0, :, c:c + MXU_TILE] = z.astype(BF16)
    u_ref[0] = _dot(xb, w_ref[:, QKV_WIDTH:])
    for (_, d), ref in zip(A_PATTERNS[1:], (qkv4_ref, qkv16_ref)):
        rows = TM // d
        for r in range(d):
            for ct in range(QKV_WIDTH // LANES):
                col = r * QKV_WIDTH + ct * LANES
                ref[0, :, col:col + LANES] = zbuf[ct, pl.ds(r, rows, stride=d), :].astype(BF16)


def _proj_ab(x, w):
    b, n, _ = x.shape
    dils = [d for _, d in A_PATTERNS]
    return pl.pallas_call(
        _proj_ab_kernel,
        grid=(b, n // TM),
        in_specs=[pl.BlockSpec((1, TM, D_MODEL), lambda i, j: (i, j, 0)),
                  pl.BlockSpec(w.shape, lambda i, j: (0, 0))],
        out_specs=[pl.BlockSpec((1, TM // d, d * QKV_WIDTH), lambda i, j: (i, j, 0)) for d in dils]
        + [pl.BlockSpec((1, TM, B_WIDTH), lambda i, j: (i, j, 0))],
        out_shape=[jax.ShapeDtypeStruct((b, n // d, d * QKV_WIDTH), BF16) for d in dils]
        + [jax.ShapeDtypeStruct((b, n, B_WIDTH), F32)],
        scratch_shapes=[pltpu.VMEM((QKV_WIDTH // LANES, TM, LANES), F32)],
        compiler_params=_cparams("parallel", "parallel"),
        name="proj_ab",
    )(x, w)


def _t5_bucket_np(rel):
    nb = N_BUCKETS // 2
    max_exact = nb // 2
    ret = np.where(rel > 0, nb, 0)
    n = np.abs(rel)
    nf = np.maximum(n, 1).astype(np.float32)
    large = max_exact + (np.log(nf / max_exact) / math.log(REL_MAX_DIST / max_exact)
                         * (nb - max_exact)).astype(np.int32)
    large = np.minimum(large, nb - 1)
    return ret + np.where(n < max_exact, n, large)


def _band_bias(rel_bias, dilation):
    tq, tk = DIL_SQ, DIL_SK
    band = 2 * A_HALF + 1
    bucket = _t5_bucket_np((np.arange(band) - A_HALF) * dilation)
    row = (rel_bias[jnp.asarray(bucket)].T * LOG2E).astype(F32)
    row = jnp.concatenate([row, jnp.full((A_HEADS, tk + 1 - band), NEG_INF, F32)], axis=1)
    bias = jnp.tile(row, (1, tq))[:, :tq * tk].reshape(A_HEADS, tq, tk)
    bias = bias.transpose(0, 2, 1)
    bias = bias.reshape(A_HEADS // 2, 2, tk, tq).transpose(0, 2, 1, 3).reshape(A_HEADS // 2, tk, 2 * tq)
    key = np.arange(tk)[None, :, None]
    before = jnp.asarray(key < A_HALF)
    after = jnp.asarray(key >= A_HALF + tq)
    first = jnp.where(before, NEG_INF, bias)
    return jnp.stack([bias, first, jnp.where(after, NEG_INF, bias), jnp.where(after, NEG_INF, first)])


def _dilated_kernel(prev_ref, cur_ref, next_ref, bias_ref, o_ref, lse_ref, kbuf, vbuf, *, tq, n_res):
    for rr in range(n_res):
        for buf, c0 in ((kbuf, rr * QKV_WIDTH + A_WIDTH), (vbuf, rr * QKV_WIDTH + 2 * A_WIDTH)):
            buf[rr, 0:A_HALF, :] = prev_ref[0, :, c0:c0 + A_WIDTH]
            buf[rr, A_HALF:A_HALF + tq, :] = cur_ref[0, :, c0:c0 + A_WIDTH]
            buf[rr, A_HALF + tq:, :] = next_ref[0, :, c0:c0 + A_WIDTH]

    i = pl.program_id(1)
    n_sub = tq // DIL_SQ
    row_low = lax.broadcasted_iota(jnp.int32, (LANES, 1), 0) < HEAD_DIM
    row16 = lax.broadcasted_iota(jnp.int32, (BF16_ROWS, 1), 0)
    ones = jnp.ones((BF16_ROWS, DIL_SK), BF16)
    qts, vts = {}, {}
    pieces = [(rr, pair, sub) for rr in range(n_res) for pair in range(A_HEADS // 2) for sub in range(n_sub)]

    def variant(sub):
        v = 0
        if sub == 0:
            v = v + jnp.where(i == 0, 1, 0)
        if sub == n_sub - 1:
            v = v + jnp.where(i == pl.num_programs(1) - 1, 2, 0)
        return v

    def scores(idx):
        rr, pair, sub = pieces[idx]
        cols = slice(pair * LANES, (pair + 1) * LANES)
        if (rr, pair) not in qts:
            qcols = slice(rr * QKV_WIDTH + pair * LANES, rr * QKV_WIDTH + (pair + 1) * LANES)
            qts[rr, pair] = cur_ref[0, :, qcols].astype(F32).T
        qsub = qts[rr, pair][:, sub * DIL_SQ:(sub + 1) * DIL_SQ]
        rhs = jnp.concatenate([jnp.where(row_low, qsub, 0.0), jnp.where(row_low, 0.0, qsub)], axis=1)
        keys = kbuf[rr, sub * DIL_SQ:sub * DIL_SQ + DIL_SK, cols]
        return _dot(keys, rhs.astype(BF16)) + bias_ref[variant(sub), pair]

    pending = [scores(idx) for idx in range(DIL_AHEAD)]
    lse_t = {(rr, sub): jnp.zeros((BF16_ROWS, DIL_SQ), F32) for rr in range(n_res) for sub in range(n_sub)}
    for idx, (rr, pair, sub) in enumerate(pieces):
        st = pending.pop(0)
        if idx + DIL_AHEAD < len(pieces):
            pending.append(scores(idx + DIL_AHEAD))
        cols = slice(pair * LANES, (pair + 1) * LANES)
        m = jnp.max(st, axis=0, keepdims=True)
        p = jnp.exp2(st - m).astype(BF16)
        if (rr, pair) not in vts:
            vts[rr, pair] = vbuf[rr, :, cols].astype(F32).T.astype(BF16)
        vaug = jnp.concatenate([vts[rr, pair][:, sub * DIL_SQ:sub * DIL_SQ + DIL_SK], ones], axis=0)
        ot = _dot(vaug, p)
        l = ot[LANES:LANES + 1]
        lse2 = m + jnp.log2(l)
        o_pair = jnp.concatenate([ot[:HEAD_DIM, :DIL_SQ] / l[:, :DIL_SQ],
                                  ot[HEAD_DIM:LANES, DIL_SQ:] / l[:, DIL_SQ:]], axis=0)
        ocols = slice(rr * A_WIDTH + pair * LANES, rr * A_WIDTH + (pair + 1) * LANES)
        o_ref[0, sub * DIL_SQ:(sub + 1) * DIL_SQ, ocols] = o_pair.T.astype(BF16)
        lse_t[rr, sub] = jnp.where(row16 == 2 * pair, lse2[:, :DIL_SQ],
                                   jnp.where(row16 == 2 * pair + 1, lse2[:, DIL_SQ:], lse_t[rr, sub]))
    for (rr, sub), rows16 in lse_t.items():
        full = jnp.concatenate([rows16, jnp.zeros((LANES - BF16_ROWS, DIL_SQ), F32)], axis=0)
        lse_ref[0, sub * DIL_SQ:(sub + 1) * DIL_SQ, rr * LANES:(rr + 1) * LANES] = full.T


def _dilated(view, rel_bias, dilation):
    b, seq, _ = view.shape
    tq = min(DIL_TQ, seq)
    tk = tq + 2 * A_HALF
    n_res = min(dilation, max(1, DIL_PIECES // (A_HEADS // 2 * (tq // DIL_SQ))))
    bias = _band_bias(rel_bias, dilation)
    per = tq // A_HALF
    last = seq // A_HALF - 1
    width = n_res * QKV_WIDTH

    o, lse = pl.pallas_call(
        functools.partial(_dilated_kernel, tq=tq, n_res=n_res),
        grid=(b, seq // tq, dilation // n_res),
        in_specs=[pl.BlockSpec((1, A_HALF, width), lambda bi, i, r: (bi, jnp.maximum(i * per - 1, 0), r)),
                  pl.BlockSpec((1, tq, width), lambda bi, i, r: (bi, i, r)),
                  pl.BlockSpec((1, A_HALF, width), lambda bi, i, r: (bi, jnp.minimum((i + 1) * per, last), r)),
                  pl.BlockSpec(bias.shape, lambda bi, i, r: (0, 0, 0, 0))],
        out_specs=[pl.BlockSpec((1, tq, n_res * A_WIDTH), lambda bi, i, r: (bi, i, r)),
                   pl.BlockSpec((1, tq, n_res * LANES), lambda bi, i, r: (bi, i, r))],
        out_shape=[jax.ShapeDtypeStruct((b, seq, dilation * A_WIDTH), BF16),
                   jax.ShapeDtypeStruct((b, seq, dilation * LANES), F32)],
        scratch_shapes=[pltpu.VMEM((n_res, tk, A_WIDTH), BF16), pltpu.VMEM((n_res, tk, A_WIDTH), BF16)],
        compiler_params=_cparams("parallel", "parallel", "parallel"),
        name=f"dilated_d{dilation}",
    )(view, view, view, bias)
    return o, lse


def _mixture_tile(o1_ref, o4_ref, o16_ref, l1_ref, l4_ref, l16_ref, expand_ref, obuf, lbuf):
    for idx, (d, o_ref, l_ref) in enumerate(((A_PATTERNS[1][1], o4_ref, l4_ref), (A_PATTERNS[2][1], o16_ref, l16_ref))):
        rows = TM // d
        for r in range(d):
            for ct in range(A_WIDTH // LANES):
                col = r * A_WIDTH + ct * LANES
                obuf[idx, ct, pl.ds(r, rows, stride=d), :] = o_ref[0, :, col:col + LANES].astype(F32)
            lbuf[idx, pl.ds(r, rows, stride=d), :] = l_ref[0, :, r * LANES:(r + 1) * LANES]
    ls = [l1_ref[0], lbuf[0], lbuf[1]]
    mx = jnp.maximum(jnp.maximum(ls[0], ls[1]), ls[2])
    es = [jnp.exp2(l - mx) for l in ls]
    inv = 1.0 / (es[0] + es[1] + es[2])
    ws = []
    for e in es:
        wh, wl = _split(e * inv)
        ws.append(_dot(wh, expand_ref[...]) + _dot(wl, expand_ref[...]))
    pairs = []
    for pair in range(A_HEADS // 2):
        cols = slice(pair * LANES, (pair + 1) * LANES)
        os_ = [o1_ref[0, :, cols].astype(F32), obuf[0, pair], obuf[1, pair]]
        acc = None
        for g in range(3):
            t = ws[g][:, cols] * os_[g]
            acc = t if acc is None else acc + t
        pairs.append(acc.astype(BF16))
    return jnp.concatenate(pairs, axis=1)


def _group_mean_matrix():
    g = np.kron(np.eye(4), np.full((64, 64), 1.0 / 64))
    return jnp.asarray(g, BF16)


def _fnet_front_kernel(u_ref, gm_ref, gain_ref, ch_ref, cl_ref, sh_ref, sl_ref, m1h_ref, m1l_ref,
                       zr_ref, zi_ref, ybuf, zbuf, *, n1, tcn):
    rows = n1 * tcn
    u = u_ref[0].reshape(rows, B_WIDTH)
    gm = gm_ref[...]
    uh, ul = _split(u)
    mean = _dot(uh, gm) + _dot(ul, gm)
    xc = u - mean
    qh, ql = _split(xc * xc)
    var = _dot(qh, gm) + _dot(ql, gm)
    un = xc * lax.rsqrt(var + LN_EPS) * gain_ref[...]
    nh, nl = _split(un)
    yr = _dot3(nh, nl, ch_ref[...], cl_ref[...])
    yi = -_dot3(nh, nl, sh_ref[...], sl_ref[...])
    for q, val in enumerate((yr[:, :LANES], yr[:, LANES:], yi[:, :LANES], yi[:, LANES:])):
        ybuf[q] = val
    def first_stage(n2):
        col = [ybuf[q, pl.ds(n2, n1, stride=tcn), :] for q in range(4)]
        y = jnp.concatenate([jnp.concatenate(col[:2], axis=1), jnp.concatenate(col[2:], axis=1)], axis=0)
        yh, yl = _split(y)
        return _dot3(m1h_ref[...], m1l_ref[...], yh, yl)

    pending = [first_stage(n2) for n2 in range(min(FFT_AHEAD, tcn))]
    for n2 in range(tcn):
        z = pending.pop(0)
        if n2 + FFT_AHEAD < tcn:
            pending.append(first_stage(n2 + FFT_AHEAD))
        for q, val in enumerate((z[:n1, :LANES], z[:n1, LANES:], z[n1:, :LANES], z[n1:, LANES:])):
            zbuf[q, pl.ds(n2, n1, stride=tcn), :] = val
    zr_ref[0] = jnp.concatenate([zbuf[0], zbuf[1]], axis=1).reshape(n1, tcn, B_WIDTH)
    zi_ref[0] = jnp.concatenate([zbuf[2], zbuf[3]], axis=1).reshape(n1, tcn, B_WIDTH)


def _fnet_back_kernel(zr_ref, zi_ref, tch_ref, tcl_ref, tsh_ref, tsl_ref, w_ref, o_ref, obuf, *, kc, scale):
    n2 = FFT_N2

    def second_stage(kk):
        rh, rl = _split(zr_ref[0, kk])
        ih, il = _split(zi_ref[0, kk])
        return _dot3(tch_ref[kk], tcl_ref[kk], rh, rl) + _dot3(tsh_ref[kk], tsl_ref[kk], ih, il)

    pending = [second_stage(kk) for kk in range(FFT_AHEAD)]
    for kk in range(kc):
        f = pending.pop(0)
        if kk + FFT_AHEAD < kc:
            pending.append(second_stage(kk + FFT_AHEAD))
        o = _dot((f * scale).astype(BF16), w_ref[...])
        obuf[0, pl.ds(kk, n2, stride=kc), :] = o[:, :LANES]
        obuf[1, pl.ds(kk, n2, stride=kc), :] = o[:, LANES:]
    out = jnp.concatenate([obuf[0], obuf[1]], axis=1).reshape(n2, kc, B_WIDTH)
    o_ref[0] = out.astype(BF16)


def _fnet_tables(n):
    n2 = FFT_N2
    n1 = n // n2
    c = np.arange(64)
    ang = 2 * np.pi * np.outer(c, c) / 64
    cbd = np.kron(np.eye(4), np.cos(ang))
    sbd = np.kron(np.eye(4), np.sin(ang))
    k1 = np.arange(n1)
    ang1 = 2 * np.pi * np.outer(k1, k1) / n1
    c1, s1 = np.cos(ang1), np.sin(ang1)
    m1 = np.block([[c1, s1], [-s1, c1]])
    k2 = np.arange(n2)
    npr = k1[:, None, None] + n1 * k2[None, :, None]
    prod = (npr * k2[None, None, :]) % n
    ang2 = 2 * np.pi * prod / n
    return (_np_split(cbd), _np_split(sbd), _np_split(m1), _np_split(np.cos(ang2)), _np_split(np.sin(ang2)))


def _fnet(u, fnet_g, fnet_w):
    b, n, _ = u.shape
    n2 = FFT_N2
    n1 = n // n2
    tcn = FFT_ROWS // n1
    (ch, cl), (sh, sl), (m1h, m1l), (tch, tcl), (tsh, tsl) = _fnet_tables(n)
    gain = fnet_g.reshape(1, B_WIDTH)
    wbd = jax.scipy.linalg.block_diag(*[fnet_w[g] for g in range(4)]).astype(BF16)

    mat = pl.BlockSpec((B_WIDTH, B_WIDTH), lambda i, j: (0, 0))
    m1spec = pl.BlockSpec((2 * n1, 2 * n1), lambda i, j: (0, 0))
    tile = pl.BlockSpec((1, n1, tcn, B_WIDTH), lambda i, j: (i, 0, j, 0))
    zr, zi = pl.pallas_call(
        functools.partial(_fnet_front_kernel, n1=n1, tcn=tcn),
        grid=(b, n2 // tcn),
        in_specs=[tile, mat, pl.BlockSpec((1, B_WIDTH), lambda i, j: (0, 0)), mat, mat, mat, mat, m1spec, m1spec],
        out_specs=[tile, tile],
        out_shape=[jax.ShapeDtypeStruct((b, n1, n2, B_WIDTH), F32)] * 2,
        scratch_shapes=[pltpu.VMEM((4, FFT_ROWS, LANES), F32), pltpu.VMEM((4, FFT_ROWS, LANES), F32)],
        compiler_params=_cparams("parallel", "parallel"),
        name="fnet_front",
    )(u.reshape(b, n1, n2, B_WIDTH), _group_mean_matrix(), gain, ch, cl, sh, sl, m1h, m1l)

    kc = FFT_KC
    zspec = pl.BlockSpec((1, kc, n2, B_WIDTH), lambda i, j: (i, j, 0, 0))
    tspec = pl.BlockSpec((kc, n2, n2), lambda i, j: (j, 0, 0))
    out = pl.pallas_call(
        functools.partial(_fnet_back_kernel, kc=kc, scale=1.0 / math.sqrt(64.0 * n)),
        grid=(b, n1 // kc),
        in_specs=[zspec, zspec, tspec, tspec, tspec, tspec,
                  pl.BlockSpec((B_WIDTH, B_WIDTH), lambda i, j: (0, 0))],
        out_specs=pl.BlockSpec((1, n2, kc, B_WIDTH), lambda i, j: (i, 0, j, 0)),
        out_shape=jax.ShapeDtypeStruct((b, n2, n1, B_WIDTH), BF16),
        scratch_shapes=[pltpu.VMEM((2, n2 * kc, LANES), F32)],
        compiler_params=_cparams("parallel", "parallel"),
        name="fnet_back",
    )(zr, zi, tch, tcl, tsh, tsl, wbd)
    return out.reshape(b, n, B_WIDTH)


def _mem_kv_kernel(mem_ref, wk_ref, wvt_ref, k_ref, vt_ref):
    m = mem_ref[0].astype(BF16)
    k_ref[0] = _dot(m, wk_ref[...]).astype(BF16)
    vt_ref[0] = _dot_nt(wvt_ref[...], m).astype(BF16)


def _mem_kv(mem, w_kv):
    b, m, _ = mem.shape
    wk = w_kv[:, :D_MODEL].astype(BF16)
    wvt = w_kv[:, D_MODEL:].T.astype(BF16)
    wspec = pl.BlockSpec((D_MODEL, D_MODEL), lambda i: (0, 0))
    return pl.pallas_call(
        _mem_kv_kernel,
        grid=(b,),
        in_specs=[pl.BlockSpec((1, m, D_MODEL), lambda i: (i, 0, 0)), wspec, wspec],
        out_specs=[pl.BlockSpec((1, m, D_MODEL), lambda i: (i, 0, 0)),
                   pl.BlockSpec((1, D_MODEL, m), lambda i: (i, 0, 0))],
        out_shape=[jax.ShapeDtypeStruct((b, m, D_MODEL), BF16),
                   jax.ShapeDtypeStruct((b, D_MODEL, m), BF16)],
        compiler_params=_cparams("parallel"),
        name="mem_kv",
    )(mem, wk, wvt)


HALVES = (slice(0, TM // 2), slice(TM // 2, TM))


def _tail_after_mixer(hs, x_ref, g0_ref, b0_ref, wqt_ref, k_ref, vt_ref, wo_ref, g1_ref, b1_ref, o_ref):
    halves = HALVES
    x1s, qts = [], []
    for rows, h in zip(halves, hs):
        x1 = _layer_norm(DN_ALPHA * x_ref[0, rows, :] + h, g0_ref[...], b0_ref[...])
        x1s.append(x1)
        qts.append(_dot_nt(wqt_ref[...], x1.astype(BF16)).astype(BF16))
    qt = jnp.concatenate(qts, axis=1)
    heads = [slice(h * XA_HEAD_DIM, (h + 1) * XA_HEAD_DIM) for h in range(XA_HEADS)]
    scores = [_dot(k_ref[0, :, hd], qt[hd]) for hd in heads]
    ones = jnp.ones((BF16_ROWS, k_ref.shape[1]), BF16)
    outs = []
    for hd, st in zip(heads, scores):
        p = jnp.exp2(st - jnp.max(st, axis=0, keepdims=True)).astype(BF16)
        ot = _dot(jnp.concatenate([vt_ref[0, hd, :], ones], axis=0), p)
        outs.append((ot[:XA_HEAD_DIM] / ot[XA_HEAD_DIM:XA_HEAD_DIM + 1]).astype(BF16))
    ot_all = jnp.concatenate(outs, axis=0)
    hs = [_dot_tn(ot_all[:, rows], wo_ref[...]) for rows in halves]
    for rows, x1, h in zip(halves, x1s, hs):
        o_ref[0, rows, :] = _layer_norm(DN_ALPHA * x1 + h, g1_ref[...], b1_ref[...])


def _tail_ab_kernel(o1_ref, o4_ref, o16_ref, l1_ref, l4_ref, l16_ref, expand_ref, c_ref, wa_ref, wc_ref,
                    x_ref, g0_ref, b0_ref, wqt_ref, k_ref, vt_ref, wo_ref, g1_ref, b1_ref, o_ref, obuf, lbuf):
    o_a = _mixture_tile(o1_ref, o4_ref, o16_ref, l1_ref, l4_ref, l16_ref, expand_ref, obuf, lbuf)
    hs = [_dot(o_a[rows], wa_ref[...]) + _dot(c_ref[0, rows, :], wc_ref[...]) for rows in HALVES]
    _tail_after_mixer(hs, x_ref, g0_ref, b0_ref, wqt_ref, k_ref, vt_ref, wo_ref, g1_ref, b1_ref, o_ref)


def _tail_cd_kernel(at_ref, up_ref, uc_ref, un_ref, pw_ref, ps_ref, wa_ref, wc_ref,
                    x_ref, g0_ref, b0_ref, wqt_ref, k_ref, vt_ref, wo_ref, g1_ref, b1_ref, o_ref, pool_buf, *, n):
    o_d = _pool_tile(up_ref, uc_ref, un_ref, pw_ref, ps_ref, pool_buf, tm=TM, n=n)
    hs = [_dot_tn(at_ref[0, :, rows], wa_ref[...]) + _dot(o_d[rows], wc_ref[...]) for rows in HALVES]
    _tail_after_mixer(hs, x_ref, g0_ref, b0_ref, wqt_ref, k_ref, vt_ref, wo_ref, g1_ref, b1_ref, o_ref)


def _tail_specs(x, mem):
    m = mem.shape[1]
    xspec = pl.BlockSpec((1, TM, D_MODEL), lambda i, j: (i, j, 0))
    wspec = pl.BlockSpec((D_MODEL, D_MODEL), lambda i, j: (0, 0))
    vec = pl.BlockSpec((1, D_MODEL), lambda i, j: (0, 0))
    return [xspec, vec, vec, wspec, pl.BlockSpec((1, m, D_MODEL), lambda i, j: (i, 0, 0)),
            pl.BlockSpec((1, D_MODEL, m), lambda i, j: (i, 0, 0)), wspec, vec, vec], xspec


def _tail_operands(x, g0, b0, mem, w_q, w_kv, w_o, g1, b1):
    k, vt = _mem_kv(mem, w_kv)
    wqt = (w_q * (XA_HEAD_DIM ** -0.5 * LOG2E)).T.astype(BF16)
    row = lambda v: v.reshape(1, D_MODEL)
    return x, row(g0), row(b0), wqt, k, vt, w_o.astype(BF16), row(g1), row(b1)


def _tail_ab(os_, ls_, o_b, w_out, x, g0, b0, mem, w_q, w_kv, w_o, g1, b1):
    b, n, _ = x.shape
    dils = [d for _, d in A_PATTERNS]
    expand = jnp.asarray(np.arange(LANES)[:, None] == np.arange(A_WIDTH)[None, :] // HEAD_DIM, BF16)
    wa = w_out[:A_WIDTH].astype(BF16)
    wc = w_out[A_WIDTH:].astype(BF16)
    tail_specs, xspec = _tail_specs(x, mem)
    return pl.pallas_call(
        _tail_ab_kernel,
        grid=(b, n // TM),
        in_specs=[pl.BlockSpec((1, TM // d, d * A_WIDTH), lambda i, j: (i, j, 0)) for d in dils]
        + [pl.BlockSpec((1, TM // d, d * LANES), lambda i, j: (i, j, 0)) for d in dils]
        + [pl.BlockSpec((LANES, A_WIDTH), lambda i, j: (0, 0)),
           pl.BlockSpec((1, TM, B_WIDTH), lambda i, j: (i, j, 0)),
           pl.BlockSpec(wa.shape, lambda i, j: (0, 0)), pl.BlockSpec(wc.shape, lambda i, j: (0, 0))]
        + tail_specs,
        out_specs=xspec,
        out_shape=jax.ShapeDtypeStruct((b, n, D_MODEL), F32),
        scratch_shapes=[pltpu.VMEM((2, A_WIDTH // LANES, TM, LANES), F32), pltpu.VMEM((2, TM, LANES), F32)],
        compiler_params=_cparams("parallel", "parallel"),
        name="tail_ab",
    )(*os_, *ls_, expand, o_b, wa, wc, *_tail_operands(x, g0, b0, mem, w_q, w_kv, w_o, g1, b1))


def _tail_cd(o_ct, u, pool_w, pool_scale, w_out, x, g0, b0, mem, w_q, w_kv, w_o, g1, b1):
    b, n, _ = x.shape
    per = TM // POOL_HALO
    last = n // POOL_HALO - 1
    pool_wbd = jax.scipy.linalg.block_diag(*[pool_w[g] for g in range(4)]).astype(BF16)
    wa = w_out[:C_WIDTH].astype(BF16)
    wc = w_out[C_WIDTH:].astype(BF16)
    tail_specs, xspec = _tail_specs(x, mem)
    return pl.pallas_call(
        functools.partial(_tail_cd_kernel, n=n),
        grid=(b, n // TM),
        in_specs=[pl.BlockSpec((1, C_WIDTH, TM), lambda i, j: (i, 0, j)),
                  pl.BlockSpec((1, POOL_HALO, D_WIDTH), lambda i, j: (i, jnp.maximum(j * per - 1, 0), 0)),
                  pl.BlockSpec((1, TM, D_WIDTH), lambda i, j: (i, j, 0)),
                  pl.BlockSpec((1, POOL_HALO, D_WIDTH), lambda i, j: (i, jnp.minimum((j + 1) * per, last), 0)),
                  pl.BlockSpec((D_WIDTH, D_WIDTH), lambda i, j: (0, 0)),
                  pl.BlockSpec((1, D_WIDTH), lambda i, j: (0, 0)),
                  pl.BlockSpec(wa.shape, lambda i, j: (0, 0)), pl.BlockSpec(wc.shape, lambda i, j: (0, 0))]
        + tail_specs,
        out_specs=xspec,
        out_shape=jax.ShapeDtypeStruct((b, n, D_MODEL), F32),
        scratch_shapes=[pltpu.VMEM((TM + 2 * POOL_HALO, D_WIDTH), F32)],
        compiler_params=_cparams("parallel", "parallel"),
        name="tail_cd",
    )(o_ct, u, u, u, pool_wbd, pool_scale.reshape(1, D_WIDTH), wa, wc,
      *_tail_operands(x, g0, b0, mem, w_q, w_kv, w_o, g1, b1))


def _swiglu_kernel(x_ref, win_ref, wo_ref, g_ref, b_ref, o_ref):
    halves = [slice(0, TM // 2), slice(TM // 2, TM)]
    gate_up = []
    for rows in halves:
        xb = x_ref[0, rows, :].astype(BF16)
        gate_up.append((_dot(xb, win_ref[:, :FFN_HIDDEN]), _dot(xb, win_ref[:, FFN_HIDDEN:])))
    parts = []
    for gate, up in gate_up:
        hid = (gate * (1.0 / (1.0 + jnp.exp(-gate))) * up).astype(BF16)
        parts.append(_dot(hid, wo_ref[...]))
    for rows, part in zip(halves, parts):
        o_ref[0, rows, :] = _layer_norm(DN_ALPHA * x_ref[0, rows, :] + part, g_ref[...], b_ref[...])


def _swiglu(x, w_in, w_out, g, bias):
    b, n, _ = x.shape
    win = w_in.astype(BF16)
    wout = w_out.astype(BF16)
    xspec = pl.BlockSpec((1, TM, D_MODEL), lambda i, t: (i, t, 0))
    vec = pl.BlockSpec((1, D_MODEL), lambda i, t: (0, 0))
    resident = pl.Buffered(1)
    return pl.pallas_call(
        _swiglu_kernel,
        grid=(b, n // TM),
        in_specs=[xspec,
                  pl.BlockSpec(win.shape, lambda i, t: (0, 0), pipeline_mode=resident),
                  pl.BlockSpec(wout.shape, lambda i, t: (0, 0), pipeline_mode=resident), vec, vec],
        out_specs=xspec,
        out_shape=jax.ShapeDtypeStruct((b, n, D_MODEL), F32),
        compiler_params=_cparams("parallel", "parallel"),
        name="swiglu_ln",
    )(x, win, wout, g.reshape(1, D_MODEL), bias.reshape(1, D_MODEL))


N_QK_HEADS = C_Q_HEADS + C_KV_HEADS
QK_ROWS = N_QK_HEADS * HEAD_DIM


def _proj_cd_kernel(x_ref, wt_ref, wu_ref, gain_ref, cos_ref, sin_ref,
                    qt_ref, k_ref, vt_ref, u_ref, kn_ref, *, tm):
    xb = x_ref[0].astype(BF16)
    u_ref[0] = _dot(xb, wu_ref[...])
    zt = _dot_nt(wt_ref[...], xb)
    z = zt[:QK_ROWS].reshape(N_QK_HEADS, HEAD_DIM, tm)
    ssq = jnp.sum(z * z, axis=1, keepdims=True)
    zn = z * lax.rsqrt(ssq * (1.0 / HEAD_DIM) + RMS_EPS) * gain_ref[...]
    half = HEAD_DIM // 2
    x1 = zn[:, :half]
    x2 = zn[:, half:]
    c = cos_ref[...][None]
    s = sin_ref[...][None]
    rot = jnp.concatenate([x1 * c - x2 * s, x1 * s + x2 * c], axis=1).reshape(QK_ROWS, tm)
    qt_ref[0] = rot[:C_WIDTH].astype(BF16)
    kb = rot[C_WIDTH:].astype(BF16)
    k_ref[0] = kb.astype(F32).T.astype(BF16)
    kf = kb.astype(F32).reshape(C_KV_HEADS, HEAD_DIM, tm)
    kn_ref[0] = jnp.sum(kf * kf, axis=1, keepdims=True)
    vt = zt[QK_ROWS:].astype(BF16)
    for c0 in range(tm // GQA_TKC):
        vt_ref[0, c0] = vt[:, c0 * GQA_TKC:(c0 + 1) * GQA_TKC]


def _rope_tables_t(n):
    rows = n // GRID_W
    row_id = jnp.broadcast_to(jnp.arange(rows)[:, None], (rows, GRID_W)).reshape(n)
    col_id = jnp.broadcast_to(jnp.arange(GRID_W)[None, :], (rows, GRID_W)).reshape(n)
    axis_dim = HEAD_DIM // 2
    freqs = ROPE_THETA ** (-jnp.arange(0, axis_dim, 2, dtype=F32) / axis_dim)
    ang = jnp.concatenate([row_id[:, None] * freqs, col_id[:, None] * freqs], axis=-1)
    return jnp.cos(ang).T, jnp.sin(ang).T


def _proj_cd(x, w_in, q_norm, k_norm):
    b, n, _ = x.shape
    tm = TM
    half = HEAD_DIM // 2
    wqk = w_in[:, :QK_ROWS].reshape(D_MODEL, N_QK_HEADS, half, 2)
    wqk = wqk.transpose(1, 3, 2, 0).reshape(QK_ROWS, D_MODEL)
    wt = jnp.concatenate([wqk, w_in[:, QK_ROWS:QK_ROWS + C_KV_WIDTH].T], axis=0).astype(BF16)
    wu = w_in[:, QK_ROWS + C_KV_WIDTH:].astype(BF16)
    qg = q_norm.reshape(half, 2).T.reshape(HEAD_DIM) * (HEAD_DIM ** -0.5 * LOG2E)
    kg = k_norm.reshape(half, 2).T.reshape(HEAD_DIM)
    gain = jnp.concatenate([jnp.tile(qg[None], (C_Q_HEADS, 1)), jnp.tile(kg[None], (C_KV_HEADS, 1))])
    gain = gain.reshape(N_QK_HEADS, HEAD_DIM, 1).astype(F32)
    cos_t, sin_t = _rope_tables_t(n)
    nc = n // GQA_TKC
    return pl.pallas_call(
        functools.partial(_proj_cd_kernel, tm=tm),
        grid=(b, n // tm),
        in_specs=[pl.BlockSpec((1, tm, D_MODEL), lambda i, j: (i, j, 0)),
                  pl.BlockSpec(wt.shape, lambda i, j: (0, 0)),
                  pl.BlockSpec(wu.shape, lambda i, j: (0, 0)),
                  pl.BlockSpec(gain.shape, lambda i, j: (0, 0, 0)),
                  pl.BlockSpec((HEAD_DIM // 2, tm), lambda i, j: (0, j)),
                  pl.BlockSpec((HEAD_DIM // 2, tm), lambda i, j: (0, j))],
        out_specs=[pl.BlockSpec((1, C_WIDTH, tm), lambda i, j: (i, 0, j)),
                   pl.BlockSpec((1, tm, C_KV_WIDTH), lambda i, j: (i, j, 0)),
                   pl.BlockSpec((1, tm // GQA_TKC, C_KV_WIDTH, GQA_TKC), lambda i, j: (i, j, 0, 0)),
                   pl.BlockSpec((1, tm, D_WIDTH), lambda i, j: (i, j, 0)),
                   pl.BlockSpec((1, C_KV_HEADS, 1, tm), lambda i, j: (i, 0, 0, j))],
        out_shape=[jax.ShapeDtypeStruct((b, C_WIDTH, n), BF16),
                   jax.ShapeDtypeStruct((b, n, C_KV_WIDTH), BF16),
                   jax.ShapeDtypeStruct((b, nc, C_KV_WIDTH, GQA_TKC), BF16),
                   jax.ShapeDtypeStruct((b, n, D_WIDTH), F32),
                   jax.ShapeDtypeStruct((b, C_KV_HEADS, 1, n), F32)],
        compiler_params=_cparams("parallel", "parallel"),
        name="proj_cd",
    )(x, wt, wu, gain, cos_t, sin_t)


def _gqa_kernel(qt_ref, k_ref, vt_ref, kn_ref, ot_ref, m_ref, l_ref, acc_ref, qpad_ref, *, tq, n):
    g = pl.program_id(1)
    row_half = lax.broadcasted_iota(jnp.int32, (LANES, 1), 0) // HEAD_DIM
    mine = row_half == (g % 2)
    k_max2 = jnp.max(kn_ref[0, 0], axis=1, keepdims=True)
    bound_max = jnp.zeros((1, 1), F32)
    for j in range(C_REP):
        qj = qt_ref[0, j * HEAD_DIM:(j + 1) * HEAD_DIM, :]
        q2 = jnp.concatenate([qj, qj], axis=0)
        qpad_ref[j] = jnp.where(mine, q2, jnp.zeros_like(q2))
        qf = qj.astype(F32)
        bound = jnp.sqrt(jnp.sum(qf * qf, axis=0, keepdims=True) * k_max2) * GQA_BOUND_SLACK
        m_ref[j] = bound
        bound_max = jnp.maximum(bound_max, jnp.max(bound, axis=1, keepdims=True))
    fixed_shift = bound_max[0, 0] <= GQA_BOUND_LIMIT
    acc_ref[...] = jnp.zeros(acc_ref.shape, F32)
    ones = jnp.ones((GQA_VROWS - HEAD_DIM, GQA_TKC), BF16)
    unroll = GQA_PIECES // (C_REP * (tq // GQA_TW))

    def body(c, carry, *, online):
        kchs, vchs = [], []
        for u in range(unroll):
            cc = c * unroll + u
            start = pl.multiple_of(cc * GQA_TKC, GQA_TKC)
            kchs.append(k_ref[0, pl.ds(start, GQA_TKC), :])
            vchs.append(jnp.concatenate([vt_ref[0, cc], ones], axis=0))
        pieces = [(u, j, slice(s * GQA_TW, (s + 1) * GQA_TW))
                  for u in range(unroll) for j in range(C_REP) for s in range(tq // GQA_TW)]

        def scores(i):
            u, j, cols = pieces[i]
            return _dot(kchs[u], qpad_ref[j, :, cols])

        pending = [scores(i) for i in range(GQA_AHEAD)]
        for i, (u, j, cols) in enumerate(pieces):
            st = pending.pop(0)
            if i + GQA_AHEAD < len(pieces):
                pending.append(scores(i + GQA_AHEAD))
            m_old = m_ref[j, :, cols]
            if not online:
                p = jnp.exp2(st - m_old)
                l_ref[j, :, cols] = l_ref[j, :, cols] + jnp.sum(p.reshape(GQA_TKC // 8, 8, GQA_TW), axis=0)
                acc_ref[j, :HEAD_DIM, cols] = acc_ref[j, :HEAD_DIM, cols] + _dot(
                    vchs[u][:HEAD_DIM], p.astype(BF16))
                continue
            m_new = jnp.maximum(m_old, jnp.max(st, axis=0, keepdims=True))
            alpha = jnp.exp2(m_old - m_new)
            p = jnp.exp2(st - m_new)
            acc_ref[j, :, cols] = alpha * acc_ref[j, :, cols] + _dot(vchs[u], p.astype(BF16))
            m_ref[j, :, cols] = m_new
        return carry

    trips = n // (GQA_TKC * unroll)

    @pl.when(fixed_shift)
    def _():
        l_ref[...] = jnp.zeros(l_ref.shape, F32)
        lax.fori_loop(0, trips, functools.partial(body, online=False), 0)
        for j in range(C_REP):
            acc_ref[j, HEAD_DIM:HEAD_DIM + 1, :] = jnp.sum(l_ref[j], axis=0, keepdims=True)

    @pl.when(jnp.logical_not(fixed_shift))
    def _():
        m_ref[...] = jnp.full(m_ref.shape, NEG_INF, F32)
        lax.fori_loop(0, trips, functools.partial(body, online=True), 0)

    for j in range(C_REP):
        l = acc_ref[j, HEAD_DIM:HEAD_DIM + 1, :]
        ot_ref[0, j * HEAD_DIM:(j + 1) * HEAD_DIM, :] = (acc_ref[j, :HEAD_DIM, :] / l).astype(BF16)


def _gqa(qt, k, vt, kn):
    b, _, n = qt.shape
    tq = min(GQA_TQ, n)
    nc = n // GQA_TKC
    rows = C_REP * HEAD_DIM
    return pl.pallas_call(
        functools.partial(_gqa_kernel, tq=tq, n=n),
        grid=(b, C_KV_HEADS, n // tq),
        in_specs=[pl.BlockSpec((1, rows, tq), lambda i, g, t: (i, g, t)),
                  pl.BlockSpec((1, n, LANES), lambda i, g, t: (i, 0, g // 2)),
                  pl.BlockSpec((1, nc, HEAD_DIM, GQA_TKC), lambda i, g, t: (i, 0, g, 0)),
                  pl.BlockSpec((1, 1, 1, n), lambda i, g, t: (i, g, 0, 0))],
        out_specs=pl.BlockSpec((1, rows, tq), lambda i, g, t: (i, g, t)),
        out_shape=jax.ShapeDtypeStruct((b, C_WIDTH, n), BF16),
        scratch_shapes=[pltpu.VMEM((C_REP, 1, tq), F32), pltpu.VMEM((C_REP, 8, tq), F32),
                        pltpu.VMEM((C_REP, GQA_VROWS, tq), F32), pltpu.VMEM((C_REP, LANES, tq), BF16)],
        compiler_params=_cparams("parallel", "parallel", "parallel"),
        name="gqa_flash",
    )(qt, k, vt, kn)


def _pool_tile(up_ref, uc_ref, un_ref, w_ref, scale_ref, buf, *, tm, n):
    i = pl.program_id(1)
    cur = uc_ref[0]
    buf[0:POOL_HALO, :] = jnp.where(i > 0, up_ref[0], 0.0)
    buf[POOL_HALO:POOL_HALO + tm, :] = cur
    buf[POOL_HALO + tm:, :] = jnp.where(i < pl.num_programs(1) - 1, un_ref[0], 0.0)
    lane_group = lax.broadcasted_iota(jnp.int32, (1, D_WIDTH), 1) // 64
    half_w = jnp.left_shift(1, lane_group)
    acc = jnp.zeros((tm, D_WIDTH), F32)
    for j in range(-POOL_HALO, POOL_HALO):
        inside = (j >= -half_w) & (j < half_w)
        acc = acc + jnp.where(inside, buf[POOL_HALO + j:POOL_HALO + j + tm, :], 0.0)
    t = i * tm + lax.broadcasted_iota(jnp.int32, (tm, 1), 0)
    cnt = jnp.minimum(t + half_w, n) - jnp.maximum(t - half_w, 0)
    mixed = (acc / cnt.astype(F32) - cur).astype(BF16)
    return (_dot(mixed, w_ref[...]) * scale_ref[...]).astype(BF16)


def _trunk(x, mem, rel_bias, ab_w_in, ab_fnet_g, ab_fnet_w, ab_w_out,
           cd_w_in, cd_q_norm, cd_k_norm, cd_pool_w, cd_pool_scale, cd_w_out,
           xa_w_q, xa_w_kv, xa_w_o, ffn_w_in, ffn_w_out, ln_g, ln_b):
    def tail_args(layer):
        return (ln_g[layer, 0], ln_b[layer, 0], mem, xa_w_q[layer], xa_w_kv[layer], xa_w_o[layer],
                ln_g[layer, 1], ln_b[layer, 1])

    for layer in range(DEPTH):
        i = layer // 2
        if layer % 2 == 0:
            w_in = ab_w_in[i]
            w_in = jnp.concatenate([w_in[:, :A_WIDTH] * (HEAD_DIM ** -0.5 * LOG2E), w_in[:, A_WIDTH:]], axis=1)
            *views, u = _proj_ab(x, w_in.astype(BF16))
            outs = [_dilated(view, rel_bias, d) for view, (_, d) in zip(views, A_PATTERNS)]
            o_b = _fnet(u, ab_fnet_g[i], ab_fnet_w[i])
            x = _tail_ab([o for o, _ in outs], [l for _, l in outs], o_b, ab_w_out[i], x, *tail_args(layer))
        else:
            qt, k, vt, u, kn = _proj_cd(x, cd_w_in[i], cd_q_norm[i], cd_k_norm[i])
            o_c = _gqa(qt, k, vt, kn)
            x = _tail_cd(o_c, u, cd_pool_w[i], cd_pool_scale[i], cd_w_out[i], x, *tail_args(layer))
        x = _swiglu(x, ffn_w_in[layer], ffn_w_out[layer], ln_g[layer, 2], ln_b[layer, 2])
    return x


def kernel(x_prompt, x_sample, mem_prompt, mem_sample, rel_bias, ab_w_in, ab_fnet_g, ab_fnet_w, ab_w_out, cd_w_in, cd_q_norm, cd_k_norm, cd_pool_w, cd_pool_scale, cd_w_out, xa_w_q, xa_w_kv, xa_w_o, ffn_w_in, ffn_w_out, ln_g, ln_b):
    params = (rel_bias, ab_w_in, ab_fnet_g, ab_fnet_w, ab_w_out,
              cd_w_in, cd_q_norm, cd_k_norm, cd_pool_w, cd_pool_scale, cd_w_out,
              xa_w_q, xa_w_kv, xa_w_o, ffn_w_in, ffn_w_out, ln_g, ln_b)
    return (_trunk(x_prompt, mem_prompt, *params), _trunk(x_sample, mem_sample, *params))
```
